```python
import math
import jax, jax.numpy as jnp
from jax import lax
import numpy as np

D_MODEL = 1024
BATCH = 8
SEQ = 2048
DEPTH = 2

N_A_LAYERS = DEPTH // 2
N_B_LAYERS = DEPTH - N_A_LAYERS
N_EVEN = (DEPTH + 1) // 2
N_ODD = DEPTH // 2
DEEPNORM_ALPHA = (2.0 * DEPTH) ** 0.25
DEEPNORM_BETA = (8.0 * DEPTH) ** -0.25
LN_EPS = 1e-5

DN_HEADS = 8
DN_HEAD_DIM = D_MODEL // DN_HEADS
DN_WIDTH = DN_HEADS * DN_HEAD_DIM
DN_CONV = 4
DN_CHUNK = 64
DN_NORM_EPS = 1e-6
DN_PROJ = 4 * DN_WIDTH + 2 * DN_HEADS

MB_HEADS = 8
MB_HEAD_DIM = D_MODEL // MB_HEADS
MB_WIDTH = MB_HEADS * MB_HEAD_DIM
MOBA_BLOCK = 256
MOBA_TOPK = 3
MOBA_Q_CHUNK = 8
ROPE_THETA = 10000.0
NEG_INF = -1e30

FFN_HIDDEN = D_MODEL * 7 // 2
N_EXPERTS = 8
TOP_K = 2

kernel_name = "yoco_deltanet_moba_moe_block"


def layer_norm(x, g, b):
    xf = x.astype(jnp.float32)
    mu = jnp.mean(xf, axis=-1, keepdims=True)
    var = jnp.mean(jnp.square(xf - mu), axis=-1, keepdims=True)
    return ((xf - mu) * lax.rsqrt(var + LN_EPS) * g + b).astype(x.dtype)


def apply_rope(t, positions):
    dh = t.shape[-1]
    half = dh // 2
    inv_freq = ROPE_THETA ** (-jnp.arange(half, dtype=jnp.float32) / half)
    ang = positions.astype(jnp.float32)[:, None] * inv_freq[None, :]
    cos = jnp.cos(ang)[None, :, None, :]
    sin = jnp.sin(ang)[None, :, None, :]
    tf = t.astype(jnp.float32)
    t1, t2 = tf[..., :half], tf[..., half:]
    return jnp.concatenate([t1 * cos - t2 * sin, t2 * cos + t1 * sin], axis=-1).astype(t.dtype)


def causal_short_conv(x, w):
    k_len = w.shape[0]
    s = x.shape[1]
    xp = jnp.pad(x, ((0, 0), (k_len - 1, 0), (0, 0)))
    y = xp[:, 0:s, :] * w[0]
    for j in range(1, k_len):
        y = y + xp[:, j:j + s, :] * w[j]
    return y


def l2norm(t):
    return t * lax.rsqrt(jnp.sum(t * t, axis=-1, keepdims=True) + DN_NORM_EPS)


def gated_delta_rule(q, k, v, g, beta):
    bsz, s, h, dk = q.shape
    dv = v.shape[-1]
    c = DN_CHUNK
    n = s // c

    def to_chunks(t):
        return jnp.moveaxis(t.reshape(bsz, n, c, h, *t.shape[3:]), 3, 1)

    q, k, v, g, beta = (to_chunks(t) for t in (q, k, v, g, beta))
    gc = jnp.cumsum(g, axis=-1)
    tri = jnp.tril(jnp.ones((c, c), dtype=bool))
    stri = jnp.tril(jnp.ones((c, c), dtype=bool), -1)
    decay = jnp.exp(jnp.where(tri, gc[..., :, None] - gc[..., None, :], NEG_INF))
    kb = k * beta[..., None]
    vb = v * beta[..., None]
    lmat = jnp.where(stri, jnp.einsum('bhnid,bhnjd->bhnij', kb, k) * decay, 0.0)
    amat = lmat + jnp.eye(c, dtype=jnp.float32)
    rhs = jnp.concatenate([vb, kb * jnp.exp(gc)[..., None]], axis=-1)
    sol = lax.linalg.triangular_solve(amat, rhs, left_side=True, lower=True, unit_diagonal=True)
    u, w = sol[..., :dv], sol[..., dv:]
    intra = jnp.einsum('bhnid,bhnjd->bhnij', q, k) * decay
    q_dec = q * jnp.exp(gc)[..., None]
    glast = gc[..., -1]
    k_dec = k * jnp.exp(glast[..., None] - gc)[..., None]
    xs = tuple(jnp.moveaxis(t, 2, 0) for t in (q_dec, k_dec, u, w, intra, glast))

    def step(state, inp):
        qd, kd, ui, wi, ai, gl = inp
        v_new = ui - jnp.einsum('bhcd,bhde->bhce', wi, state)
        o = jnp.einsum('bhcd,bhde->bhce', qd, state) + jnp.einsum('bhcs,bhse->bhce', ai, v_new)
        state = state * jnp.exp(gl)[..., None, None] + jnp.einsum('bhcd,bhce->bhde', kd, v_new)
        return state, o

    s0 = jnp.zeros((bsz, h, dk, dv), jnp.float32)
    _, o = lax.scan(step, s0, xs)
    o = jnp.moveaxis(o, 0, 2)
    return o.transpose(0, 2, 3, 1, 4).reshape(bsz, s, h, dv)


def gated_deltanet_mixer(x, w_in, conv_w, a_log, dt_bias, norm_w, w_out):
    bsz, s, _ = x.shape
    wd = DN_WIDTH
    proj = x @ w_in
    qkv = jax.nn.silu(causal_short_conv(proj[..., :3 * wd], conv_w))
    gate = proj[..., 3 * wd:4 * wd].reshape(bsz, s, DN_HEADS, DN_HEAD_DIM).astype(jnp.float32)
    a_in = proj[..., 4 * wd:4 * wd + DN_HEADS].astype(jnp.float32)
    b_in = proj[..., 4 * wd + DN_HEADS:].astype(jnp.float32)
    shp = (bsz, s, DN_HEADS, DN_HEAD_DIM)
    q = l2norm(qkv[..., :wd].reshape(shp).astype(jnp.float32)) * (DN_HEAD_DIM ** -0.5)
    k = l2norm(qkv[..., wd:2 * wd].reshape(shp).astype(jnp.float32))
    v = qkv[..., 2 * wd:].reshape(shp).astype(jnp.float32)
    beta = jax.nn.sigmoid(b_in)
    g = -jnp.exp(a_log.astype(jnp.float32)) * jax.nn.softplus(a_in + dt_bias.astype(jnp.float32))
    o = gated_delta_rule(q, k, v, g, beta)
    o = o * lax.rsqrt(jnp.mean(o * o, axis=-1, keepdims=True) + DN_NORM_EPS) * norm_w
    o = o * jax.nn.silu(gate)
    return o.reshape(bsz, s, wd).astype(x.dtype) @ w_out


def moba_shared_kv(h, w_kv):
    bsz, s, _ = h.shape
    kv = h @ w_kv
    k = kv[..., :MB_WIDTH].reshape(bsz, s, MB_HEADS, MB_HEAD_DIM)
    v = kv[..., MB_WIDTH:].reshape(bsz, s, MB_HEADS, MB_HEAD_DIM)
    k = apply_rope(k, jnp.arange(s))
    nb = -(-s // MOBA_BLOCK)
    pad = nb * MOBA_BLOCK - s

    def blocks(t):
        t = jnp.pad(t.astype(jnp.float32).transpose(0, 2, 1, 3), ((0, 0), (0, 0), (0, pad), (0, 0)))
        return t.reshape(bsz, MB_HEADS, nb, MOBA_BLOCK, MB_HEAD_DIM)

    kb, vb = blocks(k), blocks(v)
    kmean = jnp.mean(kb, axis=3)
    return kb, vb, kmean


def moba_mixer(x, w_q, w_o, kb, vb, kmean):
    bsz, s, _ = x.shape
    h, dh = MB_HEADS, MB_HEAD_DIM
    nb = kb.shape[2]
    sp = nb * MOBA_BLOCK
    scale = dh ** -0.5
    q = apply_rope((x @ w_q).reshape(bsz, s, h, dh), jnp.arange(s))
    q = q.astype(jnp.float32).transpose(0, 2, 1, 3)

    qb = jnp.pad(q, ((0, 0), (0, 0), (0, sp - s), (0, 0))).reshape(bsz, h, nb, MOBA_BLOCK, dh)
    s_own = jnp.einsum('bhnqd,bhnkd->bhnqk', qb, kb) * scale
    causal = jnp.tril(jnp.ones((MOBA_BLOCK, MOBA_BLOCK), dtype=bool))
    s_own = jnp.where(causal, s_own, NEG_INF)
    m_own = jnp.max(s_own, axis=-1)
    p_own = jnp.exp(s_own - m_own[..., None])
    l_own = jnp.sum(p_own, axis=-1).reshape(bsz, h, sp)[..., :s]
    o_own = jnp.einsum('bhnqk,bhnkd->bhnqd', p_own, vb).reshape(bsz, h, sp, dh)[:, :, :s]
    m_own = m_own.reshape(bsz, h, sp)[..., :s]

    n_sel = min(MOBA_TOPK, nb - 1)
    if n_sel == 0:
        out = o_own / l_own[..., None]
    else:
        q_blk = jnp.arange(s) // MOBA_BLOCK
        gate = jnp.einsum('bhsd,bhnd->bhsn', q, kmean)
        past = jnp.arange(nb)[None, :] < q_blk[:, None]
        gate = jnp.where(past, gate, NEG_INF)
        _, idx = lax.top_k(gate, n_sel)
        sel_ok = idx < q_blk[:, None]
        nc = s // MOBA_Q_CHUNK

        def to_chunks(t):
            return jnp.moveaxis(t.reshape(bsz, h, nc, MOBA_Q_CHUNK, *t.shape[3:]), 2, 0)

        def from_chunks(t):
            t = jnp.moveaxis(t, 0, 2)
            return t.reshape(bsz, h, s, *t.shape[4:])

        bi = jnp.arange(bsz)[:, None, None, None]
        hi = jnp.arange(h)[None, :, None, None]

        def attend(args):
            qc, ic, okc = args
            kg = kb[bi, hi, ic]
            sc = jnp.einsum('bhqd,bhqnkd->bhqnk', qc, kg) * scale
            sc = jnp.where(okc[..., None], sc, NEG_INF)
            m = jnp.max(sc, axis=(-2, -1))
            p = jnp.where(okc[..., None], jnp.exp(sc - m[..., None, None]), 0.0)
            l = jnp.sum(p, axis=(-2, -1))
            vg = vb[bi, hi, ic]
            o = jnp.einsum('bhqnk,bhqnkd->bhqd', p, vg)
            return m, l, o

        m_s, l_s, o_s = lax.map(attend, (to_chunks(q), to_chunks(idx), to_chunks(sel_ok)))
        m_s, l_s, o_s = from_chunks(m_s), from_chunks(l_s), from_chunks(o_s)
        m = jnp.maximum(m_own, m_s)
        w1 = jnp.exp(m_own - m)
        w2 = jnp.exp(m_s - m)
        out = (o_own * w1[..., None] + o_s * w2[..., None]) / (l_own * w1 + l_s * w2)[..., None]
    out = out.transpose(0, 2, 1, 3).reshape(bsz, s, MB_WIDTH).astype(x.dtype)
    return out @ w_o


def swiglu(x, w1, w3, w2):
    return (jax.nn.silu(x @ w1) * (x @ w3)) @ w2


def moe_ffn(x, w_router, w1, w3, w2):
    xt = x.reshape(-1, x.shape[-1])
    probs = jax.nn.softmax((xt @ w_router).astype(jnp.float32), axis=-1)
    top_p, top_i = lax.top_k(probs, TOP_K)
    top_p = top_p / jnp.sum(top_p, axis=-1, keepdims=True)
    gates = jnp.sum(jax.nn.one_hot(top_i, N_EXPERTS, dtype=jnp.float32) * top_p[..., None], axis=-2)
    y = jnp.zeros(xt.shape, jnp.float32)
    for e in range(N_EXPERTS):
        y = y + gates[:, e:e + 1] * swiglu(xt, w1[e], w3[e], w2[e])
    return y.reshape(x.shape).astype(x.dtype)


def setup_inputs(seed: int = 0) -> dict:
    key = jax.random.key(seed)
    ks = jax.random.split(key, 24)
    f32 = jnp.float32

    def nrm(i, shape, scale):
        return jax.random.normal(ks[i], shape, f32) * scale

    x = nrm(0, (BATCH, SEQ, D_MODEL), 1.0)
    a_w_in = nrm(1, (N_A_LAYERS, D_MODEL, DN_PROJ), D_MODEL ** -0.5)
    a_w_in = a_w_in.at[:, :, 2 * DN_WIDTH:3 * DN_WIDTH].multiply(DEEPNORM_BETA)
    a_conv_w = nrm(2, (N_A_LAYERS, DN_CONV, 3 * DN_WIDTH), DN_CONV ** -0.5)
    a_log_decay = jnp.log(jax.random.uniform(ks[3], (N_A_LAYERS, DN_HEADS), f32, 1.0, 16.0))
    dt = jnp.exp(jax.random.uniform(ks[4], (N_A_LAYERS, DN_HEADS), f32, math.log(1e-3), math.log(1e-1)))
    a_dt_bias = dt + jnp.log(-jnp.expm1(-dt))
    a_norm_w = 1.0 + nrm(5, (N_A_LAYERS, DN_HEAD_DIM), 0.02)
    a_w_out = nrm(6, (N_A_LAYERS, DN_WIDTH, D_MODEL), DN_WIDTH ** -0.5 * DEEPNORM_BETA)
    b_w_kv = nrm(7, (D_MODEL, 2 * MB_WIDTH), D_MODEL ** -0.5)
    b_w_kv = b_w_kv.at[:, MB_WIDTH:].multiply(DEEPNORM_BETA)
    b_w_q = nrm(8, (N_B_LAYERS, D_MODEL, MB_WIDTH), D_MODEL ** -0.5)
    b_w_o = nrm(9, (N_B_LAYERS, MB_WIDTH, D_MODEL), MB_WIDTH ** -0.5 * DEEPNORM_BETA)
    ffn_w1 = nrm(10, (N_EVEN, D_MODEL, FFN_HIDDEN), D_MODEL ** -0.5)
    ffn_w3 = nrm(11, (N_EVEN, D_MODEL, FFN_HIDDEN), D_MODEL ** -0.5)
    ffn_w2 = nrm(12, (N_EVEN, FFN_HIDDEN, D_MODEL), FFN_HIDDEN ** -0.5 * DEEPNORM_BETA)
    moe_router = nrm(13, (N_ODD, D_MODEL, N_EXPERTS), D_MODEL ** -0.5)
    moe_w1 = nrm(14, (N_ODD, N_EXPERTS, D_MODEL, FFN_HIDDEN), D_MODEL ** -0.5)
    moe_w3 = nrm(15, (N_ODD, N_EXPERTS, D_MODEL, FFN_HIDDEN), D_MODEL ** -0.5)
    moe_w2 = nrm(16, (N_ODD, N_EXPERTS, FFN_HIDDEN, D_MODEL), FFN_HIDDEN ** -0.5 * DEEPNORM_BETA)
    ln_g = 1.0 + nrm(17, (DEPTH, 2, D_MODEL), 0.02)
    ln_b = nrm(18, (DEPTH, 2, D_MODEL), 0.02)
    return {"x": x, "a_w_in": a_w_in, "a_conv_w": a_conv_w, "a_log_decay": a_log_decay,
            "a_dt_bias": a_dt_bias, "a_norm_w": a_norm_w, "a_w_out": a_w_out,
            "b_w_kv": b_w_kv, "b_w_q": b_w_q, "b_w_o": b_w_o,
            "ffn_w1": ffn_w1, "ffn_w3": ffn_w3, "ffn_w2": ffn_w2,
            "moe_router": moe_router, "moe_w1": moe_w1, "moe_w3": moe_w3, "moe_w2": moe_w2,
            "ln_g": ln_g, "ln_b": ln_b}


def reference(x, a_w_in, a_conv_w, a_log_decay, a_dt_bias, a_norm_w, a_w_out, b_w_kv, b_w_q, b_w_o,
              ffn_w1, ffn_w3, ffn_w2, moe_router, moe_w1, moe_w3, moe_w2, ln_g, ln_b):
    shared = None
    for layer in range(DEPTH):
        if layer < N_A_LAYERS:
            i = layer
            mix = gated_deltanet_mixer(x, a_w_in[i], a_conv_w[i], a_log_decay[i], a_dt_bias[i],
                                       a_norm_w[i], a_w_out[i])
        else:
            if shared is None:
                shared = moba_shared_kv(x, b_w_kv)
            j = layer - N_A_LAYERS
            mix = moba_mixer(x, b_w_q[j], b_w_o[j], *shared)
        x = layer_norm(DEEPNORM_ALPHA * x + mix, ln_g[layer, 0], ln_b[layer, 0])
        f = layer // 2
        if layer % 2 == 0:
            ffn = swiglu(x, ffn_w1[f], ffn_w3[f], ffn_w2[f])
        else:
            ffn = moe_ffn(x, moe_router[f], moe_w1[f], moe_w3[f], moe_w2[f])
        x = layer_norm(DEEPNORM_ALPHA * x + ffn, ln_g[layer, 1], ln_b[layer, 1])
    return x
```

```python
import functools
import math

import jax
import jax.numpy as jnp
from jax import lax
from jax.experimental import pallas as pl
from jax.experimental.pallas import tpu as pltpu

D_MODEL = 1024
DEPTH = 2
DEEPNORM_ALPHA = (2.0 * DEPTH) ** 0.25
LN_EPS = 1e-5
N_HEADS = 8
HEAD_DIM = 128
DN_CONV = 4
DN_CHUNK = 64
DN_NORM_EPS = 1e-6
MOBA_BLOCK = 256
MOBA_TOPK = 3
ROPE_THETA = 10000.0
NEG_INF = -1e30
FFN_HIDDEN = 3584
N_EXPERTS = 8
TOP_K = 2

LANES = 128
SUBLANES = 8
VMEM_LIMIT_BYTES = 56 * 1024 * 1024

ROWS_PROJ = 256
ROWS_LN = 512
ROWS_FFN = 512
FFN_CHUNK = 512
ROWS_COMBINE = 256

F32 = jnp.float32
BF16 = jnp.bfloat16

_NT = (((1,), (1,)), ((), ()))


def _dot(a, b):
    return jnp.dot(a, b, preferred_element_type=F32)


def _dot_nt(a, b):
    return lax.dot_general(a, b, _NT, preferred_element_type=F32)


def _params(*sem):
    return pltpu.CompilerParams(dimension_semantics=sem, vmem_limit_bytes=VMEM_LIMIT_BYTES)


def _layer_norm(y, g, b):
    mu = jnp.mean(y, axis=-1, keepdims=True)
    d = y - mu
    var = jnp.mean(d * d, axis=-1, keepdims=True)
    return d * lax.rsqrt(var + LN_EPS) * g + b


def _sigmoid(x):
    return 1.0 / (1.0 + jnp.exp(-x))


def _split3(x):
    hi = x.astype(BF16)
    r = x - hi.astype(F32)
    mid = r.astype(BF16)
    lo = (r - mid.astype(F32)).astype(BF16)
    return hi, mid, lo


def _in_proj_kernel(x_ref, w_ref, wab_ref, cw_ref, alog_ref, dtb_ref,
                    q_ref, k_ref, v_ref, gate_ref, gb_ref, conv_sc):
    rows = x_ref.shape[0]
    width = N_HEADS * HEAD_DIM
    pad = SUBLANES

    @pl.when(pl.program_id(1) == 0)
    def _():
        conv_sc[0:pad, :] = jnp.zeros((pad, 3 * width), F32)

    xb = x_ref[...].astype(BF16)
    chunk = 2 * HEAD_DIM
    for c in range(3 * width // chunk):
        cols = slice(c * chunk, (c + 1) * chunk)
        conv_sc[pad:pad + rows, cols] = _dot(xb, w_ref[:, cols])
        y = conv_sc[pad:pad + rows, cols] * cw_ref[3:4, cols]
        for j in range(DN_CONV - 1):
            y = y + conv_sc[pad - 3 + j:pad - 3 + j + rows, cols] * cw_ref[j:j + 1, cols]
        conv_sc[pad - 3:pad, cols] = conv_sc[pad + rows - 3:pad + rows, cols]
        y = y * _sigmoid(y)
        which = c * chunk // width
        for hh in range(chunk // HEAD_DIM):
            t = y[:, hh * HEAD_DIM:(hh + 1) * HEAD_DIM]
            ocols = slice(c * chunk - which * width + hh * HEAD_DIM,
                          c * chunk - which * width + (hh + 1) * HEAD_DIM)
            if which == 2:
                v_ref[:, ocols] = t.astype(BF16)
            else:
                t = t * lax.rsqrt(jnp.sum(t * t, axis=-1, keepdims=True) + DN_NORM_EPS)
                if which == 0:
                    q_ref[:, ocols] = (t * (HEAD_DIM ** -0.5)).astype(BF16)
                else:
                    k_ref[:, ocols] = t.astype(BF16)
    for c in range(width // chunk):
        cols = slice(3 * width + c * chunk, 3 * width + (c + 1) * chunk)
        gate_ref[:, c * chunk:(c + 1) * chunk] = _dot(xb, w_ref[:, cols]).astype(BF16)
    ab = _dot(xb, wab_ref[...])
    z = ab + dtb_ref[...]
    softplus = jnp.maximum(z, 0.0) + jnp.log(1.0 + jnp.exp(-jnp.abs(z)))
    g = -jnp.exp(alog_ref[...]) * softplus
    lane = lax.broadcasted_iota(jnp.int32, ab.shape, 1)
    gb_ref[...] = jnp.where(lane < N_HEADS, g, _sigmoid(ab))


def _in_proj(x2d, w_main, w_ab, conv_w, a_log_row, dt_row, batch, seq):
    t = x2d.shape[0]
    width = N_HEADS * HEAD_DIM
    rows = ROWS_PROJ
    nt = seq // rows
    row_map = lambda b, s: (b * nt + s, 0)
    const = lambda b, s: (0, 0)
    act = jax.ShapeDtypeStruct((t, width), BF16)
    return pl.pallas_call(
        _in_proj_kernel,
        grid=(batch, nt),
        in_specs=[pl.BlockSpec((rows, D_MODEL), row_map),
                  pl.BlockSpec((D_MODEL, 4 * width), const),
                  pl.BlockSpec((D_MODEL, LANES), const),
                  pl.BlockSpec((DN_CONV, 3 * width), const),
                  pl.BlockSpec((1, LANES), const),
                  pl.BlockSpec((1, LANES), const)],
        out_specs=[pl.BlockSpec((rows, width), row_map)] * 4 + [pl.BlockSpec((rows, LANES), row_map)],
        out_shape=[act, act, act, act, jax.ShapeDtypeStruct((t, LANES), F32)],
        scratch_shapes=[pltpu.VMEM((rows + SUBLANES, 3 * width), F32)],
        compiler_params=_params("arbitrary", "arbitrary"),
        name="dn_in_proj",
    )(x2d, w_main, w_ab, conv_w, a_log_row, dt_row)


def _delta_kernel(q_ref, k_ref, v_ref, gate_ref, gb_ref, nw_ref, o_ref, state_sc):
    c = DN_CHUNK
    two = 2 * c

    @pl.when(pl.program_id(1) == 0)
    def _():
        state_sc[...] = jnp.zeros(state_sc.shape, F32)

    gb = gb_ref[...]
    row64 = lax.broadcasted_iota(jnp.int32, gb.shape, 0)
    gc = gb
    sh = 1
    while sh < c:
        gc = gc + jnp.where(row64 >= sh, pltpu.roll(gc, sh, axis=0), 0.0)
        sh *= 2
    gc_pair = jnp.concatenate([gc, pltpu.roll(gc, LANES - 1, axis=1)], axis=0)
    gb_pair = jnp.concatenate([gb, pltpu.roll(gb, LANES - 1, axis=1)], axis=0)
    gc_pair_t = gc_pair.T

    r = lax.broadcasted_iota(jnp.int32, (two, two), 0)
    cc = lax.broadcasted_iota(jnp.int32, (two, two), 1)
    same = (r < c) == (cc < c)
    tri = same & (r >= cc)
    stri = same & (r > cc)
    eye = (r == cc).astype(F32)
    top_rows = lax.broadcasted_iota(jnp.int32, (two, 1), 0) < c
    top_lanes = lax.broadcasted_iota(jnp.int32, (1, two), 1) < c

    for p in range(N_HEADS // 2):
        h = 2 * p
        lanes = slice(h * HEAD_DIM, (h + 2) * HEAD_DIM)

        def stack(ref):
            t = ref[:, lanes].astype(F32)
            return jnp.concatenate([t[:, :HEAD_DIM], t[:, HEAD_DIM:]], axis=0)

        q2, k2, v2, gate2 = stack(q_ref), stack(k_ref), stack(v_ref), stack(gate_ref)
        gcol = gc_pair[:, h:h + 1]
        grow = gc_pair_t[h:h + 1, :]
        beta = gb_pair[:, N_HEADS + h:N_HEADS + h + 1]
        glast = jnp.where(top_rows, gc_pair[c - 1:c, h:h + 1], gc_pair[two - 1:two, h:h + 1])
        decay = jnp.exp(jnp.where(tri, gcol - grow, NEG_INF))
        eg = jnp.exp(gcol)
        kb = k2 * beta
        vb = v2 * beta
        k2b = k2.astype(BF16)
        lmat = jnp.where(stri, _dot_nt(kb.astype(BF16), k2b) * decay, 0.0)
        inv = eye - lmat
        pw = lmat
        for _ in range(int(math.log2(c)) - 1):
            pwb = pw.astype(BF16)
            pw = _dot(pwb, pwb)
            inv = inv + _dot(inv.astype(BF16), pw.astype(BF16))
        rhs = jnp.concatenate([vb, kb * eg], axis=1).astype(BF16)
        sol = _dot(inv.astype(BF16), rhs)
        u = sol[:, :HEAD_DIM]
        w = sol[:, HEAD_DIM:]
        amat = jnp.where(tri, _dot_nt(q2.astype(BF16), k2b) * decay, 0.0)
        qd = (q2 * eg).astype(BF16)
        wb = w.astype(BF16)
        kd_t = (k2 * jnp.exp(glast - gcol)).T
        s0 = state_sc[h]
        s1 = state_sc[h + 1]
        r0 = _dot(jnp.concatenate([wb[:c], qd[:c]], axis=0), s0.astype(BF16))
        r1 = _dot(jnp.concatenate([wb[c:], qd[c:]], axis=0), s1.astype(BF16))
        v_new = u - jnp.concatenate([r0[:c], r1[:c]], axis=0)
        vnb = v_new.astype(BF16)
        o = jnp.concatenate([r0[c:], r1[c:]], axis=0) + _dot(amat.astype(BF16), vnb)
        eglast = jnp.exp(glast)
        state_sc[h] = s0 * eglast[0:1, :] + _dot(jnp.where(top_lanes, kd_t, 0.0).astype(BF16), vnb)
        state_sc[h + 1] = s1 * eglast[c:c + 1, :] + _dot(jnp.where(top_lanes, 0.0, kd_t).astype(BF16), vnb)
        o = o * lax.rsqrt(jnp.mean(o * o, axis=-1, keepdims=True) + DN_NORM_EPS) * nw_ref[...]
        o = o * (gate2 * _sigmoid(gate2))
        o_ref[:, h * HEAD_DIM:(h + 1) * HEAD_DIM] = o[:c].astype(BF16)
        o_ref[:, (h + 1) * HEAD_DIM:(h + 2) * HEAD_DIM] = o[c:].astype(BF16)


def _delta_rule(q, k, v, gate, gb, norm_w_row, batch, seq):
    t = q.shape[0]
    width = N_HEADS * HEAD_DIM
    nc = seq // DN_CHUNK
    row_map = lambda b, s: (b * nc + s, 0)
    const = lambda b, s: (0, 0)
    wide = pl.BlockSpec((DN_CHUNK, width), row_map)
    return pl.pallas_call(
        _delta_kernel,
        grid=(batch, nc),
        in_specs=[wide, wide, wide, wide,
                  pl.BlockSpec((DN_CHUNK, LANES), row_map),
                  pl.BlockSpec((1, HEAD_DIM), const)],
        out_specs=wide,
        out_shape=jax.ShapeDtypeStruct((t, width), BF16),
        scratch_shapes=[pltpu.VMEM((N_HEADS, HEAD_DIM, HEAD_DIM), F32)],
        compiler_params=_params("arbitrary", "arbitrary"),
        name="dn_delta_rule",
    )(q, k, v, gate, gb, norm_w_row)


def _proj_ln_kernel(a_ref, w_ref, res_ref, g_ref, b_ref, o_ref):
    y = DEEPNORM_ALPHA * res_ref[...] + _dot(a_ref[...], w_ref[...])
    o_ref[...] = _layer_norm(y, g_ref[...], b_ref[...])


def _proj_ln_router_kernel(a_ref, w_ref, res_ref, g_ref, b_ref, wr_ref, o_ref, route_ref):
    y = DEEPNORM_ALPHA * res_ref[...] + _dot(a_ref[...], w_ref[...])
    xn = _layer_norm(y, g_ref[...], b_ref[...])
    o_ref[...] = xn
    xs = _split3(xn)
    ws = _split3(wr_ref[...])
    logits = jnp.zeros((xn.shape[0], LANES), F32)
    for i in range(3):
        for j in range(3 - i):
            logits = logits + _dot(xs[i], ws[j])
    lane = lax.broadcasted_iota(jnp.int32, logits.shape, 1)
    logits = jnp.where(lane < N_EXPERTS, logits, NEG_INF)
    l1 = jnp.max(logits, axis=-1, keepdims=True)
    i1 = jnp.min(jnp.where(logits == l1, lane, LANES), axis=-1, keepdims=True)
    rest = jnp.where(lane == i1, NEG_INF, logits)
    l2 = jnp.max(rest, axis=-1, keepdims=True)
    i2 = jnp.min(jnp.where(rest == l2, lane, LANES), axis=-1, keepdims=True)
    e2 = jnp.exp(l2 - l1)
    g1 = 1.0 / (1.0 + e2)
    g2 = e2 / (1.0 + e2)
    route_ref[...] = jnp.where(lane == 0, g1,
                     jnp.where(lane == 1, g2,
                     jnp.where(lane == 2, i1.astype(F32),
                     jnp.where(lane == 3, i2.astype(F32), 0.0))))


def _proj_ln(a, w, res, g_row, b_row, w_router=None):
    t, kdim = a.shape
    rows = ROWS_LN
    row_map = lambda i: (i, 0)
    const = lambda i: (0, 0)
    in_specs = [pl.BlockSpec((rows, kdim), row_map),
                pl.BlockSpec((kdim, D_MODEL), const),
                pl.BlockSpec((rows, D_MODEL), row_map),
                pl.BlockSpec((1, D_MODEL), const),
                pl.BlockSpec((1, D_MODEL), const)]
    out_full = jax.ShapeDtypeStruct((t, D_MODEL), F32)
    if w_router is None:
        return pl.pallas_call(
            _proj_ln_kernel, grid=(t // rows,), in_specs=in_specs,
            out_specs=pl.BlockSpec((rows, D_MODEL), row_map), out_shape=out_full,
            compiler_params=_params("arbitrary"), name="proj_ln",
        )(a, w, res, g_row, b_row)
    return pl.pallas_call(
        _proj_ln_router_kernel, grid=(t // rows,),
        in_specs=in_specs + [pl.BlockSpec((D_MODEL, LANES), const)],
        out_specs=[pl.BlockSpec((rows, D_MODEL), row_map), pl.BlockSpec((rows, LANES), row_map)],
        out_shape=[out_full, jax.ShapeDtypeStruct((t, LANES), F32)],
        compiler_params=_params("arbitrary"), name="proj_ln_router",
    )(a, w, res, g_row, b_row, w_router)


def _swiglu_acc(x_ref, w1_ref, w3_ref, w2_ref, acc_sc):
    f = pl.program_id(1)
    xb = x_ref[...].astype(BF16)
    h1 = _dot(xb, w1_ref[...])
    h3 = _dot(xb, w3_ref[...])
    hid = (h1 * _sigmoid(h1) * h3).astype(BF16)
    part = _dot(hid, w2_ref[...])

    @pl.when(f == 0)
    def _():
        acc_sc[...] = part

    @pl.when(f > 0)
    def _():
        acc_sc[...] += part


def _ffn_dense_kernel(x_ref, w1_ref, w3_ref, w2_ref, g_ref, b_ref, o_ref, acc_sc):
    _swiglu_acc(x_ref, w1_ref, w3_ref, w2_ref, acc_sc)

    @pl.when(pl.program_id(1) == pl.num_programs(1) - 1)
    def _():
        y = DEEPNORM_ALPHA * x_ref[...] + acc_sc[...]
        o_ref[...] = _layer_norm(y, g_ref[...], b_ref[...])


def _ffn_dense(x, w1, w3, w2, g_row, b_row):
    t = x.shape[0]
    rows, fc = ROWS_FFN, FFN_CHUNK
    nf = FFN_HIDDEN // fc
    return pl.pallas_call(
        _ffn_dense_kernel,
        grid=(t // rows, nf),
        in_specs=[pl.BlockSpec((rows, D_MODEL), lambda i, f: (i, 0)),
                  pl.BlockSpec((D_MODEL, fc), lambda i, f: (0, f)),
                  pl.BlockSpec((D_MODEL, fc), lambda i, f: (0, f)),
                  pl.BlockSpec((fc, D_MODEL), lambda i, f: (f, 0)),
                  pl.BlockSpec((1, D_MODEL), lambda i, f: (0, 0)),
                  pl.BlockSpec((1, D_MODEL), lambda i, f: (0, 0))],
        out_specs=pl.BlockSpec((rows, D_MODEL), lambda i, f: (i, 0)),
        out_shape=jax.ShapeDtypeStruct((t, D_MODEL), F32),
        scratch_shapes=[pltpu.VMEM((rows, D_MODEL), F32)],
        compiler_params=_params("arbitrary", "arbitrary"),
        name="ffn_dense",
    )(x, w1, w3, w2, g_row, b_row)


def _ffn_grouped_kernel(te_ref, tv_ref, x_ref, w1_ref, w3_ref, w2_ref, o_ref, acc_sc):
    i = pl.program_id(0)
    last = pl.program_id(1) == pl.num_programs(1) - 1

    @pl.when(tv_ref[i] > 0)
    def _():
        _swiglu_acc(x_ref, w1_ref, w3_ref, w2_ref, acc_sc)

        @pl.when(last)
        def _():
            o_ref[...] = acc_sc[...]

    @pl.when((tv_ref[i] == 0) & last)
    def _():
        o_ref[...] = jnp.zeros(o_ref.shape, F32)


def _ffn_grouped(tile_expert, tile_valid, xs, w1, w3, w2):
    n_rows = xs.shape[0]
    rows, fc = ROWS_FFN, FFN_CHUNK
    nf = FFN_HIDDEN // fc
    grid_spec = pltpu.PrefetchScalarGridSpec(
        num_scalar_prefetch=2,
        grid=(n_rows // rows, nf),
        in_specs=[pl.BlockSpec((rows, D_MODEL), lambda i, f, te, tv: (i, 0)),
                  pl.BlockSpec((None, D_MODEL, fc), lambda i, f, te, tv: (te[i], 0, f)),
                  pl.BlockSpec((None, D_MODEL, fc), lambda i, f, te, tv: (te[i], 0, f)),
                  pl.BlockSpec((None, fc, D_MODEL), lambda i, f, te, tv: (te[i], f, 0))],
        out_specs=pl.BlockSpec((rows, D_MODEL), lambda i, f, te, tv: (i, 0)),
        scratch_shapes=[pltpu.VMEM((rows, D_MODEL), F32)],
    )
    return pl.pallas_call(
        _ffn_grouped_kernel,
        grid_spec=grid_spec,
        out_shape=jax.ShapeDtypeStruct((n_rows, D_MODEL), F32),
        compiler_params=_params("arbitrary", "arbitrary"),
        name="ffn_grouped",
    )(tile_expert, tile_valid, xs, w1, w3, w2)


def _qkv_kernel(x_ref, w_ref, cos_ref, sin_ref, q_ref, k_ref, vt_ref, sel_ref, km_sc, qf_sc):
    i = pl.program_id(1)
    width = N_HEADS * HEAD_DIM
    half = HEAD_DIM // 2
    xb = x_ref[...].astype(BF16)
    cosf = cos_ref[...]
    sinf = sin_ref[...]

    @pl.when(i == 0)
    def _():
        km_sc[...] = jnp.zeros(km_sc.shape, F32)

    def rope(t):
        return t * cosf + pltpu.roll(t, half, axis=1) * sinf

    chunk = 2 * HEAD_DIM
    for c in range(width // chunk):
        kk = _dot(xb, w_ref[:, c * chunk:(c + 1) * chunk])
        vv = _dot(xb, w_ref[:, width + c * chunk:width + (c + 1) * chunk])
        qq = _dot(xb, w_ref[:, 2 * width + c * chunk:2 * width + (c + 1) * chunk])
        for hh in range(2):
            h = 2 * c + hh
            cols = slice(h * HEAD_DIM, (h + 1) * HEAD_DIM)
            part = slice(hh * HEAD_DIM, (hh + 1) * HEAD_DIM)
            kr = rope(kk[:, part])
            k_ref[:, cols] = kr.astype(BF16)
            km_row = lax.broadcasted_iota(jnp.int32, (km_sc.shape[0], HEAD_DIM), 0)
            km_sc[:, cols] = jnp.where(km_row == i, jnp.mean(kr, axis=0, keepdims=True), km_sc[:, cols])
            qr = rope(qq[:, part])
            qf_sc[:, cols] = qr
            q_ref[:, cols] = (qr * (HEAD_DIM ** -0.5)).astype(BF16)
            vt_ref[h] = vv[:, part].T.astype(BF16)

    nb = km_sc.shape[0]
    km = km_sc[...]
    lane_head = lax.broadcasted_iota(jnp.int32, km.shape, 1) // HEAD_DIM
    km_rows = jnp.concatenate([jnp.where(lane_head == h, km, 0.0) for h in range(N_HEADS)], axis=0)
    ks = _split3(km_rows)
    qs = _split3(qf_sc[...])
    gate_t = jnp.zeros((N_HEADS * nb, x_ref.shape[0]), F32)
    for a in range(3):
        for b in range(3 - a):
            gate_t = gate_t + _dot_nt(ks[a], qs[b])
    blk = lax.broadcasted_iota(jnp.int32, (nb, x_ref.shape[0]), 0)
    past = blk < i
    for h in range(N_HEADS):
        g = jnp.where(past, gate_t[h * nb:(h + 1) * nb, :], NEG_INF)
        rank = jnp.zeros(g.shape, jnp.int32)
        for n in range(nb):
            gn = g[n:n + 1, :]
            ahead = (gn > g) | ((gn == g) & (blk > n))
            rank = rank + ahead.astype(jnp.int32)
        sel_ref[h * nb:(h + 1) * nb, :] = jnp.where(past & (rank < MOBA_TOPK), 0.0, NEG_INF)


def _qkv_proj(x, w_kvq, cos_full, sin_signed, batch, seq):
    t = x.shape[0]
    width = N_HEADS * HEAD_DIM
    rows = MOBA_BLOCK
    nb = seq // rows
    row_map = lambda b, i: (b * nb + i, 0)
    act = jax.ShapeDtypeStruct((t, width), BF16)
    return pl.pallas_call(
        _qkv_kernel,
        grid=(batch, nb),
        in_specs=[pl.BlockSpec((rows, D_MODEL), row_map),
                  pl.BlockSpec((D_MODEL, 3 * width), lambda b, i: (0, 0)),
                  pl.BlockSpec((rows, HEAD_DIM), lambda b, i: (i, 0)),
                  pl.BlockSpec((rows, HEAD_DIM), lambda b, i: (i, 0))],
        out_specs=[pl.BlockSpec((rows, width), row_map),
                   pl.BlockSpec((rows, width), row_map),
                   pl.BlockSpec((None, N_HEADS, None, HEAD_DIM, rows), lambda b, i: (b, 0, i, 0, 0)),
                   pl.BlockSpec((None, N_HEADS * nb, rows), lambda b, i: (b * nb + i, 0, 0))],
        out_shape=[act, act,
                   jax.ShapeDtypeStruct((batch, N_HEADS, nb, HEAD_DIM, rows), BF16),
                   jax.ShapeDtypeStruct((batch * nb, N_HEADS * nb, rows), F32)],
        scratch_shapes=[pltpu.VMEM((nb, width), F32), pltpu.VMEM((rows, width), F32)],
        compiler_params=_params("arbitrary", "arbitrary"),
        name="moba_qkv_proj",
    )(x, w_kvq, cos_full, sin_signed)


def _attn_kernel(q_ref, kown_ref, kall_ref, vtown_ref, vtall_ref, sel_ref, o_ref):
    i = pl.program_id(2)
    blk = MOBA_BLOCK
    q = q_ref[...]
    s = _dot_nt(kown_ref[...], q)
    key = lax.broadcasted_iota(jnp.int32, s.shape, 0)
    qry = lax.broadcasted_iota(jnp.int32, s.shape, 1)
    s = jnp.where(key <= qry, s, NEG_INF)
    m = jnp.max(s, axis=0, keepdims=True)
    p = jnp.exp(s - m)
    l = jnp.sum(p, axis=0, keepdims=True)
    acc = _dot(vtown_ref[...], p.astype(BF16))

    def body(j, carry):
        m, l, acc = carry
        kj = kall_ref[pl.ds(pl.multiple_of(j * blk, blk), blk), :]
        s = _dot_nt(kj, q) + sel_ref[pl.ds(j, 1), :]
        m_new = jnp.maximum(m, jnp.max(s, axis=0, keepdims=True))
        a = jnp.exp(m - m_new)
        p = jnp.exp(s - m_new)
        l = a * l + jnp.sum(p, axis=0, keepdims=True)
        acc = a * acc + _dot(vtall_ref[j], p.astype(BF16))
        return m_new, l, acc

    m, l, acc = lax.fori_loop(0, i, body, (m, l, acc))
    o_ref[...] = (acc / l).T.astype(BF16)


def _moba_attention(q, k, vt, sel, batch, seq):
    t = q.shape[0]
    width = N_HEADS * HEAD_DIM
    blk = MOBA_BLOCK
    nb = seq // blk
    return pl.pallas_call(
        _attn_kernel,
        grid=(batch, N_HEADS, nb),
        in_specs=[pl.BlockSpec((blk, HEAD_DIM), lambda b, h, i: (b * nb + i, h)),
                  pl.BlockSpec((blk, HEAD_DIM), lambda b, h, i: (b * nb + i, h)),
                  pl.BlockSpec((seq, HEAD_DIM), lambda b, h, i: (b, h)),
                  pl.BlockSpec((None, None, None, HEAD_DIM, blk), lambda b, h, i: (b, h, i, 0, 0)),
                  pl.BlockSpec((None, None, nb, HEAD_DIM, blk), lambda b, h, i: (b, h, 0, 0, 0)),
                  pl.BlockSpec((None, nb, blk), lambda b, h, i: (b * nb + i, h, 0))],
        out_specs=pl.BlockSpec((blk, HEAD_DIM), lambda b, h, i: (b * nb + i, h)),
        out_shape=jax.ShapeDtypeStruct((t, width), BF16),
        compiler_params=_params("arbitrary", "arbitrary", "arbitrary"),
        name="moba_attention",
    )(q, k, k, vt, vt, sel)


def _gather_kernel(tv_ref, idx_ref, x_hbm, o_ref, sem):
    i = pl.program_id(0)
    rows = o_ref.shape[0]

    def row_copy(r):
        return pltpu.make_async_copy(x_hbm.at[pl.ds(idx_ref[0, r], 1)],
                                     o_ref.at[pl.ds(r, 1)], sem)

    @pl.when(tv_ref[i] > 0)
    def _():
        def start(r, _):
            row_copy(r).start()
            return 0

        def wait(r, _):
            row_copy(r).wait()
            return 0

        lax.fori_loop(0, rows, start, 0)
        lax.fori_loop(0, rows, wait, 0)

    @pl.when(tv_ref[i] == 0)
    def _():
        o_ref[...] = jnp.zeros(o_ref.shape, o_ref.dtype)


def _gather_rows(row_token, tile_valid, x):
    n_rows = row_token.shape[0]
    rows = ROWS_FFN
    grid_spec = pltpu.PrefetchScalarGridSpec(
        num_scalar_prefetch=1,
        grid=(n_rows // rows,),
        in_specs=[pl.BlockSpec((None, 1, rows), lambda i, tv: (i, 0, 0), memory_space=pltpu.SMEM),
                  pl.BlockSpec(memory_space=pl.ANY)],
        out_specs=pl.BlockSpec((rows, D_MODEL), lambda i, tv: (i, 0)),
        scratch_shapes=[pltpu.SemaphoreType.DMA(())],
    )
    return pl.pallas_call(
        _gather_kernel,
        grid_spec=grid_spec,
        out_shape=jax.ShapeDtypeStruct((n_rows, D_MODEL), x.dtype),
        compiler_params=_params("arbitrary"),
        name="moe_dispatch",
    )(tile_valid, row_token.reshape(n_rows // rows, 1, rows), x)


def _combine_kernel(p0_ref, p1_ref, y_hbm, x_ref, route_ref, g_ref, b_ref, o_ref, ya_sc, yb_sc, sem):
    rows = o_ref.shape[0]

    def copies(r):
        return (pltpu.make_async_copy(y_hbm.at[pl.ds(p0_ref[0, r], 1)], ya_sc.at[pl.ds(r, 1)], sem.at[0]),
                pltpu.make_async_copy(y_hbm.at[pl.ds(p1_ref[0, r], 1)], yb_sc.at[pl.ds(r, 1)], sem.at[1]))

    def start(r, _):
        for cp in copies(r):
            cp.start()
        return 0

    def wait(r, _):
        for cp in copies(r):
            cp.wait()
        return 0

    lax.fori_loop(0, rows, start, 0)
    lax.fori_loop(0, rows, wait, 0)
    route = route_ref[...]
    y = DEEPNORM_ALPHA * x_ref[...] + route[:, 0:1] * ya_sc[...] + route[:, 1:2] * yb_sc[...]
    o_ref[...] = _layer_norm(y, g_ref[...], b_ref[...])


def _moe_combine(pos0, pos1, ys, x, route, g_row, b_row):
    t = x.shape[0]
    rows = ROWS_COMBINE
    row_map = lambda i: (i, 0)
    const = lambda i: (0, 0)
    idx_spec = pl.BlockSpec((None, 1, rows), lambda i: (i, 0, 0), memory_space=pltpu.SMEM)
    return pl.pallas_call(
        _combine_kernel,
        grid=(t // rows,),
        in_specs=[idx_spec, idx_spec,
                  pl.BlockSpec(memory_space=pl.ANY),
                  pl.BlockSpec((rows, D_MODEL), row_map),
                  pl.BlockSpec((rows, LANES), row_map),
                  pl.BlockSpec((1, D_MODEL), const),
                  pl.BlockSpec((1, D_MODEL), const)],
        out_specs=pl.BlockSpec((rows, D_MODEL), row_map),
        out_shape=jax.ShapeDtypeStruct((t, D_MODEL), F32),
        scratch_shapes=[pltpu.VMEM((rows, D_MODEL), F32), pltpu.VMEM((rows, D_MODEL), F32),
                        pltpu.SemaphoreType.DMA((2,))],
        compiler_params=_params("arbitrary"),
        name="moe_combine",
    )(pos0.reshape(t // rows, 1, rows), pos1.reshape(t // rows, 1, rows), ys, x, route, g_row, b_row)


def _routing_tables(route, n_tokens):
    rows = ROWS_FFN
    n_tiles = (n_tokens * TOP_K) // rows + N_EXPERTS
    experts = route[:, 2:4].astype(jnp.int32).reshape(-1)
    onehot = (experts[:, None] == jnp.arange(N_EXPERTS, dtype=jnp.int32)[None, :]).astype(jnp.int32)
    running = jnp.cumsum(onehot, axis=0)
    counts = running[-1]
    rank = jnp.sum(running * onehot, axis=1) - 1
    tiles_per = (counts + rows - 1) // rows
    tile_end = jnp.cumsum(tiles_per)
    group_start = (tile_end - tiles_per) * rows
    pos = group_start[experts] + rank
    token = jnp.arange(n_tokens * TOP_K, dtype=jnp.int32) // TOP_K
    row_token = jnp.zeros((n_tiles * rows,), jnp.int32).at[pos].set(token)
    tile_id = jnp.arange(n_tiles, dtype=jnp.int32)
    tile_expert = jnp.minimum(jnp.sum((tile_id[:, None] >= tile_end[None, :]).astype(jnp.int32), axis=1),
                              N_EXPERTS - 1)
    tile_valid = (tile_id < tile_end[-1]).astype(jnp.int32)
    last_expert = tile_expert[jnp.maximum(tile_end[-1] - 1, 0)]
    tile_expert = jnp.where(tile_valid > 0, tile_expert, last_expert)
    pos2 = pos.reshape(n_tokens, TOP_K)
    return row_token, tile_expert, tile_valid, pos2[:, 0], pos2[:, 1]


def _rope_tables(seq):
    half = HEAD_DIM // 2
    inv_freq = ROPE_THETA ** (-jnp.arange(half, dtype=F32) / half)
    ang = jnp.arange(seq).astype(F32)[:, None] * inv_freq[None, :]
    cos, sin = jnp.cos(ang), jnp.sin(ang)
    return jnp.concatenate([cos, cos], axis=1), jnp.concatenate([-sin, sin], axis=1)


def _pad_lanes(w):
    return jnp.pad(w, ((0, 0), (0, LANES - w.shape[1])))


def kernel(x, a_w_in, a_conv_w, a_log_decay, a_dt_bias, a_norm_w, a_w_out, b_w_kv, b_w_q, b_w_o,
           ffn_w1, ffn_w3, ffn_w2, moe_router, moe_w1, moe_w3, moe_w2, ln_g, ln_b):
    batch, seq, _ = x.shape
    t = batch * seq
    width = N_HEADS * HEAD_DIM
    x0 = x.reshape(t, D_MODEL)
    row = lambda v: v.reshape(1, -1).astype(F32)

    w_in = a_w_in[0]
    q, k, v, gate, gb = _in_proj(
        x0, w_in[:, :4 * width].astype(BF16), _pad_lanes(w_in[:, 4 * width:]).astype(BF16),
        a_conv_w[0], _pad_lanes(row(a_log_decay[0])), _pad_lanes(row(a_dt_bias[0])), batch, seq)
    og = _delta_rule(q, k, v, gate, gb, row(a_norm_w[0]), batch, seq)
    x1 = _proj_ln(og, a_w_out[0].astype(BF16), x0, row(ln_g[0, 0]), row(ln_b[0, 0]))
    x2 = _ffn_dense(x1, ffn_w1[0].astype(BF16), ffn_w3[0].astype(BF16), ffn_w2[0].astype(BF16),
                    row(ln_g[0, 1]), row(ln_b[0, 1]))

    cos_full, sin_signed = _rope_tables(seq)
    w_kvq = jnp.concatenate([b_w_kv, b_w_q[0]], axis=1).astype(BF16)
    qm, km, vt, sel = _qkv_proj(x2, w_kvq, cos_full, sin_signed, batch, seq)
    attn = _moba_attention(qm, km, vt, sel, batch, seq)
    x3, route = _proj_ln(attn, b_w_o[0].astype(BF16), x2, row(ln_g[1, 0]), row(ln_b[1, 0]),
                         w_router=_pad_lanes(moe_router[0]))
    row_token, tile_expert, tile_valid, pos0, pos1 = _routing_tables(route, t)
    xs = _gather_rows(row_token, tile_valid, x3)
    ys = _ffn_grouped(tile_expert, tile_valid, xs,
                      moe_w1[0].astype(BF16), moe_w3[0].astype(BF16), moe_w2[0].astype(BF16))
    x4 = _moe_combine(pos0, pos1, ys, x3, route, row(ln_g[1, 1]), row(ln_b[1, 1]))
    return x4.reshape(batch, seq, D_MODEL)
```

```python
import functools
import math

import jax
import jax.numpy as jnp
from jax import lax
from jax.experimental import pallas as pl
from jax.experimental.pallas import tpu as pltpu

D_MODEL = 1024
DEPTH = 2
DEEPNORM_ALPHA = (2.0 * DEPTH) ** 0.25
LN_EPS = 1e-5
N_HEADS = 8
HEAD_DIM = 128
DN_CONV = 4
DN_CHUNK = 64
DN_NORM_EPS = 1e-6
MOBA_BLOCK = 256
MOBA_TOPK = 3
ROPE_THETA = 10000.0
NEG_INF = -1e30
FFN_HIDDEN = 3584
N_EXPERTS = 8
TOP_K = 2

LANES = 128
SUBLANES = 8
VMEM_LIMIT_BYTES = 56 * 1024 * 1024

ROWS_PROJ = 256
ROWS_DELTA = 128
ROWS_LN = 512
ROWS_FFN = 512
FFN_CHUNK = 512
ROWS_COMBINE = 256

F32 = jnp.float32
BF16 = jnp.bfloat16

_NT = (((1,), (1,)), ((), ()))


def _dot(a, b):
    return jnp.dot(a, b, preferred_element_type=F32)


def _dot_nt(a, b):
    return lax.dot_general(a, b, _NT, preferred_element_type=F32)


def _params(*sem):
    return pltpu.CompilerParams(dimension_semantics=sem, vmem_limit_bytes=VMEM_LIMIT_BYTES)


def _layer_norm(y, g, b):
    mu = jnp.mean(y, axis=-1, keepdims=True)
    d = y - mu
    var = jnp.mean(d * d, axis=-1, keepdims=True)
    return d * lax.rsqrt(var + LN_EPS) * g + b


def _sigmoid(x):
    return 1.0 / (1.0 + jnp.exp(-x))


def _split3(x):
    hi = x.astype(BF16)
    r = x - hi.astype(F32)
    mid = r.astype(BF16)
    lo = (r - mid.astype(F32)).astype(BF16)
    return hi, mid, lo


def _in_proj_kernel(x_ref, w_ref, wab_ref, cw_ref, alog_ref, dtb_ref,
                    q_ref, k_ref, v_ref, gate_ref, gb_ref, conv_sc):
    rows = x_ref.shape[0]
    width = N_HEADS * HEAD_DIM
    pad = SUBLANES

    @pl.when(pl.program_id(1) == 0)
    def _():
        conv_sc[0:pad, :] = jnp.zeros((pad, 3 * width), F32)

    xb = x_ref[...].astype(BF16)
    chunk = 2 * HEAD_DIM
    for c in range(3 * width // chunk):
        cols = slice(c * chunk, (c + 1) * chunk)
        conv_sc[pad:pad + rows, cols] = _dot(xb, w_ref[:, cols])
        y = conv_sc[pad:pad + rows, cols] * cw_ref[3:4, cols]
        for j in range(DN_CONV - 1):
            y = y + conv_sc[pad - 3 + j:pad - 3 + j + rows, cols] * cw_ref[j:j + 1, cols]
        conv_sc[pad - 3:pad, cols] = conv_sc[pad + rows - 3:pad + rows, cols]
        y = y * _sigmoid(y)
        which = c * chunk // width
        for hh in range(chunk // HEAD_DIM):
            t = y[:, hh * HEAD_DIM:(hh + 1) * HEAD_DIM]
            ocols = slice(c * chunk - which * width + hh * HEAD_DIM,
                          c * chunk - which * width + (hh + 1) * HEAD_DIM)
            if which == 2:
                v_ref[:, ocols] = t.astype(BF16)
            else:
                t = t * lax.rsqrt(jnp.sum(t * t, axis=-1, keepdims=True) + DN_NORM_EPS)
                if which == 0:
                    q_ref[:, ocols] = (t * (HEAD_DIM ** -0.5)).astype(BF16)
                else:
                    k_ref[:, ocols] = t.astype(BF16)
    for c in range(width // chunk):
        cols = slice(3 * width + c * chunk, 3 * width + (c + 1) * chunk)
        gate_ref[:, c * chunk:(c + 1) * chunk] = _dot(xb, w_ref[:, cols]).astype(BF16)
    ab = _dot(xb, wab_ref[...])
    z = ab + dtb_ref[...]
    softplus = jnp.maximum(z, 0.0) + jnp.log(1.0 + jnp.exp(-jnp.abs(z)))
    g = -jnp.exp(alog_ref[...]) * softplus
    lane = lax.broadcasted_iota(jnp.int32, ab.shape, 1)
    gb_ref[...] = jnp.where(lane < N_HEADS, g, _sigmoid(ab))


def _in_proj(x2d, w_main, w_ab, conv_w, a_log_row, dt_row, batch, seq):
    t = x2d.shape[0]
    width = N_HEADS * HEAD_DIM
    rows = ROWS_PROJ
    nt = seq // rows
    row_map = lambda b, s: (b * nt + s, 0)
    const = lambda b, s: (0, 0)
    act = jax.ShapeDtypeStruct((t, width), BF16)
    return pl.pallas_call(
        _in_proj_kernel,
        grid=(batch, nt),
        in_specs=[pl.BlockSpec((rows, D_MODEL), row_map),
                  pl.BlockSpec((D_MODEL, 4 * width), const),
                  pl.BlockSpec((D_MODEL, LANES), const),
                  pl.BlockSpec((DN_CONV, 3 * width), const),
                  pl.BlockSpec((1, LANES), const),
                  pl.BlockSpec((1, LANES), const)],
        out_specs=[pl.BlockSpec((rows, width), row_map)] * 4 + [pl.BlockSpec((rows, LANES), row_map)],
        out_shape=[act, act, act, act, jax.ShapeDtypeStruct((t, LANES), F32)],
        scratch_shapes=[pltpu.VMEM((rows + SUBLANES, 3 * width), F32)],
        compiler_params=_params("arbitrary", "arbitrary"),
        name="dn_in_proj",
    )(x2d, w_main, w_ab, conv_w, a_log_row, dt_row)


def _delta_kernel(q_ref, k_ref, v_ref, gate_ref, gb_ref, nw_ref, o_ref, state_sc):
    c = DN_CHUNK
    two = 2 * c
    n_sub = q_ref.shape[0] // c
    n_pairs = N_HEADS // 2

    @pl.when(pl.program_id(1) == 0)
    def _():
        state_sc[...] = jnp.zeros(state_sc.shape, F32)

    r = lax.broadcasted_iota(jnp.int32, (two, two), 0)
    cc = lax.broadcasted_iota(jnp.int32, (two, two), 1)
    same = (r < c) == (cc < c)
    tri = same & (r >= cc)
    stri = same & (r > cc)
    eye = (r == cc).astype(F32)
    top_rows = lax.broadcasted_iota(jnp.int32, (two, 1), 0) < c
    top_lanes = lax.broadcasted_iota(jnp.int32, (1, two), 1) < c
    row64 = lax.broadcasted_iota(jnp.int32, (c, LANES), 0)

    def stack(ref, rows, h):
        t = ref[rows, h * HEAD_DIM:(h + 2) * HEAD_DIM].astype(F32)
        return jnp.concatenate([t[:, :HEAD_DIM], t[:, HEAD_DIM:]], axis=0)

    items = []
    for s in range(n_sub):
        rows = slice(s * c, (s + 1) * c)
        gb = gb_ref[rows, :]
        gc = gb
        sh = 1
        while sh < c:
            gc = gc + jnp.where(row64 >= sh, pltpu.roll(gc, sh, axis=0), 0.0)
            sh *= 2
        gc_pair = jnp.concatenate([gc, pltpu.roll(gc, LANES - 1, axis=1)], axis=0)
        gb_pair = jnp.concatenate([gb, pltpu.roll(gb, LANES - 1, axis=1)], axis=0)
        gc_pair_t = gc_pair.T
        for p in range(n_pairs):
            h = 2 * p
            q2, k2, v2 = stack(q_ref, rows, h), stack(k_ref, rows, h), stack(v_ref, rows, h)
            gcol = gc_pair[:, h:h + 1]
            grow = gc_pair_t[h:h + 1, :]
            beta = gb_pair[:, N_HEADS + h:N_HEADS + h + 1]
            glast = jnp.where(top_rows, gc_pair[c - 1:c, h:h + 1], gc_pair[two - 1:two, h:h + 1])
            eg = jnp.exp(gcol)
            kb = k2 * beta
            items.append(dict(
                rows=rows, h=h, k2b=k2.astype(BF16), q2b=q2.astype(BF16), kbb=kb.astype(BF16),
                decay=jnp.exp(jnp.where(tri, gcol - grow, NEG_INF)),
                rhs=jnp.concatenate([v2 * beta, kb * eg], axis=1).astype(BF16),
                qd=(q2 * eg).astype(BF16),
                kd_t=(k2 * jnp.exp(glast - gcol)).T,
                eglast=jnp.exp(glast)))
    for it in items:
        it["pw"] = jnp.where(stri, _dot_nt(it["kbb"], it["k2b"]) * it["decay"], 0.0)
        it["inv"] = eye - it["pw"]
    for _ in range(int(math.log2(c)) - 1):
        for it in items:
            pwb = it["pw"].astype(BF16)
            it["pw"] = _dot(pwb, pwb)
        for it in items:
            it["inv"] = it["inv"] + _dot(it["inv"].astype(BF16), it["pw"].astype(BF16))
    for it in items:
        sol = _dot(it["inv"].astype(BF16), it["rhs"])
        it["u"] = sol[:, :HEAD_DIM]
        it["wb"] = sol[:, HEAD_DIM:].astype(BF16)
    for it in items:
        it["amat"] = jnp.where(tri, _dot_nt(it["q2b"], it["k2b"]) * it["decay"], 0.0).astype(BF16)

    for s in range(n_sub):
        group = items[s * n_pairs:(s + 1) * n_pairs]
        for it in group:
            h = it["h"]
            it["s0"] = state_sc[h]
            it["s1"] = state_sc[h + 1]
            it["r0"] = _dot(jnp.concatenate([it["wb"][:c], it["qd"][:c]], axis=0), it["s0"].astype(BF16))
            it["r1"] = _dot(jnp.concatenate([it["wb"][c:], it["qd"][c:]], axis=0), it["s1"].astype(BF16))
        for it in group:
            v_new = it["u"] - jnp.concatenate([it["r0"][:c], it["r1"][:c]], axis=0)
            it["vnb"] = v_new.astype(BF16)
            it["o"] = jnp.concatenate([it["r0"][c:], it["r1"][c:]], axis=0) + _dot(it["amat"], it["vnb"])
        for it in group:
            h = it["h"]
            kd_t = it["kd_t"]
            state_sc[h] = it["s0"] * it["eglast"][0:1, :] + _dot(
                jnp.where(top_lanes, kd_t, 0.0).astype(BF16), it["vnb"])
            state_sc[h + 1] = it["s1"] * it["eglast"][c:c + 1, :] + _dot(
                jnp.where(top_lanes, 0.0, kd_t).astype(BF16), it["vnb"])
        for it in group:
            h, rows, o = it["h"], it["rows"], it["o"]
            gate2 = stack(gate_ref, rows, h)
            o = o * lax.rsqrt(jnp.mean(o * o, axis=-1, keepdims=True) + DN_NORM_EPS) * nw_ref[...]
            o = o * (gate2 * _sigmoid(gate2))
            o_ref[rows, h * HEAD_DIM:(h + 1) * HEAD_DIM] = o[:c].astype(BF16)
            o_ref[rows, (h + 1) * HEAD_DIM:(h + 2) * HEAD_DIM] = o[c:].astype(BF16)


def _delta_rule(q, k, v, gate, gb, norm_w_row, batch, seq):
    t = q.shape[0]
    width = N_HEADS * HEAD_DIM
    rows = ROWS_DELTA
    ns = seq // rows
    row_map = lambda b, s: (b * ns + s, 0)
    const = lambda b, s: (0, 0)
    wide = pl.BlockSpec((rows, width), row_map)
    return pl.pallas_call(
        _delta_kernel,
        grid=(batch, ns),
        in_specs=[wide, wide, wide, wide,
                  pl.BlockSpec((rows, LANES), row_map),
                  pl.BlockSpec((1, HEAD_DIM), const)],
        out_specs=wide,
        out_shape=jax.ShapeDtypeStruct((t, width), BF16),
        scratch_shapes=[pltpu.VMEM((N_HEADS, HEAD_DIM, HEAD_DIM), F32)],
        compiler_params=_params("arbitrary", "arbitrary"),
        name="dn_delta_rule",
    )(q, k, v, gate, gb, norm_w_row)


def _proj_ln_kernel(a_ref, w_ref, res_ref, g_ref, b_ref, o_ref):
    y = DEEPNORM_ALPHA * res_ref[...] + _dot(a_ref[...], w_ref[...])
    o_ref[...] = _layer_norm(y, g_ref[...], b_ref[...])


def _proj_ln_router_kernel(a_ref, w_ref, res_ref, g_ref, b_ref, wr_ref, o_ref, route_ref):
    y = DEEPNORM_ALPHA * res_ref[...] + _dot(a_ref[...], w_ref[...])
    xn = _layer_norm(y, g_ref[...], b_ref[...])
    o_ref[...] = xn
    xs = _split3(xn)
    ws = _split3(wr_ref[...])
    logits = jnp.zeros((xn.shape[0], LANES), F32)
    for i in range(3):
        for j in range(3 - i):
            logits = logits + _dot(xs[i], ws[j])
    lane = lax.broadcasted_iota(jnp.int32, logits.shape, 1)
    logits = jnp.where(lane < N_EXPERTS, logits, NEG_INF)
    l1 = jnp.max(logits, axis=-1, keepdims=True)
    i1 = jnp.min(jnp.where(logits == l1, lane, LANES), axis=-1, keepdims=True)
    rest = jnp.where(lane == i1, NEG_INF, logits)
    l2 = jnp.max(rest, axis=-1, keepdims=True)
    i2 = jnp.min(jnp.where(rest == l2, lane, LANES), axis=-1, keepdims=True)
    e2 = jnp.exp(l2 - l1)
    g1 = 1.0 / (1.0 + e2)
    g2 = e2 / (1.0 + e2)
    route_ref[...] = jnp.where(lane == 0, g1,
                     jnp.where(lane == 1, g2,
                     jnp.where(lane == 2, i1.astype(F32),
                     jnp.where(lane == 3, i2.astype(F32), 0.0))))


def _proj_ln(a, w, res, g_row, b_row, w_router=None):
    t, kdim = a.shape
    rows = ROWS_LN
    row_map = lambda i: (i, 0)
    const = lambda i: (0, 0)
    in_specs = [pl.BlockSpec((rows, kdim), row_map),
                pl.BlockSpec((kdim, D_MODEL), const),
                pl.BlockSpec((rows, D_MODEL), row_map),
                pl.BlockSpec((1, D_MODEL), const),
                pl.BlockSpec((1, D_MODEL), const)]
    out_full = jax.ShapeDtypeStruct((t, D_MODEL), F32)
    if w_router is None:
        return pl.pallas_call(
            _proj_ln_kernel, grid=(t // rows,), in_specs=in_specs,
            out_specs=pl.BlockSpec((rows, D_MODEL), row_map), out_shape=out_full,
            compiler_params=_params("arbitrary"), name="proj_ln",
        )(a, w, res, g_row, b_row)
    return pl.pallas_call(
        _proj_ln_router_kernel, grid=(t // rows,),
        in_specs=in_specs + [pl.BlockSpec((D_MODEL, LANES), const)],
        out_specs=[pl.BlockSpec((rows, D_MODEL), row_map), pl.BlockSpec((rows, LANES), row_map)],
        out_shape=[out_full, jax.ShapeDtypeStruct((t, LANES), F32)],
        compiler_params=_params("arbitrary"), name="proj_ln_router",
    )(a, w, res, g_row, b_row, w_router)


def _swiglu_acc(x_ref, w1_ref, w3_ref, w2_ref, acc_sc):
    f = pl.program_id(1)
    xb = x_ref[...].astype(BF16)
    h1 = _dot(xb, w1_ref[...])
    h3 = _dot(xb, w3_ref[...])
    hid = (h1 * _sigmoid(h1) * h3).astype(BF16)
    part = _dot(hid, w2_ref[...])

    @pl.when(f == 0)
    def _():
        acc_sc[...] = part

    @pl.when(f > 0)
    def _():
        acc_sc[...] += part


def _ffn_dense_kernel(x_ref, w1_ref, w3_ref, w2_ref, g_ref, b_ref, o_ref, acc_sc):
    _swiglu_acc(x_ref, w1_ref, w3_ref, w2_ref, acc_sc)

    @pl.when(pl.program_id(1) == pl.num_programs(1) - 1)
    def _():
        y = DEEPNORM_ALPHA * x_ref[...] + acc_sc[...]
        o_ref[...] = _layer_norm(y, g_ref[...], b_ref[...])


def _ffn_dense(x, w1, w3, w2, g_row, b_row):
    t = x.shape[0]
    rows, fc = ROWS_FFN, FFN_CHUNK
    nf = FFN_HIDDEN // fc
    return pl.pallas_call(
        _ffn_dense_kernel,
        grid=(t // rows, nf),
        in_specs=[pl.BlockSpec((rows, D_MODEL), lambda i, f: (i, 0)),
                  pl.BlockSpec((D_MODEL, fc), lambda i, f: (0, f)),
                  pl.BlockSpec((D_MODEL, fc), lambda i, f: (0, f)),
                  pl.BlockSpec((fc, D_MODEL), lambda i, f: (f, 0)),
                  pl.BlockSpec((1, D_MODEL), lambda i, f: (0, 0)),
                  pl.BlockSpec((1, D_MODEL), lambda i, f: (0, 0))],
        out_specs=pl.BlockSpec((rows, D_MODEL), lambda i, f: (i, 0)),
        out_shape=jax.ShapeDtypeStruct((t, D_MODEL), F32),
        scratch_shapes=[pltpu.VMEM((rows, D_MODEL), F32)],
        compiler_params=_params("arbitrary", "arbitrary"),
        name="ffn_dense",
    )(x, w1, w3, w2, g_row, b_row)


def _ffn_grouped_kernel(te_ref, tv_ref, x_ref, w1_ref, w3_ref, w2_ref, o_ref, acc_sc):
    i = pl.program_id(0)
    last = pl.program_id(1) == pl.num_programs(1) - 1

    @pl.when(tv_ref[i] > 0)
    def _():
        _swiglu_acc(x_ref, w1_ref, w3_ref, w2_ref, acc_sc)

        @pl.when(last)
        def _():
            o_ref[...] = acc_sc[...]

    @pl.when((tv_ref[i] == 0) & last)
    def _():
        o_ref[...] = jnp.zeros(o_ref.shape, F32)


def _ffn_grouped(tile_expert, tile_valid, xs, w1, w3, w2):
    n_rows = xs.shape[0]
    rows, fc = ROWS_FFN, FFN_CHUNK
    nf = FFN_HIDDEN // fc
    grid_spec = pltpu.PrefetchScalarGridSpec(
        num_scalar_prefetch=2,
        grid=(n_rows // rows, nf),
        in_specs=[pl.BlockSpec((rows, D_MODEL), lambda i, f, te, tv: (i, 0)),
                  pl.BlockSpec((None, D_MODEL, fc), lambda i, f, te, tv: (te[i], 0, f)),
                  pl.BlockSpec((None, D_MODEL, fc), lambda i, f, te, tv: (te[i], 0, f)),
                  pl.BlockSpec((None, fc, D_MODEL), lambda i, f, te, tv: (te[i], f, 0))],
        out_specs=pl.BlockSpec((rows, D_MODEL), lambda i, f, te, tv: (i, 0)),
        scratch_shapes=[pltpu.VMEM((rows, D_MODEL), F32)],
    )
    return pl.pallas_call(
        _ffn_grouped_kernel,
        grid_spec=grid_spec,
        out_shape=jax.ShapeDtypeStruct((n_rows, D_MODEL), F32),
        compiler_params=_params("arbitrary", "arbitrary"),
        name="ffn_grouped",
    )(tile_expert, tile_valid, xs, w1, w3, w2)


def _qkv_kernel(x_ref, w_ref, cos_ref, sin_ref, q_ref, k_ref, vt_ref, sel_ref, km_sc, qf_sc):
    i = pl.program_id(1)
    width = N_HEADS * HEAD_DIM
    half = HEAD_DIM // 2
    xb = x_ref[...].astype(BF16)
    cosf = cos_ref[...]
    sinf = sin_ref[...]

    @pl.when(i == 0)
    def _():
        km_sc[...] = jnp.zeros(km_sc.shape, F32)

    def rope(t):
        return t * cosf + pltpu.roll(t, half, axis=1) * sinf

    chunk = 2 * HEAD_DIM
    for c in range(width // chunk):
        kk = _dot(xb, w_ref[:, c * chunk:(c + 1) * chunk])
        vv = _dot(xb, w_ref[:, width + c * chunk:width + (c + 1) * chunk])
        qq = _dot(xb, w_ref[:, 2 * width + c * chunk:2 * width + (c + 1) * chunk])
        for hh in range(2):
            h = 2 * c + hh
            cols = slice(h * HEAD_DIM, (h + 1) * HEAD_DIM)
            part = slice(hh * HEAD_DIM, (hh + 1) * HEAD_DIM)
            kr = rope(kk[:, part])
            k_ref[:, cols] = kr.astype(BF16)
            km_row = lax.broadcasted_iota(jnp.int32, (km_sc.shape[0], HEAD_DIM), 0)
            km_sc[:, cols] = jnp.where(km_row == i, jnp.mean(kr, axis=0, keepdims=True), km_sc[:, cols])
            qr = rope(qq[:, part])
            qf_sc[:, cols] = qr
            q_ref[:, cols] = (qr * (HEAD_DIM ** -0.5)).astype(BF16)
            vt_ref[h] = vv[:, part].T.astype(BF16)

    nb = km_sc.shape[0]
    km = km_sc[...]
    lane_head = lax.broadcasted_iota(jnp.int32, km.shape, 1) // HEAD_DIM
    km_rows = jnp.concatenate([jnp.where(lane_head == h, km, 0.0) for h in range(N_HEADS)], axis=0)
    ks = _split3(km_rows)
    qs = _split3(qf_sc[...])
    gate_t = jnp.zeros((N_HEADS * nb, x_ref.shape[0]), F32)
    for a in range(3):
        for b in range(3 - a):
            gate_t = gate_t + _dot_nt(ks[a], qs[b])
    blk = lax.broadcasted_iota(jnp.int32, (nb, x_ref.shape[0]), 0)
    past = blk < i
    for h in range(N_HEADS):
        g = jnp.where(past, gate_t[h * nb:(h + 1) * nb, :], NEG_INF)
        rank = jnp.zeros(g.shape, jnp.int32)
        for n in range(nb):
            gn = g[n:n + 1, :]
            ahead = (gn > g) | ((gn == g) & (blk > n))
            rank = rank + ahead.astype(jnp.int32)
        sel_ref[h * nb:(h + 1) * nb, :] = jnp.where(past & (rank < MOBA_TOPK), 0.0, NEG_INF)


def _qkv_proj(x, w_kvq, cos_full, sin_signed, batch, seq):
    t = x.shape[0]
    width = N_HEADS * HEAD_DIM
    rows = MOBA_BLOCK
    nb = seq // rows
    row_map = lambda b, i: (b * nb + i, 0)
    act = jax.ShapeDtypeStruct((t, width), BF16)
    return pl.pallas_call(
        _qkv_kernel,
        grid=(batch, nb),
        in_specs=[pl.BlockSpec((rows, D_MODEL), row_map),
                  pl.BlockSpec((D_MODEL, 3 * width), lambda b, i: (0, 0)),
                  pl.BlockSpec((rows, HEAD_DIM), lambda b, i: (i, 0)),
                  pl.BlockSpec((rows, HEAD_DIM), lambda b, i: (i, 0))],
        out_specs=[pl.BlockSpec((rows, width), row_map),
                   pl.BlockSpec((rows, width), row_map),
                   pl.BlockSpec((None, N_HEADS, HEAD_DIM, rows), lambda b, i: (b, 0, 0, i)),
                   pl.BlockSpec((None, N_HEADS * nb, rows), lambda b, i: (b * nb + i, 0, 0))],
        out_shape=[act, act,
                   jax.ShapeDtypeStruct((batch, N_HEADS, HEAD_DIM, seq), BF16),
                   jax.ShapeDtypeStruct((batch * nb, N_HEADS * nb, rows), F32)],
        scratch_shapes=[pltpu.VMEM((nb, width), F32), pltpu.VMEM((rows, width), F32)],
        compiler_params=_params("arbitrary", "arbitrary"),
        name="moba_qkv_proj",
    )(x, w_kvq, cos_full, sin_signed)


def _attn_kernel(q_ref, k_ref, vt_ref, sel_ref, o_ref):
    blk = MOBA_BLOCK
    nb = q_ref.shape[0] // blk
    key = lax.broadcasted_iota(jnp.int32, (blk, blk), 0)
    qry = lax.broadcasted_iota(jnp.int32, (blk, blk), 1)
    causal = key <= qry

    def scores(i):
        return _dot_nt(k_ref[0:(i + 1) * blk, :], q_ref[i * blk:(i + 1) * blk, :])

    def finish(i, s):
        parts = [s[j * blk:(j + 1) * blk] + sel_ref[i, j:j + 1, :] for j in range(i)]
        parts.append(jnp.where(causal, s[i * blk:(i + 1) * blk], NEG_INF))
        m = jnp.max(parts[0], axis=0, keepdims=True)
        for part in parts[1:]:
            m = jnp.maximum(m, jnp.max(part, axis=0, keepdims=True))
        probs = [jnp.exp(part - m) for part in parts]
        l = jnp.sum(probs[0], axis=0, keepdims=True)
        for pr in probs[1:]:
            l = l + jnp.sum(pr, axis=0, keepdims=True)
        p = jnp.concatenate([pr.astype(BF16) for pr in probs], axis=0)
        acc = _dot(vt_ref[:, 0:(i + 1) * blk], p)
        o_ref[i * blk:(i + 1) * blk, :] = (acc / l).T.astype(BF16)

    s_cur = scores(0)
    for i in range(nb):
        s_next = scores(i + 1) if i + 1 < nb else None
        finish(i, s_cur)
        s_cur = s_next


def _moba_attention(q, k, vt, sel, batch, seq):
    t = q.shape[0]
    width = N_HEADS * HEAD_DIM
    nb = seq // MOBA_BLOCK
    col = pl.BlockSpec((seq, HEAD_DIM), lambda b, h: (b, h))
    return pl.pallas_call(
        _attn_kernel,
        grid=(batch, N_HEADS),
        in_specs=[col, col,
                  pl.BlockSpec((None, None, HEAD_DIM, seq), lambda b, h: (b, h, 0, 0)),
                  pl.BlockSpec((nb, nb, MOBA_BLOCK), lambda b, h: (b, h, 0))],
        out_specs=col,
        out_shape=jax.ShapeDtypeStruct((t, width), BF16),
        compiler_params=_params("arbitrary", "arbitrary"),
        name="moba_attention",
    )(q, k, vt, sel)


def _gather_kernel(tv_ref, idx_ref, x_hbm, o_ref, sem):
    i = pl.program_id(0)
    rows = o_ref.shape[0]

    def row_copy(r):
        return pltpu.make_async_copy(x_hbm.at[pl.ds(idx_ref[0, r], 1)],
                                     o_ref.at[pl.ds(r, 1)], sem)

    @pl.when(tv_ref[i] > 0)
    def _():
        def start(r, _):
            row_copy(r).start()
            return 0

        def wait(r, _):
            row_copy(r).wait()
            return 0

        lax.fori_loop(0, rows, start, 0)
        lax.fori_loop(0, rows, wait, 0)

    @pl.when(tv_ref[i] == 0)
    def _():
        o_ref[...] = jnp.zeros(o_ref.shape, o_ref.dtype)


def _gather_rows(row_token, tile_valid, x):
    n_rows = row_token.shape[0]
    rows = ROWS_FFN
    grid_spec = pltpu.PrefetchScalarGridSpec(
        num_scalar_prefetch=1,
        grid=(n_rows // rows,),
        in_specs=[pl.BlockSpec((None, 1, rows), lambda i, tv: (i, 0, 0), memory_space=pltpu.SMEM),
                  pl.BlockSpec(memory_space=pl.ANY)],
        out_specs=pl.BlockSpec((rows, D_MODEL), lambda i, tv: (i, 0)),
        scratch_shapes=[pltpu.SemaphoreType.DMA(())],
    )
    return pl.pallas_call(
        _gather_kernel,
        grid_spec=grid_spec,
        out_shape=jax.ShapeDtypeStruct((n_rows, D_MODEL), x.dtype),
        compiler_params=_params("arbitrary"),
        name="moe_dispatch",
    )(tile_valid, row_token.reshape(n_rows // rows, 1, rows), x)


def _combine_kernel(p0_ref, p1_ref, y_hbm, x_ref, route_ref, g_ref, b_ref, o_ref, ya_sc, yb_sc, sem):
    rows = o_ref.shape[0]

    def copies(r):
        return (pltpu.make_async_copy(y_hbm.at[pl.ds(p0_ref[0, r], 1)], ya_sc.at[pl.ds(r, 1)], sem.at[0]),
                pltpu.make_async_copy(y_hbm.at[pl.ds(p1_ref[0, r], 1)], yb_sc.at[pl.ds(r, 1)], sem.at[1]))

    def start(r, _):
        for cp in copies(r):
            cp.start()
        return 0

    def wait(r, _):
        for cp in copies(r):
            cp.wait()
        return 0

    lax.fori_loop(0, rows, start, 0)
    lax.fori_loop(0, rows, wait, 0)
    route = route_ref[...]
    y = DEEPNORM_ALPHA * x_ref[...] + route[:, 0:1] * ya_sc[...] + route[:, 1:2] * yb_sc[...]
    o_ref[...] = _layer_norm(y, g_ref[...], b_ref[...])


def _moe_combine(pos0, pos1, ys, x, route, g_row, b_row):
    t = x.shape[0]
    rows = ROWS_COMBINE
    row_map = lambda i: (i, 0)
    const = lambda i: (0, 0)
    idx_spec = pl.BlockSpec((None, 1, rows), lambda i: (i, 0, 0), memory_space=pltpu.SMEM)
    return pl.pallas_call(
        _combine_kernel,
        grid=(t // rows,),
        in_specs=[idx_spec, idx_spec,
                  pl.BlockSpec(memory_space=pl.ANY),
                  pl.BlockSpec((rows, D_MODEL), row_map),
                  pl.BlockSpec((rows, LANES), row_map),
                  pl.BlockSpec((1, D_MODEL), const),
                  pl.BlockSpec((1, D_MODEL), const)],
        out_specs=pl.BlockSpec((rows, D_MODEL), row_map),
        out_shape=jax.ShapeDtypeStruct((t, D_MODEL), F32),
        scratch_shapes=[pltpu.VMEM((rows, D_MODEL), F32), pltpu.VMEM((rows, D_MODEL), F32),
                        pltpu.SemaphoreType.DMA((2,))],
        compiler_params=_params("arbitrary"),
        name="moe_combine",
    )(pos0.reshape(t // rows, 1, rows), pos1.reshape(t // rows, 1, rows), ys, x, route, g_row, b_row)


def _routing_tables(route, n_tokens):
    rows = ROWS_FFN
    n_tiles = (n_tokens * TOP_K) // rows + N_EXPERTS
    experts = route[:, 2:4].astype(jnp.int32).reshape(-1)
    onehot = (experts[:, None] == jnp.arange(N_EXPERTS, dtype=jnp.int32)[None, :]).astype(jnp.int32)
    running = jnp.cumsum(onehot, axis=0)
    counts = running[-1]
    rank = jnp.sum(running * onehot, axis=1) - 1
    tiles_per = (counts + rows - 1) // rows
    tile_end = jnp.cumsum(tiles_per)
    group_start = (tile_end - tiles_per) * rows
    pos = group_start[experts] + rank
    token = jnp.arange(n_tokens * TOP_K, dtype=jnp.int32) // TOP_K
    row_token = jnp.zeros((n_tiles * rows,), jnp.int32).at[pos].set(token)
    tile_id = jnp.arange(n_tiles, dtype=jnp.int32)
    tile_expert = jnp.minimum(jnp.sum((tile_id[:, None] >= tile_end[None, :]).astype(jnp.int32), axis=1),
                              N_EXPERTS - 1)
    tile_valid = (tile_id < tile_end[-1]).astype(jnp.int32)
    last_expert = tile_expert[jnp.maximum(tile_end[-1] - 1, 0)]
    tile_expert = jnp.where(tile_valid > 0, tile_expert, last_expert)
    pos2 = pos.reshape(n_tokens, TOP_K)
    return row_token, tile_expert, tile_valid, pos2[:, 0], pos2[:, 1]


def _rope_tables(seq):
    half = HEAD_DIM // 2
    inv_freq = ROPE_THETA ** (-jnp.arange(half, dtype=F32) / half)
    ang = jnp.arange(seq).astype(F32)[:, None] * inv_freq[None, :]
    cos, sin = jnp.cos(ang), jnp.sin(ang)
    return jnp.concatenate([cos, cos], axis=1), jnp.concatenate([-sin, sin], axis=1)


def _pad_lanes(w):
    return jnp.pad(w, ((0, 0), (0, LANES - w.shape[1])))


def kernel(x, a_w_in, a_conv_w, a_log_decay, a_dt_bias, a_norm_w, a_w_out, b_w_kv, b_w_q, b_w_o,
           ffn_w1, ffn_w3, ffn_w2, moe_router, moe_w1, moe_w3, moe_w2, ln_g, ln_b):
    batch, seq, _ = x.shape
    t = batch * seq
    width = N_HEADS * HEAD_DIM
    x0 = x.reshape(t, D_MODEL)
    row = lambda v: v.reshape(1, -1).astype(F32)

    w_in = a_w_in[0]
    q, k, v, gate, gb = _in_proj(
        x0, w_in[:, :4 * width].astype(BF16), _pad_lanes(w_in[:, 4 * width:]).astype(BF16),
        a_conv_w[0], _pad_lanes(row(a_log_decay[0])), _pad_lanes(row(a_dt_bias[0])), batch, seq)
    og = _delta_rule(q, k, v, gate, gb, row(a_norm_w[0]), batch, seq)
    x1 = _proj_ln(og, a_w_out[0].astype(BF16), x0, row(ln_g[0, 0]), row(ln_b[0, 0]))
    x2 = _ffn_dense(x1, ffn_w1[0].astype(BF16), ffn_w3[0].astype(BF16), ffn_w2[0].astype(BF16),
                    row(ln_g[0, 1]), row(ln_b[0, 1]))

    cos_full, sin_signed = _rope_tables(seq)
    w_kvq = jnp.concatenate([b_w_kv, b_w_q[0]], axis=1).astype(BF16)
    qm, km, vt, sel = _qkv_proj(x2, w_kvq, cos_full, sin_signed, batch, seq)
    attn = _moba_attention(qm, km, vt, sel, batch, seq)
    x3, route = _proj_ln(attn, b_w_o[0].astype(BF16), x2, row(ln_g[1, 0]), row(ln_b[1, 0]),
                         w_router=_pad_lanes(moe_router[0]))
    row_token, tile_expert, tile_valid, pos0, pos1 = _routing_tables(route, t)
    xs = _gather_rows(row_token, tile_valid, x3)
    ys = _ffn_grouped(tile_expert, tile_valid, xs,
                      moe_w1[0].astype(BF16), moe_w3[0].astype(BF16), moe_w2[0].astype(BF16))
    x4 = _moe_combine(pos0, pos1, ys, x3, route, row(ln_g[1, 1]), row(ln_b[1, 1]))
    return x4.reshape(batch, seq, D_MODEL)
```

```python
import functools
import math

import jax
import jax.numpy as jnp
from jax import lax
from jax.experimental import pallas as pl
from jax.experimental.pallas import tpu as pltpu

D_MODEL = 1024
DEPTH = 2
DEEPNORM_ALPHA = (2.0 * DEPTH) ** 0.25
LN_EPS = 1e-5
N_HEADS = 8
HEAD_DIM = 128
DN_CONV = 4
DN_CHUNK = 64
DN_NORM_EPS = 1e-6
MOBA_BLOCK = 256
MOBA_TOPK = 3
ROPE_THETA = 10000.0
NEG_INF = -1e30
FFN_HIDDEN = 3584
N_EXPERTS = 8
TOP_K = 2

LANES = 128
SUBLANES = 8
VMEM_LIMIT_BYTES = 56 * 1024 * 1024

ROWS_PROJ = 256
ROWS_DELTA = 128
ROWS_LN = 512
ROWS_FFN = 512
FFN_CHUNK = 1792
ROWS_COMBINE = 512

F32 = jnp.float32
BF16 = jnp.bfloat16

_NT = (((1,), (1,)), ((), ()))


def _dot(a, b):
    return jnp.dot(a, b, preferred_element_type=F32)


def _dot_nt(a, b):
    return lax.dot_general(a, b, _NT, preferred_element_type=F32)


def _params(*sem):
    return pltpu.CompilerParams(dimension_semantics=sem, vmem_limit_bytes=VMEM_LIMIT_BYTES)


def _layer_norm(y, g, b):
    mu = jnp.mean(y, axis=-1, keepdims=True)
    d = y - mu
    var = jnp.mean(d * d, axis=-1, keepdims=True)
    return d * lax.rsqrt(var + LN_EPS) * g + b


def _sigmoid(x):
    return 1.0 / (1.0 + jnp.exp(-x))


def _split3(x):
    hi = x.astype(BF16)
    r = x - hi.astype(F32)
    mid = r.astype(BF16)
    lo = (r - mid.astype(F32)).astype(BF16)
    return hi, mid, lo


def _in_proj_kernel(x_ref, w_ref, wab_ref, cw_ref, alog_ref, dtb_ref,
                    q_ref, k_ref, v_ref, gate_ref, gb_ref, conv_sc):
    rows = x_ref.shape[0]
    width = N_HEADS * HEAD_DIM
    pad = SUBLANES

    @pl.when(pl.program_id(1) == 0)
    def _():
        conv_sc[0:pad, :] = jnp.zeros((pad, 3 * width), F32)

    xb = x_ref[...].astype(BF16)
    chunk = 2 * HEAD_DIM
    for c in range(3 * width // chunk):
        cols = slice(c * chunk, (c + 1) * chunk)
        conv_sc[pad:pad + rows, cols] = _dot(xb, w_ref[:, cols])
        y = conv_sc[pad:pad + rows, cols] * cw_ref[3:4, cols]
        for j in range(DN_CONV - 1):
            y = y + conv_sc[pad - 3 + j:pad - 3 + j + rows, cols] * cw_ref[j:j + 1, cols]
        conv_sc[pad - 3:pad, cols] = conv_sc[pad + rows - 3:pad + rows, cols]
        y = y * _sigmoid(y)
        which = c * chunk // width
        for hh in range(chunk // HEAD_DIM):
            t = y[:, hh * HEAD_DIM:(hh + 1) * HEAD_DIM]
            ocols = slice(c * chunk - which * width + hh * HEAD_DIM,
                          c * chunk - which * width + (hh + 1) * HEAD_DIM)
            if which == 2:
                v_ref[:, ocols] = t.astype(BF16)
            else:
                t = t * lax.rsqrt(jnp.sum(t * t, axis=-1, keepdims=True) + DN_NORM_EPS)
                if which == 0:
                    q_ref[:, ocols] = (t * (HEAD_DIM ** -0.5)).astype(BF16)
                else:
                    k_ref[:, ocols] = t.astype(BF16)
    for c in range(width // chunk):
        cols = slice(3 * width + c * chunk, 3 * width + (c + 1) * chunk)
        gate_ref[:, c * chunk:(c + 1) * chunk] = _dot(xb, w_ref[:, cols]).astype(BF16)
    ab = _dot(xb, wab_ref[...])
    z = ab + dtb_ref[...]
    softplus = jnp.maximum(z, 0.0) + jnp.log(1.0 + jnp.exp(-jnp.abs(z)))
    g = -jnp.exp(alog_ref[...]) * softplus
    lane = lax.broadcasted_iota(jnp.int32, ab.shape, 1)
    gb_ref[...] = jnp.where(lane < N_HEADS, g, _sigmoid(ab))


def _in_proj(x2d, w_main, w_ab, conv_w, a_log_row, dt_row, batch, seq):
    t = x2d.shape[0]
    width = N_HEADS * HEAD_DIM
    rows = ROWS_PROJ
    nt = seq // rows
    row_map = lambda b, s: (b * nt + s, 0)
    const = lambda b, s: (0, 0)
    act = jax.ShapeDtypeStruct((t, width), BF16)
    return pl.pallas_call(
        _in_proj_kernel,
        grid=(batch, nt),
        in_specs=[pl.BlockSpec((rows, D_MODEL), row_map),
                  pl.BlockSpec((D_MODEL, 4 * width), const),
                  pl.BlockSpec((D_MODEL, LANES), const),
                  pl.BlockSpec((DN_CONV, 3 * width), const),
                  pl.BlockSpec((1, LANES), const),
                  pl.BlockSpec((1, LANES), const)],
        out_specs=[pl.BlockSpec((rows, width), row_map)] * 4 + [pl.BlockSpec((rows, LANES), row_map)],
        out_shape=[act, act, act, act, jax.ShapeDtypeStruct((t, LANES), F32)],
        scratch_shapes=[pltpu.VMEM((rows + SUBLANES, 3 * width), F32)],
        compiler_params=_params("arbitrary", "arbitrary"),
        name="dn_in_proj",
    )(x2d, w_main, w_ab, conv_w, a_log_row, dt_row)


def _delta_kernel(q_ref, k_ref, v_ref, gate_ref, gb_ref, nw_ref, o_ref, state_sc):
    c = DN_CHUNK
    two = 2 * c
    n_sub = q_ref.shape[0] // c
    n_pairs = N_HEADS // 2

    @pl.when(pl.program_id(1) == 0)
    def _():
        state_sc[...] = jnp.zeros(state_sc.shape, F32)

    r = lax.broadcasted_iota(jnp.int32, (two, two), 0)
    cc = lax.broadcasted_iota(jnp.int32, (two, two), 1)
    same = (r < c) == (cc < c)
    tri = same & (r >= cc)
    stri = same & (r > cc)
    eye = (r == cc).astype(F32)
    top_rows = lax.broadcasted_iota(jnp.int32, (two, 1), 0) < c
    top_lanes = lax.broadcasted_iota(jnp.int32, (1, two), 1) < c
    row64 = lax.broadcasted_iota(jnp.int32, (c, LANES), 0)

    def stack(ref, rows, h):
        t = ref[rows, h * HEAD_DIM:(h + 2) * HEAD_DIM].astype(F32)
        return jnp.concatenate([t[:, :HEAD_DIM], t[:, HEAD_DIM:]], axis=0)

    items = []
    for s in range(n_sub):
        rows = slice(s * c, (s + 1) * c)
        gb = gb_ref[rows, :]
        gc = gb
        sh = 1
        while sh < c:
            gc = gc + jnp.where(row64 >= sh, pltpu.roll(gc, sh, axis=0), 0.0)
            sh *= 2
        gc_pair = jnp.concatenate([gc, pltpu.roll(gc, LANES - 1, axis=1)], axis=0)
        gb_pair = jnp.concatenate([gb, pltpu.roll(gb, LANES - 1, axis=1)], axis=0)
        gc_pair_t = gc_pair.T
        for p in range(n_pairs):
            h = 2 * p
            q2, k2, v2 = stack(q_ref, rows, h), stack(k_ref, rows, h), stack(v_ref, rows, h)
            gcol = gc_pair[:, h:h + 1]
            grow = gc_pair_t[h:h + 1, :]
            beta = gb_pair[:, N_HEADS + h:N_HEADS + h + 1]
            glast = jnp.where(top_rows, gc_pair[c - 1:c, h:h + 1], gc_pair[two - 1:two, h:h + 1])
            eg = jnp.exp(gcol)
            kb = k2 * beta
            items.append(dict(
                rows=rows, h=h, k2b=k2.astype(BF16), q2b=q2.astype(BF16), kbb=kb.astype(BF16),
                decay=jnp.exp(jnp.where(tri, gcol - grow, NEG_INF)),
                rhs=jnp.concatenate([v2 * beta, kb * eg], axis=1).astype(BF16),
                qd=(q2 * eg).astype(BF16),
                kd_t=(k2 * jnp.exp(glast - gcol)).T,
                eglast=jnp.exp(glast)))
    for it in items:
        it["pw"] = jnp.where(stri, _dot_nt(it["kbb"], it["k2b"]) * it["decay"], 0.0)
        it["inv"] = eye - it["pw"]
    for _ in range(int(math.log2(c)) - 1):
        for it in items:
            pwb = it["pw"].astype(BF16)
            it["pw"] = _dot(pwb, pwb)
        for it in items:
            it["inv"] = it["inv"] + _dot(it["inv"].astype(BF16), it["pw"].astype(BF16))
    for it in items:
        sol = _dot(it["inv"].astype(BF16), it["rhs"])
        it["u"] = sol[:, :HEAD_DIM]
        it["wb"] = sol[:, HEAD_DIM:].astype(BF16)
    for it in items:
        it["amat"] = jnp.where(tri, _dot_nt(it["q2b"], it["k2b"]) * it["decay"], 0.0).astype(BF16)

    for s in range(n_sub):
        group = items[s * n_pairs:(s + 1) * n_pairs]
        for it in group:
            h = it["h"]
            it["s0"] = state_sc[h]
            it["s1"] = state_sc[h + 1]
            it["r0"] = _dot(jnp.concatenate([it["wb"][:c], it["qd"][:c]], axis=0), it["s0"].astype(BF16))
            it["r1"] = _dot(jnp.concatenate([it["wb"][c:], it["qd"][c:]], axis=0), it["s1"].astype(BF16))
        for it in group:
            v_new = it["u"] - jnp.concatenate([it["r0"][:c], it["r1"][:c]], axis=0)
            it["vnb"] = v_new.astype(BF16)
            it["o"] = jnp.concatenate([it["r0"][c:], it["r1"][c:]], axis=0) + _dot(it["amat"], it["vnb"])
        for it in group:
            h = it["h"]
            kd_t = it["kd_t"]
            state_sc[h] = it["s0"] * it["eglast"][0:1, :] + _dot(
                jnp.where(top_lanes, kd_t, 0.0).astype(BF16), it["vnb"])
            state_sc[h + 1] = it["s1"] * it["eglast"][c:c + 1, :] + _dot(
                jnp.where(top_lanes, 0.0, kd_t).astype(BF16), it["vnb"])
        for it in group:
            h, rows, o = it["h"], it["rows"], it["o"]
            gate2 = stack(gate_ref, rows, h)
            o = o * lax.rsqrt(jnp.mean(o * o, axis=-1, keepdims=True) + DN_NORM_EPS) * nw_ref[...]
            o = o * (gate2 * _sigmoid(gate2))
            o_ref[rows, h * HEAD_DIM:(h + 1) * HEAD_DIM] = o[:c].astype(BF16)
            o_ref[rows, (h + 1) * HEAD_DIM:(h + 2) * HEAD_DIM] = o[c:].astype(BF16)


def _delta_rule(q, k, v, gate, gb, norm_w_row, batch, seq):
    t = q.shape[0]
    width = N_HEADS * HEAD_DIM
    rows = ROWS_DELTA
    ns = seq // rows
    row_map = lambda b, s: (b * ns + s, 0)
    const = lambda b, s: (0, 0)
    wide = pl.BlockSpec((rows, width), row_map)
    return pl.pallas_call(
        _delta_kernel,
        grid=(batch, ns),
        in_specs=[wide, wide, wide, wide,
                  pl.BlockSpec((rows, LANES), row_map),
                  pl.BlockSpec((1, HEAD_DIM), const)],
        out_specs=wide,
        out_shape=jax.ShapeDtypeStruct((t, width), BF16),
        scratch_shapes=[pltpu.VMEM((N_HEADS, HEAD_DIM, HEAD_DIM), F32)],
        compiler_params=_params("arbitrary", "arbitrary"),
        name="dn_delta_rule",
    )(q, k, v, gate, gb, norm_w_row)


def _proj_ln_kernel(a_ref, w_ref, res_ref, g_ref, b_ref, o_ref):
    y = DEEPNORM_ALPHA * res_ref[...] + _dot(a_ref[...], w_ref[...])
    o_ref[...] = _layer_norm(y, g_ref[...], b_ref[...])


def _proj_ln_router_kernel(a_ref, w_ref, res_ref, g_ref, b_ref, wr_ref, o_ref, route_ref):
    y = DEEPNORM_ALPHA * res_ref[...] + _dot(a_ref[...], w_ref[...])
    xn = _layer_norm(y, g_ref[...], b_ref[...])
    o_ref[...] = xn
    xs = _split3(xn)
    ws = _split3(wr_ref[...])
    logits = jnp.zeros((xn.shape[0], LANES), F32)
    for i in range(3):
        for j in range(3 - i):
            logits = logits + _dot(xs[i], ws[j])
    lane = lax.broadcasted_iota(jnp.int32, logits.shape, 1)
    logits = jnp.where(lane < N_EXPERTS, logits, NEG_INF)
    l1 = jnp.max(logits, axis=-1, keepdims=True)
    i1 = jnp.min(jnp.where(logits == l1, lane, LANES), axis=-1, keepdims=True)
    rest = jnp.where(lane == i1, NEG_INF, logits)
    l2 = jnp.max(rest, axis=-1, keepdims=True)
    i2 = jnp.min(jnp.where(rest == l2, lane, LANES), axis=-1, keepdims=True)
    e2 = jnp.exp(l2 - l1)
    g1 = 1.0 / (1.0 + e2)
    g2 = e2 / (1.0 + e2)
    route_ref[...] = jnp.where(lane == 0, g1,
                     jnp.where(lane == 1, g2,
                     jnp.where(lane == 2, i1.astype(F32),
                     jnp.where(lane == 3, i2.astype(F32), 0.0))))


def _proj_ln(a, w, res, g_row, b_row, w_router=None):
    t, kdim = a.shape
    rows = ROWS_LN
    row_map = lambda i: (i, 0)
    const = lambda i: (0, 0)
    in_specs = [pl.BlockSpec((rows, kdim), row_map),
                pl.BlockSpec((kdim, D_MODEL), const),
                pl.BlockSpec((rows, D_MODEL), row_map),
                pl.BlockSpec((1, D_MODEL), const),
                pl.BlockSpec((1, D_MODEL), const)]
    out_full = jax.ShapeDtypeStruct((t, D_MODEL), F32)
    if w_router is None:
        return pl.pallas_call(
            _proj_ln_kernel, grid=(t // rows,), in_specs=in_specs,
            out_specs=pl.BlockSpec((rows, D_MODEL), row_map), out_shape=out_full,
            compiler_params=_params("arbitrary"), name="proj_ln",
        )(a, w, res, g_row, b_row)
    return pl.pallas_call(
        _proj_ln_router_kernel, grid=(t // rows,),
        in_specs=in_specs + [pl.BlockSpec((D_MODEL, LANES), const)],
        out_specs=[pl.BlockSpec((rows, D_MODEL), row_map), pl.BlockSpec((rows, LANES), row_map)],
        out_shape=[out_full, jax.ShapeDtypeStruct((t, LANES), F32)],
        compiler_params=_params("arbitrary"), name="proj_ln_router",
    )(a, w, res, g_row, b_row, w_router)


def _swiglu_part(x, w1_ref, w3_ref, w2_ref, hid_sc):
    xb = x.astype(BF16)
    sub = 2 * LANES
    for c in range(hid_sc.shape[1] // sub):
        cols = slice(c * sub, (c + 1) * sub)
        h1 = _dot(xb, w1_ref[:, cols])
        h3 = _dot(xb, w3_ref[:, cols])
        hid_sc[:, cols] = (h1 * _sigmoid(h1) * h3).astype(BF16)
    return _dot(hid_sc[...], w2_ref[...])


def _ffn_dense_kernel(x_ref, w1_ref, w3_ref, w2_ref, g_ref, b_ref, o_ref, acc_sc, hid_sc):
    f = pl.program_id(1)
    part = _swiglu_part(x_ref[...], w1_ref, w3_ref, w2_ref, hid_sc)

    @pl.when(f == 0)
    def _():
        acc_sc[...] = part

    @pl.when((f > 0) & (f < pl.num_programs(1) - 1))
    def _():
        acc_sc[...] += part

    @pl.when(f == pl.num_programs(1) - 1)
    def _():
        y = DEEPNORM_ALPHA * x_ref[...] + (acc_sc[...] + part)
        o_ref[...] = _layer_norm(y, g_ref[...], b_ref[...])


def _ffn_dense(x, w1, w3, w2, g_row, b_row):
    t = x.shape[0]
    rows, fc = ROWS_FFN, FFN_CHUNK
    nf = FFN_HIDDEN // fc
    return pl.pallas_call(
        _ffn_dense_kernel,
        grid=(t // rows, nf),
        in_specs=[pl.BlockSpec((rows, D_MODEL), lambda i, f: (i, 0)),
                  pl.BlockSpec((D_MODEL, fc), lambda i, f: (0, f)),
                  pl.BlockSpec((D_MODEL, fc), lambda i, f: (0, f)),
                  pl.BlockSpec((fc, D_MODEL), lambda i, f: (f, 0)),
                  pl.BlockSpec((1, D_MODEL), lambda i, f: (0, 0)),
                  pl.BlockSpec((1, D_MODEL), lambda i, f: (0, 0))],
        out_specs=pl.BlockSpec((rows, D_MODEL), lambda i, f: (i, 0)),
        out_shape=jax.ShapeDtypeStruct((t, D_MODEL), F32),
        scratch_shapes=[pltpu.VMEM((rows, D_MODEL), F32), pltpu.VMEM((rows, fc), BF16)],
        compiler_params=_params("arbitrary", "arbitrary"),
        name="ffn_dense",
    )(x, w1, w3, w2, g_row, b_row)


def _moe_experts_kernel(nu_ref, te_ref, idx_ref, idx_next_ref, dst_ref, x_hbm, w1_ref, w3_ref, w2_ref,
                        y_hbm, xbuf, ybuf, acc_sc, hid_sc, gsem, ssem):
    i = pl.program_id(0)
    f = pl.program_id(1)
    last_f = pl.num_programs(1) - 1
    rows = xbuf.shape[1]
    nu = nu_ref[0]
    slot = i % 2

    def gather_start(idx, s):
        for r in range(rows):
            pltpu.make_async_copy(x_hbm.at[pl.ds(idx[0, r], 1)], xbuf.at[s, pl.ds(r, 1)], gsem.at[s]).start()

    def scatter_start(s):
        for r in range(rows):
            pltpu.make_async_copy(ybuf.at[s, pl.ds(r, 1)], y_hbm.at[pl.ds(dst_ref[0, r], 1)], ssem.at[s]).start()

    def wait_all(buf, sem, s):
        pltpu.make_async_copy(buf.at[s], buf.at[s], sem.at[s]).wait()

    first = f == 0

    @pl.when(first & (i == 0))
    def _():
        n_real = TOP_K * x_hbm.shape[0]
        ybuf[1] = jnp.zeros((rows, D_MODEL), F32)
        fills = [pltpu.make_async_copy(ybuf.at[1], y_hbm.at[pl.ds(n_real + k * rows, rows)], ssem.at[1])
                 for k in range((y_hbm.shape[0] - n_real) // rows)]
        for cp in fills:
            cp.start()
        for cp in fills:
            cp.wait()

    @pl.when(first & (i == 0) & (nu > 0))
    def _():
        gather_start(idx_ref, 0)

    @pl.when(first & (i < nu))
    def _():
        wait_all(xbuf, gsem, slot)

    @pl.when(first & (i + 1 < nu))
    def _():
        gather_start(idx_next_ref, 1 - slot)

    @pl.when(i < nu)
    def _():
        part = _swiglu_part(xbuf[slot], w1_ref, w3_ref, w2_ref, hid_sc)

        @pl.when(first)
        def _():
            acc_sc[...] = part

        @pl.when((f > 0) & (f < last_f))
        def _():
            acc_sc[...] += part

        @pl.when(f == last_f)
        def _():
            @pl.when(i >= 2)
            def _():
                wait_all(ybuf, ssem, slot)
            ybuf[slot] = acc_sc[...] + part
            scatter_start(slot)

    @pl.when((i == pl.num_programs(0) - 1) & (f == last_f))
    def _():
        @pl.when(nu >= 1)
        def _():
            wait_all(ybuf, ssem, (nu - 1) % 2)

        @pl.when(nu >= 2)
        def _():
            wait_all(ybuf, ssem, nu % 2)


def _moe_experts(n_used, tile_expert, row_token, row_dst, x, w1, w3, w2, n_out_rows):
    n_tiles = tile_expert.shape[0]
    rows, fc = ROWS_FFN, FFN_CHUNK
    nf = FFN_HIDDEN // fc
    idx3 = row_token.reshape(n_tiles, 1, rows)
    dst3 = row_dst.reshape(n_tiles, 1, rows)
    smem_tile = lambda fn: pl.BlockSpec((None, 1, rows), fn, memory_space=pltpu.SMEM)
    grid_spec = pltpu.PrefetchScalarGridSpec(
        num_scalar_prefetch=2,
        grid=(n_tiles, nf),
        in_specs=[smem_tile(lambda i, f, nu, te: (i, 0, 0)),
                  smem_tile(lambda i, f, nu, te: (jnp.minimum(i + 1, n_tiles - 1), 0, 0)),
                  smem_tile(lambda i, f, nu, te: (i, 0, 0)),
                  pl.BlockSpec(memory_space=pl.ANY),
                  pl.BlockSpec((None, D_MODEL, fc), lambda i, f, nu, te: (te[i], 0, f)),
                  pl.BlockSpec((None, D_MODEL, fc), lambda i, f, nu, te: (te[i], 0, f)),
                  pl.BlockSpec((None, fc, D_MODEL), lambda i, f, nu, te: (te[i], f, 0))],
        out_specs=pl.BlockSpec(memory_space=pl.ANY),
        scratch_shapes=[pltpu.VMEM((2, rows, D_MODEL), F32), pltpu.VMEM((2, rows, D_MODEL), F32),
                        pltpu.VMEM((rows, D_MODEL), F32), pltpu.VMEM((rows, fc), BF16),
                        pltpu.SemaphoreType.DMA((2,)), pltpu.SemaphoreType.DMA((2,))],
    )
    return pl.pallas_call(
        _moe_experts_kernel,
        grid_spec=grid_spec,
        out_shape=jax.ShapeDtypeStruct((n_out_rows, D_MODEL), F32),
        compiler_params=_params("arbitrary", "arbitrary"),
        name="moe_experts",
    )(n_used, tile_expert, idx3, idx3, dst3, x, w1, w3, w2)


def _qkv_kernel(x_ref, w_ref, cos_ref, sin_ref, q_ref, k_ref, vt_ref, sel_ref, km_sc, qf_sc):
    i = pl.program_id(1)
    width = N_HEADS * HEAD_DIM
    half = HEAD_DIM // 2
    xb = x_ref[...].astype(BF16)
    cosf = cos_ref[...]
    sinf = sin_ref[...]

    @pl.when(i == 0)
    def _():
        km_sc[...] = jnp.zeros(km_sc.shape, F32)

    def rope(t):
        return t * cosf + pltpu.roll(t, half, axis=1) * sinf

    chunk = 2 * HEAD_DIM
    for c in range(width // chunk):
        kk = _dot(xb, w_ref[:, c * chunk:(c + 1) * chunk])
        vv = _dot(xb, w_ref[:, width + c * chunk:width + (c + 1) * chunk])
        qq = _dot(xb, w_ref[:, 2 * width + c * chunk:2 * width + (c + 1) * chunk])
        for hh in range(2):
            h = 2 * c + hh
            cols = slice(h * HEAD_DIM, (h + 1) * HEAD_DIM)
            part = slice(hh * HEAD_DIM, (hh + 1) * HEAD_DIM)
            kr = rope(kk[:, part])
            k_ref[:, cols] = kr.astype(BF16)
            km_row = lax.broadcasted_iota(jnp.int32, (km_sc.shape[0], HEAD_DIM), 0)
            km_sc[:, cols] = jnp.where(km_row == i, jnp.mean(kr, axis=0, keepdims=True), km_sc[:, cols])
            qr = rope(qq[:, part])
            qf_sc[:, cols] = qr
            q_ref[:, cols] = (qr * (HEAD_DIM ** -0.5)).astype(BF16)
            vt_ref[h] = vv[:, part].T.astype(BF16)

    nb = km_sc.shape[0]
    km = km_sc[...]
    lane_head = lax.broadcasted_iota(jnp.int32, km.shape, 1) // HEAD_DIM
    km_rows = jnp.concatenate([jnp.where(lane_head == h, km, 0.0) for h in range(N_HEADS)], axis=0)
    ks = _split3(km_rows)
    qs = _split3(qf_sc[...])
    gate_t = jnp.zeros((N_HEADS * nb, x_ref.shape[0]), F32)
    for a in range(3):
        for b in range(3 - a):
            gate_t = gate_t + _dot_nt(ks[a], qs[b])
    blk = lax.broadcasted_iota(jnp.int32, (nb, x_ref.shape[0]), 0)
    past = blk < i
    for h in range(N_HEADS):
        g = jnp.where(past, gate_t[h * nb:(h + 1) * nb, :], NEG_INF)
        rank = jnp.zeros(g.shape, jnp.int32)
        for n in range(nb):
            gn = g[n:n + 1, :]
            ahead = (gn > g) | ((gn == g) & (blk > n))
            rank = rank + ahead.astype(jnp.int32)
        sel_ref[h * nb:(h + 1) * nb, :] = jnp.where(past & (rank < MOBA_TOPK), 0.0, NEG_INF)


def _qkv_proj(x, w_kvq, cos_full, sin_signed, batch, seq):
    t = x.shape[0]
    width = N_HEADS * HEAD_DIM
    rows = MOBA_BLOCK
    nb = seq // rows
    row_map = lambda b, i: (b * nb + i, 0)
    act = jax.ShapeDtypeStruct((t, width), BF16)
    return pl.pallas_call(
        _qkv_kernel,
        grid=(batch, nb),
        in_specs=[pl.BlockSpec((rows, D_MODEL), row_map),
                  pl.BlockSpec((D_MODEL, 3 * width), lambda b, i: (0, 0)),
                  pl.BlockSpec((rows, HEAD_DIM), lambda b, i: (i, 0)),
                  pl.BlockSpec((rows, HEAD_DIM), lambda b, i: (i, 0))],
        out_specs=[pl.BlockSpec((rows, width), row_map),
                   pl.BlockSpec((rows, width), row_map),
                   pl.BlockSpec((None, N_HEADS, HEAD_DIM, rows), lambda b, i: (b, 0, 0, i)),
                   pl.BlockSpec((None, N_HEADS * nb, rows), lambda b, i: (b * nb + i, 0, 0))],
        out_shape=[act, act,
                   jax.ShapeDtypeStruct((batch, N_HEADS, HEAD_DIM, seq), BF16),
                   jax.ShapeDtypeStruct((batch * nb, N_HEADS * nb, rows), F32)],
        scratch_shapes=[pltpu.VMEM((nb, width), F32), pltpu.VMEM((rows, width), F32)],
        compiler_params=_params("arbitrary", "arbitrary"),
        name="moba_qkv_proj",
    )(x, w_kvq, cos_full, sin_signed)


def _attn_kernel(q_ref, k_ref, vt_ref, sel_ref, o_ref):
    blk = MOBA_BLOCK
    nb = q_ref.shape[0] // blk
    key = lax.broadcasted_iota(jnp.int32, (blk, blk), 0)
    qry = lax.broadcasted_iota(jnp.int32, (blk, blk), 1)
    causal = key <= qry

    def scores(i):
        return _dot_nt(k_ref[0:(i + 1) * blk, :], q_ref[i * blk:(i + 1) * blk, :])

    def finish(i, s):
        parts = [s[j * blk:(j + 1) * blk] + sel_ref[i, j:j + 1, :] for j in range(i)]
        parts.append(jnp.where(causal, s[i * blk:(i + 1) * blk], NEG_INF))
        m = jnp.max(parts[0], axis=0, keepdims=True)
        for part in parts[1:]:
            m = jnp.maximum(m, jnp.max(part, axis=0, keepdims=True))
        probs = [jnp.exp(part - m) for part in parts]
        l = jnp.sum(probs[0], axis=0, keepdims=True)
        for pr in probs[1:]:
            l = l + jnp.sum(pr, axis=0, keepdims=True)
        p = jnp.concatenate([pr.astype(BF16) for pr in probs], axis=0)
        acc = _dot(vt_ref[:, 0:(i + 1) * blk], p)
        o_ref[i * blk:(i + 1) * blk, :] = (acc / l).T.astype(BF16)

    s_cur = scores(0)
    for i in range(nb):
        s_next = scores(i + 1) if i + 1 < nb else None
        finish(i, s_cur)
        s_cur = s_next


def _moba_attention(q, k, vt, sel, batch, seq):
    t = q.shape[0]
    width = N_HEADS * HEAD_DIM
    nb = seq // MOBA_BLOCK
    col = pl.BlockSpec((seq, HEAD_DIM), lambda b, h: (b, h))
    return pl.pallas_call(
        _attn_kernel,
        grid=(batch, N_HEADS),
        in_specs=[col, col,
                  pl.BlockSpec((None, None, HEAD_DIM, seq), lambda b, h: (b, h, 0, 0)),
                  pl.BlockSpec((nb, nb, MOBA_BLOCK), lambda b, h: (b, h, 0))],
        out_specs=col,
        out_shape=jax.ShapeDtypeStruct((t, width), BF16),
        compiler_params=_params("arbitrary", "arbitrary"),
        name="moba_attention",
    )(q, k, vt, sel)


def _combine_kernel(y0_ref, y1_ref, x_ref, route_ref, g_ref, b_ref, o_ref):
    route = route_ref[...]
    y = DEEPNORM_ALPHA * x_ref[...] + route[:, 0:1] * y0_ref[...] + route[:, 1:2] * y1_ref[...]
    o_ref[...] = _layer_norm(y, g_ref[...], b_ref[...])


def _moe_combine(ys, x, route, g_row, b_row):
    t = x.shape[0]
    rows = ROWS_COMBINE
    nt = t // rows
    row_map = lambda i: (i, 0)
    const = lambda i: (0, 0)
    return pl.pallas_call(
        _combine_kernel,
        grid=(nt,),
        in_specs=[pl.BlockSpec((rows, D_MODEL), row_map),
                  pl.BlockSpec((rows, D_MODEL), lambda i: (nt + i, 0)),
                  pl.BlockSpec((rows, D_MODEL), row_map),
                  pl.BlockSpec((rows, LANES), row_map),
                  pl.BlockSpec((1, D_MODEL), const),
                  pl.BlockSpec((1, D_MODEL), const)],
        out_specs=pl.BlockSpec((rows, D_MODEL), row_map),
        out_shape=jax.ShapeDtypeStruct((t, D_MODEL), F32),
        compiler_params=_params("arbitrary"),
        name="moe_combine",
    )(ys, ys, x, route, g_row, b_row)


def _routing_tables(route, n_tokens):
    rows = ROWS_FFN
    n_assign = n_tokens * TOP_K
    n_tiles = n_assign // rows + N_EXPERTS
    experts = route[:, 2:4].astype(jnp.int32).reshape(-1)
    onehot = (experts[:, None] == jnp.arange(N_EXPERTS, dtype=jnp.int32)[None, :]).astype(jnp.int32)
    running = jnp.cumsum(onehot, axis=0)
    counts = running[-1]
    rank = jnp.sum(running * onehot, axis=1) - 1
    tiles_per = (counts + rows - 1) // rows
    tile_end = jnp.cumsum(tiles_per)
    group_start = (tile_end - tiles_per) * rows
    pos = group_start[experts] + rank
    assign = jnp.full((n_tiles * rows,), -1, jnp.int32).at[pos].set(jnp.arange(n_assign, dtype=jnp.int32))
    real = assign >= 0
    row_token = jnp.where(real, assign // TOP_K, 0)
    pad_rank = jnp.minimum(jnp.cumsum(1 - real.astype(jnp.int32)) - 1, N_EXPERTS * rows - 1)
    row_dst = jnp.where(real, (assign % TOP_K) * n_tokens + assign // TOP_K, n_assign + pad_rank)
    tile_id = jnp.arange(n_tiles, dtype=jnp.int32)
    n_used = tile_end[-1]
    tile_expert = jnp.sum((tile_id[:, None] >= tile_end[None, :]).astype(jnp.int32), axis=1)
    last_expert = jnp.sum((n_used - 1 >= tile_end).astype(jnp.int32))
    tile_expert = jnp.minimum(jnp.where(tile_id < n_used, tile_expert, last_expert), N_EXPERTS - 1)
    return n_used.reshape(1), tile_expert, row_token, row_dst, n_assign + N_EXPERTS * rows


def _rope_tables(seq):
    half = HEAD_DIM // 2
    inv_freq = ROPE_THETA ** (-jnp.arange(half, dtype=F32) / half)
    ang = jnp.arange(seq).astype(F32)[:, None] * inv_freq[None, :]
    cos, sin = jnp.cos(ang), jnp.sin(ang)
    return jnp.concatenate([cos, cos], axis=1), jnp.concatenate([-sin, sin], axis=1)


def _pad_lanes(w):
    return jnp.pad(w, ((0, 0), (0, LANES - w.shape[1])))


def kernel(x, a_w_in, a_conv_w, a_log_decay, a_dt_bias, a_norm_w, a_w_out, b_w_kv, b_w_q, b_w_o,
           ffn_w1, ffn_w3, ffn_w2, moe_router, moe_w1, moe_w3, moe_w2, ln_g, ln_b):
    batch, seq, _ = x.shape
    t = batch * seq
    width = N_HEADS * HEAD_DIM
    x0 = x.reshape(t, D_MODEL)
    row = lambda v: v.reshape(1, -1).astype(F32)

    w_in = a_w_in[0]
    q, k, v, gate, gb = _in_proj(
        x0, w_in[:, :4 * width].astype(BF16), _pad_lanes(w_in[:, 4 * width:]).astype(BF16),
        a_conv_w[0], _pad_lanes(row(a_log_decay[0])), _pad_lanes(row(a_dt_bias[0])), batch, seq)
    og = _delta_rule(q, k, v, gate, gb, row(a_norm_w[0]), batch, seq)
    x1 = _proj_ln(og, a_w_out[0].astype(BF16), x0, row(ln_g[0, 0]), row(ln_b[0, 0]))
    x2 = _ffn_dense(x1, ffn_w1[0].astype(BF16), ffn_w3[0].astype(BF16), ffn_w2[0].astype(BF16),
                    row(ln_g[0, 1]), row(ln_b[0, 1]))

    cos_full, sin_signed = _rope_tables(seq)
    w_kvq = jnp.concatenate([b_w_kv, b_w_q[0]], axis=1).astype(BF16)
    qm, km, vt, sel = _qkv_proj(x2, w_kvq, cos_full, sin_signed, batch, seq)
    attn = _moba_attention(qm, km, vt, sel, batch, seq)
    x3, route = _proj_ln(attn, b_w_o[0].astype(BF16), x2, row(ln_g[1, 0]), row(ln_b[1, 0]),
                         w_router=_pad_lanes(moe_router[0]))
    n_used, tile_expert, row_token, row_dst, n_out_rows = _routing_tables(route, t)
    ys = _moe_experts(n_used, tile_expert, row_token, row_dst, x3,
                      moe_w1[0].astype(BF16), moe_w3[0].astype(BF16), moe_w2[0].astype(BF16), n_out_rows)
    x4 = _moe_combine(ys, x3, route, row(ln_g[1, 1]), row(ln_b[1, 1]))
    return x4.reshape(batch, seq, D_MODEL)
```

```python
import functools
import math

import jax
import jax.numpy as jnp
from jax import lax
from jax.experimental import pallas as pl
from jax.experimental.pallas import tpu as pltpu

D_MODEL = 1024
DEPTH = 2
DEEPNORM_ALPHA = (2.0 * DEPTH) ** 0.25
LN_EPS = 1e-5
N_HEADS = 8
HEAD_DIM = 128
DN_CONV = 4
DN_CHUNK = 64
DN_NORM_EPS = 1e-6
MOBA_BLOCK = 256
MOBA_TOPK = 3
ROPE_THETA = 10000.0
NEG_INF = -1e30
LOG2_E = math.log2(math.e)
FFN_HIDDEN = 3584
N_EXPERTS = 8
TOP_K = 2

LANES = 128
SUBLANES = 8
VMEM_LIMIT_BYTES = 56 * 1024 * 1024

ROWS_PROJ = 256
ROWS_DELTA = 128
ROWS_LN = 512
ROWS_LN_SUB = 256
ROWS_FFN = 512
FFN_CHUNK = 1792
ROWS_COMBINE = 512

F32 = jnp.float32
BF16 = jnp.bfloat16

_NT = (((1,), (1,)), ((), ()))


def _dot(a, b):
    return jnp.dot(a, b, preferred_element_type=F32)


def _dot_nt(a, b):
    return lax.dot_general(a, b, _NT, preferred_element_type=F32)


def _params(*sem):
    return pltpu.CompilerParams(dimension_semantics=sem, vmem_limit_bytes=VMEM_LIMIT_BYTES)


def _layer_norm(y, g, b):
    mu = jnp.mean(y, axis=-1, keepdims=True)
    d = y - mu
    var = jnp.mean(d * d, axis=-1, keepdims=True)
    return d * lax.rsqrt(var + LN_EPS) * g + b


def _sigmoid(x):
    return 1.0 / (1.0 + jnp.exp(-x))


def _split2(x):
    hi = x.astype(BF16)
    lo = (x - hi.astype(F32)).astype(BF16)
    return hi, lo


def _dot_split(a, b, dot):
    a_hi, a_lo = _split2(a)
    b_hi, b_lo = _split2(b)
    return dot(a_hi, b_hi) + dot(a_hi, b_lo) + dot(a_lo, b_hi)


def _rider_specs(arrays, n_steps, index_map):
    specs = [pl.BlockSpec((a.shape[0] // n_steps, a.shape[1]), index_map) for a in arrays]
    return specs, [jax.ShapeDtypeStruct(a.shape, BF16) for a in arrays]


def _cast_riders(src_refs, dst_refs):
    for src, dst in zip(src_refs, dst_refs):
        dst[...] = src[...].astype(BF16)


def _in_proj_kernel(n_riders, x_ref, w_ref, wab_ref, cw_ref, alog_ref, dtb_ref, *rest):
    riders_in, rest = rest[:n_riders], rest[n_riders:]
    q_ref, k_ref, v_ref, gate_ref, gb_ref = rest[:5]
    riders_out, conv_sc = rest[5:5 + n_riders], rest[5 + n_riders]
    _cast_riders(riders_in, riders_out)
    rows = x_ref.shape[0]
    width = N_HEADS * HEAD_DIM
    pad = SUBLANES

    @pl.when(pl.program_id(1) == 0)
    def _():
        conv_sc[0:pad, :] = jnp.zeros((pad, 3 * width), F32)

    xb = x_ref[...].astype(BF16)
    chunk = 2 * HEAD_DIM
    for c in range(3 * width // chunk):
        cols = slice(c * chunk, (c + 1) * chunk)
        conv_sc[pad:pad + rows, cols] = _dot(xb, w_ref[:, cols])
        y = conv_sc[pad:pad + rows, cols] * cw_ref[3:4, cols]
        for j in range(DN_CONV - 1):
            y = y + conv_sc[pad - 3 + j:pad - 3 + j + rows, cols] * cw_ref[j:j + 1, cols]
        conv_sc[pad - 3:pad, cols] = conv_sc[pad + rows - 3:pad + rows, cols]
        y = y * _sigmoid(y)
        which = c * chunk // width
        for hh in range(chunk // HEAD_DIM):
            t = y[:, hh * HEAD_DIM:(hh + 1) * HEAD_DIM]
            ocols = slice(c * chunk - which * width + hh * HEAD_DIM,
                          c * chunk - which * width + (hh + 1) * HEAD_DIM)
            if which == 2:
                v_ref[:, ocols] = t.astype(BF16)
            else:
                t = t * lax.rsqrt(jnp.sum(t * t, axis=-1, keepdims=True) + DN_NORM_EPS)
                if which == 0:
                    q_ref[:, ocols] = (t * (HEAD_DIM ** -0.5)).astype(BF16)
                else:
                    k_ref[:, ocols] = t.astype(BF16)
    for c in range(width // chunk):
        cols = slice(3 * width + c * chunk, 3 * width + (c + 1) * chunk)
        gate_ref[:, c * chunk:(c + 1) * chunk] = _dot(xb, w_ref[:, cols]).astype(BF16)
    ab = _dot(xb, wab_ref[...])
    z = ab + dtb_ref[...]
    softplus = jnp.maximum(z, 0.0) + jnp.log(1.0 + jnp.exp(-jnp.abs(z)))
    g = -jnp.exp(alog_ref[...]) * softplus
    lane = lax.broadcasted_iota(jnp.int32, ab.shape, 1)
    gb_ref[...] = jnp.where(lane < N_HEADS, g, _sigmoid(ab))


def _in_proj(x2d, w_main, w_ab, conv_w, a_log_row, dt_row, batch, seq, riders=()):
    t = x2d.shape[0]
    width = N_HEADS * HEAD_DIM
    rows = ROWS_PROJ
    nt = seq // rows
    row_map = lambda b, s: (b * nt + s, 0)
    const = lambda b, s: (0, 0)
    act = jax.ShapeDtypeStruct((t, width), BF16)
    rider_specs, rider_shapes = _rider_specs(riders, batch * nt, row_map)
    return pl.pallas_call(
        functools.partial(_in_proj_kernel, len(riders)),
        grid=(batch, nt),
        in_specs=[pl.BlockSpec((rows, D_MODEL), row_map),
                  pl.BlockSpec((D_MODEL, 4 * width), const),
                  pl.BlockSpec((D_MODEL, LANES), const),
                  pl.BlockSpec((DN_CONV, 3 * width), const),
                  pl.BlockSpec((1, LANES), const),
                  pl.BlockSpec((1, LANES), const)] + rider_specs,
        out_specs=[pl.BlockSpec((rows, width), row_map)] * 4 + [pl.BlockSpec((rows, LANES), row_map)]
        + rider_specs,
        out_shape=[act, act, act, act, jax.ShapeDtypeStruct((t, LANES), F32)] + rider_shapes,
        scratch_shapes=[pltpu.VMEM((rows + SUBLANES, 3 * width), F32)],
        compiler_params=_params("arbitrary", "arbitrary"),
        name="dn_in_proj",
    )(x2d, w_main, w_ab, conv_w, a_log_row, dt_row, *riders)


def _delta_kernel(n_riders, q_ref, k_ref, v_ref, gate_ref, gb_ref, nw_ref, *rest):
    riders_in, o_ref = rest[:n_riders], rest[n_riders]
    riders_out, state_sc = rest[n_riders + 1:2 * n_riders + 1], rest[2 * n_riders + 1]
    _cast_riders(riders_in, riders_out)
    c = DN_CHUNK
    two = 2 * c
    n_sub = q_ref.shape[0] // c
    n_pairs = N_HEADS // 2

    @pl.when(pl.program_id(1) == 0)
    def _():
        state_sc[...] = jnp.zeros(state_sc.shape, F32)

    r = lax.broadcasted_iota(jnp.int32, (two, two), 0)
    cc = lax.broadcasted_iota(jnp.int32, (two, two), 1)
    same = (r < c) == (cc < c)
    tri = same & (r >= cc)
    stri = same & (r > cc)
    eye = (r == cc).astype(F32)
    top_rows = lax.broadcasted_iota(jnp.int32, (two, 1), 0) < c
    top_lanes = lax.broadcasted_iota(jnp.int32, (1, two), 1) < c
    row64 = lax.broadcasted_iota(jnp.int32, (c, LANES), 0)

    def stack(ref, rows, h):
        t = ref[rows, h * HEAD_DIM:(h + 2) * HEAD_DIM].astype(F32)
        return jnp.concatenate([t[:, :HEAD_DIM], t[:, HEAD_DIM:]], axis=0)

    items = []
    for s in range(n_sub):
        rows = slice(s * c, (s + 1) * c)
        gb = gb_ref[rows, :]
        gc = gb
        sh = 1
        while sh < c:
            gc = gc + jnp.where(row64 >= sh, pltpu.roll(gc, sh, axis=0), 0.0)
            sh *= 2
        gc_pair = jnp.concatenate([gc, pltpu.roll(gc, LANES - 1, axis=1)], axis=0)
        gb_pair = jnp.concatenate([gb, pltpu.roll(gb, LANES - 1, axis=1)], axis=0)
        gc_pair_t = gc_pair.T
        for p in range(n_pairs):
            h = 2 * p
            q2, k2, v2 = stack(q_ref, rows, h), stack(k_ref, rows, h), stack(v_ref, rows, h)
            gcol = gc_pair[:, h:h + 1]
            grow = gc_pair_t[h:h + 1, :]
            beta = gb_pair[:, N_HEADS + h:N_HEADS + h + 1]
            glast = jnp.where(top_rows, gc_pair[c - 1:c, h:h + 1], gc_pair[two - 1:two, h:h + 1])
            eg = jnp.exp(gcol)
            kb = k2 * beta
            items.append(dict(
                rows=rows, h=h, k2b=k2.astype(BF16), q2b=q2.astype(BF16), kbb=kb.astype(BF16),
                decay=jnp.exp(jnp.where(tri, gcol - grow, NEG_INF)),
                rhs=jnp.concatenate([v2 * beta, kb * eg], axis=1).astype(BF16),
                qd=(q2 * eg).astype(BF16),
                kd_t=(k2 * jnp.exp(glast - gcol)).T,
                eglast=jnp.exp(glast)))
    for it in items:
        it["pw"] = jnp.where(stri, _dot_nt(it["kbb"], it["k2b"]) * it["decay"], 0.0)
        it["inv"] = eye - it["pw"]
    for _ in range(int(math.log2(c)) - 1):
        for it in items:
            pwb = it["pw"].astype(BF16)
            it["pw"] = _dot(pwb, pwb)
        for it in items:
            it["inv"] = it["inv"] + _dot(it["inv"].astype(BF16), it["pw"].astype(BF16))
    for it in items:
        sol = _dot(it["inv"].astype(BF16), it["rhs"])
        it["u"] = sol[:, :HEAD_DIM]
        it["wb"] = sol[:, HEAD_DIM:].astype(BF16)
    for it in items:
        it["amat"] = jnp.where(tri, _dot_nt(it["q2b"], it["k2b"]) * it["decay"], 0.0).astype(BF16)

    for s in range(n_sub):
        group = items[s * n_pairs:(s + 1) * n_pairs]
        for it in group:
            h = it["h"]
            it["s0"] = state_sc[h]
            it["s1"] = state_sc[h + 1]
            it["r0"] = _dot(jnp.concatenate([it["wb"][:c], it["qd"][:c]], axis=0), it["s0"].astype(BF16))
            it["r1"] = _dot(jnp.concatenate([it["wb"][c:], it["qd"][c:]], axis=0), it["s1"].astype(BF16))
        for it in group:
            v_new = it["u"] - jnp.concatenate([it["r0"][:c], it["r1"][:c]], axis=0)
            it["vnb"] = v_new.astype(BF16)
            it["o"] = jnp.concatenate([it["r0"][c:], it["r1"][c:]], axis=0) + _dot(it["amat"], it["vnb"])
        for it in group:
            h = it["h"]
            kd_t = it["kd_t"]
            state_sc[h] = it["s0"] * it["eglast"][0:1, :] + _dot(
                jnp.where(top_lanes, kd_t, 0.0).astype(BF16), it["vnb"])
            state_sc[h + 1] = it["s1"] * it["eglast"][c:c + 1, :] + _dot(
                jnp.where(top_lanes, 0.0, kd_t).astype(BF16), it["vnb"])
        for it in group:
            h, rows, o = it["h"], it["rows"], it["o"]
            gate2 = stack(gate_ref, rows, h)
            o = o * lax.rsqrt(jnp.mean(o * o, axis=-1, keepdims=True) + DN_NORM_EPS) * nw_ref[...]
            o = o * (gate2 * _sigmoid(gate2))
            o_ref[rows, h * HEAD_DIM:(h + 1) * HEAD_DIM] = o[:c].astype(BF16)
            o_ref[rows, (h + 1) * HEAD_DIM:(h + 2) * HEAD_DIM] = o[c:].astype(BF16)


def _delta_rule(q, k, v, gate, gb, norm_w_row, batch, seq, riders=()):
    t = q.shape[0]
    width = N_HEADS * HEAD_DIM
    rows = ROWS_DELTA
    ns = seq // rows
    row_map = lambda b, s: (b * ns + s, 0)
    const = lambda b, s: (0, 0)
    wide = pl.BlockSpec((rows, width), row_map)
    rider_specs, rider_shapes = _rider_specs(riders, batch * ns, row_map)
    return pl.pallas_call(
        functools.partial(_delta_kernel, len(riders)),
        grid=(batch, ns),
        in_specs=[wide, wide, wide, wide,
                  pl.BlockSpec((rows, LANES), row_map),
                  pl.BlockSpec((1, HEAD_DIM), const)] + rider_specs,
        out_specs=[wide] + rider_specs,
        out_shape=[jax.ShapeDtypeStruct((t, width), BF16)] + rider_shapes,
        scratch_shapes=[pltpu.VMEM((N_HEADS, HEAD_DIM, HEAD_DIM), F32)],
        compiler_params=_params("arbitrary", "arbitrary"),
        name="dn_delta_rule",
    )(q, k, v, gate, gb, norm_w_row, *riders)


def _proj_ln_kernel(a_ref, w_ref, res_ref, g_ref, b_ref, o_ref):
    for rb in range(a_ref.shape[0] // ROWS_LN_SUB):
        rows = slice(rb * ROWS_LN_SUB, (rb + 1) * ROWS_LN_SUB)
        y = DEEPNORM_ALPHA * res_ref[rows, :] + _dot(a_ref[rows, :], w_ref[...])
        o_ref[rows, :] = _layer_norm(y, g_ref[...], b_ref[...])


def _proj_ln_router_kernel(a_ref, w_ref, res_ref, g_ref, b_ref, wr_ref, o_ref, route_ref):
    for rb in range(a_ref.shape[0] // ROWS_LN_SUB):
        rows = slice(rb * ROWS_LN_SUB, (rb + 1) * ROWS_LN_SUB)
        y = DEEPNORM_ALPHA * res_ref[rows, :] + _dot(a_ref[rows, :], w_ref[...])
        xn = _layer_norm(y, g_ref[...], b_ref[...])
        o_ref[rows, :] = xn
        logits = _dot_split(xn, wr_ref[...], _dot)
        lane = lax.broadcasted_iota(jnp.int32, logits.shape, 1)
        logits = jnp.where(lane < N_EXPERTS, logits, NEG_INF)
        l1 = jnp.max(logits, axis=-1, keepdims=True)
        i1 = jnp.min(jnp.where(logits == l1, lane, LANES), axis=-1, keepdims=True)
        rest = jnp.where(lane == i1, NEG_INF, logits)
        l2 = jnp.max(rest, axis=-1, keepdims=True)
        i2 = jnp.min(jnp.where(rest == l2, lane, LANES), axis=-1, keepdims=True)
        e2 = jnp.exp(l2 - l1)
        g1 = 1.0 / (1.0 + e2)
        g2 = e2 / (1.0 + e2)
        route_ref[rows, :] = jnp.where(lane == 0, g1,
                             jnp.where(lane == 1, g2,
                             jnp.where(lane == 2, i1.astype(F32),
                             jnp.where(lane == 3, i2.astype(F32), 0.0))))


def _proj_ln(a, w, res, g_row, b_row, w_router=None):
    t, kdim = a.shape
    rows = ROWS_LN
    row_map = lambda i: (i, 0)
    const = lambda i: (0, 0)
    in_specs = [pl.BlockSpec((rows, kdim), row_map),
                pl.BlockSpec((kdim, D_MODEL), const),
                pl.BlockSpec((rows, D_MODEL), row_map),
                pl.BlockSpec((1, D_MODEL), const),
                pl.BlockSpec((1, D_MODEL), const)]
    out_full = jax.ShapeDtypeStruct((t, D_MODEL), F32)
    if w_router is None:
        return pl.pallas_call(
            _proj_ln_kernel, grid=(t // rows,), in_specs=in_specs,
            out_specs=pl.BlockSpec((rows, D_MODEL), row_map), out_shape=out_full,
            compiler_params=_params("arbitrary"), name="proj_ln",
        )(a, w, res, g_row, b_row)
    return pl.pallas_call(
        _proj_ln_router_kernel, grid=(t // rows,),
        in_specs=in_specs + [pl.BlockSpec((D_MODEL, LANES), const)],
        out_specs=[pl.BlockSpec((rows, D_MODEL), row_map), pl.BlockSpec((rows, LANES), row_map)],
        out_shape=[out_full, jax.ShapeDtypeStruct((t, LANES), F32)],
        compiler_params=_params("arbitrary"), name="proj_ln_router",
    )(a, w, res, g_row, b_row, w_router)


def _swiglu_part(x, w1_ref, w3_ref, w2_ref, hid_sc):
    xb = x.astype(BF16)
    sub = 2 * LANES
    for c in range(hid_sc.shape[1] // sub):
        cols = slice(c * sub, (c + 1) * sub)
        h1 = _dot(xb, w1_ref[:, cols])
        h3 = _dot(xb, w3_ref[:, cols])
        hid_sc[:, cols] = (h1 * _sigmoid(h1) * h3).astype(BF16)
    return _dot(hid_sc[...], w2_ref[...])


def _ffn_dense_kernel(x_ref, w1_ref, w3_ref, w2_ref, g_ref, b_ref, o_ref, acc_sc, hid_sc):
    f = pl.program_id(1)
    part = _swiglu_part(x_ref[...], w1_ref, w3_ref, w2_ref, hid_sc)

    @pl.when(f == 0)
    def _():
        acc_sc[...] = part

    @pl.when((f > 0) & (f < pl.num_programs(1) - 1))
    def _():
        acc_sc[...] += part

    @pl.when(f == pl.num_programs(1) - 1)
    def _():
        y = DEEPNORM_ALPHA * x_ref[...] + (acc_sc[...] + part)
        o_ref[...] = _layer_norm(y, g_ref[...], b_ref[...])


def _ffn_dense(x, w1, w3, w2, g_row, b_row):
    t = x.shape[0]
    rows, fc = ROWS_FFN, FFN_CHUNK
    nf = FFN_HIDDEN // fc
    return pl.pallas_call(
        _ffn_dense_kernel,
        grid=(t // rows, nf),
        in_specs=[pl.BlockSpec((rows, D_MODEL), lambda i, f: (i, 0)),
                  pl.BlockSpec((D_MODEL, fc), lambda i, f: (0, f)),
                  pl.BlockSpec((D_MODEL, fc), lambda i, f: (0, f)),
                  pl.BlockSpec((fc, D_MODEL), lambda i, f: (f, 0)),
                  pl.BlockSpec((1, D_MODEL), lambda i, f: (0, 0)),
                  pl.BlockSpec((1, D_MODEL), lambda i, f: (0, 0))],
        out_specs=pl.BlockSpec((rows, D_MODEL), lambda i, f: (i, 0)),
        out_shape=jax.ShapeDtypeStruct((t, D_MODEL), F32),
        scratch_shapes=[pltpu.VMEM((rows, D_MODEL), F32), pltpu.VMEM((rows, fc), BF16)],
        compiler_params=_params("arbitrary", "arbitrary"),
        name="ffn_dense",
    )(x, w1, w3, w2, g_row, b_row)


def _moe_experts_kernel(nu_ref, te_ref, idx_ref, idx_next_ref, dst_ref, x_hbm, w1_ref, w3_ref, w2_ref,
                        y_hbm, xbuf, ybuf, acc_sc, hid_sc, gsem, ssem):
    i = pl.program_id(0)
    f = pl.program_id(1)
    last_f = pl.num_programs(1) - 1
    rows = xbuf.shape[1]
    nu = nu_ref[0]
    slot = i % 2

    def gather_start(idx, s):
        for r in range(rows):
            pltpu.make_async_copy(x_hbm.at[pl.ds(idx[0, r], 1)], xbuf.at[s, pl.ds(r, 1)], gsem.at[s]).start()

    def scatter_start(s):
        for r in range(rows):
            pltpu.make_async_copy(ybuf.at[s, pl.ds(r, 1)], y_hbm.at[pl.ds(dst_ref[0, r], 1)], ssem.at[s]).start()

    def wait_all(buf, sem, s):
        pltpu.make_async_copy(buf.at[s], buf.at[s], sem.at[s]).wait()

    first = f == 0

    @pl.when(first & (i == 0))
    def _():
        n_real = TOP_K * x_hbm.shape[0]
        ybuf[1] = jnp.zeros((rows, D_MODEL), F32)
        fills = [pltpu.make_async_copy(ybuf.at[1], y_hbm.at[pl.ds(n_real + k * rows, rows)], ssem.at[1])
                 for k in range((y_hbm.shape[0] - n_real) // rows)]
        for cp in fills:
            cp.start()
        for cp in fills:
            cp.wait()

    @pl.when(first & (i == 0) & (nu > 0))
    def _():
        gather_start(idx_ref, 0)

    @pl.when(first & (i < nu))
    def _():
        wait_all(xbuf, gsem, slot)

    @pl.when(first & (i + 1 < nu))
    def _():
        gather_start(idx_next_ref, 1 - slot)

    @pl.when(i < nu)
    def _():
        part = _swiglu_part(xbuf[slot], w1_ref, w3_ref, w2_ref, hid_sc)

        @pl.when(first)
        def _():
            acc_sc[...] = part

        @pl.when((f > 0) & (f < last_f))
        def _():
            acc_sc[...] += part

        @pl.when(f == last_f)
        def _():
            @pl.when(i >= 2)
            def _():
                wait_all(ybuf, ssem, slot)
            ybuf[slot] = acc_sc[...] + part
            scatter_start(slot)

    @pl.when((i == pl.num_programs(0) - 1) & (f == last_f))
    def _():
        @pl.when(nu >= 1)
        def _():
            wait_all(ybuf, ssem, (nu - 1) % 2)

        @pl.when(nu >= 2)
        def _():
            wait_all(ybuf, ssem, nu % 2)


def _moe_experts(n_used, tile_expert, row_token, row_dst, x, w1, w3, w2, n_out_rows):
    n_tiles = tile_expert.shape[0]
    rows, fc = ROWS_FFN, FFN_CHUNK
    nf = FFN_HIDDEN // fc
    idx3 = row_token.reshape(n_tiles, 1, rows)
    dst3 = row_dst.reshape(n_tiles, 1, rows)
    smem_tile = lambda fn: pl.BlockSpec((None, 1, rows), fn, memory_space=pltpu.SMEM)
    grid_spec = pltpu.PrefetchScalarGridSpec(
        num_scalar_prefetch=2,
        grid=(n_tiles, nf),
        in_specs=[smem_tile(lambda i, f, nu, te: (i, 0, 0)),
                  smem_tile(lambda i, f, nu, te: (jnp.minimum(i + 1, n_tiles - 1), 0, 0)),
                  smem_tile(lambda i, f, nu, te: (i, 0, 0)),
                  pl.BlockSpec(memory_space=pl.ANY),
                  pl.BlockSpec((None, D_MODEL, fc), lambda i, f, nu, te: (te[i], 0, f)),
                  pl.BlockSpec((None, D_MODEL, fc), lambda i, f, nu, te: (te[i], 0, f)),
                  pl.BlockSpec((None, fc, D_MODEL), lambda i, f, nu, te: (te[i], f, 0))],
        out_specs=pl.BlockSpec(memory_space=pl.ANY),
        scratch_shapes=[pltpu.VMEM((2, rows, D_MODEL), F32), pltpu.VMEM((2, rows, D_MODEL), F32),
                        pltpu.VMEM((rows, D_MODEL), F32), pltpu.VMEM((rows, fc), BF16),
                        pltpu.SemaphoreType.DMA((2,)), pltpu.SemaphoreType.DMA((2,))],
    )
    return pl.pallas_call(
        _moe_experts_kernel,
        grid_spec=grid_spec,
        out_shape=jax.ShapeDtypeStruct((n_out_rows, D_MODEL), F32),
        compiler_params=_params("arbitrary", "arbitrary"),
        name="moe_experts",
    )(n_used, tile_expert, idx3, idx3, dst3, x, w1, w3, w2)


def _qkv_kernel(x_ref, w_ref, cos_ref, sin_ref, q_ref, k_ref, vt_ref, sel_ref, km_sc, qf_sc):
    i = pl.program_id(1)
    width = N_HEADS * HEAD_DIM
    half = HEAD_DIM // 2
    xb = x_ref[...].astype(BF16)
    cosf = cos_ref[...]
    sinf = sin_ref[...]

    @pl.when(i == 0)
    def _():
        km_sc[...] = jnp.zeros(km_sc.shape, F32)

    def rope(t):
        return t * cosf + pltpu.roll(t, half, axis=1) * sinf

    chunk = 2 * HEAD_DIM
    for c in range(width // chunk):
        kk = _dot(xb, w_ref[:, c * chunk:(c + 1) * chunk])
        vv = _dot(xb, w_ref[:, width + c * chunk:width + (c + 1) * chunk])
        qq = _dot(xb, w_ref[:, 2 * width + c * chunk:2 * width + (c + 1) * chunk])
        for hh in range(2):
            h = 2 * c + hh
            cols = slice(h * HEAD_DIM, (h + 1) * HEAD_DIM)
            part = slice(hh * HEAD_DIM, (hh + 1) * HEAD_DIM)
            kr = rope(kk[:, part])
            k_ref[:, cols] = kr.astype(BF16)
            km_row = lax.broadcasted_iota(jnp.int32, (km_sc.shape[0], HEAD_DIM), 0)
            km_sc[:, cols] = jnp.where(km_row == i, jnp.mean(kr, axis=0, keepdims=True), km_sc[:, cols])
            qr = rope(qq[:, part])
            qf_sc[:, cols] = qr
            q_ref[:, cols] = (qr * (HEAD_DIM ** -0.5 * LOG2_E)).astype(BF16)
            vt_ref[h] = vv[:, part].T.astype(BF16)

    nb = km_sc.shape[0]
    km = km_sc[...]
    lane_head = lax.broadcasted_iota(jnp.int32, km.shape, 1) // HEAD_DIM
    km_rows = jnp.concatenate([jnp.where(lane_head == h, km, 0.0) for h in range(N_HEADS)], axis=0)
    gate_t = _dot_split(km_rows, qf_sc[...], _dot_nt)
    blk = lax.broadcasted_iota(jnp.int32, (nb, x_ref.shape[0]), 0)
    past = blk < i
    for h in range(N_HEADS):
        g = jnp.where(past, gate_t[h * nb:(h + 1) * nb, :], NEG_INF)
        rank = jnp.zeros(g.shape, jnp.int32)
        for n in range(nb):
            gn = g[n:n + 1, :]
            ahead = (gn > g) | ((gn == g) & (blk > n))
            rank = rank + ahead.astype(jnp.int32)
        sel_ref[h * nb:(h + 1) * nb, :] = jnp.where(past & (rank < MOBA_TOPK), 0.0, NEG_INF)


def _qkv_proj(x, w_kvq, cos_full, sin_signed, batch, seq):
    t = x.shape[0]
    width = N_HEADS * HEAD_DIM
    rows = MOBA_BLOCK
    nb = seq // rows
    row_map = lambda b, i: (b * nb + i, 0)
    act = jax.ShapeDtypeStruct((t, width), BF16)
    return pl.pallas_call(
        _qkv_kernel,
        grid=(batch, nb),
        in_specs=[pl.BlockSpec((rows, D_MODEL), row_map),
                  pl.BlockSpec((D_MODEL, 3 * width), lambda b, i: (0, 0)),
                  pl.BlockSpec((rows, HEAD_DIM), lambda b, i: (i, 0)),
                  pl.BlockSpec((rows, HEAD_DIM), lambda b, i: (i, 0))],
        out_specs=[pl.BlockSpec((rows, width), row_map),
                   pl.BlockSpec((rows, width), row_map),
                   pl.BlockSpec((None, N_HEADS, HEAD_DIM, rows), lambda b, i: (b, 0, 0, i)),
                   pl.BlockSpec((None, N_HEADS * nb, rows), lambda b, i: (b * nb + i, 0, 0))],
        out_shape=[act, act,
                   jax.ShapeDtypeStruct((batch, N_HEADS, HEAD_DIM, seq), BF16),
                   jax.ShapeDtypeStruct((batch * nb, N_HEADS * nb, rows), F32)],
        scratch_shapes=[pltpu.VMEM((nb, width), F32), pltpu.VMEM((rows, width), F32)],
        compiler_params=_params("arbitrary", "arbitrary"),
        name="moba_qkv_proj",
    )(x, w_kvq, cos_full, sin_signed)


def _attn_kernel(q_ref, k_ref, vt_ref, sel_ref, o_ref):
    blk = MOBA_BLOCK
    nb = q_ref.shape[0] // blk
    key = lax.broadcasted_iota(jnp.int32, (blk, blk), 0)
    qry = lax.broadcasted_iota(jnp.int32, (blk, blk), 1)
    causal = key <= qry

    def scores(i):
        return _dot_nt(k_ref[0:(i + 1) * blk, :], q_ref[i * blk:(i + 1) * blk, :])

    def finish(i, s):
        parts = [s[j * blk:(j + 1) * blk] + sel_ref[i, j:j + 1, :] for j in range(i)]
        parts.append(jnp.where(causal, s[i * blk:(i + 1) * blk], NEG_INF))
        m = jnp.max(parts[0], axis=0, keepdims=True)
        for part in parts[1:]:
            m = jnp.maximum(m, jnp.max(part, axis=0, keepdims=True))
        probs = [jnp.exp2(part - m) for part in parts]
        l = jnp.sum(probs[0], axis=0, keepdims=True)
        for pr in probs[1:]:
            l = l + jnp.sum(pr, axis=0, keepdims=True)
        p = jnp.concatenate([pr.astype(BF16) for pr in probs], axis=0)
        acc = _dot(vt_ref[:, 0:(i + 1) * blk], p)
        o_ref[i * blk:(i + 1) * blk, :] = (acc / l).T.astype(BF16)

    s_cur = scores(0)
    for i in range(nb):
        s_next = scores(i + 1) if i + 1 < nb else None
        finish(i, s_cur)
        s_cur = s_next


def _moba_attention(q, k, vt, sel, batch, seq):
    t = q.shape[0]
    width = N_HEADS * HEAD_DIM
    nb = seq // MOBA_BLOCK
    col = pl.BlockSpec((seq, HEAD_DIM), lambda b, h: (b, h))
    return pl.pallas_call(
        _attn_kernel,
        grid=(batch, N_HEADS),
        in_specs=[col, col,
                  pl.BlockSpec((None, None, HEAD_DIM, seq), lambda b, h: (b, h, 0, 0)),
                  pl.BlockSpec((nb, nb, MOBA_BLOCK), lambda b, h: (b, h, 0))],
        out_specs=col,
        out_shape=jax.ShapeDtypeStruct((t, width), BF16),
        compiler_params=_params("arbitrary", "arbitrary"),
        name="moba_attention",
    )(q, k, vt, sel)


def _combine_kernel(y0_ref, y1_ref, x_ref, route_ref, g_ref, b_ref, o_ref):
    route = route_ref[...]
    y = DEEPNORM_ALPHA * x_ref[...] + route[:, 0:1] * y0_ref[...] + route[:, 1:2] * y1_ref[...]
    o_ref[...] = _layer_norm(y, g_ref[...], b_ref[...])


def _moe_combine(ys, x, route, g_row, b_row):
    t = x.shape[0]
    rows = ROWS_COMBINE
    nt = t // rows
    row_map = lambda i: (i, 0)
    const = lambda i: (0, 0)
    return pl.pallas_call(
        _combine_kernel,
        grid=(nt,),
        in_specs=[pl.BlockSpec((rows, D_MODEL), row_map),
                  pl.BlockSpec((rows, D_MODEL), lambda i: (nt + i, 0)),
                  pl.BlockSpec((rows, D_MODEL), row_map),
                  pl.BlockSpec((rows, LANES), row_map),
                  pl.BlockSpec((1, D_MODEL), const),
                  pl.BlockSpec((1, D_MODEL), const)],
        out_specs=pl.BlockSpec((rows, D_MODEL), row_map),
        out_shape=jax.ShapeDtypeStruct((t, D_MODEL), F32),
        compiler_params=_params("arbitrary"),
        name="moe_combine",
    )(ys, ys, x, route, g_row, b_row)


def _routing_tables(route, n_tokens):
    rows = ROWS_FFN
    n_assign = n_tokens * TOP_K
    n_tiles = n_assign // rows + N_EXPERTS
    experts = route[:, 2:4].astype(jnp.int32).reshape(-1)
    onehot = (experts[:, None] == jnp.arange(N_EXPERTS, dtype=jnp.int32)[None, :]).astype(jnp.int32)
    running = jnp.cumsum(onehot, axis=0)
    counts = running[-1]
    rank = jnp.sum(running * onehot, axis=1) - 1
    tiles_per = (counts + rows - 1) // rows
    tile_end = jnp.cumsum(tiles_per)
    group_start = (tile_end - tiles_per) * rows
    pos = group_start[experts] + rank
    assign = jnp.full((n_tiles * rows,), -1, jnp.int32).at[pos].set(jnp.arange(n_assign, dtype=jnp.int32))
    real = assign >= 0
    row_token = jnp.where(real, assign // TOP_K, 0)
    pad_rank = jnp.minimum(jnp.cumsum(1 - real.astype(jnp.int32)) - 1, N_EXPERTS * rows - 1)
    row_dst = jnp.where(real, (assign % TOP_K) * n_tokens + assign // TOP_K, n_assign + pad_rank)
    tile_id = jnp.arange(n_tiles, dtype=jnp.int32)
    n_used = tile_end[-1]
    tile_expert = jnp.sum((tile_id[:, None] >= tile_end[None, :]).astype(jnp.int32), axis=1)
    last_expert = jnp.sum((n_used - 1 >= tile_end).astype(jnp.int32))
    tile_expert = jnp.minimum(jnp.where(tile_id < n_used, tile_expert, last_expert), N_EXPERTS - 1)
    return n_used.reshape(1), tile_expert, row_token, row_dst, n_assign + N_EXPERTS * rows


def _rope_tables(seq):
    half = HEAD_DIM // 2
    inv_freq = ROPE_THETA ** (-jnp.arange(half, dtype=F32) / half)
    ang = jnp.arange(seq).astype(F32)[:, None] * inv_freq[None, :]
    cos, sin = jnp.cos(ang), jnp.sin(ang)
    return jnp.concatenate([cos, cos], axis=1), jnp.concatenate([-sin, sin], axis=1)


def _pad_lanes(w):
    return jnp.pad(w, ((0, 0), (0, LANES - w.shape[1])))


def kernel(x, a_w_in, a_conv_w, a_log_decay, a_dt_bias, a_norm_w, a_w_out, b_w_kv, b_w_q, b_w_o,
           ffn_w1, ffn_w3, ffn_w2, moe_router, moe_w1, moe_w3, moe_w2, ln_g, ln_b):
    batch, seq, _ = x.shape
    t = batch * seq
    width = N_HEADS * HEAD_DIM
    x0 = x.reshape(t, D_MODEL)
    row = lambda v: v.reshape(1, -1).astype(F32)

    w_in = a_w_in[0]
    q, k, v, gate, gb, moe_w1b, moe_w2b = _in_proj(
        x0, w_in[:, :4 * width].astype(BF16), _pad_lanes(w_in[:, 4 * width:]).astype(BF16),
        a_conv_w[0], _pad_lanes(row(a_log_decay[0])), _pad_lanes(row(a_dt_bias[0])), batch, seq,
        riders=(moe_w1[0].reshape(N_EXPERTS * D_MODEL, FFN_HIDDEN),
                moe_w2[0].reshape(N_EXPERTS * FFN_HIDDEN, D_MODEL)))
    og, moe_w3b = _delta_rule(q, k, v, gate, gb, row(a_norm_w[0]), batch, seq,
                              riders=(moe_w3[0].reshape(N_EXPERTS * D_MODEL, FFN_HIDDEN),))
    x1 = _proj_ln(og, a_w_out[0].astype(BF16), x0, row(ln_g[0, 0]), row(ln_b[0, 0]))
    x2 = _ffn_dense(x1, ffn_w1[0].astype(BF16), ffn_w3[0].astype(BF16), ffn_w2[0].astype(BF16),
                    row(ln_g[0, 1]), row(ln_b[0, 1]))

    cos_full, sin_signed = _rope_tables(seq)
    w_kvq = jnp.concatenate([b_w_kv, b_w_q[0]], axis=1).astype(BF16)
    qm, km, vt, sel = _qkv_proj(x2, w_kvq, cos_full, sin_signed, batch, seq)
    attn = _moba_attention(qm, km, vt, sel, batch, seq)
    x3, route = _proj_ln(attn, b_w_o[0].astype(BF16), x2, row(ln_g[1, 0]), row(ln_b[1, 0]),
                         w_router=_pad_lanes(moe_router[0]))
    n_used, tile_expert, row_token, row_dst, n_out_rows = _routing_tables(route, t)
    ys = _moe_experts(n_used, tile_expert, row_token, row_dst, x3,
                      moe_w1b.reshape(N_EXPERTS, D_MODEL, FFN_HIDDEN),
                      moe_w3b.reshape(N_EXPERTS, D_MODEL, FFN_HIDDEN),
                      moe_w2b.reshape(N_EXPERTS, FFN_HIDDEN, D_MODEL), n_out_rows)
    x4 = _moe_combine(ys, x3, route, row(ln_g[1, 1]), row(ln_b[1, 1]))
    return x4.reshape(batch, seq, D_MODEL)
```

```python
import functools
import math

import jax
import jax.numpy as jnp
from jax import lax
from jax.experimental import pallas as pl
from jax.experimental.pallas import tpu as pltpu

D_MODEL = 1024
DEPTH = 2
DEEPNORM_ALPHA = (2.0 * DEPTH) ** 0.25
LN_EPS = 1e-5
N_HEADS = 8
HEAD_DIM = 128
DN_CONV = 4
DN_CHUNK = 64
DN_NORM_EPS = 1e-6
MOBA_BLOCK = 256
MOBA_TOPK = 3
ROPE_THETA = 10000.0
NEG_INF = -1e30
LOG2_E = math.log2(math.e)
FFN_HIDDEN = 3584
N_EXPERTS = 8
TOP_K = 2

LANES = 128
SUBLANES = 8
VMEM_LIMIT_BYTES = 56 * 1024 * 1024

ROWS_PROJ = 256
ROWS_DELTA = 256
ROWS_LN = 512
ROWS_LN_SUB = 256
ROWS_FFN = 512
FFN_CHUNK = 1792
ROWS_COMBINE = 512

F32 = jnp.float32
BF16 = jnp.bfloat16

_NT = (((1,), (1,)), ((), ()))


def _dot(a, b):
    return jnp.dot(a, b, preferred_element_type=F32)


def _dot_nt(a, b):
    return lax.dot_general(a, b, _NT, preferred_element_type=F32)


def _params(*sem):
    return pltpu.CompilerParams(dimension_semantics=sem, vmem_limit_bytes=VMEM_LIMIT_BYTES)


def _layer_norm(y, g, b):
    mu = jnp.mean(y, axis=-1, keepdims=True)
    d = y - mu
    var = jnp.mean(d * d, axis=-1, keepdims=True)
    return d * lax.rsqrt(var + LN_EPS) * g + b


def _sigmoid(x):
    return 1.0 / (1.0 + jnp.exp(-x))


def _split2(x):
    hi = x.astype(BF16)
    lo = (x - hi.astype(F32)).astype(BF16)
    return hi, lo


def _dot_split(a, b, dot):
    a_hi, a_lo = _split2(a)
    b_hi, b_lo = _split2(b)
    return dot(a_hi, b_hi) + dot(a_hi, b_lo) + dot(a_lo, b_hi)


def _rider_specs(arrays, n_steps, index_map):
    specs = [pl.BlockSpec((a.shape[0] // n_steps, a.shape[1]), index_map) for a in arrays]
    return specs, [jax.ShapeDtypeStruct(a.shape, BF16) for a in arrays]


def _cast_riders(src_refs, dst_refs):
    for src, dst in zip(src_refs, dst_refs):
        dst[...] = src[...].astype(BF16)


def _in_proj_kernel(n_riders, x_ref, w_ref, wab_ref, cw_ref, alog_ref, dtb_ref, *rest):
    riders_in, rest = rest[:n_riders], rest[n_riders:]
    q_ref, k_ref, v_ref, gate_ref, gb_ref = rest[:5]
    riders_out, conv_sc = rest[5:5 + n_riders], rest[5 + n_riders]
    _cast_riders(riders_in, riders_out)
    rows = x_ref.shape[0]
    width = N_HEADS * HEAD_DIM
    pad = SUBLANES

    @pl.when(pl.program_id(1) == 0)
    def _():
        conv_sc[0:pad, :] = jnp.zeros((pad, 3 * width), F32)

    xb = x_ref[...].astype(BF16)
    chunk = 2 * HEAD_DIM
    for c in range(3 * width // chunk):
        cols = slice(c * chunk, (c + 1) * chunk)
        conv_sc[pad:pad + rows, cols] = _dot(xb, w_ref[:, cols])
        y = conv_sc[pad:pad + rows, cols] * cw_ref[3:4, cols]
        for j in range(DN_CONV - 1):
            y = y + conv_sc[pad - 3 + j:pad - 3 + j + rows, cols] * cw_ref[j:j + 1, cols]
        conv_sc[pad - 3:pad, cols] = conv_sc[pad + rows - 3:pad + rows, cols]
        y = y * _sigmoid(y)
        which = c * chunk // width
        for hh in range(chunk // HEAD_DIM):
            t = y[:, hh * HEAD_DIM:(hh + 1) * HEAD_DIM]
            ocols = slice(c * chunk - which * width + hh * HEAD_DIM,
                          c * chunk - which * width + (hh + 1) * HEAD_DIM)
            if which == 2:
                v_ref[:, ocols] = t.astype(BF16)
            else:
                t = t * lax.rsqrt(jnp.sum(t * t, axis=-1, keepdims=True) + DN_NORM_EPS)
                if which == 0:
                    q_ref[:, ocols] = (t * (HEAD_DIM ** -0.5)).astype(BF16)
                else:
                    k_ref[:, ocols] = t.astype(BF16)
    for c in range(width // chunk):
        cols = slice(3 * width + c * chunk, 3 * width + (c + 1) * chunk)
        gate_ref[:, c * chunk:(c + 1) * chunk] = _dot(xb, w_ref[:, cols]).astype(BF16)
    ab = _dot(xb, wab_ref[...])
    z = ab + dtb_ref[...]
    softplus = jnp.maximum(z, 0.0) + jnp.log(1.0 + jnp.exp(-jnp.abs(z)))
    g = -jnp.exp(alog_ref[...]) * softplus
    lane = lax.broadcasted_iota(jnp.int32, ab.shape, 1)
    gb_ref[...] = jnp.where(lane < N_HEADS, g, _sigmoid(ab))


def _in_proj(x2d, w_main, w_ab, conv_w, a_log_row, dt_row, batch, seq, riders=()):
    t = x2d.shape[0]
    width = N_HEADS * HEAD_DIM
    rows = ROWS_PROJ
    nt = seq // rows
    row_map = lambda b, s: (b * nt + s, 0)
    const = lambda b, s: (0, 0)
    act = jax.ShapeDtypeStruct((t, width), BF16)
    rider_specs, rider_shapes = _rider_specs(riders, batch * nt, row_map)
    return pl.pallas_call(
        functools.partial(_in_proj_kernel, len(riders)),
        grid=(batch, nt),
        in_specs=[pl.BlockSpec((rows, D_MODEL), row_map),
                  pl.BlockSpec((D_MODEL, 4 * width), const),
                  pl.BlockSpec((D_MODEL, LANES), const),
                  pl.BlockSpec((DN_CONV, 3 * width), const),
                  pl.BlockSpec((1, LANES), const),
                  pl.BlockSpec((1, LANES), const)] + rider_specs,
        out_specs=[pl.BlockSpec((rows, width), row_map)] * 4 + [pl.BlockSpec((rows, LANES), row_map)]
        + rider_specs,
        out_shape=[act, act, act, act, jax.ShapeDtypeStruct((t, LANES), F32)] + rider_shapes,
        scratch_shapes=[pltpu.VMEM((rows + SUBLANES, 3 * width), F32)],
        compiler_params=_params("arbitrary", "arbitrary"),
        name="dn_in_proj",
    )(x2d, w_main, w_ab, conv_w, a_log_row, dt_row, *riders)


def _delta_kernel(n_riders, q_ref, k_ref, v_ref, gate_ref, gb_ref, nw_ref, *rest):
    riders_in, o_ref = rest[:n_riders], rest[n_riders]
    riders_out, state_sc = rest[n_riders + 1:2 * n_riders + 1], rest[2 * n_riders + 1]
    _cast_riders(riders_in, riders_out)
    c = DN_CHUNK
    two = 2 * c
    n_sub = q_ref.shape[0] // c
    n_pairs = N_HEADS // 2

    @pl.when(pl.program_id(1) == 0)
    def _():
        state_sc[...] = jnp.zeros(state_sc.shape, F32)

    r = lax.broadcasted_iota(jnp.int32, (two, two), 0)
    cc = lax.broadcasted_iota(jnp.int32, (two, two), 1)
    same = (r < c) == (cc < c)
    tri = same & (r >= cc)
    stri = same & (r > cc)
    eye = (r == cc).astype(F32)
    top_rows = lax.broadcasted_iota(jnp.int32, (two, 1), 0) < c
    top_lanes = lax.broadcasted_iota(jnp.int32, (1, two), 1) < c
    row64 = lax.broadcasted_iota(jnp.int32, (c, LANES), 0)

    def stack(ref, rows, h):
        t = ref[rows, h * HEAD_DIM:(h + 2) * HEAD_DIM].astype(F32)
        return jnp.concatenate([t[:, :HEAD_DIM], t[:, HEAD_DIM:]], axis=0)

    items = []
    for s in range(n_sub):
        rows = slice(s * c, (s + 1) * c)
        gb = gb_ref[rows, :]
        gc = gb
        sh = 1
        while sh < c:
            gc = gc + jnp.where(row64 >= sh, pltpu.roll(gc, sh, axis=0), 0.0)
            sh *= 2
        gc_pair = jnp.concatenate([gc, pltpu.roll(gc, LANES - 1, axis=1)], axis=0)
        gb_pair = jnp.concatenate([gb, pltpu.roll(gb, LANES - 1, axis=1)], axis=0)
        gc_pair_t = gc_pair.T
        for p in range(n_pairs):
            h = 2 * p
            q2, k2, v2 = stack(q_ref, rows, h), stack(k_ref, rows, h), stack(v_ref, rows, h)
            gcol = gc_pair[:, h:h + 1]
            grow = gc_pair_t[h:h + 1, :]
            beta = gb_pair[:, N_HEADS + h:N_HEADS + h + 1]
            glast = jnp.where(top_rows, gc_pair[c - 1:c, h:h + 1], gc_pair[two - 1:two, h:h + 1])
            eg = jnp.exp(gcol)
            kb = k2 * beta
            items.append(dict(
                rows=rows, h=h, k2b=k2.astype(BF16), q2b=q2.astype(BF16), kbb=kb.astype(BF16),
                decay=jnp.exp(jnp.where(tri, gcol - grow, NEG_INF)),
                rhs=jnp.concatenate([v2 * beta, kb * eg], axis=1).astype(BF16),
                qd=(q2 * eg).astype(BF16),
                kd_t=(k2 * jnp.exp(glast - gcol)).T,
                eglast=jnp.exp(glast)))
    for it in items:
        it["pw"] = jnp.where(stri, _dot_nt(it["kbb"], it["k2b"]) * it["decay"], 0.0)
        it["inv"] = eye - it["pw"]
    for _ in range(int(math.log2(c)) - 1):
        for it in items:
            pwb = it["pw"].astype(BF16)
            it["pw"] = _dot(pwb, pwb)
        for it in items:
            it["inv"] = it["inv"] + _dot(it["inv"].astype(BF16), it["pw"].astype(BF16))
    for it in items:
        sol = _dot(it["inv"].astype(BF16), it["rhs"])
        it["u"] = sol[:, :HEAD_DIM]
        it["wb"] = sol[:, HEAD_DIM:].astype(BF16)
    for it in items:
        it["amat"] = jnp.where(tri, _dot_nt(it["q2b"], it["k2b"]) * it["decay"], 0.0).astype(BF16)

    for s in range(n_sub):
        group = items[s * n_pairs:(s + 1) * n_pairs]
        for it in group:
            h = it["h"]
            it["s0"] = state_sc[h]
            it["s1"] = state_sc[h + 1]
            it["r0"] = _dot(jnp.concatenate([it["wb"][:c], it["qd"][:c]], axis=0), it["s0"].astype(BF16))
            it["r1"] = _dot(jnp.concatenate([it["wb"][c:], it["qd"][c:]], axis=0), it["s1"].astype(BF16))
        for it in group:
            v_new = it["u"] - jnp.concatenate([it["r0"][:c], it["r1"][:c]], axis=0)
            it["vnb"] = v_new.astype(BF16)
            it["o"] = jnp.concatenate([it["r0"][c:], it["r1"][c:]], axis=0) + _dot(it["amat"], it["vnb"])
        for it in group:
            h = it["h"]
            kd_t = it["kd_t"]
            state_sc[h] = it["s0"] * it["eglast"][0:1, :] + _dot(
                jnp.where(top_lanes, kd_t, 0.0).astype(BF16), it["vnb"])
            state_sc[h + 1] = it["s1"] * it["eglast"][c:c + 1, :] + _dot(
                jnp.where(top_lanes, 0.0, kd_t).astype(BF16), it["vnb"])
        for it in group:
            h, rows, o = it["h"], it["rows"], it["o"]
            gate2 = stack(gate_ref, rows, h)
            o = o * lax.rsqrt(jnp.mean(o * o, axis=-1, keepdims=True) + DN_NORM_EPS) * nw_ref[...]
            o = o * (gate2 * _sigmoid(gate2))
            o_ref[rows, h * HEAD_DIM:(h + 1) * HEAD_DIM] = o[:c].astype(BF16)
            o_ref[rows, (h + 1) * HEAD_DIM:(h + 2) * HEAD_DIM] = o[c:].astype(BF16)


def _delta_rule(q, k, v, gate, gb, norm_w_row, batch, seq, riders=()):
    t = q.shape[0]
    width = N_HEADS * HEAD_DIM
    rows = ROWS_DELTA
    ns = seq // rows
    row_map = lambda b, s: (b * ns + s, 0)
    const = lambda b, s: (0, 0)
    wide = pl.BlockSpec((rows, width), row_map)
    rider_specs, rider_shapes = _rider_specs(riders, batch * ns, row_map)
    return pl.pallas_call(
        functools.partial(_delta_kernel, len(riders)),
        grid=(batch, ns),
        in_specs=[wide, wide, wide, wide,
                  pl.BlockSpec((rows, LANES), row_map),
                  pl.BlockSpec((1, HEAD_DIM), const)] + rider_specs,
        out_specs=[wide] + rider_specs,
        out_shape=[jax.ShapeDtypeStruct((t, width), BF16)] + rider_shapes,
        scratch_shapes=[pltpu.VMEM((N_HEADS, HEAD_DIM, HEAD_DIM), F32)],
        compiler_params=_params("arbitrary", "arbitrary"),
        name="dn_delta_rule",
    )(q, k, v, gate, gb, norm_w_row, *riders)


def _proj_ln_kernel(a_ref, w_ref, res_ref, g_ref, b_ref, o_ref):
    for rb in range(a_ref.shape[0] // ROWS_LN_SUB):
        rows = slice(rb * ROWS_LN_SUB, (rb + 1) * ROWS_LN_SUB)
        y = DEEPNORM_ALPHA * res_ref[rows, :] + _dot(a_ref[rows, :], w_ref[...])
        o_ref[rows, :] = _layer_norm(y, g_ref[...], b_ref[...])


def _rows_to_tiles(ref, base, y):
    for k in range(D_MODEL // LANES):
        ref[pl.ds(base + k, y.shape[0], stride=SUBLANES), :] = y[:, k * LANES:(k + 1) * LANES]


def _tiles_to_rows(ref, base, n):
    return jnp.concatenate([ref[pl.ds(base + k, n, stride=SUBLANES), :] for k in range(D_MODEL // LANES)],
                           axis=1)


def _proj_ln_router_kernel(a_ref, w_ref, res_ref, g_ref, b_ref, wr_ref, o_ref, ot_ref, route_ref):
    for rb in range(a_ref.shape[0] // ROWS_LN_SUB):
        rows = slice(rb * ROWS_LN_SUB, (rb + 1) * ROWS_LN_SUB)
        y = DEEPNORM_ALPHA * res_ref[rows, :] + _dot(a_ref[rows, :], w_ref[...])
        xn = _layer_norm(y, g_ref[...], b_ref[...])
        o_ref[rows, :] = xn
        _rows_to_tiles(ot_ref, rb * ROWS_LN_SUB * SUBLANES, xn)
        logits = _dot_split(xn, wr_ref[...], _dot)
        lane = lax.broadcasted_iota(jnp.int32, logits.shape, 1)
        logits = jnp.where(lane < N_EXPERTS, logits, NEG_INF)
        l1 = jnp.max(logits, axis=-1, keepdims=True)
        i1 = jnp.min(jnp.where(logits == l1, lane, LANES), axis=-1, keepdims=True)
        rest = jnp.where(lane == i1, NEG_INF, logits)
        l2 = jnp.max(rest, axis=-1, keepdims=True)
        i2 = jnp.min(jnp.where(rest == l2, lane, LANES), axis=-1, keepdims=True)
        e2 = jnp.exp(l2 - l1)
        g1 = 1.0 / (1.0 + e2)
        g2 = e2 / (1.0 + e2)
        route_ref[rows, :] = jnp.where(lane == 0, g1,
                             jnp.where(lane == 1, g2,
                             jnp.where(lane == 2, i1.astype(F32),
                             jnp.where(lane == 3, i2.astype(F32), 0.0))))


def _proj_ln(a, w, res, g_row, b_row, w_router=None):
    t, kdim = a.shape
    rows = ROWS_LN
    row_map = lambda i: (i, 0)
    const = lambda i: (0, 0)
    in_specs = [pl.BlockSpec((rows, kdim), row_map),
                pl.BlockSpec((kdim, D_MODEL), const),
                pl.BlockSpec((rows, D_MODEL), row_map),
                pl.BlockSpec((1, D_MODEL), const),
                pl.BlockSpec((1, D_MODEL), const)]
    out_full = jax.ShapeDtypeStruct((t, D_MODEL), F32)
    if w_router is None:
        return pl.pallas_call(
            _proj_ln_kernel, grid=(t // rows,), in_specs=in_specs,
            out_specs=pl.BlockSpec((rows, D_MODEL), row_map), out_shape=out_full,
            compiler_params=_params("arbitrary"), name="proj_ln",
        )(a, w, res, g_row, b_row)
    return pl.pallas_call(
        _proj_ln_router_kernel, grid=(t // rows,),
        in_specs=in_specs + [pl.BlockSpec((D_MODEL, LANES), const)],
        out_specs=[pl.BlockSpec((rows, D_MODEL), row_map),
                   pl.BlockSpec((rows * SUBLANES, LANES), row_map),
                   pl.BlockSpec((rows, LANES), row_map)],
        out_shape=[out_full, jax.ShapeDtypeStruct((t * SUBLANES, LANES), F32),
                   jax.ShapeDtypeStruct((t, LANES), F32)],
        compiler_params=_params("arbitrary"), name="proj_ln_router",
    )(a, w, res, g_row, b_row, w_router)


def _swiglu_part(x, w1_ref, w3_ref, w2_ref, hid_sc):
    xb = x.astype(BF16)
    sub = 2 * LANES
    for c in range(hid_sc.shape[1] // sub):
        cols = slice(c * sub, (c + 1) * sub)
        h1 = _dot(xb, w1_ref[:, cols])
        h3 = _dot(xb, w3_ref[:, cols])
        hid_sc[:, cols] = (h1 * _sigmoid(h1) * h3).astype(BF16)
    return _dot(hid_sc[...], w2_ref[...])


def _ffn_dense_kernel(x_ref, w1_ref, w3_ref, w2_ref, g_ref, b_ref, o_ref, acc_sc, hid_sc):
    f = pl.program_id(1)
    part = _swiglu_part(x_ref[...], w1_ref, w3_ref, w2_ref, hid_sc)

    @pl.when(f == 0)
    def _():
        acc_sc[...] = part

    @pl.when((f > 0) & (f < pl.num_programs(1) - 1))
    def _():
        acc_sc[...] += part

    @pl.when(f == pl.num_programs(1) - 1)
    def _():
        y = DEEPNORM_ALPHA * x_ref[...] + (acc_sc[...] + part)
        o_ref[...] = _layer_norm(y, g_ref[...], b_ref[...])


def _ffn_dense(x, w1, w3, w2, g_row, b_row):
    t = x.shape[0]
    rows, fc = ROWS_FFN, FFN_CHUNK
    nf = FFN_HIDDEN // fc
    return pl.pallas_call(
        _ffn_dense_kernel,
        grid=(t // rows, nf),
        in_specs=[pl.BlockSpec((rows, D_MODEL), lambda i, f: (i, 0)),
                  pl.BlockSpec((D_MODEL, fc), lambda i, f: (0, f)),
                  pl.BlockSpec((D_MODEL, fc), lambda i, f: (0, f)),
                  pl.BlockSpec((fc, D_MODEL), lambda i, f: (f, 0)),
                  pl.BlockSpec((1, D_MODEL), lambda i, f: (0, 0)),
                  pl.BlockSpec((1, D_MODEL), lambda i, f: (0, 0))],
        out_specs=pl.BlockSpec((rows, D_MODEL), lambda i, f: (i, 0)),
        out_shape=jax.ShapeDtypeStruct((t, D_MODEL), F32),
        scratch_shapes=[pltpu.VMEM((rows, D_MODEL), F32), pltpu.VMEM((rows, fc), BF16)],
        compiler_params=_params("arbitrary", "arbitrary"),
        name="ffn_dense",
    )(x, w1, w3, w2, g_row, b_row)


def _moe_experts_kernel(nu_ref, te_ref, idx_ref, idx_next_ref, dst_ref, x_hbm, w1_ref, w3_ref, w2_ref,
                        y_hbm, xbuf, ybuf, xb_sc, acc_sc, hid_sc, gsem, ssem):
    i = pl.program_id(0)
    f = pl.program_id(1)
    last_f = pl.num_programs(1) - 1
    rows = xb_sc.shape[0]
    tile_rows = rows * SUBLANES
    nu = nu_ref[0]
    slot = i % 2

    def gather_start(idx, s):
        for r in range(rows):
            src = x_hbm.at[pl.ds(pl.multiple_of(idx[0, r], SUBLANES), SUBLANES)]
            pltpu.make_async_copy(src, xbuf.at[pl.ds(s * tile_rows + r * SUBLANES, SUBLANES)], gsem.at[s]).start()

    def scatter_start(s):
        for r in range(rows):
            dst = y_hbm.at[pl.ds(pl.multiple_of(dst_ref[0, r], SUBLANES), SUBLANES)]
            pltpu.make_async_copy(ybuf.at[pl.ds(s * tile_rows + r * SUBLANES, SUBLANES)], dst, ssem.at[s]).start()

    def wait_all(buf, sem, s):
        view = buf.at[pl.ds(pl.multiple_of(s * tile_rows, tile_rows), tile_rows)]
        pltpu.make_async_copy(view, view, sem.at[s]).wait()

    def for_slot(cond, s, fn):
        for static_slot in range(2):
            pl.when(cond & (s == static_slot))(functools.partial(fn, static_slot))

    first = f == 0

    @pl.when(first & (i == 0))
    def _():
        n_real = TOP_K * x_hbm.shape[0]
        ybuf[tile_rows:2 * tile_rows, :] = jnp.zeros((tile_rows, LANES), F32)
        fills = [pltpu.make_async_copy(ybuf.at[pl.ds(tile_rows, tile_rows)],
                                       y_hbm.at[pl.ds(n_real + k * tile_rows, tile_rows)], ssem.at[1])
                 for k in range((y_hbm.shape[0] - n_real) // tile_rows)]
        for cp in fills:
            cp.start()
        for cp in fills:
            cp.wait()

    @pl.when(first & (i == 0) & (nu > 0))
    def _():
        gather_start(idx_ref, 0)

    @pl.when(first & (i < nu))
    def _():
        wait_all(xbuf, gsem, slot)

    for_slot(first & (i + 1 < nu), 1 - slot, functools.partial(gather_start, idx_next_ref))

    @pl.when(i < nu)
    def _():
        @pl.when(first)
        def _():
            base = pl.multiple_of(slot * tile_rows, tile_rows)
            for k in range(D_MODEL // LANES):
                xb_sc[:, k * LANES:(k + 1) * LANES] = xbuf[pl.ds(base + k, rows, stride=SUBLANES), :].astype(BF16)

        part = _swiglu_part(xb_sc[...], w1_ref, w3_ref, w2_ref, hid_sc)

        @pl.when(first)
        def _():
            acc_sc[...] = part

        @pl.when((f > 0) & (f < last_f))
        def _():
            acc_sc[...] += part

        @pl.when(f == last_f)
        def _():
            @pl.when(i >= 2)
            def _():
                wait_all(ybuf, ssem, slot)
            _rows_to_tiles(ybuf, pl.multiple_of(slot * tile_rows, tile_rows), acc_sc[...] + part)

        for_slot(f == last_f, slot, scatter_start)

    @pl.when((i == pl.num_programs(0) - 1) & (f == last_f))
    def _():
        @pl.when(nu >= 1)
        def _():
            wait_all(ybuf, ssem, (nu - 1) % 2)

        @pl.when(nu >= 2)
        def _():
            wait_all(ybuf, ssem, nu % 2)


def _moe_experts(n_used, tile_expert, row_token, row_dst, x_tiles, w1, w3, w2, n_out_rows):
    n_tiles = tile_expert.shape[0]
    rows, fc = ROWS_FFN, FFN_CHUNK
    nf = FFN_HIDDEN // fc
    idx3 = (row_token * SUBLANES).reshape(n_tiles, 1, rows)
    dst3 = (row_dst * SUBLANES).reshape(n_tiles, 1, rows)
    smem_tile = lambda fn: pl.BlockSpec((None, 1, rows), fn, memory_space=pltpu.SMEM)
    grid_spec = pltpu.PrefetchScalarGridSpec(
        num_scalar_prefetch=2,
        grid=(n_tiles, nf),
        in_specs=[smem_tile(lambda i, f, nu, te: (i, 0, 0)),
                  smem_tile(lambda i, f, nu, te: (jnp.minimum(i + 1, n_tiles - 1), 0, 0)),
                  smem_tile(lambda i, f, nu, te: (i, 0, 0)),
                  pl.BlockSpec(memory_space=pl.ANY),
                  pl.BlockSpec((None, D_MODEL, fc), lambda i, f, nu, te: (te[i], 0, f)),
                  pl.BlockSpec((None, D_MODEL, fc), lambda i, f, nu, te: (te[i], 0, f)),
                  pl.BlockSpec((None, fc, D_MODEL), lambda i, f, nu, te: (te[i], f, 0))],
        out_specs=pl.BlockSpec(memory_space=pl.ANY),
        scratch_shapes=[pltpu.VMEM((2 * rows * SUBLANES, LANES), F32), pltpu.VMEM((2 * rows * SUBLANES, LANES), F32),
                        pltpu.VMEM((rows, D_MODEL), BF16),
                        pltpu.VMEM((rows, D_MODEL), F32), pltpu.VMEM((rows, fc), BF16),
                        pltpu.SemaphoreType.DMA((2,)), pltpu.SemaphoreType.DMA((2,))],
    )
    return pl.pallas_call(
        _moe_experts_kernel,
        grid_spec=grid_spec,
        out_shape=jax.ShapeDtypeStruct((n_out_rows * SUBLANES, LANES), F32),
        compiler_params=_params("arbitrary", "arbitrary"),
        name="moe_experts",
    )(n_used, tile_expert, idx3, idx3, dst3, x_tiles, w1, w3, w2)


def _qkv_kernel(x_ref, w_ref, cos_ref, sin_ref, q_ref, k_ref, vt_ref, sel_ref, km_sc, qf_sc):
    i = pl.program_id(1)
    width = N_HEADS * HEAD_DIM
    half = HEAD_DIM // 2
    xb = x_ref[...].astype(BF16)
    cosf = cos_ref[...]
    sinf = sin_ref[...]

    @pl.when(i == 0)
    def _():
        km_sc[...] = jnp.zeros(km_sc.shape, F32)

    def rope(t):
        return t * cosf + pltpu.roll(t, half, axis=1) * sinf

    chunk = 2 * HEAD_DIM
    for c in range(width // chunk):
        kk = _dot(xb, w_ref[:, c * chunk:(c + 1) * chunk])
        vv = _dot(xb, w_ref[:, width + c * chunk:width + (c + 1) * chunk])
        qq = _dot(xb, w_ref[:, 2 * width + c * chunk:2 * width + (c + 1) * chunk])
        for hh in range(2):
            h = 2 * c + hh
            cols = slice(h * HEAD_DIM, (h + 1) * HEAD_DIM)
            part = slice(hh * HEAD_DIM, (hh + 1) * HEAD_DIM)
            kr = rope(kk[:, part])
            k_ref[:, cols] = kr.astype(BF16)
            km_row = lax.broadcasted_iota(jnp.int32, (km_sc.shape[0], HEAD_DIM), 0)
            km_sc[:, cols] = jnp.where(km_row == i, jnp.mean(kr, axis=0, keepdims=True), km_sc[:, cols])
            qr = rope(qq[:, part])
            qf_sc[:, cols] = qr
            q_ref[:, cols] = (qr * (HEAD_DIM ** -0.5 * LOG2_E)).astype(BF16)
            vt_ref[h] = vv[:, part].T.astype(BF16)

    nb = km_sc.shape[0]
    km = km_sc[...]
    lane_head = lax.broadcasted_iota(jnp.int32, km.shape, 1) // HEAD_DIM
    km_rows = jnp.concatenate([jnp.where(lane_head == h, km, 0.0) for h in range(N_HEADS)], axis=0)
    gate_t = _dot_split(km_rows, qf_sc[...], _dot_nt)
    blk = lax.broadcasted_iota(jnp.int32, (nb, x_ref.shape[0]), 0)
    past = blk < i
    for h in range(N_HEADS):
        g = jnp.where(past, gate_t[h * nb:(h + 1) * nb, :], NEG_INF)
        rank = jnp.zeros(g.shape, jnp.int32)
        for n in range(nb):
            gn = g[n:n + 1, :]
            ahead = (gn > g) | ((gn == g) & (blk > n))
            rank = rank + ahead.astype(jnp.int32)
        sel_ref[h * nb:(h + 1) * nb, :] = jnp.where(past & (rank < MOBA_TOPK), 0.0, NEG_INF)


def _qkv_proj(x, w_kvq, cos_full, sin_signed, batch, seq):
    t = x.shape[0]
    width = N_HEADS * HEAD_DIM
    rows = MOBA_BLOCK
    nb = seq // rows
    row_map = lambda b, i: (b * nb + i, 0)
    act = jax.ShapeDtypeStruct((t, width), BF16)
    return pl.pallas_call(
        _qkv_kernel,
        grid=(batch, nb),
        in_specs=[pl.BlockSpec((rows, D_MODEL), row_map),
                  pl.BlockSpec((D_MODEL, 3 * width), lambda b, i: (0, 0)),
                  pl.BlockSpec((rows, HEAD_DIM), lambda b, i: (i, 0)),
                  pl.BlockSpec((rows, HEAD_DIM), lambda b, i: (i, 0))],
        out_specs=[pl.BlockSpec((rows, width), row_map),
                   pl.BlockSpec((rows, width), row_map),
                   pl.BlockSpec((None, N_HEADS, HEAD_DIM, rows), lambda b, i: (b, 0, 0, i)),
                   pl.BlockSpec((None, N_HEADS * nb, rows), lambda b, i: (b * nb + i, 0, 0))],
        out_shape=[act, act,
                   jax.ShapeDtypeStruct((batch, N_HEADS, HEAD_DIM, seq), BF16),
                   jax.ShapeDtypeStruct((batch * nb, N_HEADS * nb, rows), F32)],
        scratch_shapes=[pltpu.VMEM((nb, width), F32), pltpu.VMEM((rows, width), F32)],
        compiler_params=_params("arbitrary", "arbitrary"),
        name="moba_qkv_proj",
    )(x, w_kvq, cos_full, sin_signed)


def _attn_kernel(q_ref, k_ref, vt_ref, sel_ref, o_ref):
    blk = MOBA_BLOCK
    nb = q_ref.shape[0] // blk
    key = lax.broadcasted_iota(jnp.int32, (blk, blk), 0)
    qry = lax.broadcasted_iota(jnp.int32, (blk, blk), 1)
    causal = key <= qry

    def scores(i):
        return _dot_nt(k_ref[0:(i + 1) * blk, :], q_ref[i * blk:(i + 1) * blk, :])

    def finish(i, s):
        parts = [s[j * blk:(j + 1) * blk] + sel_ref[i, j:j + 1, :] for j in range(i)]
        parts.append(jnp.where(causal, s[i * blk:(i + 1) * blk], NEG_INF))
        m = jnp.max(parts[0], axis=0, keepdims=True)
        for part in parts[1:]:
            m = jnp.maximum(m, jnp.max(part, axis=0, keepdims=True))
        probs = [jnp.exp2(part - m) for part in parts]
        l = jnp.sum(probs[0], axis=0, keepdims=True)
        for pr in probs[1:]:
            l = l + jnp.sum(pr, axis=0, keepdims=True)
        p = jnp.concatenate([pr.astype(BF16) for pr in probs], axis=0)
        acc = _dot(vt_ref[:, 0:(i + 1) * blk], p)
        o_ref[i * blk:(i + 1) * blk, :] = (acc / l).T.astype(BF16)

    s_cur = scores(0)
    for i in range(nb):
        s_next = scores(i + 1) if i + 1 < nb else None
        finish(i, s_cur)
        s_cur = s_next


def _moba_attention(q, k, vt, sel, batch, seq):
    t = q.shape[0]
    width = N_HEADS * HEAD_DIM
    nb = seq // MOBA_BLOCK
    col = pl.BlockSpec((seq, HEAD_DIM), lambda b, h: (b, h))
    return pl.pallas_call(
        _attn_kernel,
        grid=(batch, N_HEADS),
        in_specs=[col, col,
                  pl.BlockSpec((None, None, HEAD_DIM, seq), lambda b, h: (b, h, 0, 0)),
                  pl.BlockSpec((nb, nb, MOBA_BLOCK), lambda b, h: (b, h, 0))],
        out_specs=col,
        out_shape=jax.ShapeDtypeStruct((t, width), BF16),
        compiler_params=_params("arbitrary", "arbitrary"),
        name="moba_attention",
    )(q, k, vt, sel)


def _combine_kernel(y0_ref, y1_ref, x_ref, route_ref, g_ref, b_ref, o_ref):
    route = route_ref[...]
    rows = x_ref.shape[0]
    y = (DEEPNORM_ALPHA * x_ref[...] + route[:, 0:1] * _tiles_to_rows(y0_ref, 0, rows)
         + route[:, 1:2] * _tiles_to_rows(y1_ref, 0, rows))
    o_ref[...] = _layer_norm(y, g_ref[...], b_ref[...])


def _moe_combine(ys, x, route, g_row, b_row):
    t = x.shape[0]
    rows = ROWS_COMBINE
    nt = t // rows
    row_map = lambda i: (i, 0)
    const = lambda i: (0, 0)
    return pl.pallas_call(
        _combine_kernel,
        grid=(nt,),
        in_specs=[pl.BlockSpec((rows * SUBLANES, LANES), row_map),
                  pl.BlockSpec((rows * SUBLANES, LANES), lambda i: (nt + i, 0)),
                  pl.BlockSpec((rows, D_MODEL), row_map),
                  pl.BlockSpec((rows, LANES), row_map),
                  pl.BlockSpec((1, D_MODEL), const),
                  pl.BlockSpec((1, D_MODEL), const)],
        out_specs=pl.BlockSpec((rows, D_MODEL), row_map),
        out_shape=jax.ShapeDtypeStruct((t, D_MODEL), F32),
        compiler_params=_params("arbitrary"),
        name="moe_combine",
    )(ys, ys, x, route, g_row, b_row)


def _routing_tables(route, n_tokens):
    rows = ROWS_FFN
    n_assign = n_tokens * TOP_K
    n_tiles = n_assign // rows + N_EXPERTS
    experts = route[:, 2:4].astype(jnp.int32).reshape(-1)
    onehot = (experts[:, None] == jnp.arange(N_EXPERTS, dtype=jnp.int32)[None, :]).astype(jnp.int32)
    running = jnp.cumsum(onehot, axis=0)
    counts = running[-1]
    rank = jnp.sum(running * onehot, axis=1) - 1
    tiles_per = (counts + rows - 1) // rows
    tile_end = jnp.cumsum(tiles_per)
    group_start = (tile_end - tiles_per) * rows
    pos = group_start[experts] + rank
    assign = jnp.full((n_tiles * rows,), -1, jnp.int32).at[pos].set(
        jnp.arange(n_assign, dtype=jnp.int32), unique_indices=True, mode="promise_in_bounds")
    real = assign >= 0
    row_token = jnp.where(real, assign // TOP_K, 0)
    pad_rank = jnp.minimum(jnp.cumsum(1 - real.astype(jnp.int32)) - 1, N_EXPERTS * rows - 1)
    row_dst = jnp.where(real, (assign % TOP_K) * n_tokens + assign // TOP_K, n_assign + pad_rank)
    tile_id = jnp.arange(n_tiles, dtype=jnp.int32)
    n_used = tile_end[-1]
    tile_expert = jnp.sum((tile_id[:, None] >= tile_end[None, :]).astype(jnp.int32), axis=1)
    last_expert = jnp.sum((n_used - 1 >= tile_end).astype(jnp.int32))
    tile_expert = jnp.minimum(jnp.where(tile_id < n_used, tile_expert, last_expert), N_EXPERTS - 1)
    return n_used.reshape(1), tile_expert, row_token, row_dst, n_assign + N_EXPERTS * rows


def _rope_tables(seq):
    half = HEAD_DIM // 2
    inv_freq = ROPE_THETA ** (-jnp.arange(half, dtype=F32) / half)
    ang = jnp.arange(seq).astype(F32)[:, None] * inv_freq[None, :]
    cos, sin = jnp.cos(ang), jnp.sin(ang)
    return jnp.concatenate([cos, cos], axis=1), jnp.concatenate([-sin, sin], axis=1)


def _pad_lanes(w):
    return jnp.pad(w, ((0, 0), (0, LANES - w.shape[1])))


def kernel(x, a_w_in, a_conv_w, a_log_decay, a_dt_bias, a_norm_w, a_w_out, b_w_kv, b_w_q, b_w_o,
           ffn_w1, ffn_w3, ffn_w2, moe_router, moe_w1, moe_w3, moe_w2, ln_g, ln_b):
    batch, seq, _ = x.shape
    t = batch * seq
    width = N_HEADS * HEAD_DIM
    x0 = x.reshape(t, D_MODEL)
    row = lambda v: v.reshape(1, -1).astype(F32)

    w_in = a_w_in[0]
    q, k, v, gate, gb, moe_w1b, moe_w2b = _in_proj(
        x0, w_in[:, :4 * width].astype(BF16), _pad_lanes(w_in[:, 4 * width:]).astype(BF16),
        a_conv_w[0], _pad_lanes(row(a_log_decay[0])), _pad_lanes(row(a_dt_bias[0])), batch, seq,
        riders=(moe_w1[0].reshape(N_EXPERTS * D_MODEL, FFN_HIDDEN),
                moe_w2[0].reshape(N_EXPERTS * FFN_HIDDEN, D_MODEL)))
    og, moe_w3b = _delta_rule(q, k, v, gate, gb, row(a_norm_w[0]), batch, seq,
                              riders=(moe_w3[0].reshape(N_EXPERTS * D_MODEL, FFN_HIDDEN),))
    x1 = _proj_ln(og, a_w_out[0].astype(BF16), x0, row(ln_g[0, 0]), row(ln_b[0, 0]))
    x2 = _ffn_dense(x1, ffn_w1[0].astype(BF16), ffn_w3[0].astype(BF16), ffn_w2[0].astype(BF16),
                    row(ln_g[0, 1]), row(ln_b[0, 1]))

    cos_full, sin_signed = _rope_tables(seq)
    w_kvq = jnp.concatenate([b_w_kv, b_w_q[0]], axis=1).astype(BF16)
    qm, km, vt, sel = _qkv_proj(x2, w_kvq, cos_full, sin_signed, batch, seq)
    attn = _moba_attention(qm, km, vt, sel, batch, seq)
    x3, x3_tiles, route = _proj_ln(attn, b_w_o[0].astype(BF16), x2, row(ln_g[1, 0]), row(ln_b[1, 0]),
                                   w_router=_pad_lanes(moe_router[0]))
    n_used, tile_expert, row_token, row_dst, n_out_rows = _routing_tables(route, t)
    ys = _moe_experts(n_used, tile_expert, row_token, row_dst, x3_tiles,
                      moe_w1b.reshape(N_EXPERTS, D_MODEL, FFN_HIDDEN),
                      moe_w3b.reshape(N_EXPERTS, D_MODEL, FFN_HIDDEN),
                      moe_w2b.reshape(N_EXPERTS, FFN_HIDDEN, D_MODEL), n_out_rows)
    x4 = _moe_combine(ys, x3, route, row(ln_g[1, 1]), row(ln_b[1, 1]))
    return x4.reshape(batch, seq, D_MODEL)
```

```python
import functools
import math

import jax
import jax.numpy as jnp
from jax import lax
from jax.experimental import pallas as pl
from jax.experimental.pallas import tpu as pltpu

D_MODEL = 1024
DEPTH = 2
DEEPNORM_ALPHA = (2.0 * DEPTH) ** 0.25
LN_EPS = 1e-5
N_HEADS = 8
HEAD_DIM = 128
DN_CONV = 4
DN_CHUNK = 64
DN_NORM_EPS = 1e-6
MOBA_BLOCK = 256
MOBA_TOPK = 3
ROPE_THETA = 10000.0
NEG_INF = -1e30
LOG2_E = math.log2(math.e)
FFN_HIDDEN = 3584
N_EXPERTS = 8
TOP_K = 2

LANES = 128
SUBLANES = 8
VMEM_LIMIT_BYTES = 56 * 1024 * 1024

ROWS_PROJ = 256
ROWS_DELTA = 256
ROWS_LN = 512
ROWS_LN_SUB = 256
ROWS_FFN = 512
FFN_CHUNK = 1792
ROWS_COMBINE = 512

F32 = jnp.float32
BF16 = jnp.bfloat16

_NT = (((1,), (1,)), ((), ()))


def _dot(a, b):
    return jnp.dot(a, b, preferred_element_type=F32)


def _dot_nt(a, b):
    return lax.dot_general(a, b, _NT, preferred_element_type=F32)


def _params(*sem):
    return pltpu.CompilerParams(dimension_semantics=sem, vmem_limit_bytes=VMEM_LIMIT_BYTES)


def _layer_norm(y, g, b):
    mu = jnp.mean(y, axis=-1, keepdims=True)
    d = y - mu
    var = jnp.mean(d * d, axis=-1, keepdims=True)
    return d * lax.rsqrt(var + LN_EPS) * g + b


def _sigmoid(x):
    return 1.0 / (1.0 + jnp.exp(-x))


def _split2(x):
    hi = x.astype(BF16)
    lo = (x - hi.astype(F32)).astype(BF16)
    return hi, lo


def _dot_split(a, b, dot):
    a_hi, a_lo = _split2(a)
    b_hi, b_lo = _split2(b)
    return dot(a_hi, b_hi) + dot(a_hi, b_lo) + dot(a_lo, b_hi)


def _rider_specs(arrays, n_steps, index_map):
    specs = [pl.BlockSpec((a.shape[0] // n_steps, a.shape[1]), index_map) for a in arrays]
    return specs, [jax.ShapeDtypeStruct(a.shape, BF16) for a in arrays]


def _cast_riders(src_refs, dst_refs):
    for src, dst in zip(src_refs, dst_refs):
        dst[...] = src[...].astype(BF16)


def _in_proj_kernel(n_riders, x_ref, w_ref, wab_ref, cw_ref, alog_ref, dtb_ref, *rest):
    riders_in, rest = rest[:n_riders], rest[n_riders:]
    q_ref, k_ref, v_ref, gate_ref, gb_ref = rest[:5]
    riders_out, conv_sc = rest[5:5 + n_riders], rest[5 + n_riders]
    _cast_riders(riders_in, riders_out)
    rows = x_ref.shape[0]
    width = N_HEADS * HEAD_DIM
    hist = SUBLANES

    @pl.when(pl.program_id(1) == 0)
    def _():
        conv_sc[...] = jnp.zeros(conv_sc.shape, F32)

    xb = x_ref[...].astype(BF16)
    chunk = 2 * HEAD_DIM
    for c in range(3 * width // chunk):
        proj = _dot(xb, w_ref[:, c * chunk:(c + 1) * chunk])
        which = c * chunk // width
        for hh in range(chunk // HEAD_DIM):
            slab = c * (chunk // HEAD_DIM) + hh
            cols = slice(slab * HEAD_DIM, (slab + 1) * HEAD_DIM)
            cur = proj[:, hh * HEAD_DIM:(hh + 1) * HEAD_DIM]
            conv_sc[slab, pl.ds(2 * hist, rows, stride=2), :] = cur
            y = cur * cw_ref[DN_CONV - 1:DN_CONV, cols]
            for j in range(DN_CONV - 1):
                shifted = conv_sc[slab, pl.ds(2 * (hist - (DN_CONV - 1) + j), rows, stride=2), :]
                y = y + shifted * cw_ref[j:j + 1, cols]
            conv_sc[slab, 0:2 * hist, :] = conv_sc[slab, 2 * rows:2 * (rows + hist), :]
            t = y * _sigmoid(y)
            ocols = slice(slab * HEAD_DIM - which * width, (slab + 1) * HEAD_DIM - which * width)
            if which == 2:
                v_ref[:, ocols] = t.astype(BF16)
            else:
                inv_norm = lax.rsqrt(jnp.sum(t * t, axis=-1, keepdims=True) + DN_NORM_EPS)
                if which == 0:
                    q_ref[:, ocols] = (t * (inv_norm * (HEAD_DIM ** -0.5))).astype(BF16)
                else:
                    k_ref[:, ocols] = (t * inv_norm).astype(BF16)
        if c % 3 == 2:
            g0 = (c // 3) * chunk
            gate_ref[:, g0:g0 + chunk] = _dot(xb, w_ref[:, 3 * width + g0:3 * width + g0 + chunk]).astype(BF16)
    ab = _dot(xb, wab_ref[...])
    z = ab + dtb_ref[...]
    softplus = jnp.maximum(z, 0.0) + jnp.log(1.0 + jnp.exp(-jnp.abs(z)))
    g = -jnp.exp(alog_ref[...]) * softplus
    lane = lax.broadcasted_iota(jnp.int32, ab.shape, 1)
    gb_ref[...] = jnp.where(lane < N_HEADS, g, _sigmoid(ab))


def _in_proj(x2d, w_main, w_ab, conv_w, a_log_row, dt_row, batch, seq, riders=()):
    t = x2d.shape[0]
    width = N_HEADS * HEAD_DIM
    rows = ROWS_PROJ
    nt = seq // rows
    row_map = lambda b, s: (b * nt + s, 0)
    const = lambda b, s: (0, 0)
    act = jax.ShapeDtypeStruct((t, width), BF16)
    rider_specs, rider_shapes = _rider_specs(riders, batch * nt, row_map)
    return pl.pallas_call(
        functools.partial(_in_proj_kernel, len(riders)),
        grid=(batch, nt),
        in_specs=[pl.BlockSpec((rows, D_MODEL), row_map),
                  pl.BlockSpec((D_MODEL, 4 * width), const),
                  pl.BlockSpec((D_MODEL, LANES), const),
                  pl.BlockSpec((DN_CONV, 3 * width), const),
                  pl.BlockSpec((1, LANES), const),
                  pl.BlockSpec((1, LANES), const)] + rider_specs,
        out_specs=[pl.BlockSpec((rows, width), row_map)] * 4 + [pl.BlockSpec((rows, LANES), row_map)]
        + rider_specs,
        out_shape=[act, act, act, act, jax.ShapeDtypeStruct((t, LANES), F32)] + rider_shapes,
        scratch_shapes=[pltpu.VMEM((3 * width // LANES, 2 * (rows + SUBLANES), LANES), F32)],
        compiler_params=_params("arbitrary", "arbitrary"),
        name="dn_in_proj",
    )(x2d, w_main, w_ab, conv_w, a_log_row, dt_row, *riders)


def _delta_kernel(n_riders, q_ref, k_ref, v_ref, gate_ref, gb_ref, nw_ref, *rest):
    riders_in, o_ref = rest[:n_riders], rest[n_riders]
    riders_out, state_sc = rest[n_riders + 1:2 * n_riders + 1], rest[2 * n_riders + 1]
    _cast_riders(riders_in, riders_out)
    c = DN_CHUNK
    two = 2 * c
    n_sub = q_ref.shape[0] // c
    n_pairs = N_HEADS // 2

    @pl.when(pl.program_id(1) == 0)
    def _():
        state_sc[...] = jnp.zeros(state_sc.shape, F32)

    r = lax.broadcasted_iota(jnp.int32, (two, two), 0)
    cc = lax.broadcasted_iota(jnp.int32, (two, two), 1)
    same = (r < c) == (cc < c)
    tri = same & (r >= cc)
    stri = same & (r > cc)
    eye = (r == cc).astype(F32)
    top_rows = lax.broadcasted_iota(jnp.int32, (two, 1), 0) < c
    top_lanes = lax.broadcasted_iota(jnp.int32, (1, two), 1) < c
    row64 = lax.broadcasted_iota(jnp.int32, (c, LANES), 0)

    def stack(ref, rows, h):
        t = ref[rows, h * HEAD_DIM:(h + 2) * HEAD_DIM].astype(F32)
        return jnp.concatenate([t[:, :HEAD_DIM], t[:, HEAD_DIM:]], axis=0)

    items = []
    for s in range(n_sub):
        rows = slice(s * c, (s + 1) * c)
        gb = gb_ref[rows, :]
        gc = gb
        sh = 1
        while sh < c:
            gc = gc + jnp.where(row64 >= sh, pltpu.roll(gc, sh, axis=0), 0.0)
            sh *= 2
        gc_pair = jnp.concatenate([gc, pltpu.roll(gc, LANES - 1, axis=1)], axis=0)
        gb_pair = jnp.concatenate([gb, pltpu.roll(gb, LANES - 1, axis=1)], axis=0)
        gc_pair_t = gc_pair.T
        for p in range(n_pairs):
            h = 2 * p
            q2, k2, v2 = stack(q_ref, rows, h), stack(k_ref, rows, h), stack(v_ref, rows, h)
            gcol = gc_pair[:, h:h + 1]
            grow = gc_pair_t[h:h + 1, :]
            beta = gb_pair[:, N_HEADS + h:N_HEADS + h + 1]
            glast = jnp.where(top_rows, gc_pair[c - 1:c, h:h + 1], gc_pair[two - 1:two, h:h + 1])
            eg = jnp.exp(gcol)
            kb = k2 * beta
            items.append(dict(
                rows=rows, h=h, k2b=k2.astype(BF16), q2b=q2.astype(BF16), kbb=kb.astype(BF16),
                decay=jnp.exp(jnp.where(tri, gcol - grow, NEG_INF)),
                rhs=jnp.concatenate([v2 * beta, kb * eg], axis=1).astype(BF16),
                qd=(q2 * eg).astype(BF16),
                kd_t=(k2 * jnp.exp(glast - gcol)).T,
                eglast=jnp.exp(glast)))
    for it in items:
        it["pw"] = jnp.where(stri, _dot_nt(it["kbb"], it["k2b"]) * it["decay"], 0.0)
        it["inv"] = eye - it["pw"]
    for _ in range(int(math.log2(c)) - 1):
        for it in items:
            pwb = it["pw"].astype(BF16)
            it["pw"] = _dot(pwb, pwb)
        for it in items:
            it["inv"] = it["inv"] + _dot(it["inv"].astype(BF16), it["pw"].astype(BF16))
    for it in items:
        sol = _dot(it["inv"].astype(BF16), it["rhs"])
        it["u"] = sol[:, :HEAD_DIM]
        it["wb"] = sol[:, HEAD_DIM:].astype(BF16)
    for it in items:
        it["amat"] = jnp.where(tri, _dot_nt(it["q2b"], it["k2b"]) * it["decay"], 0.0).astype(BF16)

    for s in range(n_sub):
        group = items[s * n_pairs:(s + 1) * n_pairs]
        for it in group:
            h = it["h"]
            it["s0"] = state_sc[h]
            it["s1"] = state_sc[h + 1]
            it["r0"] = _dot(jnp.concatenate([it["wb"][:c], it["qd"][:c]], axis=0), it["s0"].astype(BF16))
            it["r1"] = _dot(jnp.concatenate([it["wb"][c:], it["qd"][c:]], axis=0), it["s1"].astype(BF16))
        for it in group:
            v_new = it["u"] - jnp.concatenate([it["r0"][:c], it["r1"][:c]], axis=0)
            it["vnb"] = v_new.astype(BF16)
            it["o"] = jnp.concatenate([it["r0"][c:], it["r1"][c:]], axis=0) + _dot(it["amat"], it["vnb"])
        for it in group:
            h = it["h"]
            kd_t = it["kd_t"]
            state_sc[h] = it["s0"] * it["eglast"][0:1, :] + _dot(
                jnp.where(top_lanes, kd_t, 0.0).astype(BF16), it["vnb"])
            state_sc[h + 1] = it["s1"] * it["eglast"][c:c + 1, :] + _dot(
                jnp.where(top_lanes, 0.0, kd_t).astype(BF16), it["vnb"])
        for it in group:
            h, rows, o = it["h"], it["rows"], it["o"]
            gate2 = stack(gate_ref, rows, h)
            o = o * lax.rsqrt(jnp.mean(o * o, axis=-1, keepdims=True) + DN_NORM_EPS) * nw_ref[...]
            o = o * (gate2 * _sigmoid(gate2))
            o_ref[rows, h * HEAD_DIM:(h + 1) * HEAD_DIM] = o[:c].astype(BF16)
            o_ref[rows, (h + 1) * HEAD_DIM:(h + 2) * HEAD_DIM] = o[c:].astype(BF16)


def _delta_rule(q, k, v, gate, gb, norm_w_row, batch, seq, riders=()):
    t = q.shape[0]
    width = N_HEADS * HEAD_DIM
    rows = ROWS_DELTA
    ns = seq // rows
    row_map = lambda b, s: (b * ns + s, 0)
    const = lambda b, s: (0, 0)
    wide = pl.BlockSpec((rows, width), row_map)
    rider_specs, rider_shapes = _rider_specs(riders, batch * ns, row_map)
    return pl.pallas_call(
        functools.partial(_delta_kernel, len(riders)),
        grid=(batch, ns),
        in_specs=[wide, wide, wide, wide,
                  pl.BlockSpec((rows, LANES), row_map),
                  pl.BlockSpec((1, HEAD_DIM), const)] + rider_specs,
        out_specs=[wide] + rider_specs,
        out_shape=[jax.ShapeDtypeStruct((t, width), BF16)] + rider_shapes,
        scratch_shapes=[pltpu.VMEM((N_HEADS, HEAD_DIM, HEAD_DIM), F32)],
        compiler_params=_params("arbitrary", "arbitrary"),
        name="dn_delta_rule",
    )(q, k, v, gate, gb, norm_w_row, *riders)


def _proj_ln_kernel(a_ref, w_ref, res_ref, g_ref, b_ref, o_ref):
    for rb in range(a_ref.shape[0] // ROWS_LN_SUB):
        rows = slice(rb * ROWS_LN_SUB, (rb + 1) * ROWS_LN_SUB)
        y = DEEPNORM_ALPHA * res_ref[rows, :] + _dot(a_ref[rows, :], w_ref[...])
        o_ref[rows, :] = _layer_norm(y, g_ref[...], b_ref[...])


def _rows_to_tiles(ref, base, y):
    for k in range(D_MODEL // LANES):
        ref[pl.ds(base + k, y.shape[0], stride=SUBLANES), :] = y[:, k * LANES:(k + 1) * LANES]


def _tiles_to_rows(ref, base, n):
    return jnp.concatenate([ref[pl.ds(base + k, n, stride=SUBLANES), :] for k in range(D_MODEL // LANES)],
                           axis=1)


def _proj_ln_router_kernel(a_ref, w_ref, res_ref, g_ref, b_ref, wr_ref, o_ref, ot_ref, route_ref):
    for rb in range(a_ref.shape[0] // ROWS_LN_SUB):
        rows = slice(rb * ROWS_LN_SUB, (rb + 1) * ROWS_LN_SUB)
        y = DEEPNORM_ALPHA * res_ref[rows, :] + _dot(a_ref[rows, :], w_ref[...])
        xn = _layer_norm(y, g_ref[...], b_ref[...])
        o_ref[rows, :] = xn
        _rows_to_tiles(ot_ref, rb * ROWS_LN_SUB * SUBLANES, xn)
        logits = _dot_split(xn, wr_ref[...], _dot)
        lane = lax.broadcasted_iota(jnp.int32, logits.shape, 1)
        logits = jnp.where(lane < N_EXPERTS, logits, NEG_INF)
        l1 = jnp.max(logits, axis=-1, keepdims=True)
        i1 = jnp.min(jnp.where(logits == l1, lane, LANES), axis=-1, keepdims=True)
        rest = jnp.where(lane == i1, NEG_INF, logits)
        l2 = jnp.max(rest, axis=-1, keepdims=True)
        i2 = jnp.min(jnp.where(rest == l2, lane, LANES), axis=-1, keepdims=True)
        e2 = jnp.exp(l2 - l1)
        g1 = 1.0 / (1.0 + e2)
        g2 = e2 / (1.0 + e2)
        route_ref[rows, :] = jnp.where(lane == 0, g1,
                             jnp.where(lane == 1, g2,
                             jnp.where(lane == 2, i1.astype(F32),
                             jnp.where(lane == 3, i2.astype(F32), 0.0))))


def _proj_ln(a, w, res, g_row, b_row, w_router=None):
    t, kdim = a.shape
    rows = ROWS_LN
    row_map = lambda i: (i, 0)
    const = lambda i: (0, 0)
    in_specs = [pl.BlockSpec((rows, kdim), row_map),
                pl.BlockSpec((kdim, D_MODEL), const),
                pl.BlockSpec((rows, D_MODEL), row_map),
                pl.BlockSpec((1, D_MODEL), const),
                pl.BlockSpec((1, D_MODEL), const)]
    out_full = jax.ShapeDtypeStruct((t, D_MODEL), F32)
    if w_router is None:
        return pl.pallas_call(
            _proj_ln_kernel, grid=(t // rows,), in_specs=in_specs,
            out_specs=pl.BlockSpec((rows, D_MODEL), row_map), out_shape=out_full,
            compiler_params=_params("arbitrary"), name="proj_ln",
        )(a, w, res, g_row, b_row)
    return pl.pallas_call(
        _proj_ln_router_kernel, grid=(t // rows,),
        in_specs=in_specs + [pl.BlockSpec((D_MODEL, LANES), const)],
        out_specs=[pl.BlockSpec((rows, D_MODEL), row_map),
                   pl.BlockSpec((rows * SUBLANES, LANES), row_map),
                   pl.BlockSpec((rows, LANES), row_map)],
        out_shape=[out_full, jax.ShapeDtypeStruct((t * SUBLANES, LANES), F32),
                   jax.ShapeDtypeStruct((t, LANES), F32)],
        compiler_params=_params("arbitrary"), name="proj_ln_router",
    )(a, w, res, g_row, b_row, w_router)


def _swiglu_part(x, w1_ref, w3_ref, w2_ref, hid_sc):
    xb = x.astype(BF16)
    sub = 2 * LANES
    for c in range(hid_sc.shape[1] // sub):
        cols = slice(c * sub, (c + 1) * sub)
        h1 = _dot(xb, w1_ref[:, cols])
        h3 = _dot(xb, w3_ref[:, cols])
        hid_sc[:, cols] = (h1 * _sigmoid(h1) * h3).astype(BF16)
    return _dot(hid_sc[...], w2_ref[...])


def _ffn_dense_kernel(x_ref, w1_ref, w3_ref, w2_ref, g_ref, b_ref, o_ref, acc_sc, hid_sc):
    f = pl.program_id(1)
    part = _swiglu_part(x_ref[...], w1_ref, w3_ref, w2_ref, hid_sc)

    @pl.when(f == 0)
    def _():
        acc_sc[...] = part

    @pl.when((f > 0) & (f < pl.num_programs(1) - 1))
    def _():
        acc_sc[...] += part

    @pl.when(f == pl.num_programs(1) - 1)
    def _():
        y = DEEPNORM_ALPHA * x_ref[...] + (acc_sc[...] + part)
        o_ref[...] = _layer_norm(y, g_ref[...], b_ref[...])


def _ffn_dense(x, w1, w3, w2, g_row, b_row):
    t = x.shape[0]
    rows, fc = ROWS_FFN, FFN_CHUNK
    nf = FFN_HIDDEN // fc
    return pl.pallas_call(
        _ffn_dense_kernel,
        grid=(t // rows, nf),
        in_specs=[pl.BlockSpec((rows, D_MODEL), lambda i, f: (i, 0)),
                  pl.BlockSpec((D_MODEL, fc), lambda i, f: (0, f)),
                  pl.BlockSpec((D_MODEL, fc), lambda i, f: (0, f)),
                  pl.BlockSpec((fc, D_MODEL), lambda i, f: (f, 0)),
                  pl.BlockSpec((1, D_MODEL), lambda i, f: (0, 0)),
                  pl.BlockSpec((1, D_MODEL), lambda i, f: (0, 0))],
        out_specs=pl.BlockSpec((rows, D_MODEL), lambda i, f: (i, 0)),
        out_shape=jax.ShapeDtypeStruct((t, D_MODEL), F32),
        scratch_shapes=[pltpu.VMEM((rows, D_MODEL), F32), pltpu.VMEM((rows, fc), BF16)],
        compiler_params=_params("arbitrary", "arbitrary"),
        name="ffn_dense",
    )(x, w1, w3, w2, g_row, b_row)


def _moe_experts_kernel(nu_ref, te_ref, idx_ref, idx_next_ref, dst_ref, x_hbm, w1_ref, w3_ref, w2_ref,
                        y_hbm, xbuf, ybuf, xb_sc, acc_sc, hid_sc, gsem, ssem):
    i = pl.program_id(0)
    f = pl.program_id(1)
    last_f = pl.num_programs(1) - 1
    rows = xb_sc.shape[0]
    tile_rows = rows * SUBLANES
    nu = nu_ref[0]
    slot = i % 2

    def gather_start(idx, s):
        for r in range(rows):
            src = x_hbm.at[pl.ds(pl.multiple_of(idx[0, r], SUBLANES), SUBLANES)]
            pltpu.make_async_copy(src, xbuf.at[pl.ds(s * tile_rows + r * SUBLANES, SUBLANES)], gsem.at[s]).start()

    def scatter_start(s):
        for r in range(rows):
            dst = y_hbm.at[pl.ds(pl.multiple_of(dst_ref[0, r], SUBLANES), SUBLANES)]
            pltpu.make_async_copy(ybuf.at[pl.ds(s * tile_rows + r * SUBLANES, SUBLANES)], dst, ssem.at[s]).start()

    def wait_all(buf, sem, s):
        view = buf.at[pl.ds(pl.multiple_of(s * tile_rows, tile_rows), tile_rows)]
        pltpu.make_async_copy(view, view, sem.at[s]).wait()

    def for_slot(cond, s, fn):
        for static_slot in range(2):
            pl.when(cond & (s == static_slot))(functools.partial(fn, static_slot))

    first = f == 0

    @pl.when(first & (i == 0))
    def _():
        n_real = TOP_K * x_hbm.shape[0]
        ybuf[tile_rows:2 * tile_rows, :] = jnp.zeros((tile_rows, LANES), F32)
        fills = [pltpu.make_async_copy(ybuf.at[pl.ds(tile_rows, tile_rows)],
                                       y_hbm.at[pl.ds(n_real + k * tile_rows, tile_rows)], ssem.at[1])
                 for k in range((y_hbm.shape[0] - n_real) // tile_rows)]
        for cp in fills:
            cp.start()
        for cp in fills:
            cp.wait()

    @pl.when(first & (i == 0) & (nu > 0))
    def _():
        gather_start(idx_ref, 0)

    @pl.when(first & (i < nu))
    def _():
        wait_all(xbuf, gsem, slot)

    for_slot(first & (i + 1 < nu), 1 - slot, functools.partial(gather_start, idx_next_ref))

    @pl.when(i < nu)
    def _():
        @pl.when(first)
        def _():
            base = pl.multiple_of(slot * tile_rows, tile_rows)
            for k in range(D_MODEL // LANES):
                xb_sc[:, k * LANES:(k + 1) * LANES] = xbuf[pl.ds(base + k, rows, stride=SUBLANES), :].astype(BF16)

        part = _swiglu_part(xb_sc[...], w1_ref, w3_ref, w2_ref, hid_sc)

        @pl.when(first)
        def _():
            acc_sc[...] = part

        @pl.when((f > 0) & (f < last_f))
        def _():
            acc_sc[...] += part

        @pl.when(f == last_f)
        def _():
            @pl.when(i >= 2)
            def _():
                wait_all(ybuf, ssem, slot)
            _rows_to_tiles(ybuf, pl.multiple_of(slot * tile_rows, tile_rows), acc_sc[...] + part)

        for_slot(f == last_f, slot, scatter_start)

    @pl.when((i == pl.num_programs(0) - 1) & (f == last_f))
    def _():
        @pl.when(nu >= 1)
        def _():
            wait_all(ybuf, ssem, (nu - 1) % 2)

        @pl.when(nu >= 2)
        def _():
            wait_all(ybuf, ssem, nu % 2)


def _moe_experts(n_used, tile_expert, row_token, row_dst, x_tiles, w1, w3, w2, n_out_rows):
    n_tiles = tile_expert.shape[0]
    rows, fc = ROWS_FFN, FFN_CHUNK
    nf = FFN_HIDDEN // fc
    idx3 = (row_token * SUBLANES).reshape(n_tiles, 1, rows)
    dst3 = (row_dst * SUBLANES).reshape(n_tiles, 1, rows)
    smem_tile = lambda fn: pl.BlockSpec((None, 1, rows), fn, memory_space=pltpu.SMEM)
    grid_spec = pltpu.PrefetchScalarGridSpec(
        num_scalar_prefetch=2,
        grid=(n_tiles, nf),
        in_specs=[smem_tile(lambda i, f, nu, te: (i, 0, 0)),
                  smem_tile(lambda i, f, nu, te: (jnp.minimum(i + 1, n_tiles - 1), 0, 0)),
                  smem_tile(lambda i, f, nu, te: (i, 0, 0)),
                  pl.BlockSpec(memory_space=pl.ANY),
                  pl.BlockSpec((None, D_MODEL, fc), lambda i, f, nu, te: (te[i], 0, f)),
                  pl.BlockSpec((None, D_MODEL, fc), lambda i, f, nu, te: (te[i], 0, f)),
                  pl.BlockSpec((None, fc, D_MODEL), lambda i, f, nu, te: (te[i], f, 0))],
        out_specs=pl.BlockSpec(memory_space=pl.ANY),
        scratch_shapes=[pltpu.VMEM((2 * rows * SUBLANES, LANES), F32), pltpu.VMEM((2 * rows * SUBLANES, LANES), F32),
                        pltpu.VMEM((rows, D_MODEL), BF16),
                        pltpu.VMEM((rows, D_MODEL), F32), pltpu.VMEM((rows, fc), BF16),
                        pltpu.SemaphoreType.DMA((2,)), pltpu.SemaphoreType.DMA((2,))],
    )
    return pl.pallas_call(
        _moe_experts_kernel,
        grid_spec=grid_spec,
        out_shape=jax.ShapeDtypeStruct((n_out_rows * SUBLANES, LANES), F32),
        compiler_params=_params("arbitrary", "arbitrary"),
        name="moe_experts",
    )(n_used, tile_expert, idx3, idx3, dst3, x_tiles, w1, w3, w2)


def _qkv_kernel(x_ref, w_ref, cos_ref, sin_ref, q_ref, k_ref, vt_ref, sel_ref, km_sc, qf_sc):
    i = pl.program_id(1)
    width = N_HEADS * HEAD_DIM
    half = HEAD_DIM // 2
    xb = x_ref[...].astype(BF16)
    cosf = cos_ref[...]
    sinf = sin_ref[...]

    @pl.when(i == 0)
    def _():
        km_sc[...] = jnp.zeros(km_sc.shape, F32)

    def rope(t):
        return t * cosf + pltpu.roll(t, half, axis=1) * sinf

    chunk = 2 * HEAD_DIM
    for c in range(width // chunk):
        kk = _dot(xb, w_ref[:, c * chunk:(c + 1) * chunk])
        vv = _dot(xb, w_ref[:, width + c * chunk:width + (c + 1) * chunk])
        qq = _dot(xb, w_ref[:, 2 * width + c * chunk:2 * width + (c + 1) * chunk])
        for hh in range(2):
            h = 2 * c + hh
            cols = slice(h * HEAD_DIM, (h + 1) * HEAD_DIM)
            part = slice(hh * HEAD_DIM, (hh + 1) * HEAD_DIM)
            kr = rope(kk[:, part])
            k_ref[:, cols] = kr.astype(BF16)
            km_row = lax.broadcasted_iota(jnp.int32, (km_sc.shape[0], HEAD_DIM), 0)
            km_sc[:, cols] = jnp.where(km_row == i, jnp.mean(kr, axis=0, keepdims=True), km_sc[:, cols])
            qr = rope(qq[:, part])
            qf_sc[:, cols] = qr
            q_ref[:, cols] = (qr * (HEAD_DIM ** -0.5 * LOG2_E)).astype(BF16)
            vt_ref[h] = vv[:, part].T.astype(BF16)

    nb = km_sc.shape[0]
    km = km_sc[...]
    lane_head = lax.broadcasted_iota(jnp.int32, km.shape, 1) // HEAD_DIM
    km_rows = jnp.concatenate([jnp.where(lane_head == h, km, 0.0) for h in range(N_HEADS)], axis=0)
    gate_t = _dot_split(km_rows, qf_sc[...], _dot_nt)
    blk = lax.broadcasted_iota(jnp.int32, (nb, x_ref.shape[0]), 0)
    past = blk < i
    for h in range(N_HEADS):
        g = jnp.where(past, gate_t[h * nb:(h + 1) * nb, :], NEG_INF)
        rank = jnp.zeros(g.shape, jnp.int32)
        for n in range(nb):
            gn = g[n:n + 1, :]
            ahead = (gn > g) | ((gn == g) & (blk > n))
            rank = rank + ahead.astype(jnp.int32)
        sel_ref[h * nb:(h + 1) * nb, :] = jnp.where(past & (rank < MOBA_TOPK), 0.0, NEG_INF)


def _qkv_proj(x, w_kvq, cos_full, sin_signed, batch, seq):
    t = x.shape[0]
    width = N_HEADS * HEAD_DIM
    rows = MOBA_BLOCK
    nb = seq // rows
    row_map = lambda b, i: (b * nb + i, 0)
    act = jax.ShapeDtypeStruct((t, width), BF16)
    return pl.pallas_call(
        _qkv_kernel,
        grid=(batch, nb),
        in_specs=[pl.BlockSpec((rows, D_MODEL), row_map),
                  pl.BlockSpec((D_MODEL, 3 * width), lambda b, i: (0, 0)),
                  pl.BlockSpec((rows, HEAD_DIM), lambda b, i: (i, 0)),
                  pl.BlockSpec((rows, HEAD_DIM), lambda b, i: (i, 0))],
        out_specs=[pl.BlockSpec((rows, width), row_map),
                   pl.BlockSpec((rows, width), row_map),
                   pl.BlockSpec((None, N_HEADS, HEAD_DIM, rows), lambda b, i: (b, 0, 0, i)),
                   pl.BlockSpec((None, N_HEADS * nb, rows), lambda b, i: (b * nb + i, 0, 0))],
        out_shape=[act, act,
                   jax.ShapeDtypeStruct((batch, N_HEADS, HEAD_DIM, seq), BF16),
                   jax.ShapeDtypeStruct((batch * nb, N_HEADS * nb, rows), F32)],
        scratch_shapes=[pltpu.VMEM((nb, width), F32), pltpu.VMEM((rows, width), F32)],
        compiler_params=_params("arbitrary", "arbitrary"),
        name="moba_qkv_proj",
    )(x, w_kvq, cos_full, sin_signed)


def _attn_kernel(q_ref, k_ref, vt_ref, sel_ref, o_ref, kaug_sc, vtaug_sc):
    blk = MOBA_BLOCK
    seq = q_ref.shape[0]
    nb = seq // blk
    extra = 2 * SUBLANES
    key = lax.broadcasted_iota(jnp.int32, (blk, blk), 0)
    qry = lax.broadcasted_iota(jnp.int32, (blk, blk), 1)
    causal = key <= qry

    key_blk = lax.broadcasted_iota(jnp.int32, (seq, LANES), 0) // blk
    kaug_sc[:, 0:HEAD_DIM] = k_ref[...]
    kaug_sc[:, HEAD_DIM:] = jnp.where(lax.broadcasted_iota(jnp.int32, (seq, LANES), 1) == key_blk,
                                      1.0, 0.0).astype(BF16)
    vtaug_sc[0:HEAD_DIM, :] = vt_ref[...]
    vtaug_sc[HEAD_DIM:, :] = jnp.where(lax.broadcasted_iota(jnp.int32, (extra, seq), 0) == 0,
                                       1.0, 0.0).astype(BF16)
    blk_row = lax.broadcasted_iota(jnp.int32, (nb, blk), 0)

    def scores(i):
        bias = jnp.where(blk_row == i, 0.0, sel_ref[i])
        bias_t = jnp.concatenate([bias, jnp.zeros((LANES - nb, blk), F32)], axis=0).T
        q_aug = jnp.concatenate([q_ref[i * blk:(i + 1) * blk, :], bias_t.astype(BF16)], axis=1)
        return _dot_nt(kaug_sc[0:(i + 1) * blk, :], q_aug)

    def finish(i, s):
        parts = [s[j * blk:(j + 1) * blk] for j in range(i)]
        parts.append(jnp.where(causal, s[i * blk:(i + 1) * blk], NEG_INF))
        m = jnp.max(parts[0], axis=0, keepdims=True)
        for part in parts[1:]:
            m = jnp.maximum(m, jnp.max(part, axis=0, keepdims=True))
        p = jnp.concatenate([jnp.exp2(part - m).astype(BF16) for part in parts], axis=0)
        acc = _dot(vtaug_sc[:, 0:(i + 1) * blk], p)
        o_ref[i * blk:(i + 1) * blk, :] = (acc[0:HEAD_DIM] / acc[HEAD_DIM:HEAD_DIM + 1]).T.astype(BF16)

    ahead = 2
    pending = [scores(i) for i in range(min(ahead, nb))]
    for i in range(nb):
        if i + ahead < nb:
            pending.append(scores(i + ahead))
        finish(i, pending.pop(0))


def _moba_attention(q, k, vt, sel, batch, seq):
    t = q.shape[0]
    width = N_HEADS * HEAD_DIM
    nb = seq // MOBA_BLOCK
    col = pl.BlockSpec((seq, HEAD_DIM), lambda b, h: (b, h))
    return pl.pallas_call(
        _attn_kernel,
        grid=(batch, N_HEADS),
        in_specs=[col, col,
                  pl.BlockSpec((None, None, HEAD_DIM, seq), lambda b, h: (b, h, 0, 0)),
                  pl.BlockSpec((nb, nb, MOBA_BLOCK), lambda b, h: (b, h, 0))],
        out_specs=col,
        out_shape=jax.ShapeDtypeStruct((t, width), BF16),
        scratch_shapes=[pltpu.VMEM((seq, HEAD_DIM + LANES), BF16),
                        pltpu.VMEM((HEAD_DIM + 2 * SUBLANES, seq), BF16)],
        compiler_params=_params("arbitrary", "arbitrary"),
        name="moba_attention",
    )(q, k, vt, sel)


def _combine_kernel(y0_ref, y1_ref, x_ref, route_ref, g_ref, b_ref, o_ref):
    route = route_ref[...]
    rows = x_ref.shape[0]
    y = (DEEPNORM_ALPHA * x_ref[...] + route[:, 0:1] * _tiles_to_rows(y0_ref, 0, rows)
         + route[:, 1:2] * _tiles_to_rows(y1_ref, 0, rows))
    o_ref[...] = _layer_norm(y, g_ref[...], b_ref[...])


def _moe_combine(ys, x, route, g_row, b_row):
    t = x.shape[0]
    rows = ROWS_COMBINE
    nt = t // rows
    row_map = lambda i: (i, 0)
    const = lambda i: (0, 0)
    return pl.pallas_call(
        _combine_kernel,
        grid=(nt,),
        in_specs=[pl.BlockSpec((rows * SUBLANES, LANES), row_map),
                  pl.BlockSpec((rows * SUBLANES, LANES), lambda i: (nt + i, 0)),
                  pl.BlockSpec((rows, D_MODEL), row_map),
                  pl.BlockSpec((rows, LANES), row_map),
                  pl.BlockSpec((1, D_MODEL), const),
                  pl.BlockSpec((1, D_MODEL), const)],
        out_specs=pl.BlockSpec((rows, D_MODEL), row_map),
        out_shape=jax.ShapeDtypeStruct((t, D_MODEL), F32),
        compiler_params=_params("arbitrary"),
        name="moe_combine",
    )(ys, ys, x, route, g_row, b_row)


def _routing_tables(route, n_tokens):
    rows = ROWS_FFN
    n_assign = n_tokens * TOP_K
    n_tiles = n_assign // rows + N_EXPERTS
    experts = route[:, 2:4].astype(jnp.int32).reshape(-1)
    onehot = (experts[:, None] == jnp.arange(N_EXPERTS, dtype=jnp.int32)[None, :]).astype(jnp.int32)
    counts = jnp.sum(onehot, axis=0)
    tiles_per = (counts + rows - 1) // rows
    tile_end = jnp.cumsum(tiles_per)
    id_bits = max(n_assign, rows).bit_length()
    real_keys = (experts << (id_bits + 1)) | jnp.arange(n_assign, dtype=jnp.int32)
    pad_id = jnp.arange(rows, dtype=jnp.int32)[None, :]
    pad_needed = pad_id < (tiles_per * rows - counts)[:, None]
    pad_keys = (jnp.arange(N_EXPERTS, dtype=jnp.int32)[:, None] << (id_bits + 1)) | (1 << id_bits) | pad_id
    pad_keys = jnp.where(pad_needed, pad_keys, jnp.iinfo(jnp.int32).max)
    keys = jnp.sort(jnp.concatenate([real_keys, pad_keys.reshape(-1)]))
    real = ((keys >> id_bits) & 1) == 0
    assign = jnp.where(real, keys & ((1 << id_bits) - 1), -1)
    row_token = jnp.where(real, assign // TOP_K, 0)
    pad_rank = jnp.minimum(jnp.cumsum(1 - real.astype(jnp.int32)) - 1, N_EXPERTS * rows - 1)
    row_dst = jnp.where(real, (assign % TOP_K) * n_tokens + assign // TOP_K, n_assign + pad_rank)
    tile_id = jnp.arange(n_tiles, dtype=jnp.int32)
    n_used = tile_end[-1]
    tile_expert = jnp.sum((tile_id[:, None] >= tile_end[None, :]).astype(jnp.int32), axis=1)
    last_expert = jnp.sum((n_used - 1 >= tile_end).astype(jnp.int32))
    tile_expert = jnp.minimum(jnp.where(tile_id < n_used, tile_expert, last_expert), N_EXPERTS - 1)
    return n_used.reshape(1), tile_expert, row_token, row_dst, n_assign + N_EXPERTS * rows


def _rope_tables(seq):
    half = HEAD_DIM // 2
    inv_freq = ROPE_THETA ** (-jnp.arange(half, dtype=F32) / half)
    ang = jnp.arange(seq).astype(F32)[:, None] * inv_freq[None, :]
    cos, sin = jnp.cos(ang), jnp.sin(ang)
    return jnp.concatenate([cos, cos], axis=1), jnp.concatenate([-sin, sin], axis=1)


def _pad_lanes(w):
    return jnp.pad(w, ((0, 0), (0, LANES - w.shape[1])))


def kernel(x, a_w_in, a_conv_w, a_log_decay, a_dt_bias, a_norm_w, a_w_out, b_w_kv, b_w_q, b_w_o,
           ffn_w1, ffn_w3, ffn_w2, moe_router, moe_w1, moe_w3, moe_w2, ln_g, ln_b):
    batch, seq, _ = x.shape
    t = batch * seq
    width = N_HEADS * HEAD_DIM
    x0 = x.reshape(t, D_MODEL)
    row = lambda v: v.reshape(1, -1).astype(F32)

    w_in = a_w_in[0]
    q, k, v, gate, gb, moe_w1b, moe_w2b = _in_proj(
        x0, w_in[:, :4 * width].astype(BF16), _pad_lanes(w_in[:, 4 * width:]).astype(BF16),
        a_conv_w[0], _pad_lanes(row(a_log_decay[0])), _pad_lanes(row(a_dt_bias[0])), batch, seq,
        riders=(moe_w1[0].reshape(N_EXPERTS * D_MODEL, FFN_HIDDEN),
                moe_w2[0].reshape(N_EXPERTS * FFN_HIDDEN, D_MODEL)))
    og, moe_w3b = _delta_rule(q, k, v, gate, gb, row(a_norm_w[0]), batch, seq,
                              riders=(moe_w3[0].reshape(N_EXPERTS * D_MODEL, FFN_HIDDEN),))
    x1 = _proj_ln(og, a_w_out[0].astype(BF16), x0, row(ln_g[0, 0]), row(ln_b[0, 0]))
    x2 = _ffn_dense(x1, ffn_w1[0].astype(BF16), ffn_w3[0].astype(BF16), ffn_w2[0].astype(BF16),
                    row(ln_g[0, 1]), row(ln_b[0, 1]))

    cos_full, sin_signed = _rope_tables(seq)
    w_kvq = jnp.concatenate([b_w_kv, b_w_q[0]], axis=1).astype(BF16)
    qm, km, vt, sel = _qkv_proj(x2, w_kvq, cos_full, sin_signed, batch, seq)
    attn = _moba_attention(qm, km, vt, sel, batch, seq)
    x3, x3_tiles, route = _proj_ln(attn, b_w_o[0].astype(BF16), x2, row(ln_g[1, 0]), row(ln_b[1, 0]),
                                   w_router=_pad_lanes(moe_router[0]))
    n_used, tile_expert, row_token, row_dst, n_out_rows = _routing_tables(route, t)
    ys = _moe_experts(n_used, tile_expert, row_token, row_dst, x3_tiles,
                      moe_w1b.reshape(N_EXPERTS, D_MODEL, FFN_HIDDEN),
                      moe_w3b.reshape(N_EXPERTS, D_MODEL, FFN_HIDDEN),
                      moe_w2b.reshape(N_EXPERTS, FFN_HIDDEN, D_MODEL), n_out_rows)
    x4 = _moe_combine(ys, x3, route, row(ln_g[1, 1]), row(ln_b[1, 1]))
    return x4.reshape(batch, seq, D_MODEL)
```

```python
import functools
import math

import jax
import jax.numpy as jnp
from jax import lax
from jax.experimental import pallas as pl
from jax.experimental.pallas import tpu as pltpu

D_MODEL = 1024
DEPTH = 2
DEEPNORM_ALPHA = (2.0 * DEPTH) ** 0.25
LN_EPS = 1e-5
N_HEADS = 8
HEAD_DIM = 128
DN_CONV = 4
DN_CHUNK = 64
DN_NORM_EPS = 1e-6
MOBA_BLOCK = 256
MOBA_TOPK = 3
ROPE_THETA = 10000.0
NEG_INF = -1e30
LOG2_E = math.log2(math.e)
FFN_HIDDEN = 3584
N_EXPERTS = 8
TOP_K = 2

LANES = 128
SUBLANES = 8
VMEM_LIMIT_BYTES = 56 * 1024 * 1024

ROWS_PROJ = 512
ROWS_DELTA = 256
ROWS_LN = 512
ROWS_LN_SUB = 256
ROWS_FFN = 512
FFN_CHUNK = 1792
ROWS_COMBINE = 512

F32 = jnp.float32
BF16 = jnp.bfloat16

_NT = (((1,), (1,)), ((), ()))


def _dot(a, b):
    return jnp.dot(a, b, preferred_element_type=F32)


def _dot_nt(a, b):
    return lax.dot_general(a, b, _NT, preferred_element_type=F32)


def _params(*sem):
    return pltpu.CompilerParams(dimension_semantics=sem, vmem_limit_bytes=VMEM_LIMIT_BYTES)


def _layer_norm(y, g, b):
    mu = jnp.mean(y, axis=-1, keepdims=True)
    d = y - mu
    var = jnp.mean(d * d, axis=-1, keepdims=True)
    return d * lax.rsqrt(var + LN_EPS) * g + b


def _sigmoid(x):
    return 1.0 / (1.0 + jnp.exp(-x))


def _split2(x):
    hi = x.astype(BF16)
    lo = (x - hi.astype(F32)).astype(BF16)
    return hi, lo


def _dot_split(a, b, dot):
    a_hi, a_lo = _split2(a)
    b_hi, b_lo = _split2(b)
    return dot(a_hi, b_hi) + dot(a_hi, b_lo) + dot(a_lo, b_hi)


def _rider_specs(arrays, n_steps, index_map):
    specs = [pl.BlockSpec((a.shape[0] // n_steps, a.shape[1]), index_map) for a in arrays]
    return specs, [jax.ShapeDtypeStruct(a.shape, BF16) for a in arrays]


def _cast_riders(src_refs, dst_refs):
    for src, dst in zip(src_refs, dst_refs):
        dst[...] = src[...].astype(BF16)


def _in_proj_kernel(n_riders, x_ref, w_ref, wab_ref, cw_ref, alog_ref, dtb_ref, *rest):
    riders_in, rest = rest[:n_riders], rest[n_riders:]
    q_ref, k_ref, v_ref, gate_ref, gb_ref = rest[:5]
    riders_out, conv_sc = rest[5:5 + n_riders], rest[5 + n_riders]
    _cast_riders(riders_in, riders_out)
    rows = x_ref.shape[0]
    width = N_HEADS * HEAD_DIM
    hist = SUBLANES

    @pl.when(pl.program_id(1) == 0)
    def _():
        conv_sc[...] = jnp.zeros(conv_sc.shape, F32)

    xb = x_ref[...].astype(BF16)
    chunk = 2 * HEAD_DIM
    for c in range(3 * width // chunk):
        proj = _dot(xb, w_ref[:, c * chunk:(c + 1) * chunk])
        which = c * chunk // width
        for hh in range(chunk // HEAD_DIM):
            slab = c * (chunk // HEAD_DIM) + hh
            cols = slice(slab * HEAD_DIM, (slab + 1) * HEAD_DIM)
            cur = proj[:, hh * HEAD_DIM:(hh + 1) * HEAD_DIM]
            conv_sc[slab, pl.ds(2 * hist, rows, stride=2), :] = cur
            y = cur * cw_ref[DN_CONV - 1:DN_CONV, cols]
            for j in range(DN_CONV - 1):
                shifted = conv_sc[slab, pl.ds(2 * (hist - (DN_CONV - 1) + j), rows, stride=2), :]
                y = y + shifted * cw_ref[j:j + 1, cols]
            conv_sc[slab, 0:2 * hist, :] = conv_sc[slab, 2 * rows:2 * (rows + hist), :]
            t = y * _sigmoid(y)
            ocols = slice(slab * HEAD_DIM - which * width, (slab + 1) * HEAD_DIM - which * width)
            if which == 2:
                v_ref[:, ocols] = t.astype(BF16)
            else:
                inv_norm = lax.rsqrt(jnp.sum(t * t, axis=-1, keepdims=True) + DN_NORM_EPS)
                if which == 0:
                    q_ref[:, ocols] = (t * (inv_norm * (HEAD_DIM ** -0.5))).astype(BF16)
                else:
                    k_ref[:, ocols] = (t * inv_norm).astype(BF16)
        if c % 3 == 2:
            g0 = (c // 3) * chunk
            gate_ref[:, g0:g0 + chunk] = _dot(xb, w_ref[:, 3 * width + g0:3 * width + g0 + chunk]).astype(BF16)
    ab = _dot(xb, wab_ref[...])
    z = ab + dtb_ref[...]
    softplus = jnp.maximum(z, 0.0) + jnp.log(1.0 + jnp.exp(-jnp.abs(z)))
    g = -jnp.exp(alog_ref[...]) * softplus
    lane = lax.broadcasted_iota(jnp.int32, ab.shape, 1)
    gb_ref[...] = jnp.where(lane < N_HEADS, g, _sigmoid(ab))


def _in_proj(x2d, w_main, w_ab, conv_w, a_log_row, dt_row, batch, seq, riders=()):
    t = x2d.shape[0]
    width = N_HEADS * HEAD_DIM
    rows = ROWS_PROJ
    nt = seq // rows
    row_map = lambda b, s: (b * nt + s, 0)
    const = lambda b, s: (0, 0)
    act = jax.ShapeDtypeStruct((t, width), BF16)
    rider_specs, rider_shapes = _rider_specs(riders, batch * nt, row_map)
    return pl.pallas_call(
        functools.partial(_in_proj_kernel, len(riders)),
        grid=(batch, nt),
        in_specs=[pl.BlockSpec((rows, D_MODEL), row_map),
                  pl.BlockSpec((D_MODEL, 4 * width), const),
                  pl.BlockSpec((D_MODEL, LANES), const),
                  pl.BlockSpec((DN_CONV, 3 * width), const),
                  pl.BlockSpec((1, LANES), const),
                  pl.BlockSpec((1, LANES), const)] + rider_specs,
        out_specs=[pl.BlockSpec((rows, width), row_map)] * 4 + [pl.BlockSpec((rows, LANES), row_map)]
        + rider_specs,
        out_shape=[act, act, act, act, jax.ShapeDtypeStruct((t, LANES), F32)] + rider_shapes,
        scratch_shapes=[pltpu.VMEM((3 * width // LANES, 2 * (rows + SUBLANES), LANES), F32)],
        compiler_params=_params("arbitrary", "arbitrary"),
        name="dn_in_proj",
    )(x2d, w_main, w_ab, conv_w, a_log_row, dt_row, *riders)


def _delta_kernel(n_riders, q_ref, k_ref, v_ref, gate_ref, gb_ref, nw_ref, *rest):
    riders_in, o_ref = rest[:n_riders], rest[n_riders]
    riders_out, state_sc = rest[n_riders + 1:2 * n_riders + 1], rest[2 * n_riders + 1]
    _cast_riders(riders_in, riders_out)
    c = DN_CHUNK
    two = 2 * c
    n_sub = q_ref.shape[0] // c
    n_pairs = N_HEADS // 2

    @pl.when(pl.program_id(1) == 0)
    def _():
        state_sc[...] = jnp.zeros(state_sc.shape, F32)

    r = lax.broadcasted_iota(jnp.int32, (two, two), 0)
    cc = lax.broadcasted_iota(jnp.int32, (two, two), 1)
    same = (r < c) == (cc < c)
    tri = same & (r >= cc)
    stri = same & (r > cc)
    eye = (r == cc).astype(F32)
    top_rows = lax.broadcasted_iota(jnp.int32, (two, 1), 0) < c
    top_lanes = lax.broadcasted_iota(jnp.int32, (1, two), 1) < c
    row64 = lax.broadcasted_iota(jnp.int32, (c, LANES), 0)

    def stack(ref, rows, h):
        t = ref[rows, h * HEAD_DIM:(h + 2) * HEAD_DIM].astype(F32)
        return jnp.concatenate([t[:, :HEAD_DIM], t[:, HEAD_DIM:]], axis=0)

    items = []
    for s in range(n_sub):
        rows = slice(s * c, (s + 1) * c)
        gb = gb_ref[rows, :]
        gc = gb
        sh = 1
        while sh < c:
            gc = gc + jnp.where(row64 >= sh, pltpu.roll(gc, sh, axis=0), 0.0)
            sh *= 2
        gc_pair = jnp.concatenate([gc, pltpu.roll(gc, LANES - 1, axis=1)], axis=0)
        gb_pair = jnp.concatenate([gb, pltpu.roll(gb, LANES - 1, axis=1)], axis=0)
        gc_pair_t = gc_pair.T
        for p in range(n_pairs):
            h = 2 * p
            q2, k2, v2 = stack(q_ref, rows, h), stack(k_ref, rows, h), stack(v_ref, rows, h)
            gcol = gc_pair[:, h:h + 1]
            grow = gc_pair_t[h:h + 1, :]
            beta = gb_pair[:, N_HEADS + h:N_HEADS + h + 1]
            glast = jnp.where(top_rows, gc_pair[c - 1:c, h:h + 1], gc_pair[two - 1:two, h:h + 1])
            eg = jnp.exp(gcol)
            kb = k2 * beta
            items.append(dict(
                rows=rows, h=h, k2b=k2.astype(BF16), q2b=q2.astype(BF16), kbb=kb.astype(BF16),
                decay=jnp.exp(jnp.where(tri, gcol - grow, NEG_INF)),
                rhs=jnp.concatenate([v2 * beta, kb * eg], axis=1).astype(BF16),
                qd=(q2 * eg).astype(BF16),
                kd_t=(k2 * jnp.exp(glast - gcol)).T,
                eglast=jnp.exp(glast)))
    for it in items:
        it["pw"] = jnp.where(stri, _dot_nt(it["kbb"], it["k2b"]) * it["decay"], 0.0)
        it["inv"] = eye - it["pw"]
    for _ in range(int(math.log2(c)) - 1):
        for it in items:
            pwb = it["pw"].astype(BF16)
            it["pw"] = _dot(pwb, pwb)
        for it in items:
            it["inv"] = it["inv"] + _dot(it["inv"].astype(BF16), it["pw"].astype(BF16))
    for it in items:
        sol = _dot(it["inv"].astype(BF16), it["rhs"])
        it["u"] = sol[:, :HEAD_DIM]
        it["wb"] = sol[:, HEAD_DIM:].astype(BF16)
    for it in items:
        it["amat"] = jnp.where(tri, _dot_nt(it["q2b"], it["k2b"]) * it["decay"], 0.0).astype(BF16)

    for s in range(n_sub):
        group = items[s * n_pairs:(s + 1) * n_pairs]
        for it in group:
            h = it["h"]
            it["s0"] = state_sc[h]
            it["s1"] = state_sc[h + 1]
            it["r0"] = _dot(jnp.concatenate([it["wb"][:c], it["qd"][:c]], axis=0), it["s0"].astype(BF16))
            it["r1"] = _dot(jnp.concatenate([it["wb"][c:], it["qd"][c:]], axis=0), it["s1"].astype(BF16))
        for it in group:
            v_new = it["u"] - jnp.concatenate([it["r0"][:c], it["r1"][:c]], axis=0)
            it["vnb"] = v_new.astype(BF16)
            it["o"] = jnp.concatenate([it["r0"][c:], it["r1"][c:]], axis=0) + _dot(it["amat"], it["vnb"])
        for it in group:
            h = it["h"]
            kd_t = it["kd_t"]
            state_sc[h] = it["s0"] * it["eglast"][0:1, :] + _dot(
                jnp.where(top_lanes, kd_t, 0.0).astype(BF16), it["vnb"])
            state_sc[h + 1] = it["s1"] * it["eglast"][c:c + 1, :] + _dot(
                jnp.where(top_lanes, 0.0, kd_t).astype(BF16), it["vnb"])
        for it in group:
            h, rows, o = it["h"], it["rows"], it["o"]
            gate2 = stack(gate_ref, rows, h)
            o = o * lax.rsqrt(jnp.mean(o * o, axis=-1, keepdims=True) + DN_NORM_EPS) * nw_ref[...]
            o = o * (gate2 * _sigmoid(gate2))
            o_ref[rows, h * HEAD_DIM:(h + 1) * HEAD_DIM] = o[:c].astype(BF16)
            o_ref[rows, (h + 1) * HEAD_DIM:(h + 2) * HEAD_DIM] = o[c:].astype(BF16)


def _delta_rule(q, k, v, gate, gb, norm_w_row, batch, seq, riders=()):
    t = q.shape[0]
    width = N_HEADS * HEAD_DIM
    rows = ROWS_DELTA
    ns = seq // rows
    row_map = lambda b, s: (b * ns + s, 0)
    const = lambda b, s: (0, 0)
    wide = pl.BlockSpec((rows, width), row_map)
    rider_specs, rider_shapes = _rider_specs(riders, batch * ns, row_map)
    return pl.pallas_call(
        functools.partial(_delta_kernel, len(riders)),
        grid=(batch, ns),
        in_specs=[wide, wide, wide, wide,
                  pl.BlockSpec((rows, LANES), row_map),
                  pl.BlockSpec((1, HEAD_DIM), const)] + rider_specs,
        out_specs=[wide] + rider_specs,
        out_shape=[jax.ShapeDtypeStruct((t, width), BF16)] + rider_shapes,
        scratch_shapes=[pltpu.VMEM((N_HEADS, HEAD_DIM, HEAD_DIM), F32)],
        compiler_params=_params("arbitrary", "arbitrary"),
        name="dn_delta_rule",
    )(q, k, v, gate, gb, norm_w_row, *riders)


def _proj_ln_kernel(a_ref, w_ref, res_ref, g_ref, b_ref, o_ref):
    for rb in range(a_ref.shape[0] // ROWS_LN_SUB):
        rows = slice(rb * ROWS_LN_SUB, (rb + 1) * ROWS_LN_SUB)
        y = DEEPNORM_ALPHA * res_ref[rows, :] + _dot(a_ref[rows, :], w_ref[...])
        o_ref[rows, :] = _layer_norm(y, g_ref[...], b_ref[...])


def _rows_to_tiles(ref, base, y):
    for k in range(D_MODEL // LANES):
        ref[pl.ds(base + k, y.shape[0], stride=SUBLANES), :] = y[:, k * LANES:(k + 1) * LANES]


def _tiles_to_rows(ref, base, n):
    return jnp.concatenate([ref[pl.ds(base + k, n, stride=SUBLANES), :] for k in range(D_MODEL // LANES)],
                           axis=1)


def _proj_ln_router_kernel(a_ref, w_ref, res_ref, g_ref, b_ref, wr_ref, o_ref, ot_ref, route_ref):
    for rb in range(a_ref.shape[0] // ROWS_LN_SUB):
        rows = slice(rb * ROWS_LN_SUB, (rb + 1) * ROWS_LN_SUB)
        y = DEEPNORM_ALPHA * res_ref[rows, :] + _dot(a_ref[rows, :], w_ref[...])
        xn = _layer_norm(y, g_ref[...], b_ref[...])
        o_ref[rows, :] = xn
        _rows_to_tiles(ot_ref, rb * ROWS_LN_SUB * SUBLANES, xn)
        logits = _dot_split(xn, wr_ref[...], _dot)
        lane = lax.broadcasted_iota(jnp.int32, logits.shape, 1)
        logits = jnp.where(lane < N_EXPERTS, logits, NEG_INF)
        l1 = jnp.max(logits, axis=-1, keepdims=True)
        i1 = jnp.min(jnp.where(logits == l1, lane, LANES), axis=-1, keepdims=True)
        rest = jnp.where(lane == i1, NEG_INF, logits)
        l2 = jnp.max(rest, axis=-1, keepdims=True)
        i2 = jnp.min(jnp.where(rest == l2, lane, LANES), axis=-1, keepdims=True)
        e2 = jnp.exp(l2 - l1)
        g1 = 1.0 / (1.0 + e2)
        g2 = e2 / (1.0 + e2)
        route_ref[rows, :] = jnp.where(lane == 0, g1,
                             jnp.where(lane == 1, g2,
                             jnp.where(lane == 2, i1.astype(F32),
                             jnp.where(lane == 3, i2.astype(F32), 0.0))))


def _proj_ln(a, w, res, g_row, b_row, w_router=None):
    t, kdim = a.shape
    rows = ROWS_LN
    row_map = lambda i: (i, 0)
    const = lambda i: (0, 0)
    in_specs = [pl.BlockSpec((rows, kdim), row_map),
                pl.BlockSpec((kdim, D_MODEL), const),
                pl.BlockSpec((rows, D_MODEL), row_map),
                pl.BlockSpec((1, D_MODEL), const),
                pl.BlockSpec((1, D_MODEL), const)]
    out_full = jax.ShapeDtypeStruct((t, D_MODEL), F32)
    if w_router is None:
        return pl.pallas_call(
            _proj_ln_kernel, grid=(t // rows,), in_specs=in_specs,
            out_specs=pl.BlockSpec((rows, D_MODEL), row_map), out_shape=out_full,
            compiler_params=_params("arbitrary"), name="proj_ln",
        )(a, w, res, g_row, b_row)
    return pl.pallas_call(
        _proj_ln_router_kernel, grid=(t // rows,),
        in_specs=in_specs + [pl.BlockSpec((D_MODEL, LANES), const)],
        out_specs=[pl.BlockSpec((rows, D_MODEL), row_map),
                   pl.BlockSpec((rows * SUBLANES, LANES), row_map),
                   pl.BlockSpec((rows, LANES), row_map)],
        out_shape=[out_full, jax.ShapeDtypeStruct((t * SUBLANES, LANES), F32),
                   jax.ShapeDtypeStruct((t, LANES), F32)],
        compiler_params=_params("arbitrary"), name="proj_ln_router",
    )(a, w, res, g_row, b_row, w_router)


def _swiglu_hidden(x, w1_ref, w3_ref, hid_sc):
    xb = x.astype(BF16)
    sub = 2 * LANES
    for c in range(hid_sc.shape[1] // sub):
        cols = slice(c * sub, (c + 1) * sub)
        h1 = _dot(xb, w1_ref[:, cols])
        h3 = _dot(xb, w3_ref[:, cols])
        hid_sc[:, cols] = (h1 * _sigmoid(h1) * h3).astype(BF16)


def _swiglu_part(x, w1_ref, w3_ref, w2_ref, hid_sc):
    _swiglu_hidden(x, w1_ref, w3_ref, hid_sc)
    return _dot(hid_sc[...], w2_ref[...])


def _ffn_dense_kernel(n_riders, x_ref, w1_ref, w3_ref, w2_ref, g_ref, b_ref, *rest):
    riders_in, o_ref = rest[:n_riders], rest[n_riders]
    riders_out = rest[n_riders + 1:2 * n_riders + 1]
    acc_sc, hid_sc = rest[2 * n_riders + 1:]
    _cast_riders(riders_in, riders_out)
    f = pl.program_id(1)
    last_f = pl.num_programs(1) - 1
    _swiglu_hidden(x_ref[...], w1_ref, w3_ref, hid_sc)

    @pl.when(f == 0)
    def _():
        acc_sc[...] = _dot(hid_sc[...], w2_ref[...])

    @pl.when((f > 0) & (f < last_f))
    def _():
        acc_sc[...] += _dot(hid_sc[...], w2_ref[...])

    @pl.when(f == last_f)
    def _():
        for rb in range(x_ref.shape[0] // ROWS_LN_SUB):
            rows = slice(rb * ROWS_LN_SUB, (rb + 1) * ROWS_LN_SUB)
            y = DEEPNORM_ALPHA * x_ref[rows, :] + (acc_sc[rows, :] + _dot(hid_sc[rows, :], w2_ref[...]))
            o_ref[rows, :] = _layer_norm(y, g_ref[...], b_ref[...])


def _ffn_dense(x, w1, w3, w2, g_row, b_row, riders=()):
    t = x.shape[0]
    rows, fc = ROWS_FFN, FFN_CHUNK
    nf = FFN_HIDDEN // fc
    assert nf >= 2
    step_map = lambda i, f: (i * nf + f, 0)
    rider_specs, rider_shapes = _rider_specs(riders, (t // rows) * nf, step_map)
    return pl.pallas_call(
        functools.partial(_ffn_dense_kernel, len(riders)),
        grid=(t // rows, nf),
        in_specs=[pl.BlockSpec((rows, D_MODEL), lambda i, f: (i, 0)),
                  pl.BlockSpec((D_MODEL, fc), lambda i, f: (0, f)),
                  pl.BlockSpec((D_MODEL, fc), lambda i, f: (0, f)),
                  pl.BlockSpec((fc, D_MODEL), lambda i, f: (f, 0)),
                  pl.BlockSpec((1, D_MODEL), lambda i, f: (0, 0)),
                  pl.BlockSpec((1, D_MODEL), lambda i, f: (0, 0))] + rider_specs,
        out_specs=[pl.BlockSpec((rows, D_MODEL), lambda i, f: (i, 0))] + rider_specs,
        out_shape=[jax.ShapeDtypeStruct((t, D_MODEL), F32)] + rider_shapes,
        scratch_shapes=[pltpu.VMEM((rows, D_MODEL), F32), pltpu.VMEM((rows, fc), BF16)],
        compiler_params=_params("arbitrary", "arbitrary"),
        name="ffn_dense",
    )(x, w1, w3, w2, g_row, b_row, *riders)


def _moe_experts_kernel(nu_ref, te_ref, idx_ref, idx_next_ref, dst_ref, x_hbm, w1_ref, w3_ref, w2_ref,
                        y_hbm, xbuf, ybuf, xb_sc, acc_sc, hid_sc, gsem, ssem):
    i = pl.program_id(0)
    f = pl.program_id(1)
    last_f = pl.num_programs(1) - 1
    rows = xb_sc.shape[0]
    tile_rows = rows * SUBLANES
    nu = nu_ref[0]
    slot = i % 2

    def gather_start(idx, s):
        for r in range(rows):
            src = x_hbm.at[pl.ds(pl.multiple_of(idx[0, r], SUBLANES), SUBLANES)]
            pltpu.make_async_copy(src, xbuf.at[pl.ds(s * tile_rows + r * SUBLANES, SUBLANES)], gsem.at[s]).start()

    def scatter_start(s):
        for r in range(rows):
            dst = y_hbm.at[pl.ds(pl.multiple_of(dst_ref[0, r], SUBLANES), SUBLANES)]
            pltpu.make_async_copy(ybuf.at[pl.ds(s * tile_rows + r * SUBLANES, SUBLANES)], dst, ssem.at[s]).start()

    def wait_all(buf, sem, s):
        view = buf.at[pl.ds(pl.multiple_of(s * tile_rows, tile_rows), tile_rows)]
        pltpu.make_async_copy(view, view, sem.at[s]).wait()

    def for_slot(cond, s, fn):
        for static_slot in range(2):
            pl.when(cond & (s == static_slot))(functools.partial(fn, static_slot))

    first = f == 0

    @pl.when(first & (i == 0))
    def _():
        n_real = TOP_K * x_hbm.shape[0]
        ybuf[tile_rows:2 * tile_rows, :] = jnp.zeros((tile_rows, LANES), F32)
        fills = [pltpu.make_async_copy(ybuf.at[pl.ds(tile_rows, tile_rows)],
                                       y_hbm.at[pl.ds(n_real + k * tile_rows, tile_rows)], ssem.at[1])
                 for k in range((y_hbm.shape[0] - n_real) // tile_rows)]
        for cp in fills:
            cp.start()
        for cp in fills:
            cp.wait()

    @pl.when(first & (i == 0) & (nu > 0))
    def _():
        gather_start(idx_ref, 0)

    @pl.when(first & (i < nu))
    def _():
        wait_all(xbuf, gsem, slot)

    for_slot(first & (i + 1 < nu), 1 - slot, functools.partial(gather_start, idx_next_ref))

    @pl.when(i < nu)
    def _():
        @pl.when(first)
        def _():
            base = pl.multiple_of(slot * tile_rows, tile_rows)
            for k in range(D_MODEL // LANES):
                xb_sc[:, k * LANES:(k + 1) * LANES] = xbuf[pl.ds(base + k, rows, stride=SUBLANES), :].astype(BF16)

        part = _swiglu_part(xb_sc[...], w1_ref, w3_ref, w2_ref, hid_sc)

        @pl.when(first)
        def _():
            acc_sc[...] = part

        @pl.when((f > 0) & (f < last_f))
        def _():
            acc_sc[...] += part

        @pl.when(f == last_f)
        def _():
            @pl.when(i >= 2)
            def _():
                wait_all(ybuf, ssem, slot)
            _rows_to_tiles(ybuf, pl.multiple_of(slot * tile_rows, tile_rows), acc_sc[...] + part)

        for_slot(f == last_f, slot, scatter_start)

    @pl.when((i == pl.num_programs(0) - 1) & (f == last_f))
    def _():
        @pl.when(nu >= 1)
        def _():
            wait_all(ybuf, ssem, (nu - 1) % 2)

        @pl.when(nu >= 2)
        def _():
            wait_all(ybuf, ssem, nu % 2)


def _moe_experts(n_used, tile_expert, row_token, row_dst, x_tiles, w1, w3, w2, n_out_rows):
    n_tiles = tile_expert.shape[0]
    rows, fc = ROWS_FFN, FFN_CHUNK
    nf = FFN_HIDDEN // fc
    idx3 = (row_token * SUBLANES).reshape(n_tiles, 1, rows)
    dst3 = (row_dst * SUBLANES).reshape(n_tiles, 1, rows)
    smem_tile = lambda fn: pl.BlockSpec((None, 1, rows), fn, memory_space=pltpu.SMEM)
    grid_spec = pltpu.PrefetchScalarGridSpec(
        num_scalar_prefetch=2,
        grid=(n_tiles, nf),
        in_specs=[smem_tile(lambda i, f, nu, te: (i, 0, 0)),
                  smem_tile(lambda i, f, nu, te: (jnp.minimum(i + 1, n_tiles - 1), 0, 0)),
                  smem_tile(lambda i, f, nu, te: (i, 0, 0)),
                  pl.BlockSpec(memory_space=pl.ANY),
                  pl.BlockSpec((None, D_MODEL, fc), lambda i, f, nu, te: (te[i], 0, f)),
                  pl.BlockSpec((None, D_MODEL, fc), lambda i, f, nu, te: (te[i], 0, f)),
                  pl.BlockSpec((None, fc, D_MODEL), lambda i, f, nu, te: (te[i], f, 0))],
        out_specs=pl.BlockSpec(memory_space=pl.ANY),
        scratch_shapes=[pltpu.VMEM((2 * rows * SUBLANES, LANES), F32), pltpu.VMEM((2 * rows * SUBLANES, LANES), F32),
                        pltpu.VMEM((rows, D_MODEL), BF16),
                        pltpu.VMEM((rows, D_MODEL), F32), pltpu.VMEM((rows, fc), BF16),
                        pltpu.SemaphoreType.DMA((2,)), pltpu.SemaphoreType.DMA((2,))],
    )
    return pl.pallas_call(
        _moe_experts_kernel,
        grid_spec=grid_spec,
        out_shape=jax.ShapeDtypeStruct((n_out_rows * SUBLANES, LANES), F32),
        compiler_params=_params("arbitrary", "arbitrary"),
        name="moe_experts",
    )(n_used, tile_expert, idx3, idx3, dst3, x_tiles, w1, w3, w2)


def _qkv_kernel(x_ref, w_ref, cos_ref, sin_ref, q_ref, k_ref, vt_ref, sel_ref, km_sc, qf_sc):
    i = pl.program_id(1)
    width = N_HEADS * HEAD_DIM
    half = HEAD_DIM // 2
    xb = x_ref[...].astype(BF16)
    cosf = cos_ref[...]
    sinf = sin_ref[...]

    @pl.when(i == 0)
    def _():
        km_sc[...] = jnp.zeros(km_sc.shape, F32)

    def rope(t):
        return t * cosf + pltpu.roll(t, half, axis=1) * sinf

    chunk = 2 * HEAD_DIM
    for c in range(width // chunk):
        kk = _dot(xb, w_ref[:, c * chunk:(c + 1) * chunk])
        vv = _dot(xb, w_ref[:, width + c * chunk:width + (c + 1) * chunk])
        qq = _dot(xb, w_ref[:, 2 * width + c * chunk:2 * width + (c + 1) * chunk])
        for hh in range(2):
            h = 2 * c + hh
            cols = slice(h * HEAD_DIM, (h + 1) * HEAD_DIM)
            part = slice(hh * HEAD_DIM, (hh + 1) * HEAD_DIM)
            kr = rope(kk[:, part])
            k_ref[:, cols] = kr.astype(BF16)
            km_row = lax.broadcasted_iota(jnp.int32, (km_sc.shape[0], HEAD_DIM), 0)
            km_sc[:, cols] = jnp.where(km_row == i, jnp.mean(kr, axis=0, keepdims=True), km_sc[:, cols])
            qr = rope(qq[:, part])
            qf_sc[:, cols] = qr
            q_ref[:, cols] = (qr * (HEAD_DIM ** -0.5 * LOG2_E)).astype(BF16)
            vt_ref[h] = vv[:, part].T.astype(BF16)

    nb = km_sc.shape[0]
    km = km_sc[...]
    lane_head = lax.broadcasted_iota(jnp.int32, km.shape, 1) // HEAD_DIM
    km_rows = jnp.concatenate([jnp.where(lane_head == h, km, 0.0) for h in range(N_HEADS)], axis=0)
    gate_t = _dot_split(km_rows, qf_sc[...], _dot_nt)
    blk = lax.broadcasted_iota(jnp.int32, (nb, x_ref.shape[0]), 0)
    past = blk < i
    for h in range(N_HEADS):
        g = jnp.where(past, gate_t[h * nb:(h + 1) * nb, :], NEG_INF)
        rank = jnp.zeros(g.shape, jnp.int32)
        for n in range(nb):
            gn = g[n:n + 1, :]
            ahead = (gn > g) | ((gn == g) & (blk > n))
            rank = rank + ahead.astype(jnp.int32)
        sel_ref[h * nb:(h + 1) * nb, :] = jnp.where(past & (rank < MOBA_TOPK), 0.0, NEG_INF)


def _qkv_proj(x, w_kvq, cos_full, sin_signed, batch, seq):
    t = x.shape[0]
    width = N_HEADS * HEAD_DIM
    rows = MOBA_BLOCK
    nb = seq // rows
    row_map = lambda b, i: (b * nb + i, 0)
    act = jax.ShapeDtypeStruct((t, width), BF16)
    return pl.pallas_call(
        _qkv_kernel,
        grid=(batch, nb),
        in_specs=[pl.BlockSpec((rows, D_MODEL), row_map),
                  pl.BlockSpec((D_MODEL, 3 * width), lambda b, i: (0, 0)),
                  pl.BlockSpec((rows, HEAD_DIM), lambda b, i: (i, 0)),
                  pl.BlockSpec((rows, HEAD_DIM), lambda b, i: (i, 0))],
        out_specs=[pl.BlockSpec((rows, width), row_map),
                   pl.BlockSpec((rows, width), row_map),
                   pl.BlockSpec((None, N_HEADS, HEAD_DIM, rows), lambda b, i: (b, 0, 0, i)),
                   pl.BlockSpec((None, N_HEADS * nb, rows), lambda b, i: (b * nb + i, 0, 0))],
        out_shape=[act, act,
                   jax.ShapeDtypeStruct((batch, N_HEADS, HEAD_DIM, seq), BF16),
                   jax.ShapeDtypeStruct((batch * nb, N_HEADS * nb, rows), F32)],
        scratch_shapes=[pltpu.VMEM((nb, width), F32), pltpu.VMEM((rows, width), F32)],
        compiler_params=_params("arbitrary", "arbitrary"),
        name="moba_qkv_proj",
    )(x, w_kvq, cos_full, sin_signed)


def _attn_kernel(q_ref, k_ref, vt_ref, sel_ref, o_ref, kaug_sc, vtaug_sc):
    blk = MOBA_BLOCK
    seq = q_ref.shape[0]
    nb = seq // blk
    extra = 2 * SUBLANES
    key = lax.broadcasted_iota(jnp.int32, (blk, blk), 0)
    qry = lax.broadcasted_iota(jnp.int32, (blk, blk), 1)
    causal = key <= qry

    key_blk = lax.broadcasted_iota(jnp.int32, (seq, LANES), 0) // blk
    kaug_sc[:, 0:HEAD_DIM] = k_ref[...]
    kaug_sc[:, HEAD_DIM:] = jnp.where(lax.broadcasted_iota(jnp.int32, (seq, LANES), 1) == key_blk,
                                      1.0, 0.0).astype(BF16)
    vtaug_sc[0:HEAD_DIM, :] = vt_ref[...]
    vtaug_sc[HEAD_DIM:, :] = jnp.where(lax.broadcasted_iota(jnp.int32, (extra, seq), 0) == 0,
                                       1.0, 0.0).astype(BF16)
    blk_row = lax.broadcasted_iota(jnp.int32, (nb, blk), 0)

    def scores(i):
        bias = jnp.where(blk_row == i, 0.0, sel_ref[i])
        bias_t = jnp.concatenate([bias, jnp.zeros((LANES - nb, blk), F32)], axis=0).T
        q_aug = jnp.concatenate([q_ref[i * blk:(i + 1) * blk, :], bias_t.astype(BF16)], axis=1)
        return _dot_nt(kaug_sc[0:(i + 1) * blk, :], q_aug)

    def finish(i, s):
        parts = [s[j * blk:(j + 1) * blk] for j in range(i)]
        parts.append(jnp.where(causal, s[i * blk:(i + 1) * blk], NEG_INF))
        m = jnp.max(parts[0], axis=0, keepdims=True)
        for part in parts[1:]:
            m = jnp.maximum(m, jnp.max(part, axis=0, keepdims=True))
        p = jnp.concatenate([jnp.exp2(part - m).astype(BF16) for part in parts], axis=0)
        acc = _dot(vtaug_sc[:, 0:(i + 1) * blk], p)
        o_ref[i * blk:(i + 1) * blk, :] = (acc[0:HEAD_DIM] / acc[HEAD_DIM:HEAD_DIM + 1]).T.astype(BF16)

    ahead = 3
    pending = [scores(i) for i in range(min(ahead, nb))]
    for i in range(nb):
        if i + ahead < nb:
            pending.append(scores(i + ahead))
        finish(i, pending.pop(0))


def _moba_attention(q, k, vt, sel, batch, seq):
    t = q.shape[0]
    width = N_HEADS * HEAD_DIM
    nb = seq // MOBA_BLOCK
    col = pl.BlockSpec((seq, HEAD_DIM), lambda b, h: (b, h))
    return pl.pallas_call(
        _attn_kernel,
        grid=(batch, N_HEADS),
        in_specs=[col, col,
                  pl.BlockSpec((None, None, HEAD_DIM, seq), lambda b, h: (b, h, 0, 0)),
                  pl.BlockSpec((nb, nb, MOBA_BLOCK), lambda b, h: (b, h, 0))],
        out_specs=col,
        out_shape=jax.ShapeDtypeStruct((t, width), BF16),
        scratch_shapes=[pltpu.VMEM((seq, HEAD_DIM + LANES), BF16),
                        pltpu.VMEM((HEAD_DIM + 2 * SUBLANES, seq), BF16)],
        compiler_params=_params("arbitrary", "arbitrary"),
        name="moba_attention",
    )(q, k, vt, sel)


def _combine_kernel(y0_ref, y1_ref, x_ref, route_ref, g_ref, b_ref, o_ref):
    route = route_ref[...]
    rows = x_ref.shape[0]
    y = (DEEPNORM_ALPHA * x_ref[...] + route[:, 0:1] * _tiles_to_rows(y0_ref, 0, rows)
         + route[:, 1:2] * _tiles_to_rows(y1_ref, 0, rows))
    o_ref[...] = _layer_norm(y, g_ref[...], b_ref[...])


def _moe_combine(ys, x, route, g_row, b_row):
    t = x.shape[0]
    rows = ROWS_COMBINE
    nt = t // rows
    row_map = lambda i: (i, 0)
    const = lambda i: (0, 0)
    return pl.pallas_call(
        _combine_kernel,
        grid=(nt,),
        in_specs=[pl.BlockSpec((rows * SUBLANES, LANES), row_map),
                  pl.BlockSpec((rows * SUBLANES, LANES), lambda i: (nt + i, 0)),
                  pl.BlockSpec((rows, D_MODEL), row_map),
                  pl.BlockSpec((rows, LANES), row_map),
                  pl.BlockSpec((1, D_MODEL), const),
                  pl.BlockSpec((1, D_MODEL), const)],
        out_specs=pl.BlockSpec((rows, D_MODEL), row_map),
        out_shape=jax.ShapeDtypeStruct((t, D_MODEL), F32),
        compiler_params=_params("arbitrary"),
        name="moe_combine",
    )(ys, ys, x, route, g_row, b_row)


def _routing_tables(route, n_tokens):
    rows = ROWS_FFN
    n_assign = n_tokens * TOP_K
    n_tiles = n_assign // rows + N_EXPERTS
    experts = route[:, 2:4].astype(jnp.int32).reshape(-1)
    onehot = (experts[:, None] == jnp.arange(N_EXPERTS, dtype=jnp.int32)[None, :]).astype(jnp.int32)
    counts = jnp.sum(onehot, axis=0)
    tiles_per = (counts + rows - 1) // rows
    tile_end = jnp.cumsum(tiles_per)
    id_bits = max(n_assign, rows).bit_length()
    real_keys = (experts << (id_bits + 1)) | jnp.arange(n_assign, dtype=jnp.int32)
    pad_id = jnp.arange(rows, dtype=jnp.int32)[None, :]
    pad_needed = pad_id < (tiles_per * rows - counts)[:, None]
    pad_keys = (jnp.arange(N_EXPERTS, dtype=jnp.int32)[:, None] << (id_bits + 1)) | (1 << id_bits) | pad_id
    pad_keys = jnp.where(pad_needed, pad_keys, jnp.iinfo(jnp.int32).max)
    keys = jnp.sort(jnp.concatenate([real_keys, pad_keys.reshape(-1)]))
    real = ((keys >> id_bits) & 1) == 0
    assign = jnp.where(real, keys & ((1 << id_bits) - 1), -1)
    row_token = jnp.where(real, assign // TOP_K, 0)
    pad_rank = jnp.minimum(jnp.cumsum(1 - real.astype(jnp.int32)) - 1, N_EXPERTS * rows - 1)
    row_dst = jnp.where(real, (assign % TOP_K) * n_tokens + assign // TOP_K, n_assign + pad_rank)
    tile_id = jnp.arange(n_tiles, dtype=jnp.int32)
    n_used = tile_end[-1]
    tile_expert = jnp.sum((tile_id[:, None] >= tile_end[None, :]).astype(jnp.int32), axis=1)
    last_expert = jnp.sum((n_used - 1 >= tile_end).astype(jnp.int32))
    tile_expert = jnp.minimum(jnp.where(tile_id < n_used, tile_expert, last_expert), N_EXPERTS - 1)
    return n_used.reshape(1), tile_expert, row_token, row_dst, n_assign + N_EXPERTS * rows


def _rope_tables(seq):
    half = HEAD_DIM // 2
    inv_freq = ROPE_THETA ** (-jnp.arange(half, dtype=F32) / half)
    ang = jnp.arange(seq).astype(F32)[:, None] * inv_freq[None, :]
    cos, sin = jnp.cos(ang), jnp.sin(ang)
    return jnp.concatenate([cos, cos], axis=1), jnp.concatenate([-sin, sin], axis=1)


def _pad_lanes(w):
    return jnp.pad(w, ((0, 0), (0, LANES - w.shape[1])))


def kernel(x, a_w_in, a_conv_w, a_log_decay, a_dt_bias, a_norm_w, a_w_out, b_w_kv, b_w_q, b_w_o,
           ffn_w1, ffn_w3, ffn_w2, moe_router, moe_w1, moe_w3, moe_w2, ln_g, ln_b):
    batch, seq, _ = x.shape
    t = batch * seq
    width = N_HEADS * HEAD_DIM
    x0 = x.reshape(t, D_MODEL)
    row = lambda v: v.reshape(1, -1).astype(F32)

    w_in = a_w_in[0]
    q, k, v, gate, gb = _in_proj(
        x0, w_in[:, :4 * width].astype(BF16), _pad_lanes(w_in[:, 4 * width:]).astype(BF16),
        a_conv_w[0], _pad_lanes(row(a_log_decay[0])), _pad_lanes(row(a_dt_bias[0])), batch, seq)
    og, moe_w3b = _delta_rule(q, k, v, gate, gb, row(a_norm_w[0]), batch, seq,
                              riders=(moe_w3[0].reshape(N_EXPERTS * D_MODEL, FFN_HIDDEN),))
    x1 = _proj_ln(og, a_w_out[0].astype(BF16), x0, row(ln_g[0, 0]), row(ln_b[0, 0]))
    x2, moe_w1b, moe_w2b = _ffn_dense(
        x1, ffn_w1[0].astype(BF16), ffn_w3[0].astype(BF16), ffn_w2[0].astype(BF16),
        row(ln_g[0, 1]), row(ln_b[0, 1]),
        riders=(moe_w1[0].reshape(N_EXPERTS * D_MODEL, FFN_HIDDEN),
                moe_w2[0].reshape(N_EXPERTS * FFN_HIDDEN, D_MODEL)))

    cos_full, sin_signed = _rope_tables(seq)
    w_kvq = jnp.concatenate([b_w_kv, b_w_q[0]], axis=1).astype(BF16)
    qm, km, vt, sel = _qkv_proj(x2, w_kvq, cos_full, sin_signed, batch, seq)
    attn = _moba_attention(qm, km, vt, sel, batch, seq)
    x3, x3_tiles, route = _proj_ln(attn, b_w_o[0].astype(BF16), x2, row(ln_g[1, 0]), row(ln_b[1, 0]),
                                   w_router=_pad_lanes(moe_router[0]))
    n_used, tile_expert, row_token, row_dst, n_out_rows = _routing_tables(route, t)
    ys = _moe_experts(n_used, tile_expert, row_token, row_dst, x3_tiles,
                      moe_w1b.reshape(N_EXPERTS, D_MODEL, FFN_HIDDEN),
                      moe_w3b.reshape(N_EXPERTS, D_MODEL, FFN_HIDDEN),
                      moe_w2b.reshape(N_EXPERTS, FFN_HIDDEN, D_MODEL), n_out_rows)
    x4 = _moe_combine(ys, x3, route, row(ln_g[1, 1]), row(ln_b[1, 1]))
    return x4.reshape(batch, seq, D_MODEL)
```

```python
import functools
import math

import jax
import jax.numpy as jnp
from jax import lax
from jax.experimental import pallas as pl
from jax.experimental.pallas import tpu as pltpu

D_MODEL = 1024
DEPTH = 2
DEEPNORM_ALPHA = (2.0 * DEPTH) ** 0.25
LN_EPS = 1e-5
N_HEADS = 8
HEAD_DIM = 128
DN_CONV = 4
DN_CHUNK = 64
DN_NORM_EPS = 1e-6
MOBA_BLOCK = 256
MOBA_TOPK = 3
ROPE_THETA = 10000.0
NEG_INF = -1e30
LOG2_E = math.log2(math.e)
FFN_HIDDEN = 3584
N_EXPERTS = 8
TOP_K = 2

LANES = 128
SUBLANES = 8
VMEM_LIMIT_BYTES = 56 * 1024 * 1024

ROWS_PROJ = 256
ROWS_DELTA = 256
ROWS_LN = 512
ROWS_LN_SUB = 256
ROWS_FFN = 512
FFN_CHUNK = 1792
ROWS_COMBINE = 512

F32 = jnp.float32
BF16 = jnp.bfloat16

_NT = (((1,), (1,)), ((), ()))


def _dot(a, b):
    return jnp.dot(a, b, preferred_element_type=F32)


def _dot_nt(a, b):
    return lax.dot_general(a, b, _NT, preferred_element_type=F32)


def _params(*sem):
    return pltpu.CompilerParams(dimension_semantics=sem, vmem_limit_bytes=VMEM_LIMIT_BYTES)


def _layer_norm(y, g, b):
    mu = jnp.mean(y, axis=-1, keepdims=True)
    d = y - mu
    var = jnp.mean(d * d, axis=-1, keepdims=True)
    return d * lax.rsqrt(var + LN_EPS) * g + b


def _sigmoid(x):
    return 1.0 / (1.0 + jnp.exp(-x))


def _split2(x):
    hi = x.astype(BF16)
    lo = (x - hi.astype(F32)).astype(BF16)
    return hi, lo


def _dot_split(a, b, dot):
    a_hi, a_lo = _split2(a)
    b_hi, b_lo = _split2(b)
    return dot(a_hi, b_hi) + dot(a_hi, b_lo) + dot(a_lo, b_hi)


def _rider_specs(arrays, n_steps, index_map):
    specs = [pl.BlockSpec((a.shape[0] // n_steps, a.shape[1]), index_map) for a in arrays]
    return specs, [jax.ShapeDtypeStruct(a.shape, BF16) for a in arrays]


def _cast_riders(src_refs, dst_refs):
    for src, dst in zip(src_refs, dst_refs):
        dst[...] = src[...].astype(BF16)


def _in_proj_kernel(n_riders, x_ref, w_ref, wab_ref, cw_ref, alog_ref, dtb_ref, *rest):
    riders_in, rest = rest[:n_riders], rest[n_riders:]
    q_ref, k_ref, v_ref, gate_ref, gb_ref = rest[:5]
    riders_out, conv_sc = rest[5:5 + n_riders], rest[5 + n_riders]
    _cast_riders(riders_in, riders_out)
    rows = x_ref.shape[0]
    width = N_HEADS * HEAD_DIM
    hist = SUBLANES

    @pl.when(pl.program_id(1) == 0)
    def _():
        conv_sc[...] = jnp.zeros(conv_sc.shape, F32)

    xb = x_ref[...].astype(BF16)
    chunk = 2 * HEAD_DIM
    for c in range(3 * width // chunk):
        proj = _dot(xb, w_ref[:, c * chunk:(c + 1) * chunk])
        which = c * chunk // width
        for hh in range(chunk // HEAD_DIM):
            slab = c * (chunk // HEAD_DIM) + hh
            cols = slice(slab * HEAD_DIM, (slab + 1) * HEAD_DIM)
            cur = proj[:, hh * HEAD_DIM:(hh + 1) * HEAD_DIM]
            conv_sc[slab, pl.ds(2 * hist, rows, stride=2), :] = cur
            y = cur * cw_ref[DN_CONV - 1:DN_CONV, cols]
            for j in range(DN_CONV - 1):
                shifted = conv_sc[slab, pl.ds(2 * (hist - (DN_CONV - 1) + j), rows, stride=2), :]
                y = y + shifted * cw_ref[j:j + 1, cols]
            conv_sc[slab, 0:2 * hist, :] = conv_sc[slab, 2 * rows:2 * (rows + hist), :]
            t = y * _sigmoid(y)
            ocols = slice(slab * HEAD_DIM - which * width, (slab + 1) * HEAD_DIM - which * width)
            if which == 2:
                v_ref[:, ocols] = t.astype(BF16)
            else:
                inv_norm = lax.rsqrt(jnp.sum(t * t, axis=-1, keepdims=True) + DN_NORM_EPS)
                if which == 0:
                    q_ref[:, ocols] = (t * (inv_norm * (HEAD_DIM ** -0.5))).astype(BF16)
                else:
                    k_ref[:, ocols] = (t * inv_norm).astype(BF16)
        if c % 3 == 2:
            g0 = (c // 3) * chunk
            gate_ref[:, g0:g0 + chunk] = _dot(xb, w_ref[:, 3 * width + g0:3 * width + g0 + chunk]).astype(BF16)
    ab = _dot(xb, wab_ref[...])
    z = ab + dtb_ref[...]
    softplus = jnp.maximum(z, 0.0) + jnp.log(1.0 + jnp.exp(-jnp.abs(z)))
    g = -jnp.exp(alog_ref[...]) * softplus
    lane = lax.broadcasted_iota(jnp.int32, ab.shape, 1)
    gb_ref[...] = jnp.where(lane < N_HEADS, g, _sigmoid(ab))


def _in_proj(x2d, w_main, w_ab, conv_w, a_log_row, dt_row, batch, seq, riders=()):
    t = x2d.shape[0]
    width = N_HEADS * HEAD_DIM
    rows = ROWS_PROJ
    nt = seq // rows
    row_map = lambda b, s: (b * nt + s, 0)
    const = lambda b, s: (0, 0)
    act = jax.ShapeDtypeStruct((t, width), BF16)
    rider_specs, rider_shapes = _rider_specs(riders, batch * nt, row_map)
    return pl.pallas_call(
        functools.partial(_in_proj_kernel, len(riders)),
        grid=(batch, nt),
        in_specs=[pl.BlockSpec((rows, D_MODEL), row_map),
                  pl.BlockSpec((D_MODEL, 4 * width), const),
                  pl.BlockSpec((D_MODEL, LANES), const),
                  pl.BlockSpec((DN_CONV, 3 * width), const),
                  pl.BlockSpec((1, LANES), const),
                  pl.BlockSpec((1, LANES), const)] + rider_specs,
        out_specs=[pl.BlockSpec((rows, width), row_map)] * 4 + [pl.BlockSpec((rows, LANES), row_map)]
        + rider_specs,
        out_shape=[act, act, act, act, jax.ShapeDtypeStruct((t, LANES), F32)] + rider_shapes,
        scratch_shapes=[pltpu.VMEM((3 * width // LANES, 2 * (rows + SUBLANES), LANES), F32)],
        compiler_params=_params("arbitrary", "arbitrary"),
        name="dn_in_proj",
    )(x2d, w_main, w_ab, conv_w, a_log_row, dt_row, *riders)


def _delta_kernel(n_riders, q_ref, k_ref, v_ref, gate_ref, gb_ref, nw_ref, *rest):
    riders_in, o_ref = rest[:n_riders], rest[n_riders]
    riders_out, state_sc = rest[n_riders + 1:2 * n_riders + 1], rest[2 * n_riders + 1]
    _cast_riders(riders_in, riders_out)
    c = DN_CHUNK
    two = 2 * c
    n_sub = q_ref.shape[0] // c
    n_pairs = N_HEADS // 2

    @pl.when(pl.program_id(1) == 0)
    def _():
        state_sc[...] = jnp.zeros(state_sc.shape, F32)

    r = lax.broadcasted_iota(jnp.int32, (two, two), 0)
    cc = lax.broadcasted_iota(jnp.int32, (two, two), 1)
    same = (r < c) == (cc < c)
    tri = same & (r >= cc)
    stri = same & (r > cc)
    eye = (r == cc).astype(F32)
    top_rows = lax.broadcasted_iota(jnp.int32, (two, 1), 0) < c
    top_lanes = lax.broadcasted_iota(jnp.int32, (1, two), 1) < c
    row64 = lax.broadcasted_iota(jnp.int32, (c, LANES), 0)

    def stack(ref, rows, h):
        t = ref[rows, h * HEAD_DIM:(h + 2) * HEAD_DIM].astype(F32)
        return jnp.concatenate([t[:, :HEAD_DIM], t[:, HEAD_DIM:]], axis=0)

    items = []
    for s in range(n_sub):
        rows = slice(s * c, (s + 1) * c)
        gb = gb_ref[rows, :]
        gc = gb
        sh = 1
        while sh < c:
            gc = gc + jnp.where(row64 >= sh, pltpu.roll(gc, sh, axis=0), 0.0)
            sh *= 2
        gc_pair = jnp.concatenate([gc, pltpu.roll(gc, LANES - 1, axis=1)], axis=0)
        gb_pair = jnp.concatenate([gb, pltpu.roll(gb, LANES - 1, axis=1)], axis=0)
        gc_pair_t = gc_pair.T
        for p in range(n_pairs):
            h = 2 * p
            q2, k2, v2 = stack(q_ref, rows, h), stack(k_ref, rows, h), stack(v_ref, rows, h)
            gcol = gc_pair[:, h:h + 1]
            grow = gc_pair_t[h:h + 1, :]
            beta = gb_pair[:, N_HEADS + h:N_HEADS + h + 1]
            glast = jnp.where(top_rows, gc_pair[c - 1:c, h:h + 1], gc_pair[two - 1:two, h:h + 1])
            eg = jnp.exp(gcol)
            kb = k2 * beta
            items.append(dict(
                rows=rows, h=h, k2b=k2.astype(BF16), q2b=q2.astype(BF16), kbb=kb.astype(BF16),
                decay=jnp.exp(jnp.where(tri, gcol - grow, NEG_INF)),
                rhs=jnp.concatenate([v2 * beta, kb * eg], axis=1).astype(BF16),
                qd=(q2 * eg).astype(BF16),
                kd_t=(k2 * jnp.exp(glast - gcol)).T,
                eglast=jnp.exp(glast)))
    for it in items:
        it["pw"] = jnp.where(stri, _dot_nt(it["kbb"], it["k2b"]) * it["decay"], 0.0)
        it["inv"] = eye - it["pw"]
    for _ in range(int(math.log2(c)) - 1):
        for it in items:
            pwb = it["pw"].astype(BF16)
            it["pw"] = _dot(pwb, pwb)
        for it in items:
            it["inv"] = it["inv"] + _dot(it["inv"].astype(BF16), it["pw"].astype(BF16))
    for it in items:
        sol = _dot(it["inv"].astype(BF16), it["rhs"])
        it["u"] = sol[:, :HEAD_DIM]
        it["wb"] = sol[:, HEAD_DIM:].astype(BF16)
    for it in items:
        it["amat"] = jnp.where(tri, _dot_nt(it["q2b"], it["k2b"]) * it["decay"], 0.0).astype(BF16)

    for s in range(n_sub):
        group = items[s * n_pairs:(s + 1) * n_pairs]
        for it in group:
            h = it["h"]
            it["s0"] = state_sc[h]
            it["s1"] = state_sc[h + 1]
            it["r0"] = _dot(jnp.concatenate([it["wb"][:c], it["qd"][:c]], axis=0), it["s0"].astype(BF16))
            it["r1"] = _dot(jnp.concatenate([it["wb"][c:], it["qd"][c:]], axis=0), it["s1"].astype(BF16))
        for it in group:
            v_new = it["u"] - jnp.concatenate([it["r0"][:c], it["r1"][:c]], axis=0)
            it["vnb"] = v_new.astype(BF16)
            it["o"] = jnp.concatenate([it["r0"][c:], it["r1"][c:]], axis=0) + _dot(it["amat"], it["vnb"])
        for it in group:
            h = it["h"]
            kd_t = it["kd_t"]
            state_sc[h] = it["s0"] * it["eglast"][0:1, :] + _dot(
                jnp.where(top_lanes, kd_t, 0.0).astype(BF16), it["vnb"])
            state_sc[h + 1] = it["s1"] * it["eglast"][c:c + 1, :] + _dot(
                jnp.where(top_lanes, 0.0, kd_t).astype(BF16), it["vnb"])
        for it in group:
            h, rows, o = it["h"], it["rows"], it["o"]
            gate2 = stack(gate_ref, rows, h)
            o = o * lax.rsqrt(jnp.mean(o * o, axis=-1, keepdims=True) + DN_NORM_EPS) * nw_ref[...]
            o = o * (gate2 * _sigmoid(gate2))
            o_ref[rows, h * HEAD_DIM:(h + 1) * HEAD_DIM] = o[:c].astype(BF16)
            o_ref[rows, (h + 1) * HEAD_DIM:(h + 2) * HEAD_DIM] = o[c:].astype(BF16)


def _delta_rule(q, k, v, gate, gb, norm_w_row, batch, seq, riders=()):
    t = q.shape[0]
    width = N_HEADS * HEAD_DIM
    rows = ROWS_DELTA
    ns = seq // rows
    row_map = lambda b, s: (b * ns + s, 0)
    const = lambda b, s: (0, 0)
    wide = pl.BlockSpec((rows, width), row_map)
    rider_specs, rider_shapes = _rider_specs(riders, batch * ns, row_map)
    return pl.pallas_call(
        functools.partial(_delta_kernel, len(riders)),
        grid=(batch, ns),
        in_specs=[wide, wide, wide, wide,
                  pl.BlockSpec((rows, LANES), row_map),
                  pl.BlockSpec((1, HEAD_DIM), const)] + rider_specs,
        out_specs=[wide] + rider_specs,
        out_shape=[jax.ShapeDtypeStruct((t, width), BF16)] + rider_shapes,
        scratch_shapes=[pltpu.VMEM((N_HEADS, HEAD_DIM, HEAD_DIM), F32)],
        compiler_params=_params("arbitrary", "arbitrary"),
        name="dn_delta_rule",
    )(q, k, v, gate, gb, norm_w_row, *riders)


def _rows_to_tiles(ref, base, y):
    for k in range(D_MODEL // LANES):
        ref[pl.ds(base + k, y.shape[0], stride=SUBLANES), :] = y[:, k * LANES:(k + 1) * LANES]


def _tiles_to_rows(ref, base, n):
    return jnp.concatenate([ref[pl.ds(base + k, n, stride=SUBLANES), :] for k in range(D_MODEL // LANES)],
                           axis=1)


def _proj_ln_router_kernel(a_ref, w_ref, res_ref, g_ref, b_ref, wr_ref, o_ref, ot_ref, route_ref):
    for rb in range(a_ref.shape[0] // ROWS_LN_SUB):
        rows = slice(rb * ROWS_LN_SUB, (rb + 1) * ROWS_LN_SUB)
        y = DEEPNORM_ALPHA * res_ref[rows, :] + _dot(a_ref[rows, :], w_ref[...])
        xn = _layer_norm(y, g_ref[...], b_ref[...])
        o_ref[rows, :] = xn
        _rows_to_tiles(ot_ref, rb * ROWS_LN_SUB * SUBLANES, xn)
        logits = _dot_split(xn, wr_ref[...], _dot)
        lane = lax.broadcasted_iota(jnp.int32, logits.shape, 1)
        logits = jnp.where(lane < N_EXPERTS, logits, NEG_INF)
        l1 = jnp.max(logits, axis=-1, keepdims=True)
        i1 = jnp.min(jnp.where(logits == l1, lane, LANES), axis=-1, keepdims=True)
        rest = jnp.where(lane == i1, NEG_INF, logits)
        l2 = jnp.max(rest, axis=-1, keepdims=True)
        i2 = jnp.min(jnp.where(rest == l2, lane, LANES), axis=-1, keepdims=True)
        e2 = jnp.exp(l2 - l1)
        g1 = 1.0 / (1.0 + e2)
        g2 = e2 / (1.0 + e2)
        route_ref[rows, :] = jnp.where(lane == 0, g1,
                             jnp.where(lane == 1, g2,
                             jnp.where(lane == 2, i1.astype(F32),
                             jnp.where(lane == 3, i2.astype(F32), 0.0))))


def _proj_ln_router(a, w, res, g_row, b_row, w_router):
    t, kdim = a.shape
    rows = ROWS_LN
    row_map = lambda i: (i, 0)
    const = lambda i: (0, 0)
    in_specs = [pl.BlockSpec((rows, kdim), row_map),
                pl.BlockSpec((kdim, D_MODEL), const),
                pl.BlockSpec((rows, D_MODEL), row_map),
                pl.BlockSpec((1, D_MODEL), const),
                pl.BlockSpec((1, D_MODEL), const)]
    out_full = jax.ShapeDtypeStruct((t, D_MODEL), F32)
    return pl.pallas_call(
        _proj_ln_router_kernel, grid=(t // rows,),
        in_specs=in_specs + [pl.BlockSpec((D_MODEL, LANES), const)],
        out_specs=[pl.BlockSpec((rows, D_MODEL), row_map),
                   pl.BlockSpec((rows * SUBLANES, LANES), row_map),
                   pl.BlockSpec((rows, LANES), row_map)],
        out_shape=[out_full, jax.ShapeDtypeStruct((t * SUBLANES, LANES), F32),
                   jax.ShapeDtypeStruct((t, LANES), F32)],
        compiler_params=_params("arbitrary"), name="proj_ln_router",
    )(a, w, res, g_row, b_row, w_router)


def _swiglu_hidden(x, w1_ref, w3_ref, hid_sc):
    xb = x.astype(BF16)
    sub = 2 * LANES
    for c in range(hid_sc.shape[1] // sub):
        cols = slice(c * sub, (c + 1) * sub)
        h1 = _dot(xb, w1_ref[:, cols])
        h3 = _dot(xb, w3_ref[:, cols])
        hid_sc[:, cols] = (h1 * _sigmoid(h1) * h3).astype(BF16)


def _swiglu_part(x, w1_ref, w3_ref, w2_ref, hid_sc):
    _swiglu_hidden(x, w1_ref, w3_ref, hid_sc)
    return _dot(hid_sc[...], w2_ref[...])


def _mixer_ffn_kernel(a_ref, wo_ref, res_ref, g0_ref, b0_ref, w1_ref, w3_ref, w2_ref, g1_ref, b1_ref,
                      o_ref, x_sc, acc_sc, hid_sc):
    f = pl.program_id(1)
    last_f = pl.num_programs(1) - 1
    n_sub = a_ref.shape[0] // ROWS_LN_SUB

    @pl.when(f == 0)
    def _():
        for rb in range(n_sub):
            rows = slice(rb * ROWS_LN_SUB, (rb + 1) * ROWS_LN_SUB)
            y = DEEPNORM_ALPHA * res_ref[rows, :] + _dot(a_ref[rows, :], wo_ref[...])
            x_sc[rows, :] = _layer_norm(y, g0_ref[...], b0_ref[...])

    _swiglu_hidden(x_sc[...], w1_ref, w3_ref, hid_sc)

    @pl.when(f == 0)
    def _():
        acc_sc[...] = _dot(hid_sc[...], w2_ref[...])

    @pl.when((f > 0) & (f < last_f))
    def _():
        acc_sc[...] += _dot(hid_sc[...], w2_ref[...])

    @pl.when(f == last_f)
    def _():
        for rb in range(n_sub):
            rows = slice(rb * ROWS_LN_SUB, (rb + 1) * ROWS_LN_SUB)
            y = DEEPNORM_ALPHA * x_sc[rows, :] + (acc_sc[rows, :] + _dot(hid_sc[rows, :], w2_ref[...]))
            o_ref[rows, :] = _layer_norm(y, g1_ref[...], b1_ref[...])


def _mixer_ffn(a, wo, res, g0_row, b0_row, w1, w3, w2, g1_row, b1_row):
    t = res.shape[0]
    rows, fc = ROWS_FFN, FFN_CHUNK
    nf = FFN_HIDDEN // fc
    assert nf >= 2
    row_map = lambda i, f: (i, 0)
    const = lambda i, f: (0, 0)
    vec = pl.BlockSpec((1, D_MODEL), const)
    return pl.pallas_call(
        _mixer_ffn_kernel,
        grid=(t // rows, nf),
        in_specs=[pl.BlockSpec((rows, a.shape[1]), row_map),
                  pl.BlockSpec((a.shape[1], D_MODEL), const),
                  pl.BlockSpec((rows, D_MODEL), row_map), vec, vec,
                  pl.BlockSpec((D_MODEL, fc), lambda i, f: (0, f)),
                  pl.BlockSpec((D_MODEL, fc), lambda i, f: (0, f)),
                  pl.BlockSpec((fc, D_MODEL), lambda i, f: (f, 0)), vec, vec],
        out_specs=pl.BlockSpec((rows, D_MODEL), row_map),
        out_shape=jax.ShapeDtypeStruct((t, D_MODEL), F32),
        scratch_shapes=[pltpu.VMEM((rows, D_MODEL), F32), pltpu.VMEM((rows, D_MODEL), F32),
                        pltpu.VMEM((rows, fc), BF16)],
        compiler_params=_params("arbitrary", "arbitrary"),
        name="mixer_ffn_dense",
    )(a, wo, res, g0_row, b0_row, w1, w3, w2, g1_row, b1_row)


def _moe_experts_kernel(nu_ref, te_ref, idx_ref, idx_next_ref, dst_ref, x_hbm, w1_ref, w3_ref, w2_ref,
                        y_hbm, xbuf, ybuf, xb_sc, acc_sc, hid_sc, gsem, ssem):
    i = pl.program_id(0)
    f = pl.program_id(1)
    last_f = pl.num_programs(1) - 1
    rows = xb_sc.shape[0]
    tile_rows = rows * SUBLANES
    nu = nu_ref[0]
    slot = i % 2

    def gather_start(idx, s):
        for r in range(rows):
            src = x_hbm.at[pl.ds(pl.multiple_of(idx[0, r], SUBLANES), SUBLANES)]
            pltpu.make_async_copy(src, xbuf.at[pl.ds(s * tile_rows + r * SUBLANES, SUBLANES)], gsem.at[s]).start()

    def scatter_start(s):
        for r in range(rows):
            dst = y_hbm.at[pl.ds(pl.multiple_of(dst_ref[0, r], SUBLANES), SUBLANES)]
            pltpu.make_async_copy(ybuf.at[pl.ds(s * tile_rows + r * SUBLANES, SUBLANES)], dst, ssem.at[s]).start()

    def wait_all(buf, sem, s):
        view = buf.at[pl.ds(pl.multiple_of(s * tile_rows, tile_rows), tile_rows)]
        pltpu.make_async_copy(view, view, sem.at[s]).wait()

    def for_slot(cond, s, fn):
        for static_slot in range(2):
            pl.when(cond & (s == static_slot))(functools.partial(fn, static_slot))

    first = f == 0

    @pl.when(first & (i == 0))
    def _():
        n_real = TOP_K * x_hbm.shape[0]
        ybuf[tile_rows:2 * tile_rows, :] = jnp.zeros((tile_rows, LANES), F32)
        fills = [pltpu.make_async_copy(ybuf.at[pl.ds(tile_rows, tile_rows)],
                                       y_hbm.at[pl.ds(n_real + k * tile_rows, tile_rows)], ssem.at[1])
                 for k in range((y_hbm.shape[0] - n_real) // tile_rows)]
        for cp in fills:
            cp.start()
        for cp in fills:
            cp.wait()

    @pl.when(first & (i == 0) & (nu > 0))
    def _():
        gather_start(idx_ref, 0)

    @pl.when(first & (i < nu))
    def _():
        wait_all(xbuf, gsem, slot)

    for_slot(first & (i + 1 < nu), 1 - slot, functools.partial(gather_start, idx_next_ref))

    @pl.when(i < nu)
    def _():
        @pl.when(first)
        def _():
            base = pl.multiple_of(slot * tile_rows, tile_rows)
            for k in range(D_MODEL // LANES):
                xb_sc[:, k * LANES:(k + 1) * LANES] = xbuf[pl.ds(base + k, rows, stride=SUBLANES), :].astype(BF16)

        part = _swiglu_part(xb_sc[...], w1_ref, w3_ref, w2_ref, hid_sc)

        @pl.when(first)
        def _():
            acc_sc[...] = part

        @pl.when((f > 0) & (f < last_f))
        def _():
            acc_sc[...] += part

        @pl.when(f == last_f)
        def _():
            @pl.when(i >= 2)
            def _():
                wait_all(ybuf, ssem, slot)
            _rows_to_tiles(ybuf, pl.multiple_of(slot * tile_rows, tile_rows), acc_sc[...] + part)

        for_slot(f == last_f, slot, scatter_start)

    @pl.when((i == pl.num_programs(0) - 1) & (f == last_f))
    def _():
        @pl.when(nu >= 1)
        def _():
            wait_all(ybuf, ssem, (nu - 1) % 2)

        @pl.when(nu >= 2)
        def _():
            wait_all(ybuf, ssem, nu % 2)


def _moe_experts(n_used, tile_expert, row_token, row_dst, x_tiles, w1, w3, w2, n_out_rows):
    n_tiles = tile_expert.shape[0]
    rows, fc = ROWS_FFN, FFN_CHUNK
    nf = FFN_HIDDEN // fc
    idx3 = (row_token * SUBLANES).reshape(n_tiles, 1, rows)
    dst3 = (row_dst * SUBLANES).reshape(n_tiles, 1, rows)
    smem_tile = lambda fn: pl.BlockSpec((None, 1, rows), fn, memory_space=pltpu.SMEM)
    grid_spec = pltpu.PrefetchScalarGridSpec(
        num_scalar_prefetch=2,
        grid=(n_tiles, nf),
        in_specs=[smem_tile(lambda i, f, nu, te: (i, 0, 0)),
                  smem_tile(lambda i, f, nu, te: (jnp.minimum(i + 1, n_tiles - 1), 0, 0)),
                  smem_tile(lambda i, f, nu, te: (i, 0, 0)),
                  pl.BlockSpec(memory_space=pl.ANY),
                  pl.BlockSpec((None, D_MODEL, fc), lambda i, f, nu, te: (te[i], 0, f)),
                  pl.BlockSpec((None, D_MODEL, fc), lambda i, f, nu, te: (te[i], 0, f)),
                  pl.BlockSpec((None, fc, D_MODEL), lambda i, f, nu, te: (te[i], f, 0))],
        out_specs=pl.BlockSpec(memory_space=pl.ANY),
        scratch_shapes=[pltpu.VMEM((2 * rows * SUBLANES, LANES), F32), pltpu.VMEM((2 * rows * SUBLANES, LANES), F32),
                        pltpu.VMEM((rows, D_MODEL), BF16),
                        pltpu.VMEM((rows, D_MODEL), F32), pltpu.VMEM((rows, fc), BF16),
                        pltpu.SemaphoreType.DMA((2,)), pltpu.SemaphoreType.DMA((2,))],
    )
    return pl.pallas_call(
        _moe_experts_kernel,
        grid_spec=grid_spec,
        out_shape=jax.ShapeDtypeStruct((n_out_rows * SUBLANES, LANES), F32),
        compiler_params=_params("arbitrary", "arbitrary"),
        name="moe_experts",
    )(n_used, tile_expert, idx3, idx3, dst3, x_tiles, w1, w3, w2)


def _qkv_kernel(x_ref, w_ref, cos_ref, sin_ref, q_ref, k_ref, vt_ref, sel_ref, km_sc, qf_sc):
    i = pl.program_id(1)
    width = N_HEADS * HEAD_DIM
    half = HEAD_DIM // 2
    xb = x_ref[...].astype(BF16)
    cosf = cos_ref[...]
    sinf = sin_ref[...]

    @pl.when(i == 0)
    def _():
        km_sc[...] = jnp.zeros(km_sc.shape, F32)

    def rope(t):
        return t * cosf + pltpu.roll(t, half, axis=1) * sinf

    chunk = 2 * HEAD_DIM
    for c in range(width // chunk):
        kk = _dot(xb, w_ref[:, c * chunk:(c + 1) * chunk])
        vv = _dot(xb, w_ref[:, width + c * chunk:width + (c + 1) * chunk])
        qq = _dot(xb, w_ref[:, 2 * width + c * chunk:2 * width + (c + 1) * chunk])
        for hh in range(2):
            h = 2 * c + hh
            cols = slice(h * HEAD_DIM, (h + 1) * HEAD_DIM)
            part = slice(hh * HEAD_DIM, (hh + 1) * HEAD_DIM)
            kr = rope(kk[:, part])
            k_ref[:, cols] = kr.astype(BF16)
            km_row = lax.broadcasted_iota(jnp.int32, (km_sc.shape[0], HEAD_DIM), 0)
            km_sc[:, cols] = jnp.where(km_row == i, jnp.mean(kr, axis=0, keepdims=True), km_sc[:, cols])
            qr = rope(qq[:, part])
            qf_sc[:, cols] = qr
            q_ref[:, cols] = (qr * (HEAD_DIM ** -0.5 * LOG2_E)).astype(BF16)
            vt_ref[h] = vv[:, part].T.astype(BF16)

    nb = km_sc.shape[0]
    km = km_sc[...]
    lane_head = lax.broadcasted_iota(jnp.int32, km.shape, 1) // HEAD_DIM
    km_rows = jnp.concatenate([jnp.where(lane_head == h, km, 0.0) for h in range(N_HEADS)], axis=0)
    gate_t = _dot_split(km_rows, qf_sc[...], _dot_nt)
    blk = lax.broadcasted_iota(jnp.int32, (nb, x_ref.shape[0]), 0)
    past = blk < i
    for h in range(N_HEADS):
        g = jnp.where(past, gate_t[h * nb:(h + 1) * nb, :], NEG_INF)
        rank = jnp.zeros(g.shape, jnp.int32)
        for n in range(nb):
            gn = g[n:n + 1, :]
            ahead = (gn > g) | ((gn == g) & (blk > n))
            rank = rank + ahead.astype(jnp.int32)
        sel_ref[h * nb:(h + 1) * nb, :] = jnp.where(past & (rank < MOBA_TOPK), 0.0, NEG_INF)


def _qkv_proj(x, w_kvq, cos_full, sin_signed, batch, seq):
    t = x.shape[0]
    width = N_HEADS * HEAD_DIM
    rows = MOBA_BLOCK
    nb = seq // rows
    row_map = lambda b, i: (b * nb + i, 0)
    act = jax.ShapeDtypeStruct((t, width), BF16)
    return pl.pallas_call(
        _qkv_kernel,
        grid=(batch, nb),
        in_specs=[pl.BlockSpec((rows, D_MODEL), row_map),
                  pl.BlockSpec((D_MODEL, 3 * width), lambda b, i: (0, 0)),
                  pl.BlockSpec((rows, HEAD_DIM), lambda b, i: (i, 0)),
                  pl.BlockSpec((rows, HEAD_DIM), lambda b, i: (i, 0))],
        out_specs=[pl.BlockSpec((rows, width), row_map),
                   pl.BlockSpec((rows, width), row_map),
                   pl.BlockSpec((None, N_HEADS, HEAD_DIM, rows), lambda b, i: (b, 0, 0, i)),
                   pl.BlockSpec((None, N_HEADS * nb, rows), lambda b, i: (b * nb + i, 0, 0))],
        out_shape=[act, act,
                   jax.ShapeDtypeStruct((batch, N_HEADS, HEAD_DIM, seq), BF16),
                   jax.ShapeDtypeStruct((batch * nb, N_HEADS * nb, rows), F32)],
        scratch_shapes=[pltpu.VMEM((nb, width), F32), pltpu.VMEM((rows, width), F32)],
        compiler_params=_params("arbitrary", "arbitrary"),
        name="moba_qkv_proj",
    )(x, w_kvq, cos_full, sin_signed)


def _attn_kernel(q_ref, k_ref, vt_ref, sel_ref, o_ref, kaug_sc, vtaug_sc):
    blk = MOBA_BLOCK
    seq = q_ref.shape[0]
    nb = seq // blk
    extra = 2 * SUBLANES
    key = lax.broadcasted_iota(jnp.int32, (blk, blk), 0)
    qry = lax.broadcasted_iota(jnp.int32, (blk, blk), 1)
    causal = key <= qry

    key_blk = lax.broadcasted_iota(jnp.int32, (seq, LANES), 0) // blk
    kaug_sc[:, 0:HEAD_DIM] = k_ref[...]
    kaug_sc[:, HEAD_DIM:] = jnp.where(lax.broadcasted_iota(jnp.int32, (seq, LANES), 1) == key_blk,
                                      1.0, 0.0).astype(BF16)
    vtaug_sc[0:HEAD_DIM, :] = vt_ref[...]
    vtaug_sc[HEAD_DIM:, :] = jnp.where(lax.broadcasted_iota(jnp.int32, (extra, seq), 0) == 0,
                                       1.0, 0.0).astype(BF16)
    blk_row = lax.broadcasted_iota(jnp.int32, (nb, blk), 0)

    def scores(i):
        bias = jnp.where(blk_row == i, 0.0, sel_ref[i])
        bias_t = jnp.concatenate([bias, jnp.zeros((LANES - nb, blk), F32)], axis=0).T
        q_aug = jnp.concatenate([q_ref[i * blk:(i + 1) * blk, :], bias_t.astype(BF16)], axis=1)
        return _dot_nt(kaug_sc[0:(i + 1) * blk, :], q_aug)

    def finish(i, s):
        parts = [s[j * blk:(j + 1) * blk] for j in range(i)]
        parts.append(jnp.where(causal, s[i * blk:(i + 1) * blk], NEG_INF))
        m = jnp.max(parts[0], axis=0, keepdims=True)
        for part in parts[1:]:
            m = jnp.maximum(m, jnp.max(part, axis=0, keepdims=True))
        p = jnp.concatenate([jnp.exp2(part - m).astype(BF16) for part in parts], axis=0)
        acc = _dot(vtaug_sc[:, 0:(i + 1) * blk], p)
        o_ref[i * blk:(i + 1) * blk, :] = (acc[0:HEAD_DIM] / acc[HEAD_DIM:HEAD_DIM + 1]).T.astype(BF16)

    ahead = 3
    pending = [scores(i) for i in range(min(ahead, nb))]
    for i in range(nb):
        if i + ahead < nb:
            pending.append(scores(i + ahead))
        finish(i, pending.pop(0))


def _moba_attention(q, k, vt, sel, batch, seq):
    t = q.shape[0]
    width = N_HEADS * HEAD_DIM
    nb = seq // MOBA_BLOCK
    col = pl.BlockSpec((seq, HEAD_DIM), lambda b, h: (b, h))
    return pl.pallas_call(
        _attn_kernel,
        grid=(batch, N_HEADS),
        in_specs=[col, col,
                  pl.BlockSpec((None, None, HEAD_DIM, seq), lambda b, h: (b, h, 0, 0)),
                  pl.BlockSpec((nb, nb, MOBA_BLOCK), lambda b, h: (b, h, 0))],
        out_specs=col,
        out_shape=jax.ShapeDtypeStruct((t, width), BF16),
        scratch_shapes=[pltpu.VMEM((seq, HEAD_DIM + LANES), BF16),
                        pltpu.VMEM((HEAD_DIM + 2 * SUBLANES, seq), BF16)],
        compiler_params=_params("arbitrary", "arbitrary"),
        name="moba_attention",
    )(q, k, vt, sel)


def _combine_kernel(y0_ref, y1_ref, x_ref, route_ref, g_ref, b_ref, o_ref):
    route = route_ref[...]
    rows = x_ref.shape[0]
    y = (DEEPNORM_ALPHA * x_ref[...] + route[:, 0:1] * _tiles_to_rows(y0_ref, 0, rows)
         + route[:, 1:2] * _tiles_to_rows(y1_ref, 0, rows))
    o_ref[...] = _layer_norm(y, g_ref[...], b_ref[...])


def _moe_combine(ys, x, route, g_row, b_row):
    t = x.shape[0]
    rows = ROWS_COMBINE
    nt = t // rows
    row_map = lambda i: (i, 0)
    const = lambda i: (0, 0)
    return pl.pallas_call(
        _combine_kernel,
        grid=(nt,),
        in_specs=[pl.BlockSpec((rows * SUBLANES, LANES), row_map),
                  pl.BlockSpec((rows * SUBLANES, LANES), lambda i: (nt + i, 0)),
                  pl.BlockSpec((rows, D_MODEL), row_map),
                  pl.BlockSpec((rows, LANES), row_map),
                  pl.BlockSpec((1, D_MODEL), const),
                  pl.BlockSpec((1, D_MODEL), const)],
        out_specs=pl.BlockSpec((rows, D_MODEL), row_map),
        out_shape=jax.ShapeDtypeStruct((t, D_MODEL), F32),
        compiler_params=_params("arbitrary"),
        name="moe_combine",
    )(ys, ys, x, route, g_row, b_row)


def _routing_tables(route, n_tokens):
    rows = ROWS_FFN
    n_assign = n_tokens * TOP_K
    n_tiles = n_assign // rows + N_EXPERTS
    experts = route[:, 2:4].astype(jnp.int32).reshape(-1)
    onehot = (experts[:, None] == jnp.arange(N_EXPERTS, dtype=jnp.int32)[None, :]).astype(jnp.int32)
    counts = jnp.sum(onehot, axis=0)
    tiles_per = (counts + rows - 1) // rows
    tile_end = jnp.cumsum(tiles_per)
    id_bits = max(n_assign, rows).bit_length()
    real_keys = (experts << (id_bits + 1)) | jnp.arange(n_assign, dtype=jnp.int32)
    pad_id = jnp.arange(rows, dtype=jnp.int32)[None, :]
    pad_needed = pad_id < (tiles_per * rows - counts)[:, None]
    pad_keys = (jnp.arange(N_EXPERTS, dtype=jnp.int32)[:, None] << (id_bits + 1)) | (1 << id_bits) | pad_id
    pad_keys = jnp.where(pad_needed, pad_keys, jnp.iinfo(jnp.int32).max)
    keys = jnp.sort(jnp.concatenate([real_keys, pad_keys.reshape(-1)]))
    real = ((keys >> id_bits) & 1) == 0
    assign = jnp.where(real, keys & ((1 << id_bits) - 1), -1)
    row_token = jnp.where(real, assign // TOP_K, 0)
    pad_rank = jnp.minimum(jnp.cumsum(1 - real.astype(jnp.int32)) - 1, N_EXPERTS * rows - 1)
    row_dst = jnp.where(real, (assign % TOP_K) * n_tokens + assign // TOP_K, n_assign + pad_rank)
    tile_id = jnp.arange(n_tiles, dtype=jnp.int32)
    n_used = tile_end[-1]
    tile_expert = jnp.sum((tile_id[:, None] >= tile_end[None, :]).astype(jnp.int32), axis=1)
    last_expert = jnp.sum((n_used - 1 >= tile_end).astype(jnp.int32))
    tile_expert = jnp.minimum(jnp.where(tile_id < n_used, tile_expert, last_expert), N_EXPERTS - 1)
    return n_used.reshape(1), tile_expert, row_token, row_dst, n_assign + N_EXPERTS * rows


def _rope_tables(seq):
    half = HEAD_DIM // 2
    inv_freq = ROPE_THETA ** (-jnp.arange(half, dtype=F32) / half)
    ang = jnp.arange(seq).astype(F32)[:, None] * inv_freq[None, :]
    cos, sin = jnp.cos(ang), jnp.sin(ang)
    return jnp.concatenate([cos, cos], axis=1), jnp.concatenate([-sin, sin], axis=1)


def _pad_lanes(w):
    return jnp.pad(w, ((0, 0), (0, LANES - w.shape[1])))


def kernel(x, a_w_in, a_conv_w, a_log_decay, a_dt_bias, a_norm_w, a_w_out, b_w_kv, b_w_q, b_w_o,
           ffn_w1, ffn_w3, ffn_w2, moe_router, moe_w1, moe_w3, moe_w2, ln_g, ln_b):
    batch, seq, _ = x.shape
    t = batch * seq
    width = N_HEADS * HEAD_DIM
    x0 = x.reshape(t, D_MODEL)
    row = lambda v: v.reshape(1, -1).astype(F32)

    w_in = a_w_in[0]
    q, k, v, gate, gb, moe_w1b, moe_w2b = _in_proj(
        x0, w_in[:, :4 * width].astype(BF16), _pad_lanes(w_in[:, 4 * width:]).astype(BF16),
        a_conv_w[0], _pad_lanes(row(a_log_decay[0])), _pad_lanes(row(a_dt_bias[0])), batch, seq,
        riders=(moe_w1[0].reshape(N_EXPERTS * D_MODEL, FFN_HIDDEN),
                moe_w2[0].reshape(N_EXPERTS * FFN_HIDDEN, D_MODEL)))
    og, moe_w3b = _delta_rule(q, k, v, gate, gb, row(a_norm_w[0]), batch, seq,
                              riders=(moe_w3[0].reshape(N_EXPERTS * D_MODEL, FFN_HIDDEN),))
    x2 = _mixer_ffn(og, a_w_out[0].astype(BF16), x0, row(ln_g[0, 0]), row(ln_b[0, 0]),
                    ffn_w1[0].astype(BF16), ffn_w3[0].astype(BF16), ffn_w2[0].astype(BF16),
                    row(ln_g[0, 1]), row(ln_b[0, 1]))

    cos_full, sin_signed = _rope_tables(seq)
    w_kvq = jnp.concatenate([b_w_kv, b_w_q[0]], axis=1).astype(BF16)
    qm, km, vt, sel = _qkv_proj(x2, w_kvq, cos_full, sin_signed, batch, seq)
    attn = _moba_attention(qm, km, vt, sel, batch, seq)
    x3, x3_tiles, route = _proj_ln_router(attn, b_w_o[0].astype(BF16), x2, row(ln_g[1, 0]), row(ln_b[1, 0]),
                                          _pad_lanes(moe_router[0]))
    n_used, tile_expert, row_token, row_dst, n_out_rows = _routing_tables(route, t)
    ys = _moe_experts(n_used, tile_expert, row_token, row_dst, x3_tiles,
                      moe_w1b.reshape(N_EXPERTS, D_MODEL, FFN_HIDDEN),
                      moe_w3b.reshape(N_EXPERTS, D_MODEL, FFN_HIDDEN),
                      moe_w2b.reshape(N_EXPERTS, FFN_HIDDEN, D_MODEL), n_out_rows)
    x4 = _moe_combine(ys, x3, route, row(ln_g[1, 1]), row(ln_b[1, 1]))
    return x4.reshape(batch, seq, D_MODEL)
```

```python
import functools
import math

import jax
import jax.numpy as jnp
from jax import lax
from jax.experimental import pallas as pl
from jax.experimental.pallas import tpu as pltpu

D_MODEL = 1024
DEPTH = 2
DEEPNORM_ALPHA = (2.0 * DEPTH) ** 0.25
LN_EPS = 1e-5
N_HEADS = 8
HEAD_DIM = 128
DN_CONV = 4
DN_CHUNK = 64
DN_NORM_EPS = 1e-6
MOBA_BLOCK = 256
MOBA_TOPK = 3
ROPE_THETA = 10000.0
NEG_INF = -1e30
LOG2_E = math.log2(math.e)
FFN_HIDDEN = 3584
N_EXPERTS = 8
TOP_K = 2

LANES = 128
SUBLANES = 8
VMEM_LIMIT_BYTES = 56 * 1024 * 1024

ROWS_PROJ = 256
ROWS_DELTA = 256
ROWS_LN = 512
ROWS_LN_SUB = 256
ROWS_FFN = 512
FFN_CHUNK = 1792
ROWS_COMBINE = 512

F32 = jnp.float32
BF16 = jnp.bfloat16

_NT = (((1,), (1,)), ((), ()))


def _dot(a, b):
    return jnp.dot(a, b, preferred_element_type=F32)


def _dot_nt(a, b):
    return lax.dot_general(a, b, _NT, preferred_element_type=F32)


def _params(*sem):
    return pltpu.CompilerParams(dimension_semantics=sem, vmem_limit_bytes=VMEM_LIMIT_BYTES)


def _layer_norm(y, g, b):
    mu = jnp.mean(y, axis=-1, keepdims=True)
    d = y - mu
    var = jnp.mean(d * d, axis=-1, keepdims=True)
    return d * lax.rsqrt(var + LN_EPS) * g + b


def _sigmoid(x):
    return 1.0 / (1.0 + jnp.exp(-x))


def _split2(x):
    hi = x.astype(BF16)
    lo = (x - hi.astype(F32)).astype(BF16)
    return hi, lo


def _dot_split(a, b, dot):
    a_hi, a_lo = _split2(a)
    b_hi, b_lo = _split2(b)
    return dot(a_hi, b_hi) + dot(a_hi, b_lo) + dot(a_lo, b_hi)


def _rider_specs(arrays, n_steps, index_map):
    specs = [pl.BlockSpec((a.shape[0] // n_steps, a.shape[1]), index_map) for a in arrays]
    return specs, [jax.ShapeDtypeStruct(a.shape, BF16) for a in arrays]


def _cast_riders(src_refs, dst_refs):
    for src, dst in zip(src_refs, dst_refs):
        dst[...] = src[...].astype(BF16)


def _in_proj_kernel(n_riders, x_ref, w_ref, wab_ref, cw_ref, alog_ref, dtb_ref, *rest):
    riders_in, rest = rest[:n_riders], rest[n_riders:]
    q_ref, k_ref, v_ref, gate_ref, gb_ref = rest[:5]
    riders_out, conv_sc = rest[5:5 + n_riders], rest[5 + n_riders]
    _cast_riders(riders_in, riders_out)
    rows = x_ref.shape[0]
    width = N_HEADS * HEAD_DIM
    hist = SUBLANES

    @pl.when(pl.program_id(1) == 0)
    def _():
        conv_sc[...] = jnp.zeros(conv_sc.shape, F32)

    xb = x_ref[...].astype(BF16)
    chunk = 2 * HEAD_DIM
    for c in range(3 * width // chunk):
        proj = _dot(xb, w_ref[:, c * chunk:(c + 1) * chunk])
        which = c * chunk // width
        for hh in range(chunk // HEAD_DIM):
            slab = c * (chunk // HEAD_DIM) + hh
            cols = slice(slab * HEAD_DIM, (slab + 1) * HEAD_DIM)
            cur = proj[:, hh * HEAD_DIM:(hh + 1) * HEAD_DIM]
            conv_sc[slab, pl.ds(2 * hist, rows, stride=2), :] = cur
            y = cur * cw_ref[DN_CONV - 1:DN_CONV, cols]
            for j in range(DN_CONV - 1):
                shifted = conv_sc[slab, pl.ds(2 * (hist - (DN_CONV - 1) + j), rows, stride=2), :]
                y = y + shifted * cw_ref[j:j + 1, cols]
            conv_sc[slab, 0:2 * hist, :] = conv_sc[slab, 2 * rows:2 * (rows + hist), :]
            t = y * _sigmoid(y)
            ocols = slice(slab * HEAD_DIM - which * width, (slab + 1) * HEAD_DIM - which * width)
            if which == 2:
                v_ref[:, ocols] = t.astype(BF16)
            else:
                inv_norm = lax.rsqrt(jnp.sum(t * t, axis=-1, keepdims=True) + DN_NORM_EPS)
                if which == 0:
                    q_ref[:, ocols] = (t * (inv_norm * (HEAD_DIM ** -0.5))).astype(BF16)
                else:
                    k_ref[:, ocols] = (t * inv_norm).astype(BF16)
        if c % 3 == 2:
            g0 = (c // 3) * chunk
            gate_ref[:, g0:g0 + chunk] = _dot(xb, w_ref[:, 3 * width + g0:3 * width + g0 + chunk]).astype(BF16)
    ab = _dot(xb, wab_ref[...])
    z = ab + dtb_ref[...]
    softplus = jnp.maximum(z, 0.0) + jnp.log(1.0 + jnp.exp(-jnp.abs(z)))
    g = -jnp.exp(alog_ref[...]) * softplus
    lane = lax.broadcasted_iota(jnp.int32, ab.shape, 1)
    gb_ref[...] = jnp.where(lane < N_HEADS, g, _sigmoid(ab))


def _in_proj(x2d, w_main, w_ab, conv_w, a_log_row, dt_row, batch, seq, riders=()):
    t = x2d.shape[0]
    width = N_HEADS * HEAD_DIM
    rows = ROWS_PROJ
    nt = seq // rows
    row_map = lambda b, s: (b * nt + s, 0)
    const = lambda b, s: (0, 0)
    act = jax.ShapeDtypeStruct((t, width), BF16)
    rider_specs, rider_shapes = _rider_specs(riders, batch * nt, row_map)
    return pl.pallas_call(
        functools.partial(_in_proj_kernel, len(riders)),
        grid=(batch, nt),
        in_specs=[pl.BlockSpec((rows, D_MODEL), row_map),
                  pl.BlockSpec((D_MODEL, 4 * width), const),
                  pl.BlockSpec((D_MODEL, LANES), const),
                  pl.BlockSpec((DN_CONV, 3 * width), const),
                  pl.BlockSpec((1, LANES), const),
                  pl.BlockSpec((1, LANES), const)] + rider_specs,
        out_specs=[pl.BlockSpec((rows, width), row_map)] * 4 + [pl.BlockSpec((rows, LANES), row_map)]
        + rider_specs,
        out_shape=[act, act, act, act, jax.ShapeDtypeStruct((t, LANES), F32)] + rider_shapes,
        scratch_shapes=[pltpu.VMEM((3 * width // LANES, 2 * (rows + SUBLANES), LANES), F32)],
        compiler_params=_params("arbitrary", "arbitrary"),
        name="dn_in_proj",
    )(x2d, w_main, w_ab, conv_w, a_log_row, dt_row, *riders)


def _delta_kernel(n_riders, q_ref, k_ref, v_ref, gate_ref, gb_ref, nw_ref, *rest):
    riders_in, o_ref = rest[:n_riders], rest[n_riders]
    riders_out, state_sc = rest[n_riders + 1:2 * n_riders + 1], rest[2 * n_riders + 1]
    _cast_riders(riders_in, riders_out)
    c = DN_CHUNK
    two = 2 * c
    n_sub = q_ref.shape[0] // c
    n_pairs = N_HEADS // 2

    @pl.when(pl.program_id(1) == 0)
    def _():
        state_sc[...] = jnp.zeros(state_sc.shape, F32)

    r = lax.broadcasted_iota(jnp.int32, (two, two), 0)
    cc = lax.broadcasted_iota(jnp.int32, (two, two), 1)
    same = (r < c) == (cc < c)
    tri = same & (r >= cc)
    stri = same & (r > cc)
    eye = (r == cc).astype(F32)
    top_rows = lax.broadcasted_iota(jnp.int32, (two, 1), 0) < c
    top_lanes = lax.broadcasted_iota(jnp.int32, (1, two), 1) < c
    row64 = lax.broadcasted_iota(jnp.int32, (c, LANES), 0)

    def stack(ref, rows, h):
        t = ref[rows, h * HEAD_DIM:(h + 2) * HEAD_DIM].astype(F32)
        return jnp.concatenate([t[:, :HEAD_DIM], t[:, HEAD_DIM:]], axis=0)

    items = []
    for s in range(n_sub):
        rows = slice(s * c, (s + 1) * c)
        gb = gb_ref[rows, :]
        gc = gb
        sh = 1
        while sh < c:
            gc = gc + jnp.where(row64 >= sh, pltpu.roll(gc, sh, axis=0), 0.0)
            sh *= 2
        gc_pair = jnp.concatenate([gc, pltpu.roll(gc, LANES - 1, axis=1)], axis=0)
        gb_pair = jnp.concatenate([gb, pltpu.roll(gb, LANES - 1, axis=1)], axis=0)
        gc_pair_t = gc_pair.T
        for p in range(n_pairs):
            h = 2 * p
            q2, k2, v2 = stack(q_ref, rows, h), stack(k_ref, rows, h), stack(v_ref, rows, h)
            gcol = gc_pair[:, h:h + 1]
            grow = gc_pair_t[h:h + 1, :]
            beta = gb_pair[:, N_HEADS + h:N_HEADS + h + 1]
            glast = jnp.where(top_rows, gc_pair[c - 1:c, h:h + 1], gc_pair[two - 1:two, h:h + 1])
            eg = jnp.exp(gcol)
            kb = k2 * beta
            items.append(dict(
                rows=rows, h=h, k2b=k2.astype(BF16), q2b=q2.astype(BF16), kbb=kb.astype(BF16),
                decay=jnp.exp(jnp.where(tri, gcol - grow, NEG_INF)),
                rhs=jnp.concatenate([v2 * beta, kb * eg], axis=1).astype(BF16),
                qd=(q2 * eg).astype(BF16),
                kd_t=(k2 * jnp.exp(glast - gcol)).T,
                eglast=jnp.exp(glast)))
    for it in items:
        it["pw"] = jnp.where(stri, _dot_nt(it["kbb"], it["k2b"]) * it["decay"], 0.0)
        it["inv"] = eye - it["pw"]
    for _ in range(int(math.log2(c)) - 1):
        for it in items:
            pwb = it["pw"].astype(BF16)
            it["pw"] = _dot(pwb, pwb)
        for it in items:
            it["inv"] = it["inv"] + _dot(it["inv"].astype(BF16), it["pw"].astype(BF16))
    for it in items:
        sol = _dot(it["inv"].astype(BF16), it["rhs"])
        it["u"] = sol[:, :HEAD_DIM]
        it["wb"] = sol[:, HEAD_DIM:].astype(BF16)
    for it in items:
        it["amat"] = jnp.where(tri, _dot_nt(it["q2b"], it["k2b"]) * it["decay"], 0.0).astype(BF16)

    for s in range(n_sub):
        group = items[s * n_pairs:(s + 1) * n_pairs]
        for it in group:
            h = it["h"]
            it["s0"] = state_sc[h]
            it["s1"] = state_sc[h + 1]
            it["r0"] = _dot(jnp.concatenate([it["wb"][:c], it["qd"][:c]], axis=0), it["s0"].astype(BF16))
            it["r1"] = _dot(jnp.concatenate([it["wb"][c:], it["qd"][c:]], axis=0), it["s1"].astype(BF16))
        for it in group:
            v_new = it["u"] - jnp.concatenate([it["r0"][:c], it["r1"][:c]], axis=0)
            it["vnb"] = v_new.astype(BF16)
            it["o"] = jnp.concatenate([it["r0"][c:], it["r1"][c:]], axis=0) + _dot(it["amat"], it["vnb"])
        for it in group:
            h = it["h"]
            kd_t = it["kd_t"]
            state_sc[h] = it["s0"] * it["eglast"][0:1, :] + _dot(
                jnp.where(top_lanes, kd_t, 0.0).astype(BF16), it["vnb"])
            state_sc[h + 1] = it["s1"] * it["eglast"][c:c + 1, :] + _dot(
                jnp.where(top_lanes, 0.0, kd_t).astype(BF16), it["vnb"])
        for it in group:
            h, rows, o = it["h"], it["rows"], it["o"]
            gate2 = stack(gate_ref, rows, h)
            o = o * lax.rsqrt(jnp.mean(o * o, axis=-1, keepdims=True) + DN_NORM_EPS) * nw_ref[...]
            o = o * (gate2 * _sigmoid(gate2))
            o_ref[rows, h * HEAD_DIM:(h + 1) * HEAD_DIM] = o[:c].astype(BF16)
            o_ref[rows, (h + 1) * HEAD_DIM:(h + 2) * HEAD_DIM] = o[c:].astype(BF16)


def _delta_rule(q, k, v, gate, gb, norm_w_row, batch, seq, riders=()):
    t = q.shape[0]
    width = N_HEADS * HEAD_DIM
    rows = ROWS_DELTA
    ns = seq // rows
    row_map = lambda b, s: (b * ns + s, 0)
    const = lambda b, s: (0, 0)
    wide = pl.BlockSpec((rows, width), row_map)
    rider_specs, rider_shapes = _rider_specs(riders, batch * ns, row_map)
    return pl.pallas_call(
        functools.partial(_delta_kernel, len(riders)),
        grid=(batch, ns),
        in_specs=[wide, wide, wide, wide,
                  pl.BlockSpec((rows, LANES), row_map),
                  pl.BlockSpec((1, HEAD_DIM), const)] + rider_specs,
        out_specs=[wide] + rider_specs,
        out_shape=[jax.ShapeDtypeStruct((t, width), BF16)] + rider_shapes,
        scratch_shapes=[pltpu.VMEM((N_HEADS, HEAD_DIM, HEAD_DIM), F32)],
        compiler_params=_params("arbitrary", "arbitrary"),
        name="dn_delta_rule",
    )(q, k, v, gate, gb, norm_w_row, *riders)


def _rows_to_tiles(ref, base, y):
    for k in range(D_MODEL // LANES):
        ref[pl.ds(base + k, y.shape[0], stride=SUBLANES), :] = y[:, k * LANES:(k + 1) * LANES]


def _tiles_to_rows(ref, base, n):
    return jnp.concatenate([ref[pl.ds(base + k, n, stride=SUBLANES), :] for k in range(D_MODEL // LANES)],
                           axis=1)


def _proj_ln_router_kernel(a_ref, w_ref, res_ref, g_ref, b_ref, wr_ref, ot_ref, route_ref):
    for rb in range(a_ref.shape[0] // ROWS_LN_SUB):
        rows = slice(rb * ROWS_LN_SUB, (rb + 1) * ROWS_LN_SUB)
        y = DEEPNORM_ALPHA * res_ref[rows, :] + _dot(a_ref[rows, :], w_ref[...])
        xn = _layer_norm(y, g_ref[...], b_ref[...])
        _rows_to_tiles(ot_ref, rb * ROWS_LN_SUB * SUBLANES, xn)
        logits = _dot_split(xn, wr_ref[...], _dot)
        lane = lax.broadcasted_iota(jnp.int32, logits.shape, 1)
        logits = jnp.where(lane < N_EXPERTS, logits, NEG_INF)
        l1 = jnp.max(logits, axis=-1, keepdims=True)
        i1 = jnp.min(jnp.where(logits == l1, lane, LANES), axis=-1, keepdims=True)
        rest = jnp.where(lane == i1, NEG_INF, logits)
        l2 = jnp.max(rest, axis=-1, keepdims=True)
        i2 = jnp.min(jnp.where(rest == l2, lane, LANES), axis=-1, keepdims=True)
        e2 = jnp.exp(l2 - l1)
        g1 = 1.0 / (1.0 + e2)
        g2 = e2 / (1.0 + e2)
        route_ref[rows, :] = jnp.where(lane == 0, g1,
                             jnp.where(lane == 1, g2,
                             jnp.where(lane == 2, i1.astype(F32),
                             jnp.where(lane == 3, i2.astype(F32), 0.0))))


def _proj_ln_router(a, w, res, g_row, b_row, w_router):
    t, kdim = a.shape
    rows = ROWS_LN
    row_map = lambda i: (i, 0)
    const = lambda i: (0, 0)
    in_specs = [pl.BlockSpec((rows, kdim), row_map),
                pl.BlockSpec((kdim, D_MODEL), const),
                pl.BlockSpec((rows, D_MODEL), row_map),
                pl.BlockSpec((1, D_MODEL), const),
                pl.BlockSpec((1, D_MODEL), const)]
    return pl.pallas_call(
        _proj_ln_router_kernel, grid=(t // rows,),
        in_specs=in_specs + [pl.BlockSpec((D_MODEL, LANES), const)],
        out_specs=[pl.BlockSpec((rows * SUBLANES, LANES), row_map),
                   pl.BlockSpec((rows, LANES), row_map)],
        out_shape=[jax.ShapeDtypeStruct((t * SUBLANES, LANES), F32),
                   jax.ShapeDtypeStruct((t, LANES), F32)],
        compiler_params=_params("arbitrary"), name="proj_ln_router",
    )(a, w, res, g_row, b_row, w_router)


def _swiglu_hidden(x, w1_ref, w3_ref, hid_sc):
    xb = x.astype(BF16)
    sub = 2 * LANES
    for c in range(hid_sc.shape[1] // sub):
        cols = slice(c * sub, (c + 1) * sub)
        h1 = _dot(xb, w1_ref[:, cols])
        h3 = _dot(xb, w3_ref[:, cols])
        hid_sc[:, cols] = (h1 * _sigmoid(h1) * h3).astype(BF16)


def _swiglu_part(x, w1_ref, w3_ref, w2_ref, hid_sc):
    _swiglu_hidden(x, w1_ref, w3_ref, hid_sc)
    return _dot(hid_sc[...], w2_ref[...])


def _mixer_ffn_kernel(a_ref, wo_ref, res_ref, g0_ref, b0_ref, w1_ref, w3_ref, w2_ref, g1_ref, b1_ref,
                      o_ref, x_sc, acc_sc, hid_sc):
    f = pl.program_id(1)
    last_f = pl.num_programs(1) - 1
    n_sub = a_ref.shape[0] // ROWS_LN_SUB

    @pl.when(f == 0)
    def _():
        for rb in range(n_sub):
            rows = slice(rb * ROWS_LN_SUB, (rb + 1) * ROWS_LN_SUB)
            y = DEEPNORM_ALPHA * res_ref[rows, :] + _dot(a_ref[rows, :], wo_ref[...])
            x_sc[rows, :] = _layer_norm(y, g0_ref[...], b0_ref[...])

    _swiglu_hidden(x_sc[...], w1_ref, w3_ref, hid_sc)

    @pl.when(f == 0)
    def _():
        acc_sc[...] = _dot(hid_sc[...], w2_ref[...])

    @pl.when((f > 0) & (f < last_f))
    def _():
        acc_sc[...] += _dot(hid_sc[...], w2_ref[...])

    @pl.when(f == last_f)
    def _():
        for rb in range(n_sub):
            rows = slice(rb * ROWS_LN_SUB, (rb + 1) * ROWS_LN_SUB)
            y = DEEPNORM_ALPHA * x_sc[rows, :] + (acc_sc[rows, :] + _dot(hid_sc[rows, :], w2_ref[...]))
            o_ref[rows, :] = _layer_norm(y, g1_ref[...], b1_ref[...])


def _mixer_ffn(a, wo, res, g0_row, b0_row, w1, w3, w2, g1_row, b1_row):
    t = res.shape[0]
    rows, fc = ROWS_FFN, FFN_CHUNK
    nf = FFN_HIDDEN // fc
    assert nf >= 2
    row_map = lambda i, f: (i, 0)
    const = lambda i, f: (0, 0)
    vec = pl.BlockSpec((1, D_MODEL), const)
    return pl.pallas_call(
        _mixer_ffn_kernel,
        grid=(t // rows, nf),
        in_specs=[pl.BlockSpec((rows, a.shape[1]), row_map),
                  pl.BlockSpec((a.shape[1], D_MODEL), const),
                  pl.BlockSpec((rows, D_MODEL), row_map), vec, vec,
                  pl.BlockSpec((D_MODEL, fc), lambda i, f: (0, f)),
                  pl.BlockSpec((D_MODEL, fc), lambda i, f: (0, f)),
                  pl.BlockSpec((fc, D_MODEL), lambda i, f: (f, 0)), vec, vec],
        out_specs=pl.BlockSpec((rows, D_MODEL), row_map),
        out_shape=jax.ShapeDtypeStruct((t, D_MODEL), F32),
        scratch_shapes=[pltpu.VMEM((rows, D_MODEL), F32), pltpu.VMEM((rows, D_MODEL), F32),
                        pltpu.VMEM((rows, fc), BF16)],
        compiler_params=_params("arbitrary", "arbitrary"),
        name="mixer_ffn_dense",
    )(a, wo, res, g0_row, b0_row, w1, w3, w2, g1_row, b1_row)


def _moe_experts_kernel(nu_ref, te_ref, idx_ref, idx_next_ref, dst_ref, x_hbm, w1_ref, w3_ref, w2_ref,
                        y_hbm, xbuf, ybuf, xb_sc, acc_sc, hid_sc, gsem, ssem):
    i = pl.program_id(0)
    f = pl.program_id(1)
    last_f = pl.num_programs(1) - 1
    rows = xb_sc.shape[0]
    tile_rows = rows * SUBLANES
    nu = nu_ref[0]
    slot = i % 2

    def gather_start(idx, s):
        for r in range(rows):
            src = x_hbm.at[pl.ds(pl.multiple_of(idx[0, r], SUBLANES), SUBLANES)]
            pltpu.make_async_copy(src, xbuf.at[pl.ds(s * tile_rows + r * SUBLANES, SUBLANES)], gsem.at[s]).start()

    def scatter_start(s):
        for r in range(rows):
            dst = y_hbm.at[pl.ds(pl.multiple_of(dst_ref[0, r], SUBLANES), SUBLANES)]
            pltpu.make_async_copy(ybuf.at[pl.ds(s * tile_rows + r * SUBLANES, SUBLANES)], dst, ssem.at[s]).start()

    def wait_all(buf, sem, s):
        view = buf.at[pl.ds(pl.multiple_of(s * tile_rows, tile_rows), tile_rows)]
        pltpu.make_async_copy(view, view, sem.at[s]).wait()

    def for_slot(cond, s, fn):
        for static_slot in range(2):
            pl.when(cond & (s == static_slot))(functools.partial(fn, static_slot))

    first = f == 0

    @pl.when(first & (i == 0))
    def _():
        n_real = TOP_K * x_hbm.shape[0]
        ybuf[tile_rows:2 * tile_rows, :] = jnp.zeros((tile_rows, LANES), F32)
        fills = [pltpu.make_async_copy(ybuf.at[pl.ds(tile_rows, tile_rows)],
                                       y_hbm.at[pl.ds(n_real + k * tile_rows, tile_rows)], ssem.at[1])
                 for k in range((y_hbm.shape[0] - n_real) // tile_rows)]
        for cp in fills:
            cp.start()
        for cp in fills:
            cp.wait()

    @pl.when(first & (i == 0) & (nu > 0))
    def _():
        gather_start(idx_ref, 0)

    @pl.when(first & (i < nu))
    def _():
        wait_all(xbuf, gsem, slot)

    for_slot(first & (i + 1 < nu), 1 - slot, functools.partial(gather_start, idx_next_ref))

    @pl.when(i < nu)
    def _():
        @pl.when(first)
        def _():
            base = pl.multiple_of(slot * tile_rows, tile_rows)
            for k in range(D_MODEL // LANES):
                xb_sc[:, k * LANES:(k + 1) * LANES] = xbuf[pl.ds(base + k, rows, stride=SUBLANES), :].astype(BF16)

        part = _swiglu_part(xb_sc[...], w1_ref, w3_ref, w2_ref, hid_sc)

        @pl.when(first)
        def _():
            acc_sc[...] = part

        @pl.when((f > 0) & (f < last_f))
        def _():
            acc_sc[...] += part

        @pl.when(f == last_f)
        def _():
            @pl.when(i >= 2)
            def _():
                wait_all(ybuf, ssem, slot)
            _rows_to_tiles(ybuf, pl.multiple_of(slot * tile_rows, tile_rows), acc_sc[...] + part)

        for_slot(f == last_f, slot, scatter_start)

    @pl.when((i == pl.num_programs(0) - 1) & (f == last_f))
    def _():
        @pl.when(nu >= 1)
        def _():
            wait_all(ybuf, ssem, (nu - 1) % 2)

        @pl.when(nu >= 2)
        def _():
            wait_all(ybuf, ssem, nu % 2)


def _moe_experts(n_used, tile_expert, row_token, row_dst, x_tiles, w1, w3, w2, n_out_rows):
    n_tiles = tile_expert.shape[0]
    rows, fc = ROWS_FFN, FFN_CHUNK
    nf = FFN_HIDDEN // fc
    idx3 = (row_token * SUBLANES).reshape(n_tiles, 1, rows)
    dst3 = (row_dst * SUBLANES).reshape(n_tiles, 1, rows)
    smem_tile = lambda fn: pl.BlockSpec((None, 1, rows), fn, memory_space=pltpu.SMEM)
    grid_spec = pltpu.PrefetchScalarGridSpec(
        num_scalar_prefetch=2,
        grid=(n_tiles, nf),
        in_specs=[smem_tile(lambda i, f, nu, te: (i, 0, 0)),
                  smem_tile(lambda i, f, nu, te: (jnp.minimum(i + 1, n_tiles - 1), 0, 0)),
                  smem_tile(lambda i, f, nu, te: (i, 0, 0)),
                  pl.BlockSpec(memory_space=pl.ANY),
                  pl.BlockSpec((None, D_MODEL, fc), lambda i, f, nu, te: (te[i], 0, f)),
                  pl.BlockSpec((None, D_MODEL, fc), lambda i, f, nu, te: (te[i], 0, f)),
                  pl.BlockSpec((None, fc, D_MODEL), lambda i, f, nu, te: (te[i], f, 0))],
        out_specs=pl.BlockSpec(memory_space=pl.ANY),
        scratch_shapes=[pltpu.VMEM((2 * rows * SUBLANES, LANES), F32), pltpu.VMEM((2 * rows * SUBLANES, LANES), F32),
                        pltpu.VMEM((rows, D_MODEL), BF16),
                        pltpu.VMEM((rows, D_MODEL), F32), pltpu.VMEM((rows, fc), BF16),
                        pltpu.SemaphoreType.DMA((2,)), pltpu.SemaphoreType.DMA((2,))],
    )
    return pl.pallas_call(
        _moe_experts_kernel,
        grid_spec=grid_spec,
        out_shape=jax.ShapeDtypeStruct((n_out_rows * SUBLANES, LANES), F32),
        compiler_params=_params("arbitrary", "arbitrary"),
        name="moe_experts",
    )(n_used, tile_expert, idx3, idx3, dst3, x_tiles, w1, w3, w2)


def _qkv_kernel(x_ref, w_ref, cos_ref, sin_ref, q_ref, k_ref, vt_ref, sel_ref, km_sc, qf_sc):
    i = pl.program_id(1)
    width = N_HEADS * HEAD_DIM
    half = HEAD_DIM // 2
    xb = x_ref[...].astype(BF16)
    cosf = cos_ref[...]
    sinf = sin_ref[...]

    @pl.when(i == 0)
    def _():
        km_sc[...] = jnp.zeros(km_sc.shape, F32)

    def rope(t):
        return t * cosf + pltpu.roll(t, half, axis=1) * sinf

    chunk = 2 * HEAD_DIM
    n_chunks = width // chunk
    nb = km_sc.shape[0]

    def q_chunk(c):
        qq = _dot(xb, w_ref[:, 2 * width + c * chunk:2 * width + (c + 1) * chunk])
        for hh in range(2):
            cols = slice((2 * c + hh) * HEAD_DIM, (2 * c + hh + 1) * HEAD_DIM)
            qr = rope(qq[:, hh * HEAD_DIM:(hh + 1) * HEAD_DIM])
            qf_sc[:, cols] = qr
            q_ref[:, cols] = (qr * (HEAD_DIM ** -0.5 * LOG2_E)).astype(BF16)

    def kv_chunk(c):
        kk = _dot(xb, w_ref[:, c * chunk:(c + 1) * chunk])
        vv = _dot(xb, w_ref[:, width + c * chunk:width + (c + 1) * chunk])
        for hh in range(2):
            h = 2 * c + hh
            cols = slice(h * HEAD_DIM, (h + 1) * HEAD_DIM)
            part = slice(hh * HEAD_DIM, (hh + 1) * HEAD_DIM)
            kr = rope(kk[:, part])
            k_ref[:, cols] = kr.astype(BF16)
            km_row = lax.broadcasted_iota(jnp.int32, (nb, HEAD_DIM), 0)
            km_sc[:, cols] = jnp.where(km_row == i, jnp.mean(kr, axis=0, keepdims=True), km_sc[:, cols])
            vt_ref[h] = vv[:, part].T.astype(BF16)

    def block_selection():
        km = km_sc[...]
        lane_head = lax.broadcasted_iota(jnp.int32, km.shape, 1) // HEAD_DIM
        km_rows = jnp.concatenate([jnp.where(lane_head == h, km, 0.0) for h in range(N_HEADS)], axis=0)
        gate_t = _dot_split(km_rows, qf_sc[...], _dot_nt)
        blk = lax.broadcasted_iota(jnp.int32, (nb, x_ref.shape[0]), 0)
        past = blk < i
        for h in range(N_HEADS):
            g = jnp.where(past, gate_t[h * nb:(h + 1) * nb, :], NEG_INF)
            rank = jnp.zeros(g.shape, jnp.int32)
            for n in range(nb):
                gn = g[n:n + 1, :]
                ahead = (gn > g) | ((gn == g) & (blk > n))
                rank = rank + ahead.astype(jnp.int32)
            sel_ref[h * nb:(h + 1) * nb, :] = jnp.where(past & (rank < MOBA_TOPK), 0.0, NEG_INF)

    for c in range(n_chunks):
        q_chunk(c)
    kv_chunk(0)
    block_selection()
    for c in range(1, n_chunks):
        kv_chunk(c)


def _qkv_proj(x, w_kvq, cos_full, sin_signed, batch, seq):
    t = x.shape[0]
    width = N_HEADS * HEAD_DIM
    rows = MOBA_BLOCK
    nb = seq // rows
    row_map = lambda b, i: (b * nb + i, 0)
    act = jax.ShapeDtypeStruct((t, width), BF16)
    return pl.pallas_call(
        _qkv_kernel,
        grid=(batch, nb),
        in_specs=[pl.BlockSpec((rows, D_MODEL), row_map),
                  pl.BlockSpec((D_MODEL, 3 * width), lambda b, i: (0, 0)),
                  pl.BlockSpec((rows, HEAD_DIM), lambda b, i: (i, 0)),
                  pl.BlockSpec((rows, HEAD_DIM), lambda b, i: (i, 0))],
        out_specs=[pl.BlockSpec((rows, width), row_map),
                   pl.BlockSpec((rows, width), row_map),
                   pl.BlockSpec((None, N_HEADS, HEAD_DIM, rows), lambda b, i: (b, 0, 0, i)),
                   pl.BlockSpec((None, N_HEADS * nb, rows), lambda b, i: (b * nb + i, 0, 0))],
        out_shape=[act, act,
                   jax.ShapeDtypeStruct((batch, N_HEADS, HEAD_DIM, seq), BF16),
                   jax.ShapeDtypeStruct((batch * nb, N_HEADS * nb, rows), F32)],
        scratch_shapes=[pltpu.VMEM((nb, width), F32), pltpu.VMEM((rows, width), F32)],
        compiler_params=_params("arbitrary", "arbitrary"),
        name="moba_qkv_proj",
    )(x, w_kvq, cos_full, sin_signed)


def _attn_kernel(q_ref, k_ref, vt_ref, sel_ref, o_ref, kaug_sc, vtaug_sc):
    blk = MOBA_BLOCK
    seq = q_ref.shape[0]
    nb = seq // blk
    extra = 2 * SUBLANES
    key = lax.broadcasted_iota(jnp.int32, (blk, blk), 0)
    qry = lax.broadcasted_iota(jnp.int32, (blk, blk), 1)
    causal = key <= qry

    key_blk = lax.broadcasted_iota(jnp.int32, (seq, LANES), 0) // blk
    kaug_sc[:, 0:HEAD_DIM] = k_ref[...]
    kaug_sc[:, HEAD_DIM:] = jnp.where(lax.broadcasted_iota(jnp.int32, (seq, LANES), 1) == key_blk,
                                      1.0, 0.0).astype(BF16)
    vtaug_sc[0:HEAD_DIM, :] = vt_ref[...]
    vtaug_sc[HEAD_DIM:, :] = jnp.where(lax.broadcasted_iota(jnp.int32, (extra, seq), 0) == 0,
                                       1.0, 0.0).astype(BF16)
    blk_row = lax.broadcasted_iota(jnp.int32, (nb, blk), 0)

    def scores(i):
        bias = jnp.where(blk_row == i, 0.0, sel_ref[i])
        bias_t = jnp.concatenate([bias, jnp.zeros((LANES - nb, blk), F32)], axis=0).T
        q_aug = jnp.concatenate([q_ref[i * blk:(i + 1) * blk, :], bias_t.astype(BF16)], axis=1)
        return _dot_nt(kaug_sc[0:(i + 1) * blk, :], q_aug)

    def finish(i, s):
        parts = [s[j * blk:(j + 1) * blk] for j in range(i)]
        parts.append(jnp.where(causal, s[i * blk:(i + 1) * blk], NEG_INF))
        m = jnp.max(parts[0], axis=0, keepdims=True)
        for part in parts[1:]:
            m = jnp.maximum(m, jnp.max(part, axis=0, keepdims=True))
        p = jnp.concatenate([jnp.exp2(part - m).astype(BF16) for part in parts], axis=0)
        acc = _dot(vtaug_sc[:, 0:(i + 1) * blk], p)
        o_ref[i * blk:(i + 1) * blk, :] = (acc[0:HEAD_DIM] / acc[HEAD_DIM:HEAD_DIM + 1]).T.astype(BF16)

    ahead = 3
    pending = [scores(i) for i in range(min(ahead, nb))]
    for i in range(nb):
        if i + ahead < nb:
            pending.append(scores(i + ahead))
        finish(i, pending.pop(0))


def _moba_attention(q, k, vt, sel, batch, seq):
    t = q.shape[0]
    width = N_HEADS * HEAD_DIM
    nb = seq // MOBA_BLOCK
    col = pl.BlockSpec((seq, HEAD_DIM), lambda b, h: (b, h))
    return pl.pallas_call(
        _attn_kernel,
        grid=(batch, N_HEADS),
        in_specs=[col, col,
                  pl.BlockSpec((None, None, HEAD_DIM, seq), lambda b, h: (b, h, 0, 0)),
                  pl.BlockSpec((nb, nb, MOBA_BLOCK), lambda b, h: (b, h, 0))],
        out_specs=col,
        out_shape=jax.ShapeDtypeStruct((t, width), BF16),
        scratch_shapes=[pltpu.VMEM((seq, HEAD_DIM + LANES), BF16),
                        pltpu.VMEM((HEAD_DIM + 2 * SUBLANES, seq), BF16)],
        compiler_params=_params("arbitrary", "arbitrary"),
        name="moba_attention",
    )(q, k, vt, sel)


def _combine_kernel(y0_ref, y1_ref, x_ref, route_ref, g_ref, b_ref, o_ref):
    route = route_ref[...]
    rows = o_ref.shape[0]
    y = (DEEPNORM_ALPHA * _tiles_to_rows(x_ref, 0, rows) + route[:, 0:1] * _tiles_to_rows(y0_ref, 0, rows)
         + route[:, 1:2] * _tiles_to_rows(y1_ref, 0, rows))
    o_ref[...] = _layer_norm(y, g_ref[...], b_ref[...])


def _moe_combine(ys, x_tiles, route, g_row, b_row):
    t = x_tiles.shape[0] // SUBLANES
    rows = ROWS_COMBINE
    nt = t // rows
    row_map = lambda i: (i, 0)
    const = lambda i: (0, 0)
    return pl.pallas_call(
        _combine_kernel,
        grid=(nt,),
        in_specs=[pl.BlockSpec((rows * SUBLANES, LANES), row_map),
                  pl.BlockSpec((rows * SUBLANES, LANES), lambda i: (nt + i, 0)),
                  pl.BlockSpec((rows * SUBLANES, LANES), row_map),
                  pl.BlockSpec((rows, LANES), row_map),
                  pl.BlockSpec((1, D_MODEL), const),
                  pl.BlockSpec((1, D_MODEL), const)],
        out_specs=pl.BlockSpec((rows, D_MODEL), row_map),
        out_shape=jax.ShapeDtypeStruct((t, D_MODEL), F32),
        compiler_params=_params("arbitrary"),
        name="moe_combine",
    )(ys, ys, x_tiles, route, g_row, b_row)


def _routing_tables(route, n_tokens):
    rows = ROWS_FFN
    n_assign = n_tokens * TOP_K
    n_tiles = n_assign // rows + N_EXPERTS
    experts = route[:, 2:4].astype(jnp.int32).reshape(-1)
    onehot = (experts[:, None] == jnp.arange(N_EXPERTS, dtype=jnp.int32)[None, :]).astype(jnp.int32)
    counts = jnp.sum(onehot, axis=0)
    tiles_per = (counts + rows - 1) // rows
    tile_end = jnp.cumsum(tiles_per)
    id_bits = max(n_assign, rows).bit_length()
    real_keys = (experts << (id_bits + 1)) | jnp.arange(n_assign, dtype=jnp.int32)
    pad_id = jnp.arange(rows, dtype=jnp.int32)[None, :]
    pad_needed = pad_id < (tiles_per * rows - counts)[:, None]
    pad_keys = (jnp.arange(N_EXPERTS, dtype=jnp.int32)[:, None] << (id_bits + 1)) | (1 << id_bits) | pad_id
    pad_keys = jnp.where(pad_needed, pad_keys, jnp.iinfo(jnp.int32).max)
    keys = jnp.sort(jnp.concatenate([real_keys, pad_keys.reshape(-1)]))
    real = ((keys >> id_bits) & 1) == 0
    assign = jnp.where(real, keys & ((1 << id_bits) - 1), -1)
    row_token = jnp.where(real, assign // TOP_K, 0)
    pad_rank = jnp.minimum(jnp.cumsum(1 - real.astype(jnp.int32)) - 1, N_EXPERTS * rows - 1)
    row_dst = jnp.where(real, (assign % TOP_K) * n_tokens + assign // TOP_K, n_assign + pad_rank)
    tile_id = jnp.arange(n_tiles, dtype=jnp.int32)
    n_used = tile_end[-1]
    tile_expert = jnp.sum((tile_id[:, None] >= tile_end[None, :]).astype(jnp.int32), axis=1)
    last_expert = jnp.sum((n_used - 1 >= tile_end).astype(jnp.int32))
    tile_expert = jnp.minimum(jnp.where(tile_id < n_used, tile_expert, last_expert), N_EXPERTS - 1)
    return n_used.reshape(1), tile_expert, row_token, row_dst, n_assign + N_EXPERTS * rows


def _rope_tables(seq):
    half = HEAD_DIM // 2
    inv_freq = ROPE_THETA ** (-jnp.arange(half, dtype=F32) / half)
    ang = jnp.arange(seq).astype(F32)[:, None] * inv_freq[None, :]
    cos, sin = jnp.cos(ang), jnp.sin(ang)
    return jnp.concatenate([cos, cos], axis=1), jnp.concatenate([-sin, sin], axis=1)


def _pad_lanes(w):
    return jnp.pad(w, ((0, 0), (0, LANES - w.shape[1])))


def kernel(x, a_w_in, a_conv_w, a_log_decay, a_dt_bias, a_norm_w, a_w_out, b_w_kv, b_w_q, b_w_o,
           ffn_w1, ffn_w3, ffn_w2, moe_router, moe_w1, moe_w3, moe_w2, ln_g, ln_b):
    batch, seq, _ = x.shape
    t = batch * seq
    width = N_HEADS * HEAD_DIM
    x0 = x.reshape(t, D_MODEL)
    row = lambda v: v.reshape(1, -1).astype(F32)

    w_in = a_w_in[0]
    q, k, v, gate, gb, moe_w1b, moe_w2b = _in_proj(
        x0, w_in[:, :4 * width].astype(BF16), _pad_lanes(w_in[:, 4 * width:]).astype(BF16),
        a_conv_w[0], _pad_lanes(row(a_log_decay[0])), _pad_lanes(row(a_dt_bias[0])), batch, seq,
        riders=(moe_w1[0].reshape(N_EXPERTS * D_MODEL, FFN_HIDDEN),
                moe_w2[0].reshape(N_EXPERTS * FFN_HIDDEN, D_MODEL)))
    og, moe_w3b = _delta_rule(q, k, v, gate, gb, row(a_norm_w[0]), batch, seq,
                              riders=(moe_w3[0].reshape(N_EXPERTS * D_MODEL, FFN_HIDDEN),))
    x2 = _mixer_ffn(og, a_w_out[0].astype(BF16), x0, row(ln_g[0, 0]), row(ln_b[0, 0]),
                    ffn_w1[0].astype(BF16), ffn_w3[0].astype(BF16), ffn_w2[0].astype(BF16),
                    row(ln_g[0, 1]), row(ln_b[0, 1]))

    cos_full, sin_signed = _rope_tables(seq)
    w_kvq = jnp.concatenate([b_w_kv, b_w_q[0]], axis=1).astype(BF16)
    qm, km, vt, sel = _qkv_proj(x2, w_kvq, cos_full, sin_signed, batch, seq)
    attn = _moba_attention(qm, km, vt, sel, batch, seq)
    x3_tiles, route = _proj_ln_router(attn, b_w_o[0].astype(BF16), x2, row(ln_g[1, 0]), row(ln_b[1, 0]),
                                      _pad_lanes(moe_router[0]))
    n_used, tile_expert, row_token, row_dst, n_out_rows = _routing_tables(route, t)
    ys = _moe_experts(n_used, tile_expert, row_token, row_dst, x3_tiles,
                      moe_w1b.reshape(N_EXPERTS, D_MODEL, FFN_HIDDEN),
                      moe_w3b.reshape(N_EXPERTS, D_MODEL, FFN_HIDDEN),
                      moe_w2b.reshape(N_EXPERTS, FFN_HIDDEN, D_MODEL), n_out_rows)
    x4 = _moe_combine(ys, x3_tiles, route, row(ln_g[1, 1]), row(ln_b[1, 1]))
    return x4.reshape(batch, seq, D_MODEL)
```

```python
import functools
import math

import jax
import jax.numpy as jnp
from jax import lax
from jax.experimental import pallas as pl
from jax.experimental.pallas import tpu as pltpu

D_MODEL = 1024
DEPTH = 2
DEEPNORM_ALPHA = (2.0 * DEPTH) ** 0.25
LN_EPS = 1e-5
N_HEADS = 8
HEAD_DIM = 128
DN_CONV = 4
DN_CHUNK = 64
DN_NORM_EPS = 1e-6
MOBA_BLOCK = 256
MOBA_TOPK = 3
ROPE_THETA = 10000.0
NEG_INF = -1e30
LOG2_E = math.log2(math.e)
FFN_HIDDEN = 3584
N_EXPERTS = 8
TOP_K = 2

LANES = 128
SUBLANES = 8
VMEM_LIMIT_BYTES = 56 * 1024 * 1024

ROWS_PROJ = 256
ROWS_DELTA = 256
ROWS_LN = 512
ROWS_LN_SUB = 256
ROWS_FFN = 512
FFN_CHUNK = 1792
ROWS_COMBINE = 512

F32 = jnp.float32
BF16 = jnp.bfloat16

_NT = (((1,), (1,)), ((), ()))


def _dot(a, b):
    return jnp.dot(a, b, preferred_element_type=F32)


def _dot_nt(a, b):
    return lax.dot_general(a, b, _NT, preferred_element_type=F32)


def _params(*sem):
    return pltpu.CompilerParams(dimension_semantics=sem, vmem_limit_bytes=VMEM_LIMIT_BYTES)


def _layer_norm(y, g, b):
    mu = jnp.mean(y, axis=-1, keepdims=True)
    d = y - mu
    var = jnp.mean(d * d, axis=-1, keepdims=True)
    return d * lax.rsqrt(var + LN_EPS) * g + b


def _sigmoid(x):
    return 1.0 / (1.0 + jnp.exp(-x))


def _split2(x):
    hi = x.astype(BF16)
    lo = (x - hi.astype(F32)).astype(BF16)
    return hi, lo


def _dot_split(a, b, dot):
    a_hi, a_lo = _split2(a)
    b_hi, b_lo = _split2(b)
    return dot(a_hi, b_hi) + dot(a_hi, b_lo) + dot(a_lo, b_hi)


def _rider_specs(arrays, n_steps, index_map):
    specs = [pl.BlockSpec((a.shape[0] // n_steps, a.shape[1]), index_map) for a in arrays]
    return specs, [jax.ShapeDtypeStruct(a.shape, BF16) for a in arrays]


def _cast_riders(src_refs, dst_refs):
    for src, dst in zip(src_refs, dst_refs):
        dst[...] = src[...].astype(BF16)


def _in_proj_kernel(n_riders, x_ref, w_ref, wab_ref, cw_ref, alog_ref, dtb_ref, *rest):
    riders_in, rest = rest[:n_riders], rest[n_riders:]
    q_ref, k_ref, v_ref, gate_ref, gb_ref = rest[:5]
    riders_out, conv_sc = rest[5:5 + n_riders], rest[5 + n_riders]
    rows = x_ref.shape[0]
    width = N_HEADS * HEAD_DIM
    hist = SUBLANES

    @pl.when(pl.program_id(1) == 0)
    def _():
        conv_sc[...] = jnp.zeros(conv_sc.shape, F32)

    xb = x_ref[...].astype(BF16)
    chunk = 2 * HEAD_DIM
    for c in range(3 * width // chunk):
        proj = _dot(xb, w_ref[:, c * chunk:(c + 1) * chunk])
        which = c * chunk // width
        for hh in range(chunk // HEAD_DIM):
            slab = c * (chunk // HEAD_DIM) + hh
            cols = slice(slab * HEAD_DIM, (slab + 1) * HEAD_DIM)
            cur = proj[:, hh * HEAD_DIM:(hh + 1) * HEAD_DIM]
            conv_sc[slab, pl.ds(2 * hist, rows, stride=2), :] = cur
            y = cur * cw_ref[DN_CONV - 1:DN_CONV, cols]
            for j in range(DN_CONV - 1):
                shifted = conv_sc[slab, pl.ds(2 * (hist - (DN_CONV - 1) + j), rows, stride=2), :]
                y = y + shifted * cw_ref[j:j + 1, cols]
            conv_sc[slab, 0:2 * hist, :] = conv_sc[slab, 2 * rows:2 * (rows + hist), :]
            t = y * _sigmoid(y)
            ocols = slice(slab * HEAD_DIM - which * width, (slab + 1) * HEAD_DIM - which * width)
            if which == 2:
                v_ref[:, ocols] = t.astype(BF16)
            else:
                inv_norm = lax.rsqrt(jnp.sum(t * t, axis=-1, keepdims=True) + DN_NORM_EPS)
                if which == 0:
                    q_ref[:, ocols] = (t * (inv_norm * (HEAD_DIM ** -0.5))).astype(BF16)
                else:
                    k_ref[:, ocols] = (t * inv_norm).astype(BF16)
        if c % 3 == 2:
            g0 = (c // 3) * chunk
            gate_ref[:, g0:g0 + chunk] = _dot(xb, w_ref[:, 3 * width + g0:3 * width + g0 + chunk]).astype(BF16)
    ab = _dot(xb, wab_ref[...])
    z = ab + dtb_ref[...]
    softplus = jnp.maximum(z, 0.0) + jnp.log(1.0 + jnp.exp(-jnp.abs(z)))
    g = -jnp.exp(alog_ref[...]) * softplus
    lane = lax.broadcasted_iota(jnp.int32, ab.shape, 1)
    gb_ref[...] = jnp.where(lane < N_HEADS, g, _sigmoid(ab))
    _cast_riders(riders_in, riders_out)


def _in_proj(x2d, w_main, w_ab, conv_w, a_log_row, dt_row, batch, seq, riders=()):
    t = x2d.shape[0]
    width = N_HEADS * HEAD_DIM
    rows = ROWS_PROJ
    nt = seq // rows
    row_map = lambda b, s: (b * nt + s, 0)
    const = lambda b, s: (0, 0)
    act = jax.ShapeDtypeStruct((t, width), BF16)
    rider_specs, rider_shapes = _rider_specs(riders, batch * nt, row_map)
    return pl.pallas_call(
        functools.partial(_in_proj_kernel, len(riders)),
        grid=(batch, nt),
        in_specs=[pl.BlockSpec((rows, D_MODEL), row_map),
                  pl.BlockSpec((D_MODEL, 4 * width), const),
                  pl.BlockSpec((D_MODEL, LANES), const),
                  pl.BlockSpec((DN_CONV, 3 * width), const),
                  pl.BlockSpec((1, LANES), const),
                  pl.BlockSpec((1, LANES), const)] + rider_specs,
        out_specs=[pl.BlockSpec((rows, width), row_map)] * 4 + [pl.BlockSpec((rows, LANES), row_map)]
        + rider_specs,
        out_shape=[act, act, act, act, jax.ShapeDtypeStruct((t, LANES), F32)] + rider_shapes,
        scratch_shapes=[pltpu.VMEM((3 * width // LANES, 2 * (rows + SUBLANES), LANES), F32)],
        compiler_params=_params("arbitrary", "arbitrary"),
        name="dn_in_proj",
    )(x2d, w_main, w_ab, conv_w, a_log_row, dt_row, *riders)


def _delta_kernel(n_riders, q_ref, k_ref, v_ref, gate_ref, gb_ref, nw_ref, *rest):
    riders_in, o_ref = rest[:n_riders], rest[n_riders]
    riders_out, state_sc = rest[n_riders + 1:2 * n_riders + 1], rest[2 * n_riders + 1]
    c = DN_CHUNK
    two = 2 * c
    n_sub = q_ref.shape[0] // c
    n_pairs = N_HEADS // 2

    @pl.when(pl.program_id(1) == 0)
    def _():
        state_sc[...] = jnp.zeros(state_sc.shape, F32)

    r = lax.broadcasted_iota(jnp.int32, (two, two), 0)
    cc = lax.broadcasted_iota(jnp.int32, (two, two), 1)
    same = (r < c) == (cc < c)
    tri = same & (r >= cc)
    stri = same & (r > cc)
    eye = (r == cc).astype(F32)
    top_rows = lax.broadcasted_iota(jnp.int32, (two, 1), 0) < c
    top_lanes = lax.broadcasted_iota(jnp.int32, (1, two), 1) < c
    row64 = lax.broadcasted_iota(jnp.int32, (c, LANES), 0)

    def stack(ref, rows, h):
        t = ref[rows, h * HEAD_DIM:(h + 2) * HEAD_DIM].astype(F32)
        return jnp.concatenate([t[:, :HEAD_DIM], t[:, HEAD_DIM:]], axis=0)

    items = []
    for s in range(n_sub):
        rows = slice(s * c, (s + 1) * c)
        gb = gb_ref[rows, :]
        gc = gb
        sh = 1
        while sh < c:
            gc = gc + jnp.where(row64 >= sh, pltpu.roll(gc, sh, axis=0), 0.0)
            sh *= 2
        gc_pair = jnp.concatenate([gc, pltpu.roll(gc, LANES - 1, axis=1)], axis=0)
        gb_pair = jnp.concatenate([gb, pltpu.roll(gb, LANES - 1, axis=1)], axis=0)
        gc_pair_t = gc_pair.T
        for p in range(n_pairs):
            h = 2 * p
            q2, k2, v2 = stack(q_ref, rows, h), stack(k_ref, rows, h), stack(v_ref, rows, h)
            gcol = gc_pair[:, h:h + 1]
            grow = gc_pair_t[h:h + 1, :]
            beta = gb_pair[:, N_HEADS + h:N_HEADS + h + 1]
            glast = jnp.where(top_rows, gc_pair[c - 1:c, h:h + 1], gc_pair[two - 1:two, h:h + 1])
            eg = jnp.exp(gcol)
            kb = k2 * beta
            items.append(dict(
                rows=rows, h=h, k2b=k2.astype(BF16), q2b=q2.astype(BF16), kbb=kb.astype(BF16),
                decay=jnp.exp(jnp.where(tri, gcol - grow, NEG_INF)),
                rhs=jnp.concatenate([v2 * beta, kb * eg], axis=1).astype(BF16),
                qd=(q2 * eg).astype(BF16),
                kd_t=(k2 * jnp.exp(glast - gcol)).T,
                eglast=jnp.exp(glast)))
    for it in items:
        it["pw"] = jnp.where(stri, _dot_nt(it["kbb"], it["k2b"]) * it["decay"], 0.0)
        it["inv"] = eye - it["pw"]
    for _ in range(int(math.log2(c)) - 1):
        for it in items:
            pwb = it["pw"].astype(BF16)
            it["pw"] = _dot(pwb, pwb)
        for it in items:
            it["inv"] = it["inv"] + _dot(it["inv"].astype(BF16), it["pw"].astype(BF16))
    for it in items:
        sol = _dot(it["inv"].astype(BF16), it["rhs"])
        it["u"] = sol[:, :HEAD_DIM]
        it["wb"] = sol[:, HEAD_DIM:].astype(BF16)
    for it in items:
        it["amat"] = jnp.where(tri, _dot_nt(it["q2b"], it["k2b"]) * it["decay"], 0.0).astype(BF16)

    for s in range(n_sub):
        group = items[s * n_pairs:(s + 1) * n_pairs]
        for it in group:
            h = it["h"]
            it["s0"] = state_sc[h]
            it["s1"] = state_sc[h + 1]
            it["r0"] = _dot(jnp.concatenate([it["wb"][:c], it["qd"][:c]], axis=0), it["s0"].astype(BF16))
            it["r1"] = _dot(jnp.concatenate([it["wb"][c:], it["qd"][c:]], axis=0), it["s1"].astype(BF16))
        for it in group:
            v_new = it["u"] - jnp.concatenate([it["r0"][:c], it["r1"][:c]], axis=0)
            it["vnb"] = v_new.astype(BF16)
            it["o"] = jnp.concatenate([it["r0"][c:], it["r1"][c:]], axis=0) + _dot(it["amat"], it["vnb"])
        for it in group:
            h = it["h"]
            kd_t = it["kd_t"]
            state_sc[h] = it["s0"] * it["eglast"][0:1, :] + _dot(
                jnp.where(top_lanes, kd_t, 0.0).astype(BF16), it["vnb"])
            state_sc[h + 1] = it["s1"] * it["eglast"][c:c + 1, :] + _dot(
                jnp.where(top_lanes, 0.0, kd_t).astype(BF16), it["vnb"])
        for it in group:
            h, rows, o = it["h"], it["rows"], it["o"]
            gate2 = stack(gate_ref, rows, h)
            o = o * lax.rsqrt(jnp.mean(o * o, axis=-1, keepdims=True) + DN_NORM_EPS) * nw_ref[...]
            o = o * (gate2 * _sigmoid(gate2))
            o_ref[rows, h * HEAD_DIM:(h + 1) * HEAD_DIM] = o[:c].astype(BF16)
            o_ref[rows, (h + 1) * HEAD_DIM:(h + 2) * HEAD_DIM] = o[c:].astype(BF16)
    _cast_riders(riders_in, riders_out)


def _delta_rule(q, k, v, gate, gb, norm_w_row, batch, seq, riders=()):
    t = q.shape[0]
    width = N_HEADS * HEAD_DIM
    rows = ROWS_DELTA
    ns = seq // rows
    row_map = lambda b, s: (b * ns + s, 0)
    const = lambda b, s: (0, 0)
    wide = pl.BlockSpec((rows, width), row_map)
    rider_specs, rider_shapes = _rider_specs(riders, batch * ns, row_map)
    return pl.pallas_call(
        functools.partial(_delta_kernel, len(riders)),
        grid=(batch, ns),
        in_specs=[wide, wide, wide, wide,
                  pl.BlockSpec((rows, LANES), row_map),
                  pl.BlockSpec((1, HEAD_DIM), const)] + rider_specs,
        out_specs=[wide] + rider_specs,
        out_shape=[jax.ShapeDtypeStruct((t, width), BF16)] + rider_shapes,
        scratch_shapes=[pltpu.VMEM((N_HEADS, HEAD_DIM, HEAD_DIM), F32)],
        compiler_params=_params("arbitrary", "arbitrary"),
        name="dn_delta_rule",
    )(q, k, v, gate, gb, norm_w_row, *riders)


def _rows_to_tiles(ref, base, y):
    for k in range(D_MODEL // LANES):
        ref[pl.ds(base + k, y.shape[0], stride=SUBLANES), :] = y[:, k * LANES:(k + 1) * LANES]


def _tiles_to_rows(ref, base, n):
    return jnp.concatenate([ref[pl.ds(base + k, n, stride=SUBLANES), :] for k in range(D_MODEL // LANES)],
                           axis=1)


def _proj_residual(a_ref, w_ref, res_ref):
    blocks = []
    for rb in range(a_ref.shape[0] // ROWS_LN_SUB):
        rows = slice(rb * ROWS_LN_SUB, (rb + 1) * ROWS_LN_SUB)
        blocks.append((rows, DEEPNORM_ALPHA * res_ref[rows, :] + _dot(a_ref[rows, :], w_ref[...])))
    return blocks


def _proj_ln_router_kernel(a_ref, w_ref, res_ref, g_ref, b_ref, wr_ref, ot_ref, route_ref):
    for rows, y in _proj_residual(a_ref, w_ref, res_ref):
        xn = _layer_norm(y, g_ref[...], b_ref[...])
        _rows_to_tiles(ot_ref, rows.start * SUBLANES, xn)
        logits = _dot_split(xn, wr_ref[...], _dot)
        lane = lax.broadcasted_iota(jnp.int32, logits.shape, 1)
        logits = jnp.where(lane < N_EXPERTS, logits, NEG_INF)
        l1 = jnp.max(logits, axis=-1, keepdims=True)
        i1 = jnp.min(jnp.where(logits == l1, lane, LANES), axis=-1, keepdims=True)
        rest = jnp.where(lane == i1, NEG_INF, logits)
        l2 = jnp.max(rest, axis=-1, keepdims=True)
        i2 = jnp.min(jnp.where(rest == l2, lane, LANES), axis=-1, keepdims=True)
        e2 = jnp.exp(l2 - l1)
        g1 = 1.0 / (1.0 + e2)
        g2 = e2 / (1.0 + e2)
        route_ref[rows, :] = jnp.where(lane == 0, g1,
                             jnp.where(lane == 1, g2,
                             jnp.where(lane == 2, i1.astype(F32),
                             jnp.where(lane == 3, i2.astype(F32), 0.0))))


def _proj_ln_router(a, w, res, g_row, b_row, w_router):
    t, kdim = a.shape
    rows = ROWS_LN
    row_map = lambda i: (i, 0)
    const = lambda i: (0, 0)
    in_specs = [pl.BlockSpec((rows, kdim), row_map),
                pl.BlockSpec((kdim, D_MODEL), const),
                pl.BlockSpec((rows, D_MODEL), row_map),
                pl.BlockSpec((1, D_MODEL), const),
                pl.BlockSpec((1, D_MODEL), const)]
    return pl.pallas_call(
        _proj_ln_router_kernel, grid=(t // rows,),
        in_specs=in_specs + [pl.BlockSpec((D_MODEL, LANES), const)],
        out_specs=[pl.BlockSpec((rows * SUBLANES, LANES), row_map),
                   pl.BlockSpec((rows, LANES), row_map)],
        out_shape=[jax.ShapeDtypeStruct((t * SUBLANES, LANES), F32),
                   jax.ShapeDtypeStruct((t, LANES), F32)],
        compiler_params=_params("arbitrary"), name="proj_ln_router",
    )(a, w, res, g_row, b_row, w_router)


def _swiglu_hidden(x, w1_ref, w3_ref, hid_sc):
    xb = x.astype(BF16)
    sub = 2 * LANES
    for c in range(hid_sc.shape[1] // sub):
        cols = slice(c * sub, (c + 1) * sub)
        h1 = _dot(xb, w1_ref[:, cols])
        h3 = _dot(xb, w3_ref[:, cols])
        hid_sc[:, cols] = (h1 * _sigmoid(h1) * h3).astype(BF16)


def _swiglu_part(x, w1_ref, w3_ref, w2_ref, hid_sc):
    _swiglu_hidden(x, w1_ref, w3_ref, hid_sc)
    return _dot(hid_sc[...], w2_ref[...])


def _mixer_ffn_kernel(a_ref, wo_ref, res_ref, g0_ref, b0_ref, w1_ref, w3_ref, w2_ref, g1_ref, b1_ref,
                      o_ref, x_sc, acc_sc, hid_sc):
    f = pl.program_id(1)
    last_f = pl.num_programs(1) - 1
    n_sub = a_ref.shape[0] // ROWS_LN_SUB

    @pl.when(f == 0)
    def _():
        for rows, y in _proj_residual(a_ref, wo_ref, res_ref):
            x_sc[rows, :] = _layer_norm(y, g0_ref[...], b0_ref[...])

    _swiglu_hidden(x_sc[...], w1_ref, w3_ref, hid_sc)

    @pl.when(f == 0)
    def _():
        acc_sc[...] = _dot(hid_sc[...], w2_ref[...])

    @pl.when((f > 0) & (f < last_f))
    def _():
        acc_sc[...] += _dot(hid_sc[...], w2_ref[...])

    @pl.when(f == last_f)
    def _():
        for rb in range(n_sub):
            rows = slice(rb * ROWS_LN_SUB, (rb + 1) * ROWS_LN_SUB)
            y = DEEPNORM_ALPHA * x_sc[rows, :] + (acc_sc[rows, :] + _dot(hid_sc[rows, :], w2_ref[...]))
            o_ref[rows, :] = _layer_norm(y, g1_ref[...], b1_ref[...])


def _mixer_ffn(a, wo, res, g0_row, b0_row, w1, w3, w2, g1_row, b1_row):
    t = res.shape[0]
    rows, fc = ROWS_FFN, FFN_CHUNK
    nf = FFN_HIDDEN // fc
    assert nf >= 2
    row_map = lambda i, f: (i, 0)
    const = lambda i, f: (0, 0)
    vec = pl.BlockSpec((1, D_MODEL), const)
    return pl.pallas_call(
        _mixer_ffn_kernel,
        grid=(t // rows, nf),
        in_specs=[pl.BlockSpec((rows, a.shape[1]), row_map),
                  pl.BlockSpec((a.shape[1], D_MODEL), const),
                  pl.BlockSpec((rows, D_MODEL), row_map), vec, vec,
                  pl.BlockSpec((D_MODEL, fc), lambda i, f: (0, f)),
                  pl.BlockSpec((D_MODEL, fc), lambda i, f: (0, f)),
                  pl.BlockSpec((fc, D_MODEL), lambda i, f: (f, 0)), vec, vec],
        out_specs=pl.BlockSpec((rows, D_MODEL), row_map),
        out_shape=jax.ShapeDtypeStruct((t, D_MODEL), F32),
        scratch_shapes=[pltpu.VMEM((rows, D_MODEL), F32), pltpu.VMEM((rows, D_MODEL), F32),
                        pltpu.VMEM((rows, fc), BF16)],
        compiler_params=_params("arbitrary", "arbitrary"),
        name="mixer_ffn_dense",
    )(a, wo, res, g0_row, b0_row, w1, w3, w2, g1_row, b1_row)


def _moe_experts_kernel(nu_ref, te_ref, idx_ref, idx_next_ref, dst_ref, x_hbm, w1_ref, w3_ref, w2_ref,
                        y_hbm, xbuf, ybuf, xb_sc, acc_sc, hid_sc, gsem, ssem):
    i = pl.program_id(0)
    f = pl.program_id(1)
    last_f = pl.num_programs(1) - 1
    rows = xb_sc.shape[0]
    tile_rows = rows * SUBLANES
    nu = nu_ref[0]
    slot = i % 2

    def gather_start(idx, s):
        for r in range(rows):
            src = x_hbm.at[pl.ds(pl.multiple_of(idx[0, r], SUBLANES), SUBLANES)]
            pltpu.make_async_copy(src, xbuf.at[pl.ds(s * tile_rows + r * SUBLANES, SUBLANES)], gsem.at[s]).start()

    def scatter_start(s):
        for r in range(rows):
            dst = y_hbm.at[pl.ds(pl.multiple_of(dst_ref[0, r], SUBLANES), SUBLANES)]
            pltpu.make_async_copy(ybuf.at[pl.ds(s * tile_rows + r * SUBLANES, SUBLANES)], dst, ssem.at[s]).start()

    def wait_all(buf, sem, s):
        view = buf.at[pl.ds(pl.multiple_of(s * tile_rows, tile_rows), tile_rows)]
        pltpu.make_async_copy(view, view, sem.at[s]).wait()

    def for_slot(cond, s, fn):
        for static_slot in range(2):
            pl.when(cond & (s == static_slot))(functools.partial(fn, static_slot))

    first = f == 0

    @pl.when(first & (i == 0))
    def _():
        n_real = TOP_K * x_hbm.shape[0]
        ybuf[tile_rows:2 * tile_rows, :] = jnp.zeros((tile_rows, LANES), F32)
        fills = [pltpu.make_async_copy(ybuf.at[pl.ds(tile_rows, tile_rows)],
                                       y_hbm.at[pl.ds(n_real + k * tile_rows, tile_rows)], ssem.at[1])
                 for k in range((y_hbm.shape[0] - n_real) // tile_rows)]
        for cp in fills:
            cp.start()
        for cp in fills:
            cp.wait()

    @pl.when(first & (i == 0) & (nu > 0))
    def _():
        gather_start(idx_ref, 0)

    @pl.when(first & (i < nu))
    def _():
        wait_all(xbuf, gsem, slot)

    for_slot(first & (i + 1 < nu), 1 - slot, functools.partial(gather_start, idx_next_ref))

    @pl.when(i < nu)
    def _():
        @pl.when(first)
        def _():
            base = pl.multiple_of(slot * tile_rows, tile_rows)
            for k in range(D_MODEL // LANES):
                xb_sc[:, k * LANES:(k + 1) * LANES] = xbuf[pl.ds(base + k, rows, stride=SUBLANES), :].astype(BF16)

        part = _swiglu_part(xb_sc[...], w1_ref, w3_ref, w2_ref, hid_sc)

        @pl.when(first)
        def _():
            acc_sc[...] = part

        @pl.when((f > 0) & (f < last_f))
        def _():
            acc_sc[...] += part

        @pl.when(f == last_f)
        def _():
            @pl.when(i >= 2)
            def _():
                wait_all(ybuf, ssem, slot)
            _rows_to_tiles(ybuf, pl.multiple_of(slot * tile_rows, tile_rows), acc_sc[...] + part)

        for_slot(f == last_f, slot, scatter_start)

    @pl.when((i == pl.num_programs(0) - 1) & (f == last_f))
    def _():
        @pl.when(nu >= 1)
        def _():
            wait_all(ybuf, ssem, (nu - 1) % 2)

        @pl.when(nu >= 2)
        def _():
            wait_all(ybuf, ssem, nu % 2)


def _moe_experts(n_used, tile_expert, row_token, row_dst, x_tiles, w1, w3, w2, n_out_rows):
    n_tiles = tile_expert.shape[0]
    rows, fc = ROWS_FFN, FFN_CHUNK
    nf = FFN_HIDDEN // fc
    idx3 = (row_token * SUBLANES).reshape(n_tiles, 1, rows)
    dst3 = (row_dst * SUBLANES).reshape(n_tiles, 1, rows)
    smem_tile = lambda fn: pl.BlockSpec((None, 1, rows), fn, memory_space=pltpu.SMEM)
    grid_spec = pltpu.PrefetchScalarGridSpec(
        num_scalar_prefetch=2,
        grid=(n_tiles, nf),
        in_specs=[smem_tile(lambda i, f, nu, te: (i, 0, 0)),
                  smem_tile(lambda i, f, nu, te: (jnp.minimum(i + 1, n_tiles - 1), 0, 0)),
                  smem_tile(lambda i, f, nu, te: (i, 0, 0)),
                  pl.BlockSpec(memory_space=pl.ANY),
                  pl.BlockSpec((None, D_MODEL, fc), lambda i, f, nu, te: (te[i], 0, f)),
                  pl.BlockSpec((None, D_MODEL, fc), lambda i, f, nu, te: (te[i], 0, f)),
                  pl.BlockSpec((None, fc, D_MODEL), lambda i, f, nu, te: (te[i], f, 0))],
        out_specs=pl.BlockSpec(memory_space=pl.ANY),
        scratch_shapes=[pltpu.VMEM((2 * rows * SUBLANES, LANES), F32), pltpu.VMEM((2 * rows * SUBLANES, LANES), F32),
                        pltpu.VMEM((rows, D_MODEL), BF16),
                        pltpu.VMEM((rows, D_MODEL), F32), pltpu.VMEM((rows, fc), BF16),
                        pltpu.SemaphoreType.DMA((2,)), pltpu.SemaphoreType.DMA((2,))],
    )
    return pl.pallas_call(
        _moe_experts_kernel,
        grid_spec=grid_spec,
        out_shape=jax.ShapeDtypeStruct((n_out_rows * SUBLANES, LANES), F32),
        compiler_params=_params("arbitrary", "arbitrary"),
        name="moe_experts",
    )(n_used, tile_expert, idx3, idx3, dst3, x_tiles, w1, w3, w2)


def _qkv_kernel(x_ref, w_ref, cos_ref, sin_ref, q_ref, k_ref, vt_ref, sel_ref, km_sc, qf_sc):
    i = pl.program_id(1)
    width = N_HEADS * HEAD_DIM
    half = HEAD_DIM // 2
    xb = x_ref[...].astype(BF16)
    cosf = cos_ref[...]
    sinf = sin_ref[...]

    @pl.when(i == 0)
    def _():
        km_sc[...] = jnp.zeros(km_sc.shape, F32)

    def rope(t):
        return t * cosf + pltpu.roll(t, half, axis=1) * sinf

    chunk = 2 * HEAD_DIM
    n_chunks = width // chunk
    nb = km_sc.shape[0]

    def q_chunk(c):
        qq = _dot(xb, w_ref[:, 2 * width + c * chunk:2 * width + (c + 1) * chunk])
        for hh in range(2):
            cols = slice((2 * c + hh) * HEAD_DIM, (2 * c + hh + 1) * HEAD_DIM)
            qr = rope(qq[:, hh * HEAD_DIM:(hh + 1) * HEAD_DIM])
            qf_sc[:, cols] = qr
            q_ref[:, cols] = (qr * (HEAD_DIM ** -0.5 * LOG2_E)).astype(BF16)

    def kv_chunk(c):
        kk = _dot(xb, w_ref[:, c * chunk:(c + 1) * chunk])
        vv = _dot(xb, w_ref[:, width + c * chunk:width + (c + 1) * chunk])
        for hh in range(2):
            h = 2 * c + hh
            cols = slice(h * HEAD_DIM, (h + 1) * HEAD_DIM)
            part = slice(hh * HEAD_DIM, (hh + 1) * HEAD_DIM)
            kr = rope(kk[:, part])
            k_ref[:, cols] = kr.astype(BF16)
            km_row = lax.broadcasted_iota(jnp.int32, (nb, HEAD_DIM), 0)
            km_sc[:, cols] = jnp.where(km_row == i, jnp.mean(kr, axis=0, keepdims=True), km_sc[:, cols])
            vt_ref[h] = vv[:, part].T.astype(BF16)

    def block_selection():
        km = km_sc[...]
        lane_head = lax.broadcasted_iota(jnp.int32, km.shape, 1) // HEAD_DIM
        km_rows = jnp.concatenate([jnp.where(lane_head == h, km, 0.0) for h in range(N_HEADS)], axis=0)
        gate_t = _dot_split(km_rows, qf_sc[...], _dot_nt)
        blk = lax.broadcasted_iota(jnp.int32, (nb, x_ref.shape[0]), 0)
        past = blk < i
        for h in range(N_HEADS):
            g = jnp.where(past, gate_t[h * nb:(h + 1) * nb, :], NEG_INF)
            rank = jnp.zeros(g.shape, jnp.int32)
            for n in range(nb):
                gn = g[n:n + 1, :]
                ahead = (gn > g) | ((gn == g) & (blk > n))
                rank = rank + ahead.astype(jnp.int32)
            sel_ref[h * nb:(h + 1) * nb, :] = jnp.where(past & (rank < MOBA_TOPK), 0.0, NEG_INF)

    for c in range(n_chunks):
        q_chunk(c)
    kv_chunk(0)
    block_selection()
    for c in range(1, n_chunks):
        kv_chunk(c)


def _qkv_proj(x, w_kvq, cos_full, sin_signed, batch, seq):
    t = x.shape[0]
    width = N_HEADS * HEAD_DIM
    rows = MOBA_BLOCK
    nb = seq // rows
    row_map = lambda b, i: (b * nb + i, 0)
    act = jax.ShapeDtypeStruct((t, width), BF16)
    return pl.pallas_call(
        _qkv_kernel,
        grid=(batch, nb),
        in_specs=[pl.BlockSpec((rows, D_MODEL), row_map),
                  pl.BlockSpec((D_MODEL, 3 * width), lambda b, i: (0, 0)),
                  pl.BlockSpec((rows, HEAD_DIM), lambda b, i: (i, 0)),
                  pl.BlockSpec((rows, HEAD_DIM), lambda b, i: (i, 0))],
        out_specs=[pl.BlockSpec((rows, width), row_map),
                   pl.BlockSpec((rows, width), row_map),
                   pl.BlockSpec((None, N_HEADS, HEAD_DIM, rows), lambda b, i: (b, 0, 0, i)),
                   pl.BlockSpec((None, N_HEADS * nb, rows), lambda b, i: (b * nb + i, 0, 0))],
        out_shape=[act, act,
                   jax.ShapeDtypeStruct((batch, N_HEADS, HEAD_DIM, seq), BF16),
                   jax.ShapeDtypeStruct((batch * nb, N_HEADS * nb, rows), F32)],
        scratch_shapes=[pltpu.VMEM((nb, width), F32), pltpu.VMEM((rows, width), F32)],
        compiler_params=_params("arbitrary", "arbitrary"),
        name="moba_qkv_proj",
    )(x, w_kvq, cos_full, sin_signed)


def _attn_kernel(q_ref, k_ref, vt_ref, sel_ref, o_ref, kaug_sc, vtaug_sc):
    blk = MOBA_BLOCK
    seq = q_ref.shape[0]
    nb = seq // blk
    extra = 2 * SUBLANES
    key = lax.broadcasted_iota(jnp.int32, (blk, blk), 0)
    qry = lax.broadcasted_iota(jnp.int32, (blk, blk), 1)
    causal = key <= qry

    key_blk = lax.broadcasted_iota(jnp.int32, (seq, LANES), 0) // blk
    kaug_sc[:, 0:HEAD_DIM] = k_ref[...]
    kaug_sc[:, HEAD_DIM:] = jnp.where(lax.broadcasted_iota(jnp.int32, (seq, LANES), 1) == key_blk,
                                      1.0, 0.0).astype(BF16)
    vtaug_sc[0:HEAD_DIM, :] = vt_ref[...]
    vtaug_sc[HEAD_DIM:, :] = jnp.where(lax.broadcasted_iota(jnp.int32, (extra, seq), 0) == 0,
                                       1.0, 0.0).astype(BF16)
    blk_row = lax.broadcasted_iota(jnp.int32, (nb, blk), 0)

    def scores(i):
        bias = jnp.where(blk_row == i, 0.0, sel_ref[i])
        bias_t = jnp.concatenate([bias, jnp.zeros((LANES - nb, blk), F32)], axis=0).T
        q_aug = jnp.concatenate([q_ref[i * blk:(i + 1) * blk, :], bias_t.astype(BF16)], axis=1)
        return _dot_nt(kaug_sc[0:(i + 1) * blk, :], q_aug)

    def finish(i, s):
        parts = [s[j * blk:(j + 1) * blk] for j in range(i)]
        parts.append(jnp.where(causal, s[i * blk:(i + 1) * blk], NEG_INF))
        m = jnp.max(parts[0], axis=0, keepdims=True)
        for part in parts[1:]:
            m = jnp.maximum(m, jnp.max(part, axis=0, keepdims=True))
        p = jnp.concatenate([jnp.exp2(part - m).astype(BF16) for part in parts], axis=0)
        acc = _dot(vtaug_sc[:, 0:(i + 1) * blk], p)
        o_ref[i * blk:(i + 1) * blk, :] = (acc[0:HEAD_DIM] / acc[HEAD_DIM:HEAD_DIM + 1]).T.astype(BF16)

    ahead = 3
    pending = [scores(i) for i in range(min(ahead, nb))]
    for i in range(nb):
        if i + ahead < nb:
            pending.append(scores(i + ahead))
        finish(i, pending.pop(0))


def _moba_attention(q, k, vt, sel, batch, seq):
    t = q.shape[0]
    width = N_HEADS * HEAD_DIM
    nb = seq // MOBA_BLOCK
    col = pl.BlockSpec((seq, HEAD_DIM), lambda b, h: (b, h))
    return pl.pallas_call(
        _attn_kernel,
        grid=(batch, N_HEADS),
        in_specs=[col, col,
                  pl.BlockSpec((None, None, HEAD_DIM, seq), lambda b, h: (b, h, 0, 0)),
                  pl.BlockSpec((nb, nb, MOBA_BLOCK), lambda b, h: (b, h, 0))],
        out_specs=col,
        out_shape=jax.ShapeDtypeStruct((t, width), BF16),
        scratch_shapes=[pltpu.VMEM((seq, HEAD_DIM + LANES), BF16),
                        pltpu.VMEM((HEAD_DIM + 2 * SUBLANES, seq), BF16)],
        compiler_params=_params("arbitrary", "arbitrary"),
        name="moba_attention",
    )(q, k, vt, sel)


def _combine_kernel(y0_ref, y1_ref, x_ref, route_ref, g_ref, b_ref, o_ref):
    route = route_ref[...]
    rows = o_ref.shape[0]
    y = (DEEPNORM_ALPHA * _tiles_to_rows(x_ref, 0, rows) + route[:, 0:1] * _tiles_to_rows(y0_ref, 0, rows)
         + route[:, 1:2] * _tiles_to_rows(y1_ref, 0, rows))
    o_ref[...] = _layer_norm(y, g_ref[...], b_ref[...])


def _moe_combine(ys, x_tiles, route, g_row, b_row):
    t = x_tiles.shape[0] // SUBLANES
    rows = ROWS_COMBINE
    nt = t // rows
    row_map = lambda i: (i, 0)
    const = lambda i: (0, 0)
    return pl.pallas_call(
        _combine_kernel,
        grid=(nt,),
        in_specs=[pl.BlockSpec((rows * SUBLANES, LANES), row_map),
                  pl.BlockSpec((rows * SUBLANES, LANES), lambda i: (nt + i, 0)),
                  pl.BlockSpec((rows * SUBLANES, LANES), row_map),
                  pl.BlockSpec((rows, LANES), row_map),
                  pl.BlockSpec((1, D_MODEL), const),
                  pl.BlockSpec((1, D_MODEL), const)],
        out_specs=pl.BlockSpec((rows, D_MODEL), row_map),
        out_shape=jax.ShapeDtypeStruct((t, D_MODEL), F32),
        compiler_params=_params("arbitrary"),
        name="moe_combine",
    )(ys, ys, x_tiles, route, g_row, b_row)


def _routing_tables(route, n_tokens):
    rows = ROWS_FFN
    n_assign = n_tokens * TOP_K
    n_tiles = n_assign // rows + N_EXPERTS
    experts = route[:, 2:4].astype(jnp.int32).reshape(-1)
    onehot = (experts[:, None] == jnp.arange(N_EXPERTS, dtype=jnp.int32)[None, :]).astype(jnp.int32)
    counts = jnp.sum(onehot, axis=0)
    tiles_per = (counts + rows - 1) // rows
    tile_end = jnp.cumsum(tiles_per)
    id_bits = max(n_assign, rows).bit_length()
    real_keys = (experts << (id_bits + 1)) | jnp.arange(n_assign, dtype=jnp.int32)
    pad_id = jnp.arange(rows, dtype=jnp.int32)[None, :]
    pad_needed = pad_id < (tiles_per * rows - counts)[:, None]
    pad_keys = (jnp.arange(N_EXPERTS, dtype=jnp.int32)[:, None] << (id_bits + 1)) | (1 << id_bits) | pad_id
    pad_keys = jnp.where(pad_needed, pad_keys, jnp.iinfo(jnp.int32).max)
    keys = jnp.sort(jnp.concatenate([real_keys, pad_keys.reshape(-1)]))
    real = ((keys >> id_bits) & 1) == 0
    assign = jnp.where(real, keys & ((1 << id_bits) - 1), -1)
    row_token = jnp.where(real, assign // TOP_K, 0)
    pad_rank = jnp.minimum(jnp.cumsum(1 - real.astype(jnp.int32)) - 1, N_EXPERTS * rows - 1)
    row_dst = jnp.where(real, (assign % TOP_K) * n_tokens + assign // TOP_K, n_assign + pad_rank)
    tile_id = jnp.arange(n_tiles, dtype=jnp.int32)
    n_used = tile_end[-1]
    tile_expert = jnp.sum((tile_id[:, None] >= tile_end[None, :]).astype(jnp.int32), axis=1)
    last_expert = jnp.sum((n_used - 1 >= tile_end).astype(jnp.int32))
    tile_expert = jnp.minimum(jnp.where(tile_id < n_used, tile_expert, last_expert), N_EXPERTS - 1)
    return n_used.reshape(1), tile_expert, row_token, row_dst, n_assign + N_EXPERTS * rows


def _rope_tables(seq):
    half = HEAD_DIM // 2
    inv_freq = ROPE_THETA ** (-jnp.arange(half, dtype=F32) / half)
    ang = jnp.arange(seq).astype(F32)[:, None] * inv_freq[None, :]
    cos, sin = jnp.cos(ang), jnp.sin(ang)
    return jnp.concatenate([cos, cos], axis=1), jnp.concatenate([-sin, sin], axis=1)


def _pad_lanes(w):
    return jnp.pad(w, ((0, 0), (0, LANES - w.shape[1])))


def kernel(x, a_w_in, a_conv_w, a_log_decay, a_dt_bias, a_norm_w, a_w_out, b_w_kv, b_w_q, b_w_o,
           ffn_w1, ffn_w3, ffn_w2, moe_router, moe_w1, moe_w3, moe_w2, ln_g, ln_b):
    batch, seq, _ = x.shape
    t = batch * seq
    width = N_HEADS * HEAD_DIM
    x0 = x.reshape(t, D_MODEL)
    row = lambda v: v.reshape(1, -1).astype(F32)

    w_in = a_w_in[0]
    q, k, v, gate, gb, moe_w1b, moe_w2b = _in_proj(
        x0, w_in[:, :4 * width].astype(BF16), _pad_lanes(w_in[:, 4 * width:]).astype(BF16),
        a_conv_w[0], _pad_lanes(row(a_log_decay[0])), _pad_lanes(row(a_dt_bias[0])), batch, seq,
        riders=(moe_w1[0].reshape(N_EXPERTS * D_MODEL, FFN_HIDDEN),
                moe_w2[0].reshape(N_EXPERTS * FFN_HIDDEN, D_MODEL)))
    og, moe_w3b = _delta_rule(q, k, v, gate, gb, row(a_norm_w[0]), batch, seq,
                              riders=(moe_w3[0].reshape(N_EXPERTS * D_MODEL, FFN_HIDDEN),))
    x2 = _mixer_ffn(og, a_w_out[0].astype(BF16), x0, row(ln_g[0, 0]), row(ln_b[0, 0]),
                    ffn_w1[0].astype(BF16), ffn_w3[0].astype(BF16), ffn_w2[0].astype(BF16),
                    row(ln_g[0, 1]), row(ln_b[0, 1]))

    cos_full, sin_signed = _rope_tables(seq)
    w_kvq = jnp.concatenate([b_w_kv, b_w_q[0]], axis=1).astype(BF16)
    qm, km, vt, sel = _qkv_proj(x2, w_kvq, cos_full, sin_signed, batch, seq)
    attn = _moba_attention(qm, km, vt, sel, batch, seq)
    x3_tiles, route = _proj_ln_router(attn, b_w_o[0].astype(BF16), x2, row(ln_g[1, 0]), row(ln_b[1, 0]),
                                      _pad_lanes(moe_router[0]))
    n_used, tile_expert, row_token, row_dst, n_out_rows = _routing_tables(route, t)
    ys = _moe_experts(n_used, tile_expert, row_token, row_dst, x3_tiles,
                      moe_w1b.reshape(N_EXPERTS, D_MODEL, FFN_HIDDEN),
                      moe_w3b.reshape(N_EXPERTS, D_MODEL, FFN_HIDDEN),
                      moe_w2b.reshape(N_EXPERTS, FFN_HIDDEN, D_MODEL), n_out_rows)
    x4 = _moe_combine(ys, x3_tiles, route, row(ln_g[1, 1]), row(ln_b[1, 1]))
    return x4.reshape(batch, seq, D_MODEL)
```

```python
import functools
import math

import jax
import jax.numpy as jnp
from jax import lax
from jax.experimental import pallas as pl
from jax.experimental.pallas import tpu as pltpu

D_MODEL = 1024
DEPTH = 2
DEEPNORM_ALPHA = (2.0 * DEPTH) ** 0.25
LN_EPS = 1e-5
N_HEADS = 8
HEAD_DIM = 128
DN_CONV = 4
DN_CHUNK = 64
DN_NORM_EPS = 1e-6
MOBA_BLOCK = 256
MOBA_TOPK = 3
ROPE_THETA = 10000.0
NEG_INF = -1e30
LOG2_E = math.log2(math.e)
FFN_HIDDEN = 3584
N_EXPERTS = 8
TOP_K = 2

LANES = 128
SUBLANES = 8
VMEM_LIMIT_BYTES = 56 * 1024 * 1024

ROWS_PROJ = 256
ROWS_DELTA = 512
ROWS_LN = 512
ROWS_LN_SUB = 256
ROWS_FFN = 512
FFN_CHUNK = 1792
ROWS_COMBINE = 512

F32 = jnp.float32
BF16 = jnp.bfloat16

_NT = (((1,), (1,)), ((), ()))


def _dot(a, b):
    return jnp.dot(a, b, preferred_element_type=F32)


def _dot_nt(a, b):
    return lax.dot_general(a, b, _NT, preferred_element_type=F32)


def _params(*sem):
    return pltpu.CompilerParams(dimension_semantics=sem, vmem_limit_bytes=VMEM_LIMIT_BYTES)


def _layer_norm(y, g, b):
    mu = jnp.mean(y, axis=-1, keepdims=True)
    d = y - mu
    var = jnp.mean(d * d, axis=-1, keepdims=True)
    return d * lax.rsqrt(var + LN_EPS) * g + b


def _sigmoid(x):
    return 1.0 / (1.0 + jnp.exp(-x))


def _split2(x):
    hi = x.astype(BF16)
    lo = (x - hi.astype(F32)).astype(BF16)
    return hi, lo


def _dot_split(a, b, dot):
    a_hi, a_lo = _split2(a)
    b_hi, b_lo = _split2(b)
    return dot(a_hi, b_hi) + dot(a_hi, b_lo) + dot(a_lo, b_hi)


def _rider_specs(arrays, n_steps, index_map):
    specs = [pl.BlockSpec((a.shape[0] // n_steps, a.shape[1]), index_map) for a in arrays]
    return specs, [jax.ShapeDtypeStruct(a.shape, BF16) for a in arrays]


def _cast_riders(src_refs, dst_refs):
    for src, dst in zip(src_refs, dst_refs):
        dst[...] = src[...].astype(BF16)


def _in_proj_kernel(n_riders, x_ref, w_ref, wab_ref, cw_ref, alog_ref, dtb_ref, *rest):
    riders_in, rest = rest[:n_riders], rest[n_riders:]
    q_ref, k_ref, v_ref, gate_ref, gb_ref = rest[:5]
    riders_out, conv_sc = rest[5:5 + n_riders], rest[5 + n_riders]
    rows = x_ref.shape[0]
    width = N_HEADS * HEAD_DIM
    hist = SUBLANES

    @pl.when(pl.program_id(1) == 0)
    def _():
        conv_sc[...] = jnp.zeros(conv_sc.shape, F32)

    xb = x_ref[...].astype(BF16)
    chunk = 2 * HEAD_DIM
    for c in range(3 * width // chunk):
        proj = _dot(xb, w_ref[:, c * chunk:(c + 1) * chunk])
        which = c * chunk // width
        for hh in range(chunk // HEAD_DIM):
            slab = c * (chunk // HEAD_DIM) + hh
            cols = slice(slab * HEAD_DIM, (slab + 1) * HEAD_DIM)
            cur = proj[:, hh * HEAD_DIM:(hh + 1) * HEAD_DIM]
            conv_sc[slab, pl.ds(2 * hist, rows, stride=2), :] = cur
            y = cur * cw_ref[DN_CONV - 1:DN_CONV, cols]
            for j in range(DN_CONV - 1):
                shifted = conv_sc[slab, pl.ds(2 * (hist - (DN_CONV - 1) + j), rows, stride=2), :]
                y = y + shifted * cw_ref[j:j + 1, cols]
            conv_sc[slab, 0:2 * hist, :] = conv_sc[slab, 2 * rows:2 * (rows + hist), :]
            t = y * _sigmoid(y)
            ocols = slice(slab * HEAD_DIM - which * width, (slab + 1) * HEAD_DIM - which * width)
            if which == 2:
                v_ref[:, ocols] = t.astype(BF16)
            else:
                inv_norm = lax.rsqrt(jnp.sum(t * t, axis=-1, keepdims=True) + DN_NORM_EPS)
                if which == 0:
                    q_ref[:, ocols] = (t * (inv_norm * (HEAD_DIM ** -0.5))).astype(BF16)
                else:
                    k_ref[:, ocols] = (t * inv_norm).astype(BF16)
        if c % 3 == 2:
            g0 = (c // 3) * chunk
            gate_ref[:, g0:g0 + chunk] = _dot(xb, w_ref[:, 3 * width + g0:3 * width + g0 + chunk]).astype(BF16)
    ab = _dot(xb, wab_ref[...])
    z = ab + dtb_ref[...]
    softplus = jnp.maximum(z, 0.0) + jnp.log(1.0 + jnp.exp(-jnp.abs(z)))
    g = -jnp.exp(alog_ref[...]) * softplus
    lane = lax.broadcasted_iota(jnp.int32, ab.shape, 1)
    gb_ref[...] = jnp.where(lane < N_HEADS, g, _sigmoid(ab))
    _cast_riders(riders_in, riders_out)


def _in_proj(x2d, w_main, w_ab, conv_w, a_log_row, dt_row, batch, seq, riders=()):
    t = x2d.shape[0]
    width = N_HEADS * HEAD_DIM
    rows = ROWS_PROJ
    nt = seq // rows
    row_map = lambda b, s: (b * nt + s, 0)
    const = lambda b, s: (0, 0)
    act = jax.ShapeDtypeStruct((t, width), BF16)
    rider_specs, rider_shapes = _rider_specs(riders, batch * nt, row_map)
    return pl.pallas_call(
        functools.partial(_in_proj_kernel, len(riders)),
        grid=(batch, nt),
        in_specs=[pl.BlockSpec((rows, D_MODEL), row_map),
                  pl.BlockSpec((D_MODEL, 4 * width), const),
                  pl.BlockSpec((D_MODEL, LANES), const),
                  pl.BlockSpec((DN_CONV, 3 * width), const),
                  pl.BlockSpec((1, LANES), const),
                  pl.BlockSpec((1, LANES), const)] + rider_specs,
        out_specs=[pl.BlockSpec((rows, width), row_map)] * 4 + [pl.BlockSpec((rows, LANES), row_map)]
        + rider_specs,
        out_shape=[act, act, act, act, jax.ShapeDtypeStruct((t, LANES), F32)] + rider_shapes,
        scratch_shapes=[pltpu.VMEM((3 * width // LANES, 2 * (rows + SUBLANES), LANES), F32)],
        compiler_params=_params("arbitrary", "arbitrary"),
        name="dn_in_proj",
    )(x2d, w_main, w_ab, conv_w, a_log_row, dt_row, *riders)


def _delta_kernel(n_riders, q_ref, k_ref, v_ref, gate_ref, gb_ref, nw_ref, *rest):
    riders_in, o_ref = rest[:n_riders], rest[n_riders]
    riders_out, state_sc = rest[n_riders + 1:2 * n_riders + 1], rest[2 * n_riders + 1]
    c = DN_CHUNK
    two = 2 * c
    n_sub = q_ref.shape[0] // c
    n_pairs = N_HEADS // 2

    @pl.when(pl.program_id(1) == 0)
    def _():
        state_sc[...] = jnp.zeros(state_sc.shape, F32)

    r = lax.broadcasted_iota(jnp.int32, (two, two), 0)
    cc = lax.broadcasted_iota(jnp.int32, (two, two), 1)
    same = (r < c) == (cc < c)
    tri = same & (r >= cc)
    stri = same & (r > cc)
    eye = (r == cc).astype(F32)
    top_rows = lax.broadcasted_iota(jnp.int32, (two, 1), 0) < c
    top_lanes = lax.broadcasted_iota(jnp.int32, (1, two), 1) < c
    row64 = lax.broadcasted_iota(jnp.int32, (c, LANES), 0)

    def stack(ref, rows, h):
        t = ref[rows, h * HEAD_DIM:(h + 2) * HEAD_DIM].astype(F32)
        return jnp.concatenate([t[:, :HEAD_DIM], t[:, HEAD_DIM:]], axis=0)

    items = []
    for s in range(n_sub):
        rows = slice(s * c, (s + 1) * c)
        gb = gb_ref[rows, :]
        gc = gb
        sh = 1
        while sh < c:
            gc = gc + jnp.where(row64 >= sh, pltpu.roll(gc, sh, axis=0), 0.0)
            sh *= 2
        gc_pair = jnp.concatenate([gc, pltpu.roll(gc, LANES - 1, axis=1)], axis=0)
        gb_pair = jnp.concatenate([gb, pltpu.roll(gb, LANES - 1, axis=1)], axis=0)
        gc_pair_t = gc_pair.T
        for p in range(n_pairs):
            h = 2 * p
            q2, k2, v2 = stack(q_ref, rows, h), stack(k_ref, rows, h), stack(v_ref, rows, h)
            gcol = gc_pair[:, h:h + 1]
            grow = gc_pair_t[h:h + 1, :]
            beta = gb_pair[:, N_HEADS + h:N_HEADS + h + 1]
            glast = jnp.where(top_rows, gc_pair[c - 1:c, h:h + 1], gc_pair[two - 1:two, h:h + 1])
            eg = jnp.exp(gcol)
            kb = k2 * beta
            items.append(dict(
                rows=rows, h=h, k2b=k2.astype(BF16), q2b=q2.astype(BF16), kbb=kb.astype(BF16),
                decay=jnp.exp(jnp.where(tri, gcol - grow, NEG_INF)),
                rhs=jnp.concatenate([v2 * beta, kb * eg], axis=1).astype(BF16),
                qd=(q2 * eg).astype(BF16),
                kd_t=(k2 * jnp.exp(glast - gcol)).T,
                eglast=jnp.exp(glast)))
    for it in items:
        it["pw"] = jnp.where(stri, _dot_nt(it["kbb"], it["k2b"]) * it["decay"], 0.0)
        it["inv"] = eye - it["pw"]
    for _ in range(int(math.log2(c)) - 1):
        for it in items:
            pwb = it["pw"].astype(BF16)
            it["pw"] = _dot(pwb, pwb)
        for it in items:
            it["inv"] = it["inv"] + _dot(it["inv"].astype(BF16), it["pw"].astype(BF16))
    for it in items:
        sol = _dot(it["inv"].astype(BF16), it["rhs"])
        it["u"] = sol[:, :HEAD_DIM]
        it["wb"] = sol[:, HEAD_DIM:].astype(BF16)
    for it in items:
        it["amat"] = jnp.where(tri, _dot_nt(it["q2b"], it["k2b"]) * it["decay"], 0.0).astype(BF16)

    for s in range(n_sub):
        group = items[s * n_pairs:(s + 1) * n_pairs]
        for it in group:
            h = it["h"]
            it["s0"] = state_sc[h]
            it["s1"] = state_sc[h + 1]
            it["r0"] = _dot(jnp.concatenate([it["wb"][:c], it["qd"][:c]], axis=0), it["s0"].astype(BF16))
            it["r1"] = _dot(jnp.concatenate([it["wb"][c:], it["qd"][c:]], axis=0), it["s1"].astype(BF16))
        for it in group:
            v_new = it["u"] - jnp.concatenate([it["r0"][:c], it["r1"][:c]], axis=0)
            it["vnb"] = v_new.astype(BF16)
            it["o"] = jnp.concatenate([it["r0"][c:], it["r1"][c:]], axis=0) + _dot(it["amat"], it["vnb"])
        for it in group:
            h = it["h"]
            kd_t = it["kd_t"]
            state_sc[h] = it["s0"] * it["eglast"][0:1, :] + _dot(
                jnp.where(top_lanes, kd_t, 0.0).astype(BF16), it["vnb"])
            state_sc[h + 1] = it["s1"] * it["eglast"][c:c + 1, :] + _dot(
                jnp.where(top_lanes, 0.0, kd_t).astype(BF16), it["vnb"])
        for it in group:
            h, rows, o = it["h"], it["rows"], it["o"]
            gate2 = stack(gate_ref, rows, h)
            o = o * lax.rsqrt(jnp.mean(o * o, axis=-1, keepdims=True) + DN_NORM_EPS) * nw_ref[...]
            o = o * (gate2 * _sigmoid(gate2))
            o_ref[rows, h * HEAD_DIM:(h + 1) * HEAD_DIM] = o[:c].astype(BF16)
            o_ref[rows, (h + 1) * HEAD_DIM:(h + 2) * HEAD_DIM] = o[c:].astype(BF16)
    _cast_riders(riders_in, riders_out)


def _delta_rule(q, k, v, gate, gb, norm_w_row, batch, seq, riders=()):
    t = q.shape[0]
    width = N_HEADS * HEAD_DIM
    rows = ROWS_DELTA
    ns = seq // rows
    row_map = lambda b, s: (b * ns + s, 0)
    const = lambda b, s: (0, 0)
    wide = pl.BlockSpec((rows, width), row_map)
    rider_specs, rider_shapes = _rider_specs(riders, batch * ns, row_map)
    return pl.pallas_call(
        functools.partial(_delta_kernel, len(riders)),
        grid=(batch, ns),
        in_specs=[wide, wide, wide, wide,
                  pl.BlockSpec((rows, LANES), row_map),
                  pl.BlockSpec((1, HEAD_DIM), const)] + rider_specs,
        out_specs=[wide] + rider_specs,
        out_shape=[jax.ShapeDtypeStruct((t, width), BF16)] + rider_shapes,
        scratch_shapes=[pltpu.VMEM((N_HEADS, HEAD_DIM, HEAD_DIM), F32)],
        compiler_params=_params("arbitrary", "arbitrary"),
        name="dn_delta_rule",
    )(q, k, v, gate, gb, norm_w_row, *riders)


def _rows_to_tiles(ref, base, y):
    for k in range(D_MODEL // LANES):
        ref[pl.ds(base + k, y.shape[0], stride=SUBLANES), :] = y[:, k * LANES:(k + 1) * LANES]


def _tiles_to_rows(ref, base, n):
    return jnp.concatenate([ref[pl.ds(base + k, n, stride=SUBLANES), :] for k in range(D_MODEL // LANES)],
                           axis=1)


def _proj_residual(a_ref, w_ref, res_ref):
    blocks = []
    for rb in range(a_ref.shape[0] // ROWS_LN_SUB):
        rows = slice(rb * ROWS_LN_SUB, (rb + 1) * ROWS_LN_SUB)
        blocks.append((rows, DEEPNORM_ALPHA * res_ref[rows, :] + _dot(a_ref[rows, :], w_ref[...])))
    return blocks


def _proj_ln_router_kernel(a_ref, w_ref, res_ref, g_ref, b_ref, wr_ref, ot_ref, route_ref):
    for rows, y in _proj_residual(a_ref, w_ref, res_ref):
        xn = _layer_norm(y, g_ref[...], b_ref[...])
        _rows_to_tiles(ot_ref, rows.start * SUBLANES, xn)
        logits = _dot_split(xn, wr_ref[...], _dot)
        lane = lax.broadcasted_iota(jnp.int32, logits.shape, 1)
        logits = jnp.where(lane < N_EXPERTS, logits, NEG_INF)
        l1 = jnp.max(logits, axis=-1, keepdims=True)
        i1 = jnp.min(jnp.where(logits == l1, lane, LANES), axis=-1, keepdims=True)
        rest = jnp.where(lane == i1, NEG_INF, logits)
        l2 = jnp.max(rest, axis=-1, keepdims=True)
        i2 = jnp.min(jnp.where(rest == l2, lane, LANES), axis=-1, keepdims=True)
        e2 = jnp.exp(l2 - l1)
        g1 = 1.0 / (1.0 + e2)
        g2 = e2 / (1.0 + e2)
        route_ref[rows, :] = jnp.where(lane == 0, g1,
                             jnp.where(lane == 1, g2,
                             jnp.where(lane == 2, i1.astype(F32),
                             jnp.where(lane == 3, i2.astype(F32), 0.0))))


def _proj_ln_router(a, w, res, g_row, b_row, w_router):
    t, kdim = a.shape
    rows = ROWS_LN
    row_map = lambda i: (i, 0)
    const = lambda i: (0, 0)
    in_specs = [pl.BlockSpec((rows, kdim), row_map),
                pl.BlockSpec((kdim, D_MODEL), const),
                pl.BlockSpec((rows, D_MODEL), row_map),
                pl.BlockSpec((1, D_MODEL), const),
                pl.BlockSpec((1, D_MODEL), const)]
    return pl.pallas_call(
        _proj_ln_router_kernel, grid=(t // rows,),
        in_specs=in_specs + [pl.BlockSpec((D_MODEL, LANES), const)],
        out_specs=[pl.BlockSpec((rows * SUBLANES, LANES), row_map),
                   pl.BlockSpec((rows, LANES), row_map)],
        out_shape=[jax.ShapeDtypeStruct((t * SUBLANES, LANES), F32),
                   jax.ShapeDtypeStruct((t, LANES), F32)],
        compiler_params=_params("arbitrary"), name="proj_ln_router",
    )(a, w, res, g_row, b_row, w_router)


def _swiglu_hidden(x, w1_ref, w3_ref, hid_sc):
    xb = x.astype(BF16)
    sub = 2 * LANES
    for c in range(hid_sc.shape[1] // sub):
        cols = slice(c * sub, (c + 1) * sub)
        h1 = _dot(xb, w1_ref[:, cols])
        h3 = _dot(xb, w3_ref[:, cols])
        hid_sc[:, cols] = (h1 * _sigmoid(h1) * h3).astype(BF16)


def _swiglu_part(x, w1_ref, w3_ref, w2_ref, hid_sc):
    _swiglu_hidden(x, w1_ref, w3_ref, hid_sc)
    return _dot(hid_sc[...], w2_ref[...])


def _mixer_ffn_kernel(a_ref, wo_ref, res_ref, g0_ref, b0_ref, w1_ref, w3_ref, w2_ref, g1_ref, b1_ref,
                      o_ref, x_sc, acc_sc, hid_sc):
    f = pl.program_id(1)
    last_f = pl.num_programs(1) - 1
    n_sub = a_ref.shape[0] // ROWS_LN_SUB

    @pl.when(f == 0)
    def _():
        for rows, y in _proj_residual(a_ref, wo_ref, res_ref):
            x_sc[rows, :] = _layer_norm(y, g0_ref[...], b0_ref[...])

    _swiglu_hidden(x_sc[...], w1_ref, w3_ref, hid_sc)

    @pl.when(f == 0)
    def _():
        acc_sc[...] = _dot(hid_sc[...], w2_ref[...])

    @pl.when((f > 0) & (f < last_f))
    def _():
        acc_sc[...] += _dot(hid_sc[...], w2_ref[...])

    @pl.when(f == last_f)
    def _():
        for rb in range(n_sub):
            rows = slice(rb * ROWS_LN_SUB, (rb + 1) * ROWS_LN_SUB)
            y = DEEPNORM_ALPHA * x_sc[rows, :] + (acc_sc[rows, :] + _dot(hid_sc[rows, :], w2_ref[...]))
            o_ref[rows, :] = _layer_norm(y, g1_ref[...], b1_ref[...])


def _mixer_ffn(a, wo, res, g0_row, b0_row, w1, w3, w2, g1_row, b1_row):
    t = res.shape[0]
    rows, fc = ROWS_FFN, FFN_CHUNK
    nf = FFN_HIDDEN // fc
    assert nf >= 2
    row_map = lambda i, f: (i, 0)
    const = lambda i, f: (0, 0)
    vec = pl.BlockSpec((1, D_MODEL), const)
    return pl.pallas_call(
        _mixer_ffn_kernel,
        grid=(t // rows, nf),
        in_specs=[pl.BlockSpec((rows, a.shape[1]), row_map),
                  pl.BlockSpec((a.shape[1], D_MODEL), const),
                  pl.BlockSpec((rows, D_MODEL), row_map), vec, vec,
                  pl.BlockSpec((D_MODEL, fc), lambda i, f: (0, f)),
                  pl.BlockSpec((D_MODEL, fc), lambda i, f: (0, f)),
                  pl.BlockSpec((fc, D_MODEL), lambda i, f: (f, 0)), vec, vec],
        out_specs=pl.BlockSpec((rows, D_MODEL), row_map),
        out_shape=jax.ShapeDtypeStruct((t, D_MODEL), F32),
        scratch_shapes=[pltpu.VMEM((rows, D_MODEL), F32), pltpu.VMEM((rows, D_MODEL), F32),
                        pltpu.VMEM((rows, fc), BF16)],
        compiler_params=_params("arbitrary", "arbitrary"),
        name="mixer_ffn_dense",
    )(a, wo, res, g0_row, b0_row, w1, w3, w2, g1_row, b1_row)


def _moe_experts_kernel(nu_ref, te_ref, idx_ref, idx_next_ref, dst_ref, x_hbm, w1_ref, w3_ref, w2_ref,
                        y_hbm, xbuf, ybuf, xb_sc, acc_sc, hid_sc, gsem, ssem):
    i = pl.program_id(0)
    f = pl.program_id(1)
    last_f = pl.num_programs(1) - 1
    rows = xb_sc.shape[0]
    tile_rows = rows * SUBLANES
    nu = nu_ref[0]
    slot = i % 2

    def gather_start(idx, s):
        for r in range(rows):
            src = x_hbm.at[pl.ds(pl.multiple_of(idx[0, r], SUBLANES), SUBLANES)]
            pltpu.make_async_copy(src, xbuf.at[pl.ds(s * tile_rows + r * SUBLANES, SUBLANES)], gsem.at[s]).start()

    def scatter_start(s):
        for r in range(rows):
            dst = y_hbm.at[pl.ds(pl.multiple_of(dst_ref[0, r], SUBLANES), SUBLANES)]
            pltpu.make_async_copy(ybuf.at[pl.ds(s * tile_rows + r * SUBLANES, SUBLANES)], dst, ssem.at[s]).start()

    def wait_all(buf, sem, s):
        view = buf.at[pl.ds(pl.multiple_of(s * tile_rows, tile_rows), tile_rows)]
        pltpu.make_async_copy(view, view, sem.at[s]).wait()

    def for_slot(cond, s, fn):
        for static_slot in range(2):
            pl.when(cond & (s == static_slot))(functools.partial(fn, static_slot))

    first = f == 0

    @pl.when(first & (i == 0))
    def _():
        n_real = TOP_K * x_hbm.shape[0]
        ybuf[tile_rows:2 * tile_rows, :] = jnp.zeros((tile_rows, LANES), F32)
        fills = [pltpu.make_async_copy(ybuf.at[pl.ds(tile_rows, tile_rows)],
                                       y_hbm.at[pl.ds(n_real + k * tile_rows, tile_rows)], ssem.at[1])
                 for k in range((y_hbm.shape[0] - n_real) // tile_rows)]
        for cp in fills:
            cp.start()
        for cp in fills:
            cp.wait()

    @pl.when(first & (i == 0) & (nu > 0))
    def _():
        gather_start(idx_ref, 0)

    @pl.when(first & (i < nu))
    def _():
        wait_all(xbuf, gsem, slot)

    for_slot(first & (i + 1 < nu), 1 - slot, functools.partial(gather_start, idx_next_ref))

    @pl.when(i < nu)
    def _():
        @pl.when(first)
        def _():
            base = pl.multiple_of(slot * tile_rows, tile_rows)
            for k in range(D_MODEL // LANES):
                xb_sc[:, k * LANES:(k + 1) * LANES] = xbuf[pl.ds(base + k, rows, stride=SUBLANES), :].astype(BF16)

        part = _swiglu_part(xb_sc[...], w1_ref, w3_ref, w2_ref, hid_sc)

        @pl.when(first)
        def _():
            acc_sc[...] = part

        @pl.when((f > 0) & (f < last_f))
        def _():
            acc_sc[...] += part

        @pl.when(f == last_f)
        def _():
            @pl.when(i >= 2)
            def _():
                wait_all(ybuf, ssem, slot)
            _rows_to_tiles(ybuf, pl.multiple_of(slot * tile_rows, tile_rows), acc_sc[...] + part)

        for_slot(f == last_f, slot, scatter_start)

    @pl.when((i == pl.num_programs(0) - 1) & (f == last_f))
    def _():
        @pl.when(nu >= 1)
        def _():
            wait_all(ybuf, ssem, (nu - 1) % 2)

        @pl.when(nu >= 2)
        def _():
            wait_all(ybuf, ssem, nu % 2)


def _moe_experts(n_used, tile_expert, row_token, row_dst, x_tiles, w1, w3, w2, n_out_rows):
    n_tiles = tile_expert.shape[0]
    rows, fc = ROWS_FFN, FFN_CHUNK
    nf = FFN_HIDDEN // fc
    idx3 = (row_token * SUBLANES).reshape(n_tiles, 1, rows)
    dst3 = (row_dst * SUBLANES).reshape(n_tiles, 1, rows)
    smem_tile = lambda fn: pl.BlockSpec((None, 1, rows), fn, memory_space=pltpu.SMEM)
    grid_spec = pltpu.PrefetchScalarGridSpec(
        num_scalar_prefetch=2,
        grid=(n_tiles, nf),
        in_specs=[smem_tile(lambda i, f, nu, te: (i, 0, 0)),
                  smem_tile(lambda i, f, nu, te: (jnp.minimum(i + 1, n_tiles - 1), 0, 0)),
                  smem_tile(lambda i, f, nu, te: (i, 0, 0)),
                  pl.BlockSpec(memory_space=pl.ANY),
                  pl.BlockSpec((None, D_MODEL, fc), lambda i, f, nu, te: (te[i], 0, f)),
                  pl.BlockSpec((None, D_MODEL, fc), lambda i, f, nu, te: (te[i], 0, f)),
                  pl.BlockSpec((None, fc, D_MODEL), lambda i, f, nu, te: (te[i], f, 0))],
        out_specs=pl.BlockSpec(memory_space=pl.ANY),
        scratch_shapes=[pltpu.VMEM((2 * rows * SUBLANES, LANES), F32), pltpu.VMEM((2 * rows * SUBLANES, LANES), F32),
                        pltpu.VMEM((rows, D_MODEL), BF16),
                        pltpu.VMEM((rows, D_MODEL), F32), pltpu.VMEM((rows, fc), BF16),
                        pltpu.SemaphoreType.DMA((2,)), pltpu.SemaphoreType.DMA((2,))],
    )
    return pl.pallas_call(
        _moe_experts_kernel,
        grid_spec=grid_spec,
        out_shape=jax.ShapeDtypeStruct((n_out_rows * SUBLANES, LANES), F32),
        compiler_params=_params("arbitrary", "arbitrary"),
        name="moe_experts",
    )(n_used, tile_expert, idx3, idx3, dst3, x_tiles, w1, w3, w2)


def _qkv_kernel(x_ref, w_ref, cos_ref, sin_ref, q_ref, k_ref, vt_ref, sel_ref, km_sc, qf_sc):
    i = pl.program_id(1)
    width = N_HEADS * HEAD_DIM
    half = HEAD_DIM // 2
    xb = x_ref[...].astype(BF16)
    cosf = cos_ref[...]
    sinf = sin_ref[...]

    @pl.when(i == 0)
    def _():
        km_sc[...] = jnp.zeros(km_sc.shape, F32)

    def rope(t):
        return t * cosf + pltpu.roll(t, half, axis=1) * sinf

    chunk = 2 * HEAD_DIM
    n_chunks = width // chunk
    nb = km_sc.shape[0]

    def q_chunk(c):
        qq = _dot(xb, w_ref[:, 2 * width + c * chunk:2 * width + (c + 1) * chunk])
        for hh in range(2):
            cols = slice((2 * c + hh) * HEAD_DIM, (2 * c + hh + 1) * HEAD_DIM)
            qr = rope(qq[:, hh * HEAD_DIM:(hh + 1) * HEAD_DIM])
            qf_sc[:, cols] = qr
            q_ref[:, cols] = (qr * (HEAD_DIM ** -0.5 * LOG2_E)).astype(BF16)

    def kv_chunk(c):
        kk = _dot(xb, w_ref[:, c * chunk:(c + 1) * chunk])
        vv = _dot(xb, w_ref[:, width + c * chunk:width + (c + 1) * chunk])
        for hh in range(2):
            h = 2 * c + hh
            cols = slice(h * HEAD_DIM, (h + 1) * HEAD_DIM)
            part = slice(hh * HEAD_DIM, (hh + 1) * HEAD_DIM)
            kr = rope(kk[:, part])
            k_ref[:, cols] = kr.astype(BF16)
            km_row = lax.broadcasted_iota(jnp.int32, (nb, HEAD_DIM), 0)
            km_sc[:, cols] = jnp.where(km_row == i, jnp.mean(kr, axis=0, keepdims=True), km_sc[:, cols])
            vt_ref[h] = vv[:, part].T.astype(BF16)

    def block_selection():
        km = km_sc[...]
        lane_head = lax.broadcasted_iota(jnp.int32, km.shape, 1) // HEAD_DIM
        km_rows = jnp.concatenate([jnp.where(lane_head == h, km, 0.0) for h in range(N_HEADS)], axis=0)
        gate_t = _dot_split(km_rows, qf_sc[...], _dot_nt)
        blk = lax.broadcasted_iota(jnp.int32, (nb, x_ref.shape[0]), 0)
        past = blk < i
        for h in range(N_HEADS):
            g = jnp.where(past, gate_t[h * nb:(h + 1) * nb, :], NEG_INF)
            rank = jnp.zeros(g.shape, jnp.int32)
            for n in range(nb):
                gn = g[n:n + 1, :]
                ahead = (gn > g) | ((gn == g) & (blk > n))
                rank = rank + ahead.astype(jnp.int32)
            sel_ref[h * nb:(h + 1) * nb, :] = jnp.where(past & (rank < MOBA_TOPK), 0.0, NEG_INF)

    for c in range(n_chunks):
        q_chunk(c)
    kv_chunk(0)
    block_selection()
    for c in range(1, n_chunks):
        kv_chunk(c)


def _qkv_proj(x, w_kvq, cos_full, sin_signed, batch, seq):
    t = x.shape[0]
    width = N_HEADS * HEAD_DIM
    rows = MOBA_BLOCK
    nb = seq // rows
    row_map = lambda b, i: (b * nb + i, 0)
    act = jax.ShapeDtypeStruct((t, width), BF16)
    return pl.pallas_call(
        _qkv_kernel,
        grid=(batch, nb),
        in_specs=[pl.BlockSpec((rows, D_MODEL), row_map),
                  pl.BlockSpec((D_MODEL, 3 * width), lambda b, i: (0, 0)),
                  pl.BlockSpec((rows, HEAD_DIM), lambda b, i: (i, 0)),
                  pl.BlockSpec((rows, HEAD_DIM), lambda b, i: (i, 0))],
        out_specs=[pl.BlockSpec((rows, width), row_map),
                   pl.BlockSpec((rows, width), row_map),
                   pl.BlockSpec((None, N_HEADS, HEAD_DIM, rows), lambda b, i: (b, 0, 0, i)),
                   pl.BlockSpec((None, N_HEADS * nb, rows), lambda b, i: (b * nb + i, 0, 0))],
        out_shape=[act, act,
                   jax.ShapeDtypeStruct((batch, N_HEADS, HEAD_DIM, seq), BF16),
                   jax.ShapeDtypeStruct((batch * nb, N_HEADS * nb, rows), F32)],
        scratch_shapes=[pltpu.VMEM((nb, width), F32), pltpu.VMEM((rows, width), F32)],
        compiler_params=_params("arbitrary", "arbitrary"),
        name="moba_qkv_proj",
    )(x, w_kvq, cos_full, sin_signed)


def _attn_kernel(q_ref, k_ref, vt_ref, sel_ref, o_ref, kaug_sc, vtaug_sc):
    blk = MOBA_BLOCK
    seq = q_ref.shape[0]
    nb = seq // blk
    extra = 2 * SUBLANES
    key = lax.broadcasted_iota(jnp.int32, (blk, blk), 0)
    qry = lax.broadcasted_iota(jnp.int32, (blk, blk), 1)
    causal = key <= qry

    key_blk = lax.broadcasted_iota(jnp.int32, (seq, LANES), 0) // blk
    kaug_sc[:, 0:HEAD_DIM] = k_ref[...]
    kaug_sc[:, HEAD_DIM:] = jnp.where(lax.broadcasted_iota(jnp.int32, (seq, LANES), 1) == key_blk,
                                      1.0, 0.0).astype(BF16)
    vtaug_sc[0:HEAD_DIM, :] = vt_ref[...]
    vtaug_sc[HEAD_DIM:, :] = jnp.where(lax.broadcasted_iota(jnp.int32, (extra, seq), 0) == 0,
                                       1.0, 0.0).astype(BF16)
    blk_row = lax.broadcasted_iota(jnp.int32, (nb, blk), 0)

    def scores(i):
        bias = jnp.where(blk_row == i, 0.0, sel_ref[i])
        bias_t = jnp.concatenate([bias, jnp.zeros((LANES - nb, blk), F32)], axis=0).T
        q_aug = jnp.concatenate([q_ref[i * blk:(i + 1) * blk, :], bias_t.astype(BF16)], axis=1)
        return _dot_nt(kaug_sc[0:(i + 1) * blk, :], q_aug)

    def finish(i, s):
        parts = [s[j * blk:(j + 1) * blk] for j in range(i)]
        parts.append(jnp.where(causal, s[i * blk:(i + 1) * blk], NEG_INF))
        m = jnp.max(parts[0], axis=0, keepdims=True)
        for part in parts[1:]:
            m = jnp.maximum(m, jnp.max(part, axis=0, keepdims=True))
        p = jnp.concatenate([jnp.exp2(part - m).astype(BF16) for part in parts], axis=0)
        acc = _dot(vtaug_sc[:, 0:(i + 1) * blk], p)
        o_ref[i * blk:(i + 1) * blk, :] = (acc[0:HEAD_DIM] / acc[HEAD_DIM:HEAD_DIM + 1]).T.astype(BF16)

    ahead = 3
    pending = [scores(i) for i in range(min(ahead, nb))]
    for i in range(nb):
        if i + ahead < nb:
            pending.append(scores(i + ahead))
        finish(i, pending.pop(0))


def _moba_attention(q, k, vt, sel, batch, seq):
    t = q.shape[0]
    width = N_HEADS * HEAD_DIM
    nb = seq // MOBA_BLOCK
    col = pl.BlockSpec((seq, HEAD_DIM), lambda b, h: (b, h))
    return pl.pallas_call(
        _attn_kernel,
        grid=(batch, N_HEADS),
        in_specs=[col, col,
                  pl.BlockSpec((None, None, HEAD_DIM, seq), lambda b, h: (b, h, 0, 0)),
                  pl.BlockSpec((nb, nb, MOBA_BLOCK), lambda b, h: (b, h, 0))],
        out_specs=col,
        out_shape=jax.ShapeDtypeStruct((t, width), BF16),
        scratch_shapes=[pltpu.VMEM((seq, HEAD_DIM + LANES), BF16),
                        pltpu.VMEM((HEAD_DIM + 2 * SUBLANES, seq), BF16)],
        compiler_params=_params("arbitrary", "arbitrary"),
        name="moba_attention",
    )(q, k, vt, sel)


def _combine_kernel(y0_ref, y1_ref, x_ref, route_ref, g_ref, b_ref, o_ref):
    route = route_ref[...]
    rows = o_ref.shape[0]
    y = (DEEPNORM_ALPHA * _tiles_to_rows(x_ref, 0, rows) + route[:, 0:1] * _tiles_to_rows(y0_ref, 0, rows)
         + route[:, 1:2] * _tiles_to_rows(y1_ref, 0, rows))
    o_ref[...] = _layer_norm(y, g_ref[...], b_ref[...])


def _moe_combine(ys, x_tiles, route, g_row, b_row):
    t = x_tiles.shape[0] // SUBLANES
    rows = ROWS_COMBINE
    nt = t // rows
    row_map = lambda i: (i, 0)
    const = lambda i: (0, 0)
    return pl.pallas_call(
        _combine_kernel,
        grid=(nt,),
        in_specs=[pl.BlockSpec((rows * SUBLANES, LANES), row_map),
                  pl.BlockSpec((rows * SUBLANES, LANES), lambda i: (nt + i, 0)),
                  pl.BlockSpec((rows * SUBLANES, LANES), row_map),
                  pl.BlockSpec((rows, LANES), row_map),
                  pl.BlockSpec((1, D_MODEL), const),
                  pl.BlockSpec((1, D_MODEL), const)],
        out_specs=pl.BlockSpec((rows, D_MODEL), row_map),
        out_shape=jax.ShapeDtypeStruct((t, D_MODEL), F32),
        compiler_params=_params("arbitrary"),
        name="moe_combine",
    )(ys, ys, x_tiles, route, g_row, b_row)


def _routing_tables(route, n_tokens):
    rows = ROWS_FFN
    n_assign = n_tokens * TOP_K
    n_tiles = n_assign // rows + N_EXPERTS
    experts = route[:, 2:4].astype(jnp.int32).reshape(-1)
    onehot = (experts[:, None] == jnp.arange(N_EXPERTS, dtype=jnp.int32)[None, :]).astype(jnp.int32)
    counts = jnp.sum(onehot, axis=0)
    tiles_per = (counts + rows - 1) // rows
    tile_end = jnp.cumsum(tiles_per)
    id_bits = max(n_assign, rows).bit_length()
    real_keys = (experts << (id_bits + 1)) | jnp.arange(n_assign, dtype=jnp.int32)
    pad_id = jnp.arange(rows, dtype=jnp.int32)[None, :]
    pad_needed = pad_id < (tiles_per * rows - counts)[:, None]
    pad_keys = (jnp.arange(N_EXPERTS, dtype=jnp.int32)[:, None] << (id_bits + 1)) | (1 << id_bits) | pad_id
    pad_keys = jnp.where(pad_needed, pad_keys, jnp.iinfo(jnp.int32).max)
    keys = jnp.sort(jnp.concatenate([real_keys, pad_keys.reshape(-1)]))
    real = ((keys >> id_bits) & 1) == 0
    assign = jnp.where(real, keys & ((1 << id_bits) - 1), -1)
    row_token = jnp.where(real, assign // TOP_K, 0)
    pad_rank = jnp.minimum(jnp.cumsum(1 - real.astype(jnp.int32)) - 1, N_EXPERTS * rows - 1)
    row_dst = jnp.where(real, (assign % TOP_K) * n_tokens + assign // TOP_K, n_assign + pad_rank)
    tile_id = jnp.arange(n_tiles, dtype=jnp.int32)
    n_used = tile_end[-1]
    tile_expert = jnp.sum((tile_id[:, None] >= tile_end[None, :]).astype(jnp.int32), axis=1)
    last_expert = jnp.sum((n_used - 1 >= tile_end).astype(jnp.int32))
    tile_expert = jnp.minimum(jnp.where(tile_id < n_used, tile_expert, last_expert), N_EXPERTS - 1)
    return n_used.reshape(1), tile_expert, row_token, row_dst, n_assign + N_EXPERTS * rows


def _rope_tables(seq):
    half = HEAD_DIM // 2
    inv_freq = ROPE_THETA ** (-jnp.arange(half, dtype=F32) / half)
    ang = jnp.arange(seq).astype(F32)[:, None] * inv_freq[None, :]
    cos, sin = jnp.cos(ang), jnp.sin(ang)
    return jnp.concatenate([cos, cos], axis=1), jnp.concatenate([-sin, sin], axis=1)


def _pad_lanes(w):
    return jnp.pad(w, ((0, 0), (0, LANES - w.shape[1])))


def kernel(x, a_w_in, a_conv_w, a_log_decay, a_dt_bias, a_norm_w, a_w_out, b_w_kv, b_w_q, b_w_o,
           ffn_w1, ffn_w3, ffn_w2, moe_router, moe_w1, moe_w3, moe_w2, ln_g, ln_b):
    batch, seq, _ = x.shape
    t = batch * seq
    width = N_HEADS * HEAD_DIM
    x0 = x.reshape(t, D_MODEL)
    row = lambda v: v.reshape(1, -1).astype(F32)

    w_in = a_w_in[0]
    q, k, v, gate, gb, moe_w1b, moe_w2b = _in_proj(
        x0, w_in[:, :4 * width].astype(BF16), _pad_lanes(w_in[:, 4 * width:]).astype(BF16),
        a_conv_w[0], _pad_lanes(row(a_log_decay[0])), _pad_lanes(row(a_dt_bias[0])), batch, seq,
        riders=(moe_w1[0].reshape(N_EXPERTS * D_MODEL, FFN_HIDDEN),
                moe_w2[0].reshape(N_EXPERTS * FFN_HIDDEN, D_MODEL)))
    og, moe_w3b = _delta_rule(q, k, v, gate, gb, row(a_norm_w[0]), batch, seq,
                              riders=(moe_w3[0].reshape(N_EXPERTS * D_MODEL, FFN_HIDDEN),))
    x2 = _mixer_ffn(og, a_w_out[0].astype(BF16), x0, row(ln_g[0, 0]), row(ln_b[0, 0]),
                    ffn_w1[0].astype(BF16), ffn_w3[0].astype(BF16), ffn_w2[0].astype(BF16),
                    row(ln_g[0, 1]), row(ln_b[0, 1]))

    cos_full, sin_signed = _rope_tables(seq)
    w_kvq = jnp.concatenate([b_w_kv, b_w_q[0]], axis=1).astype(BF16)
    qm, km, vt, sel = _qkv_proj(x2, w_kvq, cos_full, sin_signed, batch, seq)
    attn = _moba_attention(qm, km, vt, sel, batch, seq)
    x3_tiles, route = _proj_ln_router(attn, b_w_o[0].astype(BF16), x2, row(ln_g[1, 0]), row(ln_b[1, 0]),
                                      _pad_lanes(moe_router[0]))
    n_used, tile_expert, row_token, row_dst, n_out_rows = _routing_tables(route, t)
    ys = _moe_experts(n_used, tile_expert, row_token, row_dst, x3_tiles,
                      moe_w1b.reshape(N_EXPERTS, D_MODEL, FFN_HIDDEN),
                      moe_w3b.reshape(N_EXPERTS, D_MODEL, FFN_HIDDEN),
                      moe_w2b.reshape(N_EXPERTS, FFN_HIDDEN, D_MODEL), n_out_rows)
    x4 = _moe_combine(ys, x3_tiles, route, row(ln_g[1, 1]), row(ln_b[1, 1]))
    return x4.reshape(batch, seq, D_MODEL)
```

```python
import functools
import math

import jax
import jax.numpy as jnp
from jax import lax
from jax.experimental import pallas as pl
from jax.experimental.pallas import tpu as pltpu

D_MODEL = 1024
DEPTH = 2
DEEPNORM_ALPHA = (2.0 * DEPTH) ** 0.25
LN_EPS = 1e-5
N_HEADS = 8
HEAD_DIM = 128
DN_CONV = 4
DN_CHUNK = 64
DN_NORM_EPS = 1e-6
MOBA_BLOCK = 256
MOBA_TOPK = 3
ROPE_THETA = 10000.0
NEG_INF = -1e30
LOG2_E = math.log2(math.e)
FFN_HIDDEN = 3584
N_EXPERTS = 8
TOP_K = 2

LANES = 128
SUBLANES = 8
VMEM_LIMIT_BYTES = 56 * 1024 * 1024

ROWS_PROJ = 256
ROWS_DELTA = 256
ROWS_LN = 512
ROWS_LN_SUB = 256
ROWS_FFN = 512
FFN_CHUNK = 1792
ROWS_COMBINE = 512

F32 = jnp.float32
BF16 = jnp.bfloat16

_NT = (((1,), (1,)), ((), ()))


def _dot(a, b):
    return jnp.dot(a, b, preferred_element_type=F32)


def _dot_nt(a, b):
    return lax.dot_general(a, b, _NT, preferred_element_type=F32)


def _params(*sem):
    return pltpu.CompilerParams(dimension_semantics=sem, vmem_limit_bytes=VMEM_LIMIT_BYTES)


def _layer_norm(y, g, b):
    mu = jnp.mean(y, axis=-1, keepdims=True)
    d = y - mu
    var = jnp.mean(d * d, axis=-1, keepdims=True)
    return d * lax.rsqrt(var + LN_EPS) * g + b


def _sigmoid(x):
    return 1.0 / (1.0 + jnp.exp(-x))


def _split2(x):
    hi = x.astype(BF16)
    lo = (x - hi.astype(F32)).astype(BF16)
    return hi, lo


def _dot_split(a, b, dot):
    a_hi, a_lo = _split2(a)
    b_hi, b_lo = _split2(b)
    return dot(a_hi, b_hi) + dot(a_hi, b_lo) + dot(a_lo, b_hi)


def _rider_specs(arrays, n_steps, index_map):
    assert all(a.shape[0] % (n_steps * 2 * SUBLANES) == 0 for a in arrays)
    specs = [pl.BlockSpec((a.shape[0] // n_steps, a.shape[1]), index_map) for a in arrays]
    return specs, [jax.ShapeDtypeStruct(a.shape, BF16) for a in arrays]


def _cast_riders(src_refs, dst_refs):
    for src, dst in zip(src_refs, dst_refs):
        dst[...] = src[...].astype(BF16)


def _in_proj_kernel(n_riders, x_ref, w_ref, wab_ref, cw_ref, alog_ref, dtb_ref, *rest):
    riders_in, rest = rest[:n_riders], rest[n_riders:]
    q_ref, k_ref, v_ref, gate_ref, gb_ref = rest[:5]
    riders_out, conv_sc = rest[5:5 + n_riders], rest[5 + n_riders]
    rows = x_ref.shape[0]
    width = N_HEADS * HEAD_DIM
    hist = SUBLANES

    @pl.when(pl.program_id(1) == 0)
    def _():
        conv_sc[...] = jnp.zeros(conv_sc.shape, F32)

    xb = x_ref[...].astype(BF16)
    chunk = 2 * HEAD_DIM
    for c in range(3 * width // chunk):
        proj = _dot(xb, w_ref[:, c * chunk:(c + 1) * chunk])
        which = c * chunk // width
        for hh in range(chunk // HEAD_DIM):
            slab = c * (chunk // HEAD_DIM) + hh
            cols = slice(slab * HEAD_DIM, (slab + 1) * HEAD_DIM)
            cur = proj[:, hh * HEAD_DIM:(hh + 1) * HEAD_DIM]
            conv_sc[slab, pl.ds(2 * hist, rows, stride=2), :] = cur
            y = cur * cw_ref[DN_CONV - 1:DN_CONV, cols]
            for j in range(DN_CONV - 1):
                shifted = conv_sc[slab, pl.ds(2 * (hist - (DN_CONV - 1) + j), rows, stride=2), :]
                y = y + shifted * cw_ref[j:j + 1, cols]
            conv_sc[slab, 0:2 * hist, :] = conv_sc[slab, 2 * rows:2 * (rows + hist), :]
            t = y * _sigmoid(y)
            ocols = slice(slab * HEAD_DIM - which * width, (slab + 1) * HEAD_DIM - which * width)
            if which == 2:
                v_ref[:, ocols] = t.astype(BF16)
            else:
                inv_norm = lax.rsqrt(jnp.sum(t * t, axis=-1, keepdims=True) + DN_NORM_EPS)
                if which == 0:
                    q_ref[:, ocols] = (t * (inv_norm * (HEAD_DIM ** -0.5))).astype(BF16)
                else:
                    k_ref[:, ocols] = (t * inv_norm).astype(BF16)
        if c % 3 == 2:
            g0 = (c // 3) * chunk
            gate_ref[:, g0:g0 + chunk] = _dot(xb, w_ref[:, 3 * width + g0:3 * width + g0 + chunk]).astype(BF16)
    ab = _dot(xb, wab_ref[...])
    z = ab + dtb_ref[...]
    softplus = jnp.maximum(z, 0.0) + jnp.log(1.0 + jnp.exp(-jnp.abs(z)))
    g = -jnp.exp(alog_ref[...]) * softplus
    lane = lax.broadcasted_iota(jnp.int32, ab.shape, 1)
    gb_ref[...] = jnp.where(lane < N_HEADS, g, _sigmoid(ab))
    _cast_riders(riders_in, riders_out)


def _in_proj(x2d, w_main, w_ab, conv_w, a_log_row, dt_row, batch, seq, riders=()):
    t = x2d.shape[0]
    width = N_HEADS * HEAD_DIM
    rows = ROWS_PROJ
    nt = seq // rows
    row_map = lambda b, s: (b * nt + s, 0)
    const = lambda b, s: (0, 0)
    act = jax.ShapeDtypeStruct((t, width), BF16)
    rider_specs, rider_shapes = _rider_specs(riders, batch * nt, row_map)
    return pl.pallas_call(
        functools.partial(_in_proj_kernel, len(riders)),
        grid=(batch, nt),
        in_specs=[pl.BlockSpec((rows, D_MODEL), row_map),
                  pl.BlockSpec((D_MODEL, 4 * width), const),
                  pl.BlockSpec((D_MODEL, LANES), const),
                  pl.BlockSpec((DN_CONV, 3 * width), const),
                  pl.BlockSpec((1, LANES), const),
                  pl.BlockSpec((1, LANES), const)] + rider_specs,
        out_specs=[pl.BlockSpec((rows, width), row_map)] * 4 + [pl.BlockSpec((rows, LANES), row_map)]
        + rider_specs,
        out_shape=[act, act, act, act, jax.ShapeDtypeStruct((t, LANES), F32)] + rider_shapes,
        scratch_shapes=[pltpu.VMEM((3 * width // LANES, 2 * (rows + SUBLANES), LANES), F32)],
        compiler_params=_params("arbitrary", "arbitrary"),
        name="dn_in_proj",
    )(x2d, w_main, w_ab, conv_w, a_log_row, dt_row, *riders)


def _delta_kernel(n_riders, q_ref, k_ref, v_ref, gate_ref, gb_ref, nw_ref, *rest):
    riders_in, o_ref = rest[:n_riders], rest[n_riders]
    riders_out, state_sc = rest[n_riders + 1:2 * n_riders + 1], rest[2 * n_riders + 1]
    c = DN_CHUNK
    two = 2 * c
    n_sub = q_ref.shape[0] // c
    n_pairs = N_HEADS // 2

    @pl.when(pl.program_id(1) == 0)
    def _():
        state_sc[...] = jnp.zeros(state_sc.shape, F32)

    r = lax.broadcasted_iota(jnp.int32, (two, two), 0)
    cc = lax.broadcasted_iota(jnp.int32, (two, two), 1)
    same = (r < c) == (cc < c)
    tri = same & (r >= cc)
    stri = same & (r > cc)
    eye = (r == cc).astype(F32)
    top_rows = lax.broadcasted_iota(jnp.int32, (two, 1), 0) < c
    top_lanes = lax.broadcasted_iota(jnp.int32, (1, two), 1) < c
    row64 = lax.broadcasted_iota(jnp.int32, (c, LANES), 0)

    def stack(ref, rows, h):
        t = ref[rows, h * HEAD_DIM:(h + 2) * HEAD_DIM].astype(F32)
        return jnp.concatenate([t[:, :HEAD_DIM], t[:, HEAD_DIM:]], axis=0)

    items = []
    for s in range(n_sub):
        rows = slice(s * c, (s + 1) * c)
        gb = gb_ref[rows, :]
        gc = gb
        sh = 1
        while sh < c:
            gc = gc + jnp.where(row64 >= sh, pltpu.roll(gc, sh, axis=0), 0.0)
            sh *= 2
        gc_pair = jnp.concatenate([gc, pltpu.roll(gc, LANES - 1, axis=1)], axis=0)
        gb_pair = jnp.concatenate([gb, pltpu.roll(gb, LANES - 1, axis=1)], axis=0)
        gc_pair_t = gc_pair.T
        for p in range(n_pairs):
            h = 2 * p
            q2, k2, v2 = stack(q_ref, rows, h), stack(k_ref, rows, h), stack(v_ref, rows, h)
            gcol = gc_pair[:, h:h + 1]
            grow = gc_pair_t[h:h + 1, :]
            beta = gb_pair[:, N_HEADS + h:N_HEADS + h + 1]
            glast = jnp.where(top_rows, gc_pair[c - 1:c, h:h + 1], gc_pair[two - 1:two, h:h + 1])
            eg = jnp.exp(gcol)
            kb = k2 * beta
            items.append(dict(
                rows=rows, h=h, k2b=k2.astype(BF16), q2b=q2.astype(BF16), kbb=kb.astype(BF16),
                decay=jnp.exp(jnp.where(tri, gcol - grow, NEG_INF)),
                rhs=jnp.concatenate([v2 * beta, kb * eg], axis=1).astype(BF16),
                qd=(q2 * eg).astype(BF16),
                kd_t=(k2 * jnp.exp(glast - gcol)).T,
                eglast=jnp.exp(glast)))
    for it in items:
        it["pw"] = jnp.where(stri, _dot_nt(it["kbb"], it["k2b"]) * it["decay"], 0.0)
        it["inv"] = eye - it["pw"]
    for _ in range(int(math.log2(c)) - 1):
        for it in items:
            pwb = it["pw"].astype(BF16)
            it["pw"] = _dot(pwb, pwb)
        for it in items:
            it["inv"] = it["inv"] + _dot(it["inv"].astype(BF16), it["pw"].astype(BF16))
    for it in items:
        sol = _dot(it["inv"].astype(BF16), it["rhs"])
        it["u"] = sol[:, :HEAD_DIM]
        it["wb"] = sol[:, HEAD_DIM:].astype(BF16)
    for it in items:
        it["amat"] = jnp.where(tri, _dot_nt(it["q2b"], it["k2b"]) * it["decay"], 0.0).astype(BF16)

    for s in range(n_sub):
        group = items[s * n_pairs:(s + 1) * n_pairs]
        for it in group:
            h = it["h"]
            it["s0"] = state_sc[h]
            it["s1"] = state_sc[h + 1]
            it["r0"] = _dot(jnp.concatenate([it["wb"][:c], it["qd"][:c]], axis=0), it["s0"].astype(BF16))
            it["r1"] = _dot(jnp.concatenate([it["wb"][c:], it["qd"][c:]], axis=0), it["s1"].astype(BF16))
        for it in group:
            v_new = it["u"] - jnp.concatenate([it["r0"][:c], it["r1"][:c]], axis=0)
            it["vnb"] = v_new.astype(BF16)
            it["o"] = jnp.concatenate([it["r0"][c:], it["r1"][c:]], axis=0) + _dot(it["amat"], it["vnb"])
        for it in group:
            h = it["h"]
            kd_t = it["kd_t"]
            state_sc[h] = it["s0"] * it["eglast"][0:1, :] + _dot(
                jnp.where(top_lanes, kd_t, 0.0).astype(BF16), it["vnb"])
            state_sc[h + 1] = it["s1"] * it["eglast"][c:c + 1, :] + _dot(
                jnp.where(top_lanes, 0.0, kd_t).astype(BF16), it["vnb"])
        for it in group:
            h, rows, o = it["h"], it["rows"], it["o"]
            gate2 = stack(gate_ref, rows, h)
            o = o * lax.rsqrt(jnp.mean(o * o, axis=-1, keepdims=True) + DN_NORM_EPS) * nw_ref[...]
            o = o * (gate2 * _sigmoid(gate2))
            o_ref[rows, h * HEAD_DIM:(h + 1) * HEAD_DIM] = o[:c].astype(BF16)
            o_ref[rows, (h + 1) * HEAD_DIM:(h + 2) * HEAD_DIM] = o[c:].astype(BF16)
    _cast_riders(riders_in, riders_out)


def _delta_rule(q, k, v, gate, gb, norm_w_row, batch, seq, riders=()):
    t = q.shape[0]
    width = N_HEADS * HEAD_DIM
    rows = ROWS_DELTA
    ns = seq // rows
    row_map = lambda b, s: (b * ns + s, 0)
    const = lambda b, s: (0, 0)
    wide = pl.BlockSpec((rows, width), row_map)
    rider_specs, rider_shapes = _rider_specs(riders, batch * ns, row_map)
    return pl.pallas_call(
        functools.partial(_delta_kernel, len(riders)),
        grid=(batch, ns),
        in_specs=[wide, wide, wide, wide,
                  pl.BlockSpec((rows, LANES), row_map),
                  pl.BlockSpec((1, HEAD_DIM), const)] + rider_specs,
        out_specs=[wide] + rider_specs,
        out_shape=[jax.ShapeDtypeStruct((t, width), BF16)] + rider_shapes,
        scratch_shapes=[pltpu.VMEM((N_HEADS, HEAD_DIM, HEAD_DIM), F32)],
        compiler_params=_params("arbitrary", "arbitrary"),
        name="dn_delta_rule",
    )(q, k, v, gate, gb, norm_w_row, *riders)


def _rows_to_tiles(ref, base, y):
    for k in range(D_MODEL // LANES):
        ref[pl.ds(base + k, y.shape[0], stride=SUBLANES), :] = y[:, k * LANES:(k + 1) * LANES]


def _tiles_to_rows(ref, base, n):
    return jnp.concatenate([ref[pl.ds(base + k, n, stride=SUBLANES), :] for k in range(D_MODEL // LANES)],
                           axis=1)


def _proj_residual(a_ref, w_ref, res_ref):
    blocks = []
    for rb in range(a_ref.shape[0] // ROWS_LN_SUB):
        rows = slice(rb * ROWS_LN_SUB, (rb + 1) * ROWS_LN_SUB)
        blocks.append((rows, DEEPNORM_ALPHA * res_ref[rows, :] + _dot(a_ref[rows, :], w_ref[...])))
    return blocks


def _proj_ln_router_kernel(a_ref, w_ref, res_ref, g_ref, b_ref, wr_ref, ot_ref, route_ref):
    for rows, y in _proj_residual(a_ref, w_ref, res_ref):
        xn = _layer_norm(y, g_ref[...], b_ref[...])
        _rows_to_tiles(ot_ref, rows.start * SUBLANES, xn)
        logits = _dot_split(xn, wr_ref[...], _dot)
        lane = lax.broadcasted_iota(jnp.int32, logits.shape, 1)
        logits = jnp.where(lane < N_EXPERTS, logits, NEG_INF)
        l1 = jnp.max(logits, axis=-1, keepdims=True)
        i1 = jnp.min(jnp.where(logits == l1, lane, LANES), axis=-1, keepdims=True)
        rest = jnp.where(lane == i1, NEG_INF, logits)
        l2 = jnp.max(rest, axis=-1, keepdims=True)
        i2 = jnp.min(jnp.where(rest == l2, lane, LANES), axis=-1, keepdims=True)
        e2 = jnp.exp(l2 - l1)
        g1 = 1.0 / (1.0 + e2)
        g2 = e2 / (1.0 + e2)
        route_ref[rows, :] = jnp.where(lane == 0, g1,
                             jnp.where(lane == 1, g2,
                             jnp.where(lane == 2, i1.astype(F32),
                             jnp.where(lane == 3, i2.astype(F32), 0.0))))


def _proj_ln_router(a, w, res, g_row, b_row, w_router):
    t, kdim = a.shape
    rows = ROWS_LN
    row_map = lambda i: (i, 0)
    const = lambda i: (0, 0)
    in_specs = [pl.BlockSpec((rows, kdim), row_map),
                pl.BlockSpec((kdim, D_MODEL), const),
                pl.BlockSpec((rows, D_MODEL), row_map),
                pl.BlockSpec((1, D_MODEL), const),
                pl.BlockSpec((1, D_MODEL), const)]
    return pl.pallas_call(
        _proj_ln_router_kernel, grid=(t // rows,),
        in_specs=in_specs + [pl.BlockSpec((D_MODEL, LANES), const)],
        out_specs=[pl.BlockSpec((rows * SUBLANES, LANES), row_map),
                   pl.BlockSpec((rows, LANES), row_map)],
        out_shape=[jax.ShapeDtypeStruct((t * SUBLANES, LANES), F32),
                   jax.ShapeDtypeStruct((t, LANES), F32)],
        compiler_params=_params("arbitrary"), name="proj_ln_router",
    )(a, w, res, g_row, b_row, w_router)


def _swiglu_hidden(x, w1_ref, w3_ref, hid_sc):
    xb = x.astype(BF16)
    sub = 2 * LANES
    for c in range(hid_sc.shape[1] // sub):
        cols = slice(c * sub, (c + 1) * sub)
        h1 = _dot(xb, w1_ref[:, cols])
        h3 = _dot(xb, w3_ref[:, cols])
        hid_sc[:, cols] = (h1 * _sigmoid(h1) * h3).astype(BF16)


def _swiglu_part(x, w1_ref, w3_ref, w2_ref, hid_sc):
    _swiglu_hidden(x, w1_ref, w3_ref, hid_sc)
    return _dot(hid_sc[...], w2_ref[...])


def _mixer_ffn_kernel(a_ref, wo_ref, res_ref, g0_ref, b0_ref, w1_ref, w3_ref, w2_ref, g1_ref, b1_ref,
                      o_ref, x_sc, acc_sc, hid_sc):
    f = pl.program_id(1)
    last_f = pl.num_programs(1) - 1
    n_sub = a_ref.shape[0] // ROWS_LN_SUB

    @pl.when(f == 0)
    def _():
        for rows, y in _proj_residual(a_ref, wo_ref, res_ref):
            x_sc[rows, :] = _layer_norm(y, g0_ref[...], b0_ref[...])

    _swiglu_hidden(x_sc[...], w1_ref, w3_ref, hid_sc)

    @pl.when(f == 0)
    def _():
        acc_sc[...] = _dot(hid_sc[...], w2_ref[...])

    @pl.when((f > 0) & (f < last_f))
    def _():
        acc_sc[...] += _dot(hid_sc[...], w2_ref[...])

    @pl.when(f == last_f)
    def _():
        for rb in range(n_sub):
            rows = slice(rb * ROWS_LN_SUB, (rb + 1) * ROWS_LN_SUB)
            y = DEEPNORM_ALPHA * x_sc[rows, :] + (acc_sc[rows, :] + _dot(hid_sc[rows, :], w2_ref[...]))
            o_ref[rows, :] = _layer_norm(y, g1_ref[...], b1_ref[...])


def _mixer_ffn(a, wo, res, g0_row, b0_row, w1, w3, w2, g1_row, b1_row):
    t = res.shape[0]
    rows, fc = ROWS_FFN, FFN_CHUNK
    nf = FFN_HIDDEN // fc
    assert nf >= 2
    row_map = lambda i, f: (i, 0)
    const = lambda i, f: (0, 0)
    vec = pl.BlockSpec((1, D_MODEL), const)
    return pl.pallas_call(
        _mixer_ffn_kernel,
        grid=(t // rows, nf),
        in_specs=[pl.BlockSpec((rows, a.shape[1]), row_map),
                  pl.BlockSpec((a.shape[1], D_MODEL), const),
                  pl.BlockSpec((rows, D_MODEL), row_map), vec, vec,
                  pl.BlockSpec((D_MODEL, fc), lambda i, f: (0, f)),
                  pl.BlockSpec((D_MODEL, fc), lambda i, f: (0, f)),
                  pl.BlockSpec((fc, D_MODEL), lambda i, f: (f, 0)), vec, vec],
        out_specs=pl.BlockSpec((rows, D_MODEL), row_map),
        out_shape=jax.ShapeDtypeStruct((t, D_MODEL), F32),
        scratch_shapes=[pltpu.VMEM((rows, D_MODEL), F32), pltpu.VMEM((rows, D_MODEL), F32),
                        pltpu.VMEM((rows, fc), BF16)],
        compiler_params=_params("arbitrary", "arbitrary"),
        name="mixer_ffn_dense",
    )(a, wo, res, g0_row, b0_row, w1, w3, w2, g1_row, b1_row)


def _moe_experts_kernel(nu_ref, te_ref, idx_ref, idx_next_ref, dst_ref, x_hbm, w1_ref, w3_ref, w2_ref,
                        y_hbm, xbuf, ybuf, xb_sc, acc_sc, hid_sc, gsem, ssem):
    i = pl.program_id(0)
    f = pl.program_id(1)
    last_f = pl.num_programs(1) - 1
    rows = xb_sc.shape[0]
    tile_rows = rows * SUBLANES
    nu = nu_ref[0]
    slot = i % 2

    def gather_start(idx, s):
        for r in range(rows):
            src = x_hbm.at[pl.ds(pl.multiple_of(idx[0, r], SUBLANES), SUBLANES)]
            pltpu.make_async_copy(src, xbuf.at[pl.ds(s * tile_rows + r * SUBLANES, SUBLANES)], gsem.at[s]).start()

    def scatter_start(s):
        for r in range(rows):
            dst = y_hbm.at[pl.ds(pl.multiple_of(dst_ref[0, r], SUBLANES), SUBLANES)]
            pltpu.make_async_copy(ybuf.at[pl.ds(s * tile_rows + r * SUBLANES, SUBLANES)], dst, ssem.at[s]).start()

    def wait_all(buf, sem, s):
        view = buf.at[pl.ds(pl.multiple_of(s * tile_rows, tile_rows), tile_rows)]
        pltpu.make_async_copy(view, view, sem.at[s]).wait()

    def for_slot(cond, s, fn):
        for static_slot in range(2):
            pl.when(cond & (s == static_slot))(functools.partial(fn, static_slot))

    first = f == 0

    @pl.when(first & (i == 0))
    def _():
        n_real = TOP_K * x_hbm.shape[0]
        ybuf[tile_rows:2 * tile_rows, :] = jnp.zeros((tile_rows, LANES), F32)
        fills = [pltpu.make_async_copy(ybuf.at[pl.ds(tile_rows, tile_rows)],
                                       y_hbm.at[pl.ds(n_real + k * tile_rows, tile_rows)], ssem.at[1])
                 for k in range((y_hbm.shape[0] - n_real) // tile_rows)]
        for cp in fills:
            cp.start()
        for cp in fills:
            cp.wait()

    @pl.when(first & (i == 0) & (nu > 0))
    def _():
        gather_start(idx_ref, 0)

    @pl.when(first & (i < nu))
    def _():
        wait_all(xbuf, gsem, slot)

    for_slot(first & (i + 1 < nu), 1 - slot, functools.partial(gather_start, idx_next_ref))

    @pl.when(i < nu)
    def _():
        @pl.when(first)
        def _():
            base = pl.multiple_of(slot * tile_rows, tile_rows)
            for k in range(D_MODEL // LANES):
                xb_sc[:, k * LANES:(k + 1) * LANES] = xbuf[pl.ds(base + k, rows, stride=SUBLANES), :].astype(BF16)

        part = _swiglu_part(xb_sc[...], w1_ref, w3_ref, w2_ref, hid_sc)

        @pl.when(first)
        def _():
            acc_sc[...] = part

        @pl.when((f > 0) & (f < last_f))
        def _():
            acc_sc[...] += part

        @pl.when(f == last_f)
        def _():
            @pl.when(i >= 2)
            def _():
                wait_all(ybuf, ssem, slot)
            _rows_to_tiles(ybuf, pl.multiple_of(slot * tile_rows, tile_rows), acc_sc[...] + part)

        for_slot(f == last_f, slot, scatter_start)

    @pl.when((i == pl.num_programs(0) - 1) & (f == last_f))
    def _():
        @pl.when(nu >= 1)
        def _():
            wait_all(ybuf, ssem, (nu - 1) % 2)

        @pl.when(nu >= 2)
        def _():
            wait_all(ybuf, ssem, nu % 2)


def _moe_experts(n_used, tile_expert, row_token, row_dst, x_tiles, w1, w3, w2, n_out_rows):
    n_tiles = tile_expert.shape[0]
    rows, fc = ROWS_FFN, FFN_CHUNK
    nf = FFN_HIDDEN // fc
    idx3 = (row_token * SUBLANES).reshape(n_tiles, 1, rows)
    dst3 = (row_dst * SUBLANES).reshape(n_tiles, 1, rows)
    smem_tile = lambda fn: pl.BlockSpec((None, 1, rows), fn, memory_space=pltpu.SMEM)
    grid_spec = pltpu.PrefetchScalarGridSpec(
        num_scalar_prefetch=2,
        grid=(n_tiles, nf),
        in_specs=[smem_tile(lambda i, f, nu, te: (i, 0, 0)),
                  smem_tile(lambda i, f, nu, te: (jnp.minimum(i + 1, n_tiles - 1), 0, 0)),
                  smem_tile(lambda i, f, nu, te: (i, 0, 0)),
                  pl.BlockSpec(memory_space=pl.ANY),
                  pl.BlockSpec((None, D_MODEL, fc), lambda i, f, nu, te: (te[i], 0, f)),
                  pl.BlockSpec((None, D_MODEL, fc), lambda i, f, nu, te: (te[i], 0, f)),
                  pl.BlockSpec((None, fc, D_MODEL), lambda i, f, nu, te: (te[i], f, 0))],
        out_specs=pl.BlockSpec(memory_space=pl.ANY),
        scratch_shapes=[pltpu.VMEM((2 * rows * SUBLANES, LANES), F32), pltpu.VMEM((2 * rows * SUBLANES, LANES), F32),
                        pltpu.VMEM((rows, D_MODEL), BF16),
                        pltpu.VMEM((rows, D_MODEL), F32), pltpu.VMEM((rows, fc), BF16),
                        pltpu.SemaphoreType.DMA((2,)), pltpu.SemaphoreType.DMA((2,))],
    )
    return pl.pallas_call(
        _moe_experts_kernel,
        grid_spec=grid_spec,
        out_shape=jax.ShapeDtypeStruct((n_out_rows * SUBLANES, LANES), F32),
        compiler_params=_params("arbitrary", "arbitrary"),
        name="moe_experts",
    )(n_used, tile_expert, idx3, idx3, dst3, x_tiles, w1, w3, w2)


def _qkv_kernel(x_ref, wkv_ref, wq_ref, cos_ref, sin_ref, q_ref, k_ref, vt_ref, sel_ref, km_sc, qf_sc):
    i = pl.program_id(1)
    width = N_HEADS * HEAD_DIM
    half = HEAD_DIM // 2
    xb = x_ref[...].astype(BF16)
    cosf = cos_ref[...]
    sinf = sin_ref[...]

    @pl.when(i == 0)
    def _():
        km_sc[...] = jnp.zeros(km_sc.shape, F32)

    def rope(t):
        return t * cosf + pltpu.roll(t, half, axis=1) * sinf

    chunk = 2 * HEAD_DIM
    n_chunks = width // chunk
    nb = km_sc.shape[0]

    def q_chunk(c):
        qq = _dot(xb, wq_ref[:, c * chunk:(c + 1) * chunk])
        for hh in range(2):
            cols = slice((2 * c + hh) * HEAD_DIM, (2 * c + hh + 1) * HEAD_DIM)
            qr = rope(qq[:, hh * HEAD_DIM:(hh + 1) * HEAD_DIM])
            qf_sc[:, cols] = qr
            q_ref[:, cols] = (qr * (HEAD_DIM ** -0.5 * LOG2_E)).astype(BF16)

    def kv_chunk(c):
        kk = _dot(xb, wkv_ref[:, c * chunk:(c + 1) * chunk])
        vv = _dot(xb, wkv_ref[:, width + c * chunk:width + (c + 1) * chunk])
        for hh in range(2):
            h = 2 * c + hh
            cols = slice(h * HEAD_DIM, (h + 1) * HEAD_DIM)
            part = slice(hh * HEAD_DIM, (hh + 1) * HEAD_DIM)
            kr = rope(kk[:, part])
            k_ref[:, cols] = kr.astype(BF16)
            km_row = lax.broadcasted_iota(jnp.int32, (nb, HEAD_DIM), 0)
            km_sc[:, cols] = jnp.where(km_row == i, jnp.mean(kr, axis=0, keepdims=True), km_sc[:, cols])
            vt_ref[h] = vv[:, part].T.astype(BF16)

    def block_selection():
        km = km_sc[...]
        lane_head = lax.broadcasted_iota(jnp.int32, km.shape, 1) // HEAD_DIM
        km_rows = jnp.concatenate([jnp.where(lane_head == h, km, 0.0) for h in range(N_HEADS)], axis=0)
        gate_t = _dot_split(km_rows, qf_sc[...], _dot_nt)
        blk = lax.broadcasted_iota(jnp.int32, (nb, x_ref.shape[0]), 0)
        past = blk < i
        for h in range(N_HEADS):
            g = jnp.where(past, gate_t[h * nb:(h + 1) * nb, :], NEG_INF)
            rank = jnp.zeros(g.shape, jnp.int32)
            for n in range(nb):
                gn = g[n:n + 1, :]
                ahead = (gn > g) | ((gn == g) & (blk > n))
                rank = rank + ahead.astype(jnp.int32)
            sel_ref[h * nb:(h + 1) * nb, :] = jnp.where(past & (rank < MOBA_TOPK), 0.0, NEG_INF)

    for c in range(n_chunks):
        q_chunk(c)
    kv_chunk(0)
    block_selection()
    for c in range(1, n_chunks):
        kv_chunk(c)


def _qkv_proj(x, w_kv, w_q, cos_full, sin_signed, batch, seq):
    t = x.shape[0]
    width = N_HEADS * HEAD_DIM
    rows = MOBA_BLOCK
    nb = seq // rows
    row_map = lambda b, i: (b * nb + i, 0)
    act = jax.ShapeDtypeStruct((t, width), BF16)
    return pl.pallas_call(
        _qkv_kernel,
        grid=(batch, nb),
        in_specs=[pl.BlockSpec((rows, D_MODEL), row_map),
                  pl.BlockSpec((D_MODEL, 2 * width), lambda b, i: (0, 0)),
                  pl.BlockSpec((D_MODEL, width), lambda b, i: (0, 0)),
                  pl.BlockSpec((rows, HEAD_DIM), lambda b, i: (i, 0)),
                  pl.BlockSpec((rows, HEAD_DIM), lambda b, i: (i, 0))],
        out_specs=[pl.BlockSpec((rows, width), row_map),
                   pl.BlockSpec((rows, width), row_map),
                   pl.BlockSpec((None, N_HEADS, HEAD_DIM, rows), lambda b, i: (b, 0, 0, i)),
                   pl.BlockSpec((None, N_HEADS * nb, rows), lambda b, i: (b * nb + i, 0, 0))],
        out_shape=[act, act,
                   jax.ShapeDtypeStruct((batch, N_HEADS, HEAD_DIM, seq), BF16),
                   jax.ShapeDtypeStruct((batch * nb, N_HEADS * nb, rows), F32)],
        scratch_shapes=[pltpu.VMEM((nb, width), F32), pltpu.VMEM((rows, width), F32)],
        compiler_params=_params("arbitrary", "arbitrary"),
        name="moba_qkv_proj",
    )(x, w_kv, w_q, cos_full, sin_signed)


def _attn_kernel(q_ref, k_ref, vt_ref, sel_ref, o_ref, kaug_sc, vtaug_sc):
    blk = MOBA_BLOCK
    seq = q_ref.shape[0]
    nb = seq // blk
    extra = 2 * SUBLANES
    key = lax.broadcasted_iota(jnp.int32, (blk, blk), 0)
    qry = lax.broadcasted_iota(jnp.int32, (blk, blk), 1)
    causal = key <= qry

    key_blk = lax.broadcasted_iota(jnp.int32, (seq, LANES), 0) // blk
    kaug_sc[:, 0:HEAD_DIM] = k_ref[...]
    kaug_sc[:, HEAD_DIM:] = jnp.where(lax.broadcasted_iota(jnp.int32, (seq, LANES), 1) == key_blk,
                                      1.0, 0.0).astype(BF16)
    vtaug_sc[0:HEAD_DIM, :] = vt_ref[...]
    vtaug_sc[HEAD_DIM:, :] = jnp.where(lax.broadcasted_iota(jnp.int32, (extra, seq), 0) == 0,
                                       1.0, 0.0).astype(BF16)
    blk_row = lax.broadcasted_iota(jnp.int32, (nb, blk), 0)

    def scores(i):
        bias = jnp.where(blk_row == i, 0.0, sel_ref[i])
        bias_t = jnp.concatenate([bias, jnp.zeros((LANES - nb, blk), F32)], axis=0).T
        q_aug = jnp.concatenate([q_ref[i * blk:(i + 1) * blk, :], bias_t.astype(BF16)], axis=1)
        return _dot_nt(kaug_sc[0:(i + 1) * blk, :], q_aug)

    def finish(i, s):
        parts = [s[j * blk:(j + 1) * blk] for j in range(i)]
        parts.append(jnp.where(causal, s[i * blk:(i + 1) * blk], NEG_INF))
        m = jnp.max(parts[0], axis=0, keepdims=True)
        for part in parts[1:]:
            m = jnp.maximum(m, jnp.max(part, axis=0, keepdims=True))
        p = jnp.concatenate([jnp.exp2(part - m).astype(BF16) for part in parts], axis=0)
        acc = _dot(vtaug_sc[:, 0:(i + 1) * blk], p)
        o_ref[i * blk:(i + 1) * blk, :] = (acc[0:HEAD_DIM] / acc[HEAD_DIM:HEAD_DIM + 1]).T.astype(BF16)

    ahead = 3
    pending = [scores(i) for i in range(min(ahead, nb))]
    for i in range(nb):
        if i + ahead < nb:
            pending.append(scores(i + ahead))
        finish(i, pending.pop(0))


def _moba_attention(q, k, vt, sel, batch, seq):
    t = q.shape[0]
    width = N_HEADS * HEAD_DIM
    nb = seq // MOBA_BLOCK
    col = pl.BlockSpec((seq, HEAD_DIM), lambda b, h: (b, h))
    return pl.pallas_call(
        _attn_kernel,
        grid=(batch, N_HEADS),
        in_specs=[col, col,
                  pl.BlockSpec((None, None, HEAD_DIM, seq), lambda b, h: (b, h, 0, 0)),
                  pl.BlockSpec((nb, nb, MOBA_BLOCK), lambda b, h: (b, h, 0))],
        out_specs=col,
        out_shape=jax.ShapeDtypeStruct((t, width), BF16),
        scratch_shapes=[pltpu.VMEM((seq, HEAD_DIM + LANES), BF16),
                        pltpu.VMEM((HEAD_DIM + 2 * SUBLANES, seq), BF16)],
        compiler_params=_params("arbitrary", "arbitrary"),
        name="moba_attention",
    )(q, k, vt, sel)


def _combine_kernel(y0_ref, y1_ref, x_ref, route_ref, g_ref, b_ref, o_ref):
    route = route_ref[...]
    rows = o_ref.shape[0]
    y = (DEEPNORM_ALPHA * _tiles_to_rows(x_ref, 0, rows) + route[:, 0:1] * _tiles_to_rows(y0_ref, 0, rows)
         + route[:, 1:2] * _tiles_to_rows(y1_ref, 0, rows))
    o_ref[...] = _layer_norm(y, g_ref[...], b_ref[...])


def _moe_combine(ys, x_tiles, route, g_row, b_row):
    t = x_tiles.shape[0] // SUBLANES
    rows = ROWS_COMBINE
    nt = t // rows
    row_map = lambda i: (i, 0)
    const = lambda i: (0, 0)
    return pl.pallas_call(
        _combine_kernel,
        grid=(nt,),
        in_specs=[pl.BlockSpec((rows * SUBLANES, LANES), row_map),
                  pl.BlockSpec((rows * SUBLANES, LANES), lambda i: (nt + i, 0)),
                  pl.BlockSpec((rows * SUBLANES, LANES), row_map),
                  pl.BlockSpec((rows, LANES), row_map),
                  pl.BlockSpec((1, D_MODEL), const),
                  pl.BlockSpec((1, D_MODEL), const)],
        out_specs=pl.BlockSpec((rows, D_MODEL), row_map),
        out_shape=jax.ShapeDtypeStruct((t, D_MODEL), F32),
        compiler_params=_params("arbitrary"),
        name="moe_combine",
    )(ys, ys, x_tiles, route, g_row, b_row)


def _routing_tables(route, n_tokens):
    rows = ROWS_FFN
    n_assign = n_tokens * TOP_K
    n_tiles = n_assign // rows + N_EXPERTS
    experts = route[:, 2:4].astype(jnp.int32).reshape(-1)
    onehot = (experts[:, None] == jnp.arange(N_EXPERTS, dtype=jnp.int32)[None, :]).astype(jnp.int32)
    counts = jnp.sum(onehot, axis=0)
    tiles_per = (counts + rows - 1) // rows
    tile_end = jnp.cumsum(tiles_per)
    id_bits = max(n_assign, rows).bit_length()
    real_keys = (experts << (id_bits + 1)) | jnp.arange(n_assign, dtype=jnp.int32)
    pad_id = jnp.arange(rows, dtype=jnp.int32)[None, :]
    pad_needed = pad_id < (tiles_per * rows - counts)[:, None]
    pad_keys = (jnp.arange(N_EXPERTS, dtype=jnp.int32)[:, None] << (id_bits + 1)) | (1 << id_bits) | pad_id
    pad_keys = jnp.where(pad_needed, pad_keys, jnp.iinfo(jnp.int32).max)
    keys = jnp.sort(jnp.concatenate([real_keys, pad_keys.reshape(-1)]))
    real = ((keys >> id_bits) & 1) == 0
    assign = jnp.where(real, keys & ((1 << id_bits) - 1), -1)
    row_token = jnp.where(real, assign // TOP_K, 0)
    pad_rank = jnp.minimum(jnp.cumsum(1 - real.astype(jnp.int32)) - 1, N_EXPERTS * rows - 1)
    row_dst = jnp.where(real, (assign % TOP_K) * n_tokens + assign // TOP_K, n_assign + pad_rank)
    tile_id = jnp.arange(n_tiles, dtype=jnp.int32)
    n_used = tile_end[-1]
    tile_expert = jnp.sum((tile_id[:, None] >= tile_end[None, :]).astype(jnp.int32), axis=1)
    last_expert = jnp.sum((n_used - 1 >= tile_end).astype(jnp.int32))
    tile_expert = jnp.minimum(jnp.where(tile_id < n_used, tile_expert, last_expert), N_EXPERTS - 1)
    return n_used.reshape(1), tile_expert, row_token, row_dst, n_assign + N_EXPERTS * rows


def _rope_tables(seq):
    half = HEAD_DIM // 2
    inv_freq = ROPE_THETA ** (-jnp.arange(half, dtype=F32) / half)
    ang = jnp.arange(seq).astype(F32)[:, None] * inv_freq[None, :]
    cos, sin = jnp.cos(ang), jnp.sin(ang)
    return jnp.concatenate([cos, cos], axis=1), jnp.concatenate([-sin, sin], axis=1)


def _pad_lanes(w):
    return jnp.pad(w, ((0, 0), (0, LANES - w.shape[1])))


def kernel(x, a_w_in, a_conv_w, a_log_decay, a_dt_bias, a_norm_w, a_w_out, b_w_kv, b_w_q, b_w_o,
           ffn_w1, ffn_w3, ffn_w2, moe_router, moe_w1, moe_w3, moe_w2, ln_g, ln_b):
    batch, seq, _ = x.shape
    t = batch * seq
    width = N_HEADS * HEAD_DIM
    x0 = x.reshape(t, D_MODEL)
    row = lambda v: v.reshape(1, -1).astype(F32)

    w_in = a_w_in[0]
    q, k, v, gate, gb, moe_w1b, moe_w2b = _in_proj(
        x0, w_in[:, :4 * width].astype(BF16), _pad_lanes(w_in[:, 4 * width:]).astype(BF16),
        a_conv_w[0], _pad_lanes(row(a_log_decay[0])), _pad_lanes(row(a_dt_bias[0])), batch, seq,
        riders=(moe_w1[0].reshape(N_EXPERTS * D_MODEL, FFN_HIDDEN),
                moe_w2[0].reshape(N_EXPERTS * FFN_HIDDEN, D_MODEL)))
    og, moe_w3b, ffn_w1b, ffn_w3b = _delta_rule(
        q, k, v, gate, gb, row(a_norm_w[0]), batch, seq,
        riders=(moe_w3[0].reshape(N_EXPERTS * D_MODEL, FFN_HIDDEN), ffn_w1[0], ffn_w3[0]))
    x2 = _mixer_ffn(og, a_w_out[0].astype(BF16), x0, row(ln_g[0, 0]), row(ln_b[0, 0]),
                    ffn_w1b, ffn_w3b, ffn_w2[0].astype(BF16), row(ln_g[0, 1]), row(ln_b[0, 1]))

    cos_full, sin_signed = _rope_tables(seq)
    qm, km, vt, sel = _qkv_proj(x2, b_w_kv.astype(BF16), b_w_q[0].astype(BF16), cos_full, sin_signed,
                                batch, seq)
    attn = _moba_attention(qm, km, vt, sel, batch, seq)
    x3_tiles, route = _proj_ln_router(attn, b_w_o[0].astype(BF16), x2, row(ln_g[1, 0]), row(ln_b[1, 0]),
                                      _pad_lanes(moe_router[0]))
    n_used, tile_expert, row_token, row_dst, n_out_rows = _routing_tables(route, t)
    ys = _moe_experts(n_used, tile_expert, row_token, row_dst, x3_tiles,
                      moe_w1b.reshape(N_EXPERTS, D_MODEL, FFN_HIDDEN),
                      moe_w3b.reshape(N_EXPERTS, D_MODEL, FFN_HIDDEN),
                      moe_w2b.reshape(N_EXPERTS, FFN_HIDDEN, D_MODEL), n_out_rows)
    x4 = _moe_combine(ys, x3_tiles, route, row(ln_g[1, 1]), row(ln_b[1, 1]))
    return x4.reshape(batch, seq, D_MODEL)
```

```python
import functools
import math

import jax
import jax.numpy as jnp
from jax import lax
from jax.experimental import pallas as pl
from jax.experimental.pallas import tpu as pltpu

D_MODEL = 1024
DEPTH = 2
DEEPNORM_ALPHA = (2.0 * DEPTH) ** 0.25
LN_EPS = 1e-5
N_HEADS = 8
HEAD_DIM = 128
DN_CONV = 4
DN_CHUNK = 64
DN_NORM_EPS = 1e-6
MOBA_BLOCK = 256
MOBA_TOPK = 3
ROPE_THETA = 10000.0
NEG_INF = -1e30
LOG2_E = math.log2(math.e)
FFN_HIDDEN = 3584
N_EXPERTS = 8
TOP_K = 2

LANES = 128
SUBLANES = 8
VMEM_LIMIT_BYTES = 56 * 1024 * 1024

ROWS_PROJ = 256
ROWS_DELTA = 256
ROWS_LN = 512
ROWS_LN_SUB = 256
ROWS_FFN = 512
FFN_CHUNK = 1792
ROWS_COMBINE = 512

F32 = jnp.float32
BF16 = jnp.bfloat16

_NT = (((1,), (1,)), ((), ()))


def _dot(a, b):
    return jnp.dot(a, b, preferred_element_type=F32)


def _dot_nt(a, b):
    return lax.dot_general(a, b, _NT, preferred_element_type=F32)


def _params(*sem):
    return pltpu.CompilerParams(dimension_semantics=sem, vmem_limit_bytes=VMEM_LIMIT_BYTES)


def _layer_norm(y, g, b):
    mu = jnp.mean(y, axis=-1, keepdims=True)
    d = y - mu
    var = jnp.mean(d * d, axis=-1, keepdims=True)
    return d * lax.rsqrt(var + LN_EPS) * g + b


def _sigmoid(x):
    return 1.0 / (1.0 + jnp.exp(-x))


def _split2(x):
    hi = x.astype(BF16)
    lo = (x - hi.astype(F32)).astype(BF16)
    return hi, lo


def _dot_split(a, b, dot):
    a_hi, a_lo = _split2(a)
    b_hi, b_lo = _split2(b)
    return dot(a_hi, b_hi) + dot(a_hi, b_lo) + dot(a_lo, b_hi)


def _rider_specs(arrays, n_steps, index_map):
    assert all(a.shape[0] % (n_steps * 2 * SUBLANES) == 0 for a in arrays)
    specs = [pl.BlockSpec((a.shape[0] // n_steps, a.shape[1]), index_map) for a in arrays]
    return specs, [jax.ShapeDtypeStruct(a.shape, BF16) for a in arrays]


def _cast_riders(src_refs, dst_refs):
    for src, dst in zip(src_refs, dst_refs):
        dst[...] = src[...].astype(BF16)


def _in_proj_kernel(n_riders, x_ref, w_ref, wab_ref, cw_ref, alog_ref, dtb_ref, *rest):
    riders_in, rest = rest[:n_riders], rest[n_riders:]
    q_ref, k_ref, v_ref, gate_ref, gb_ref = rest[:5]
    riders_out, conv_sc = rest[5:5 + n_riders], rest[5 + n_riders]
    rows = x_ref.shape[0]
    width = N_HEADS * HEAD_DIM
    hist = SUBLANES

    @pl.when(pl.program_id(1) == 0)
    def _():
        conv_sc[...] = jnp.zeros(conv_sc.shape, F32)

    xb = x_ref[...].astype(BF16)
    chunk = 2 * HEAD_DIM
    for c in range(3 * width // chunk):
        proj = _dot(xb, w_ref[:, c * chunk:(c + 1) * chunk])
        which = c * chunk // width
        for hh in range(chunk // HEAD_DIM):
            slab = c * (chunk // HEAD_DIM) + hh
            cols = slice(slab * HEAD_DIM, (slab + 1) * HEAD_DIM)
            cur = proj[:, hh * HEAD_DIM:(hh + 1) * HEAD_DIM]
            conv_sc[slab, pl.ds(2 * hist, rows, stride=2), :] = cur
            y = cur * cw_ref[DN_CONV - 1:DN_CONV, cols]
            for j in range(DN_CONV - 1):
                shifted = conv_sc[slab, pl.ds(2 * (hist - (DN_CONV - 1) + j), rows, stride=2), :]
                y = y + shifted * cw_ref[j:j + 1, cols]
            conv_sc[slab, 0:2 * hist, :] = conv_sc[slab, 2 * rows:2 * (rows + hist), :]
            t = y * _sigmoid(y)
            ocols = slice(slab * HEAD_DIM - which * width, (slab + 1) * HEAD_DIM - which * width)
            if which == 2:
                v_ref[:, ocols] = t.astype(BF16)
            else:
                inv_norm = lax.rsqrt(jnp.sum(t * t, axis=-1, keepdims=True) + DN_NORM_EPS)
                if which == 0:
                    q_ref[:, ocols] = (t * (inv_norm * (HEAD_DIM ** -0.5))).astype(BF16)
                else:
                    k_ref[:, ocols] = (t * inv_norm).astype(BF16)
        if c % 3 == 2:
            g0 = (c // 3) * chunk
            gate_ref[:, g0:g0 + chunk] = _dot(xb, w_ref[:, 3 * width + g0:3 * width + g0 + chunk]).astype(BF16)
    ab = _dot(xb, wab_ref[...])
    z = ab + dtb_ref[...]
    softplus = jnp.maximum(z, 0.0) + jnp.log(1.0 + jnp.exp(-jnp.abs(z)))
    g = -jnp.exp(alog_ref[...]) * softplus
    lane = lax.broadcasted_iota(jnp.int32, ab.shape, 1)
    gb_ref[...] = jnp.where(lane < N_HEADS, g, _sigmoid(ab))
    _cast_riders(riders_in, riders_out)


def _in_proj(x2d, w_main, w_ab, conv_w, a_log_row, dt_row, batch, seq, riders=()):
    t = x2d.shape[0]
    width = N_HEADS * HEAD_DIM
    rows = ROWS_PROJ
    nt = seq // rows
    row_map = lambda b, s: (b * nt + s, 0)
    const = lambda b, s: (0, 0)
    act = jax.ShapeDtypeStruct((t, width), BF16)
    rider_specs, rider_shapes = _rider_specs(riders, batch * nt, row_map)
    return pl.pallas_call(
        functools.partial(_in_proj_kernel, len(riders)),
        grid=(batch, nt),
        in_specs=[pl.BlockSpec((rows, D_MODEL), row_map),
                  pl.BlockSpec((D_MODEL, 4 * width), const),
                  pl.BlockSpec((D_MODEL, LANES), const),
                  pl.BlockSpec((DN_CONV, 3 * width), const),
                  pl.BlockSpec((1, LANES), const),
                  pl.BlockSpec((1, LANES), const)] + rider_specs,
        out_specs=[pl.BlockSpec((rows, width), row_map)] * 4 + [pl.BlockSpec((rows, LANES), row_map)]
        + rider_specs,
        out_shape=[act, act, act, act, jax.ShapeDtypeStruct((t, LANES), F32)] + rider_shapes,
        scratch_shapes=[pltpu.VMEM((3 * width // LANES, 2 * (rows + SUBLANES), LANES), F32)],
        compiler_params=_params("arbitrary", "arbitrary"),
        name="dn_in_proj",
    )(x2d, w_main, w_ab, conv_w, a_log_row, dt_row, *riders)


def _delta_kernel(n_riders, q_ref, k_ref, v_ref, gate_ref, gb_ref, nw_ref, *rest):
    riders_in, o_ref = rest[:n_riders], rest[n_riders]
    riders_out, state_sc = rest[n_riders + 1:2 * n_riders + 1], rest[2 * n_riders + 1]
    c = DN_CHUNK
    two = 2 * c
    n_sub = q_ref.shape[0] // c
    n_pairs = N_HEADS // 2

    @pl.when(pl.program_id(1) == 0)
    def _():
        state_sc[...] = jnp.zeros(state_sc.shape, F32)

    r = lax.broadcasted_iota(jnp.int32, (two, two), 0)
    cc = lax.broadcasted_iota(jnp.int32, (two, two), 1)
    same = (r < c) == (cc < c)
    tri = same & (r >= cc)
    stri = same & (r > cc)
    eye = (r == cc).astype(F32)
    top_rows = lax.broadcasted_iota(jnp.int32, (two, 1), 0) < c
    top_lanes = lax.broadcasted_iota(jnp.int32, (1, two), 1) < c
    row64 = lax.broadcasted_iota(jnp.int32, (c, LANES), 0)

    def stack(ref, rows, h):
        t = ref[rows, h * HEAD_DIM:(h + 2) * HEAD_DIM].astype(F32)
        return jnp.concatenate([t[:, :HEAD_DIM], t[:, HEAD_DIM:]], axis=0)

    items = []
    for s in range(n_sub):
        rows = slice(s * c, (s + 1) * c)
        gb = gb_ref[rows, :]
        gc = gb
        sh = 1
        while sh < c:
            gc = gc + jnp.where(row64 >= sh, pltpu.roll(gc, sh, axis=0), 0.0)
            sh *= 2
        gc_pair = jnp.concatenate([gc, pltpu.roll(gc, LANES - 1, axis=1)], axis=0)
        gb_pair = jnp.concatenate([gb, pltpu.roll(gb, LANES - 1, axis=1)], axis=0)
        gc_pair_t = gc_pair.T
        for p in range(n_pairs):
            h = 2 * p
            q2, k2, v2 = stack(q_ref, rows, h), stack(k_ref, rows, h), stack(v_ref, rows, h)
            gcol = gc_pair[:, h:h + 1]
            grow = gc_pair_t[h:h + 1, :]
            beta = gb_pair[:, N_HEADS + h:N_HEADS + h + 1]
            glast = jnp.where(top_rows, gc_pair[c - 1:c, h:h + 1], gc_pair[two - 1:two, h:h + 1])
            eg = jnp.exp(gcol)
            kb = k2 * beta
            items.append(dict(
                rows=rows, h=h, k2b=k2.astype(BF16), q2b=q2.astype(BF16), kbb=kb.astype(BF16),
                decay=jnp.exp(jnp.where(tri, gcol - grow, NEG_INF)),
                rhs=jnp.concatenate([v2 * beta, kb * eg], axis=1).astype(BF16),
                qd=(q2 * eg).astype(BF16),
                kd_t=(k2 * jnp.exp(glast - gcol)).T,
                eglast=jnp.exp(glast)))
    for it in items:
        it["pw"] = jnp.where(stri, _dot_nt(it["kbb"], it["k2b"]) * it["decay"], 0.0)
        it["inv"] = eye - it["pw"]
    for _ in range(int(math.log2(c)) - 1):
        for it in items:
            pwb = it["pw"].astype(BF16)
            it["pw"] = _dot(pwb, pwb)
        for it in items:
            it["inv"] = it["inv"] + _dot(it["inv"].astype(BF16), it["pw"].astype(BF16))
    for it in items:
        sol = _dot(it["inv"].astype(BF16), it["rhs"])
        it["u"] = sol[:, :HEAD_DIM]
        it["wb"] = sol[:, HEAD_DIM:].astype(BF16)
    for it in items:
        it["amat"] = jnp.where(tri, _dot_nt(it["q2b"], it["k2b"]) * it["decay"], 0.0).astype(BF16)

    for s in range(n_sub):
        group = items[s * n_pairs:(s + 1) * n_pairs]
        for it in group:
            h = it["h"]
            it["s0"] = state_sc[h]
            it["s1"] = state_sc[h + 1]
            it["r0"] = _dot(jnp.concatenate([it["wb"][:c], it["qd"][:c]], axis=0), it["s0"].astype(BF16))
            it["r1"] = _dot(jnp.concatenate([it["wb"][c:], it["qd"][c:]], axis=0), it["s1"].astype(BF16))
        for it in group:
            v_new = it["u"] - jnp.concatenate([it["r0"][:c], it["r1"][:c]], axis=0)
            it["vnb"] = v_new.astype(BF16)
            it["o"] = jnp.concatenate([it["r0"][c:], it["r1"][c:]], axis=0) + _dot(it["amat"], it["vnb"])
        for it in group:
            h = it["h"]
            kd_t = it["kd_t"]
            state_sc[h] = it["s0"] * it["eglast"][0:1, :] + _dot(
                jnp.where(top_lanes, kd_t, 0.0).astype(BF16), it["vnb"])
            state_sc[h + 1] = it["s1"] * it["eglast"][c:c + 1, :] + _dot(
                jnp.where(top_lanes, 0.0, kd_t).astype(BF16), it["vnb"])
        for it in group:
            h, rows, o = it["h"], it["rows"], it["o"]
            gate2 = stack(gate_ref, rows, h)
            o = o * lax.rsqrt(jnp.mean(o * o, axis=-1, keepdims=True) + DN_NORM_EPS) * nw_ref[...]
            o = o * (gate2 * _sigmoid(gate2))
            o_ref[rows, h * HEAD_DIM:(h + 1) * HEAD_DIM] = o[:c].astype(BF16)
            o_ref[rows, (h + 1) * HEAD_DIM:(h + 2) * HEAD_DIM] = o[c:].astype(BF16)
    _cast_riders(riders_in, riders_out)


def _delta_rule(q, k, v, gate, gb, norm_w_row, batch, seq, riders=()):
    t = q.shape[0]
    width = N_HEADS * HEAD_DIM
    rows = ROWS_DELTA
    ns = seq // rows
    row_map = lambda b, s: (b * ns + s, 0)
    const = lambda b, s: (0, 0)
    wide = pl.BlockSpec((rows, width), row_map)
    rider_specs, rider_shapes = _rider_specs(riders, batch * ns, row_map)
    return pl.pallas_call(
        functools.partial(_delta_kernel, len(riders)),
        grid=(batch, ns),
        in_specs=[wide, wide, wide, wide,
                  pl.BlockSpec((rows, LANES), row_map),
                  pl.BlockSpec((1, HEAD_DIM), const)] + rider_specs,
        out_specs=[wide] + rider_specs,
        out_shape=[jax.ShapeDtypeStruct((t, width), BF16)] + rider_shapes,
        scratch_shapes=[pltpu.VMEM((N_HEADS, HEAD_DIM, HEAD_DIM), F32)],
        compiler_params=_params("arbitrary", "arbitrary"),
        name="dn_delta_rule",
    )(q, k, v, gate, gb, norm_w_row, *riders)


def _rows_to_tiles(ref, base, y):
    for k in range(D_MODEL // LANES):
        ref[pl.ds(base + k, y.shape[0], stride=SUBLANES), :] = y[:, k * LANES:(k + 1) * LANES]


def _tiles_to_rows(ref, base, n):
    return jnp.concatenate([ref[pl.ds(base + k, n, stride=SUBLANES), :] for k in range(D_MODEL // LANES)],
                           axis=1)


def _proj_residual(a_ref, w_ref, res_ref):
    blocks = []
    for rb in range(a_ref.shape[0] // ROWS_LN_SUB):
        rows = slice(rb * ROWS_LN_SUB, (rb + 1) * ROWS_LN_SUB)
        blocks.append((rows, DEEPNORM_ALPHA * res_ref[rows, :] + _dot(a_ref[rows, :], w_ref[...])))
    return blocks


def _proj_ln_router_kernel(a_ref, w_ref, res_ref, g_ref, b_ref, wr_ref, ot_ref, route_ref):
    for rows, y in _proj_residual(a_ref, w_ref, res_ref):
        xn = _layer_norm(y, g_ref[...], b_ref[...])
        _rows_to_tiles(ot_ref, rows.start * SUBLANES, xn)
        logits = _dot_split(xn, wr_ref[...], _dot)
        lane = lax.broadcasted_iota(jnp.int32, logits.shape, 1)
        logits = jnp.where(lane < N_EXPERTS, logits, NEG_INF)
        l1 = jnp.max(logits, axis=-1, keepdims=True)
        i1 = jnp.min(jnp.where(logits == l1, lane, LANES), axis=-1, keepdims=True)
        rest = jnp.where(lane == i1, NEG_INF, logits)
        l2 = jnp.max(rest, axis=-1, keepdims=True)
        i2 = jnp.min(jnp.where(rest == l2, lane, LANES), axis=-1, keepdims=True)
        e2 = jnp.exp(l2 - l1)
        g1 = 1.0 / (1.0 + e2)
        g2 = e2 / (1.0 + e2)
        route_ref[rows, :] = jnp.where(lane == 0, g1,
                             jnp.where(lane == 1, g2,
                             jnp.where(lane == 2, i1.astype(F32),
                             jnp.where(lane == 3, i2.astype(F32), 0.0))))


def _proj_ln_router(a, w, res, g_row, b_row, w_router):
    t, kdim = a.shape
    rows = ROWS_LN
    row_map = lambda i: (i, 0)
    const = lambda i: (0, 0)
    in_specs = [pl.BlockSpec((rows, kdim), row_map),
                pl.BlockSpec((kdim, D_MODEL), const),
                pl.BlockSpec((rows, D_MODEL), row_map),
                pl.BlockSpec((1, D_MODEL), const),
                pl.BlockSpec((1, D_MODEL), const)]
    return pl.pallas_call(
        _proj_ln_router_kernel, grid=(t // rows,),
        in_specs=in_specs + [pl.BlockSpec((D_MODEL, LANES), const)],
        out_specs=[pl.BlockSpec((rows * SUBLANES, LANES), row_map),
                   pl.BlockSpec((rows, LANES), row_map)],
        out_shape=[jax.ShapeDtypeStruct((t * SUBLANES, LANES), F32),
                   jax.ShapeDtypeStruct((t, LANES), F32)],
        compiler_params=_params("arbitrary"), name="proj_ln_router",
    )(a, w, res, g_row, b_row, w_router)


def _swiglu_hidden(x, w1_ref, w3_ref, hid_sc):
    xb = x.astype(BF16)
    sub = 2 * LANES
    for c in range(hid_sc.shape[1] // sub):
        cols = slice(c * sub, (c + 1) * sub)
        h1 = _dot(xb, w1_ref[:, cols])
        h3 = _dot(xb, w3_ref[:, cols])
        hid_sc[:, cols] = (h1 * _sigmoid(h1) * h3).astype(BF16)


def _swiglu_part(x, w1_ref, w3_ref, w2_ref, hid_sc):
    _swiglu_hidden(x, w1_ref, w3_ref, hid_sc)
    return _dot(hid_sc[...], w2_ref[...])


def _mixer_ffn_kernel(a_ref, wo_ref, res_ref, g0_ref, b0_ref, w1_ref, w3_ref, w2_ref, g1_ref, b1_ref,
                      o_ref, x_sc, acc_sc, hid_sc):
    f = pl.program_id(1)
    last_f = pl.num_programs(1) - 1
    n_sub = a_ref.shape[0] // ROWS_LN_SUB

    @pl.when(f == 0)
    def _():
        for rows, y in _proj_residual(a_ref, wo_ref, res_ref):
            x_sc[rows, :] = _layer_norm(y, g0_ref[...], b0_ref[...])

    _swiglu_hidden(x_sc[...], w1_ref, w3_ref, hid_sc)

    @pl.when(f == 0)
    def _():
        acc_sc[...] = _dot(hid_sc[...], w2_ref[...])

    @pl.when((f > 0) & (f < last_f))
    def _():
        acc_sc[...] += _dot(hid_sc[...], w2_ref[...])

    @pl.when(f == last_f)
    def _():
        for rb in range(n_sub):
            rows = slice(rb * ROWS_LN_SUB, (rb + 1) * ROWS_LN_SUB)
            y = DEEPNORM_ALPHA * x_sc[rows, :] + (acc_sc[rows, :] + _dot(hid_sc[rows, :], w2_ref[...]))
            o_ref[rows, :] = _layer_norm(y, g1_ref[...], b1_ref[...])


def _mixer_ffn(a, wo, res, g0_row, b0_row, w1, w3, w2, g1_row, b1_row):
    t = res.shape[0]
    rows, fc = ROWS_FFN, FFN_CHUNK
    nf = FFN_HIDDEN // fc
    assert nf >= 2
    row_map = lambda i, f: (i, 0)
    const = lambda i, f: (0, 0)
    vec = pl.BlockSpec((1, D_MODEL), const)
    return pl.pallas_call(
        _mixer_ffn_kernel,
        grid=(t // rows, nf),
        in_specs=[pl.BlockSpec((rows, a.shape[1]), row_map),
                  pl.BlockSpec((a.shape[1], D_MODEL), const),
                  pl.BlockSpec((rows, D_MODEL), row_map), vec, vec,
                  pl.BlockSpec((D_MODEL, fc), lambda i, f: (0, f)),
                  pl.BlockSpec((D_MODEL, fc), lambda i, f: (0, f)),
                  pl.BlockSpec((fc, D_MODEL), lambda i, f: (f, 0)), vec, vec],
        out_specs=pl.BlockSpec((rows, D_MODEL), row_map),
        out_shape=jax.ShapeDtypeStruct((t, D_MODEL), F32),
        scratch_shapes=[pltpu.VMEM((rows, D_MODEL), F32), pltpu.VMEM((rows, D_MODEL), F32),
                        pltpu.VMEM((rows, fc), BF16)],
        compiler_params=_params("arbitrary", "arbitrary"),
        name="mixer_ffn_dense",
    )(a, wo, res, g0_row, b0_row, w1, w3, w2, g1_row, b1_row)


def _moe_experts_kernel(nu_ref, te_ref, idx_ref, idx_next_ref, dst_ref, x_hbm, w1_ref, w3_ref, w2_ref,
                        y_hbm, xbuf, ybuf, xb_sc, acc_sc, hid_sc, gsem, ssem):
    i = pl.program_id(0)
    f = pl.program_id(1)
    last_f = pl.num_programs(1) - 1
    rows = xb_sc.shape[0]
    tile_rows = rows * SUBLANES
    nu = nu_ref[0]
    slot = i % 2

    def gather_start(idx, s):
        for r in range(rows):
            src = x_hbm.at[pl.ds(pl.multiple_of(idx[0, r], SUBLANES), SUBLANES)]
            pltpu.make_async_copy(src, xbuf.at[pl.ds(s * tile_rows + r * SUBLANES, SUBLANES)], gsem.at[s]).start()

    def scatter_start(s):
        for r in range(rows):
            dst = y_hbm.at[pl.ds(pl.multiple_of(dst_ref[0, r], SUBLANES), SUBLANES)]
            pltpu.make_async_copy(ybuf.at[pl.ds(s * tile_rows + r * SUBLANES, SUBLANES)], dst, ssem.at[s]).start()

    def wait_all(buf, sem, s):
        view = buf.at[pl.ds(pl.multiple_of(s * tile_rows, tile_rows), tile_rows)]
        pltpu.make_async_copy(view, view, sem.at[s]).wait()

    def for_slot(cond, s, fn):
        for static_slot in range(2):
            pl.when(cond & (s == static_slot))(functools.partial(fn, static_slot))

    first = f == 0

    @pl.when(first & (i == 0))
    def _():
        n_real = TOP_K * x_hbm.shape[0]
        ybuf[tile_rows:2 * tile_rows, :] = jnp.zeros((tile_rows, LANES), F32)
        fills = [pltpu.make_async_copy(ybuf.at[pl.ds(tile_rows, tile_rows)],
                                       y_hbm.at[pl.ds(n_real + k * tile_rows, tile_rows)], ssem.at[1])
                 for k in range((y_hbm.shape[0] - n_real) // tile_rows)]
        for cp in fills:
            cp.start()
        for cp in fills:
            cp.wait()

    @pl.when(first & (i == 0) & (nu > 0))
    def _():
        gather_start(idx_ref, 0)

    @pl.when(first & (i < nu))
    def _():
        wait_all(xbuf, gsem, slot)

    for_slot(first & (i + 1 < nu), 1 - slot, functools.partial(gather_start, idx_next_ref))

    @pl.when(i < nu)
    def _():
        @pl.when(first)
        def _():
            base = pl.multiple_of(slot * tile_rows, tile_rows)
            for k in range(D_MODEL // LANES):
                xb_sc[:, k * LANES:(k + 1) * LANES] = xbuf[pl.ds(base + k, rows, stride=SUBLANES), :].astype(BF16)

        part = _swiglu_part(xb_sc[...], w1_ref, w3_ref, w2_ref, hid_sc)

        @pl.when(first)
        def _():
            acc_sc[...] = part

        @pl.when((f > 0) & (f < last_f))
        def _():
            acc_sc[...] += part

        @pl.when(f == last_f)
        def _():
            @pl.when(i >= 2)
            def _():
                wait_all(ybuf, ssem, slot)
            _rows_to_tiles(ybuf, pl.multiple_of(slot * tile_rows, tile_rows), acc_sc[...] + part)

        for_slot(f == last_f, slot, scatter_start)

    @pl.when((i == pl.num_programs(0) - 1) & (f == last_f))
    def _():
        @pl.when(nu >= 1)
        def _():
            wait_all(ybuf, ssem, (nu - 1) % 2)

        @pl.when(nu >= 2)
        def _():
            wait_all(ybuf, ssem, nu % 2)


def _moe_experts(n_used, tile_expert, row_token, row_dst, x_tiles, w1, w3, w2, n_out_rows):
    n_tiles = tile_expert.shape[0]
    rows, fc = ROWS_FFN, FFN_CHUNK
    nf = FFN_HIDDEN // fc
    idx3 = (row_token * SUBLANES).reshape(n_tiles, 1, rows)
    dst3 = (row_dst * SUBLANES).reshape(n_tiles, 1, rows)
    smem_tile = lambda fn: pl.BlockSpec((None, 1, rows), fn, memory_space=pltpu.SMEM)
    grid_spec = pltpu.PrefetchScalarGridSpec(
        num_scalar_prefetch=2,
        grid=(n_tiles, nf),
        in_specs=[smem_tile(lambda i, f, nu, te: (i, 0, 0)),
                  smem_tile(lambda i, f, nu, te: (jnp.minimum(i + 1, n_tiles - 1), 0, 0)),
                  smem_tile(lambda i, f, nu, te: (i, 0, 0)),
                  pl.BlockSpec(memory_space=pl.ANY),
                  pl.BlockSpec((None, D_MODEL, fc), lambda i, f, nu, te: (te[i], 0, f)),
                  pl.BlockSpec((None, D_MODEL, fc), lambda i, f, nu, te: (te[i], 0, f)),
                  pl.BlockSpec((None, fc, D_MODEL), lambda i, f, nu, te: (te[i], f, 0))],
        out_specs=pl.BlockSpec(memory_space=pl.ANY),
        scratch_shapes=[pltpu.VMEM((2 * rows * SUBLANES, LANES), F32), pltpu.VMEM((2 * rows * SUBLANES, LANES), F32),
                        pltpu.VMEM((rows, D_MODEL), BF16),
                        pltpu.VMEM((rows, D_MODEL), F32), pltpu.VMEM((rows, fc), BF16),
                        pltpu.SemaphoreType.DMA((2,)), pltpu.SemaphoreType.DMA((2,))],
    )
    return pl.pallas_call(
        _moe_experts_kernel,
        grid_spec=grid_spec,
        out_shape=jax.ShapeDtypeStruct((n_out_rows * SUBLANES, LANES), F32),
        compiler_params=_params("arbitrary", "arbitrary"),
        name="moe_experts",
    )(n_used, tile_expert, idx3, idx3, dst3, x_tiles, w1, w3, w2)


def _qkv_kernel(x_ref, wkv_ref, wq_ref, cos_ref, sin_ref, q_ref, k_ref, vt_ref, sel_ref, km_sc, qf_sc):
    i = pl.program_id(1)
    width = N_HEADS * HEAD_DIM
    half = HEAD_DIM // 2
    xb = x_ref[...].astype(BF16)
    cosf = cos_ref[...]
    sinf = sin_ref[...]

    @pl.when(i == 0)
    def _():
        km_sc[...] = jnp.zeros(km_sc.shape, F32)

    def rope(t):
        return t * cosf + pltpu.roll(t, half, axis=1) * sinf

    chunk = 2 * HEAD_DIM
    n_chunks = width // chunk
    nb = km_sc.shape[0]

    def q_chunk(c):
        qq = _dot(xb, wq_ref[:, c * chunk:(c + 1) * chunk])
        for hh in range(2):
            cols = slice((2 * c + hh) * HEAD_DIM, (2 * c + hh + 1) * HEAD_DIM)
            qr = rope(qq[:, hh * HEAD_DIM:(hh + 1) * HEAD_DIM])
            qf_sc[:, cols] = qr
            q_ref[:, cols] = (qr * (HEAD_DIM ** -0.5 * LOG2_E)).astype(BF16)

    def kv_chunk(c):
        kk = _dot(xb, wkv_ref[:, c * chunk:(c + 1) * chunk])
        vv = _dot(xb, wkv_ref[:, width + c * chunk:width + (c + 1) * chunk])
        for hh in range(2):
            h = 2 * c + hh
            cols = slice(h * HEAD_DIM, (h + 1) * HEAD_DIM)
            part = slice(hh * HEAD_DIM, (hh + 1) * HEAD_DIM)
            kr = rope(kk[:, part])
            k_ref[:, cols] = kr.astype(BF16)
            km_row = lax.broadcasted_iota(jnp.int32, (nb, HEAD_DIM), 0)
            km_sc[:, cols] = jnp.where(km_row == i, jnp.mean(kr, axis=0, keepdims=True), km_sc[:, cols])
            vt_ref[h] = vv[:, part].T.astype(BF16)

    def block_selection():
        km = km_sc[...]
        lane_head = lax.broadcasted_iota(jnp.int32, km.shape, 1) // HEAD_DIM
        km_rows = jnp.concatenate([jnp.where(lane_head == h, km, 0.0) for h in range(N_HEADS)], axis=0)
        gate_t = _dot_split(km_rows, qf_sc[...], _dot_nt)
        blk = lax.broadcasted_iota(jnp.int32, (nb, x_ref.shape[0]), 0)
        past = blk < i
        for h in range(N_HEADS):
            g = jnp.where(past, gate_t[h * nb:(h + 1) * nb, :], NEG_INF)
            rank = jnp.zeros(g.shape, jnp.int32)
            for n in range(nb):
                gn = g[n:n + 1, :]
                ahead = (gn > g) | ((gn == g) & (blk > n))
                rank = rank + ahead.astype(jnp.int32)
            sel_ref[h * nb:(h + 1) * nb, :] = jnp.where(past & (rank < MOBA_TOPK), 0.0, NEG_INF)

    for c in range(n_chunks):
        q_chunk(c)
    kv_chunk(0)
    block_selection()
    for c in range(1, n_chunks):
        kv_chunk(c)


def _qkv_proj(x, w_kv, w_q, cos_full, sin_signed, batch, seq):
    t = x.shape[0]
    width = N_HEADS * HEAD_DIM
    rows = MOBA_BLOCK
    nb = seq // rows
    row_map = lambda b, i: (b * nb + i, 0)
    act = jax.ShapeDtypeStruct((t, width), BF16)
    return pl.pallas_call(
        _qkv_kernel,
        grid=(batch, nb),
        in_specs=[pl.BlockSpec((rows, D_MODEL), row_map),
                  pl.BlockSpec((D_MODEL, 2 * width), lambda b, i: (0, 0)),
                  pl.BlockSpec((D_MODEL, width), lambda b, i: (0, 0)),
                  pl.BlockSpec((rows, HEAD_DIM), lambda b, i: (i, 0)),
                  pl.BlockSpec((rows, HEAD_DIM), lambda b, i: (i, 0))],
        out_specs=[pl.BlockSpec((rows, width), row_map),
                   pl.BlockSpec((rows, width), row_map),
                   pl.BlockSpec((None, N_HEADS, HEAD_DIM, rows), lambda b, i: (b, 0, 0, i)),
                   pl.BlockSpec((None, N_HEADS * nb, rows), lambda b, i: (b * nb + i, 0, 0))],
        out_shape=[act, act,
                   jax.ShapeDtypeStruct((batch, N_HEADS, HEAD_DIM, seq), BF16),
                   jax.ShapeDtypeStruct((batch * nb, N_HEADS * nb, rows), F32)],
        scratch_shapes=[pltpu.VMEM((nb, width), F32), pltpu.VMEM((rows, width), F32)],
        compiler_params=_params("arbitrary", "arbitrary"),
        name="moba_qkv_proj",
    )(x, w_kv, w_q, cos_full, sin_signed)


def _attn_kernel(q_ref, k_ref, vt_ref, sel_ref, o_ref, kaug_sc, vtaug_sc):
    blk = MOBA_BLOCK
    seq = q_ref.shape[0]
    nb = seq // blk
    extra = 2 * SUBLANES
    key = lax.broadcasted_iota(jnp.int32, (blk, blk), 0)
    qry = lax.broadcasted_iota(jnp.int32, (blk, blk), 1)
    causal = key <= qry

    key_blk = lax.broadcasted_iota(jnp.int32, (seq, LANES), 0) // blk
    kaug_sc[:, 0:HEAD_DIM] = k_ref[...]
    kaug_sc[:, HEAD_DIM:] = jnp.where(lax.broadcasted_iota(jnp.int32, (seq, LANES), 1) == key_blk,
                                      1.0, 0.0).astype(BF16)
    vtaug_sc[0:HEAD_DIM, :] = vt_ref[...]
    vtaug_sc[HEAD_DIM:, :] = jnp.where(lax.broadcasted_iota(jnp.int32, (extra, seq), 0) == 0,
                                       1.0, 0.0).astype(BF16)
    blk_row = lax.broadcasted_iota(jnp.int32, (nb, blk), 0)

    def scores(i):
        bias = jnp.where(blk_row == i, 0.0, sel_ref[i])
        bias_t = jnp.concatenate([bias, jnp.zeros((LANES - nb, blk), F32)], axis=0).T
        q_aug = jnp.concatenate([q_ref[i * blk:(i + 1) * blk, :], bias_t.astype(BF16)], axis=1)
        return _dot_nt(kaug_sc[0:(i + 1) * blk, :], q_aug)

    def finish(i, s):
        parts = [s[j * blk:(j + 1) * blk] for j in range(i)]
        parts.append(jnp.where(causal, s[i * blk:(i + 1) * blk], NEG_INF))
        m = jnp.max(parts[0], axis=0, keepdims=True)
        for part in parts[1:]:
            m = jnp.maximum(m, jnp.max(part, axis=0, keepdims=True))
        p = jnp.concatenate([jnp.exp2(part - m).astype(BF16) for part in parts], axis=0)
        acc = _dot(vtaug_sc[:, 0:(i + 1) * blk], p)
        o_ref[i * blk:(i + 1) * blk, :] = (acc[0:HEAD_DIM] / acc[HEAD_DIM:HEAD_DIM + 1]).T.astype(BF16)

    ahead = 3
    pending = [scores(i) for i in range(min(ahead, nb))]
    for i in range(nb):
        if i + ahead < nb:
            pending.append(scores(i + ahead))
        finish(i, pending.pop(0))


def _moba_attention(q, k, vt, sel, batch, seq):
    t = q.shape[0]
    width = N_HEADS * HEAD_DIM
    nb = seq // MOBA_BLOCK
    col = pl.BlockSpec((seq, HEAD_DIM), lambda b, h: (b, h))
    return pl.pallas_call(
        _attn_kernel,
        grid=(batch, N_HEADS),
        in_specs=[col, col,
                  pl.BlockSpec((None, None, HEAD_DIM, seq), lambda b, h: (b, h, 0, 0)),
                  pl.BlockSpec((nb, nb, MOBA_BLOCK), lambda b, h: (b, h, 0))],
        out_specs=col,
        out_shape=jax.ShapeDtypeStruct((t, width), BF16),
        scratch_shapes=[pltpu.VMEM((seq, HEAD_DIM + LANES), BF16),
                        pltpu.VMEM((HEAD_DIM + 2 * SUBLANES, seq), BF16)],
        compiler_params=_params("arbitrary", "arbitrary"),
        name="moba_attention",
    )(q, k, vt, sel)


def _combine_kernel(y0_ref, y1_ref, x_ref, route_ref, g_ref, b_ref, o_ref):
    route = route_ref[...]
    rows = o_ref.shape[0]
    y = (DEEPNORM_ALPHA * _tiles_to_rows(x_ref, 0, rows) + route[:, 0:1] * _tiles_to_rows(y0_ref, 0, rows)
         + route[:, 1:2] * _tiles_to_rows(y1_ref, 0, rows))
    o_ref[...] = _layer_norm(y, g_ref[...], b_ref[...])


def _moe_combine(ys, x_tiles, route, g_row, b_row):
    t = x_tiles.shape[0] // SUBLANES
    rows = ROWS_COMBINE
    nt = t // rows
    row_map = lambda i: (i, 0)
    const = lambda i: (0, 0)
    return pl.pallas_call(
        _combine_kernel,
        grid=(nt,),
        in_specs=[pl.BlockSpec((rows * SUBLANES, LANES), row_map),
                  pl.BlockSpec((rows * SUBLANES, LANES), lambda i: (nt + i, 0)),
                  pl.BlockSpec((rows * SUBLANES, LANES), row_map),
                  pl.BlockSpec((rows, LANES), row_map),
                  pl.BlockSpec((1, D_MODEL), const),
                  pl.BlockSpec((1, D_MODEL), const)],
        out_specs=pl.BlockSpec((rows, D_MODEL), row_map),
        out_shape=jax.ShapeDtypeStruct((t, D_MODEL), F32),
        compiler_params=_params("arbitrary"),
        name="moe_combine",
    )(ys, ys, x_tiles, route, g_row, b_row)


def _routing_tables(route, n_tokens):
    rows = ROWS_FFN
    n_assign = n_tokens * TOP_K
    n_tiles = n_assign // rows + N_EXPERTS
    experts = route[:, 2:4].astype(jnp.int32).reshape(-1)
    onehot = (experts[:, None] == jnp.arange(N_EXPERTS, dtype=jnp.int32)[None, :]).astype(jnp.int32)
    counts = jnp.sum(onehot, axis=0)
    tiles_per = (counts + rows - 1) // rows
    tile_end = jnp.cumsum(tiles_per)
    id_bits = max(n_assign, rows).bit_length()
    real_keys = (experts << (id_bits + 1)) | jnp.arange(n_assign, dtype=jnp.int32)
    pad_id = jnp.arange(rows, dtype=jnp.int32)[None, :]
    pad_needed = pad_id < (tiles_per * rows - counts)[:, None]
    pad_keys = (jnp.arange(N_EXPERTS, dtype=jnp.int32)[:, None] << (id_bits + 1)) | (1 << id_bits) | pad_id
    pad_keys = jnp.where(pad_needed, pad_keys, jnp.iinfo(jnp.int32).max)
    keys = jnp.sort(jnp.concatenate([real_keys, pad_keys.reshape(-1)]))
    real = ((keys >> id_bits) & 1) == 0
    assign = jnp.where(real, keys & ((1 << id_bits) - 1), -1)
    row_token = jnp.where(real, assign // TOP_K, 0)
    pad_rank = jnp.minimum(jnp.cumsum(1 - real.astype(jnp.int32)) - 1, N_EXPERTS * rows - 1)
    row_dst = jnp.where(real, (assign % TOP_K) * n_tokens + assign // TOP_K, n_assign + pad_rank)
    tile_id = jnp.arange(n_tiles, dtype=jnp.int32)
    n_used = tile_end[-1]
    tile_expert = jnp.sum((tile_id[:, None] >= tile_end[None, :]).astype(jnp.int32), axis=1)
    last_expert = jnp.sum((n_used - 1 >= tile_end).astype(jnp.int32))
    tile_expert = jnp.minimum(jnp.where(tile_id < n_used, tile_expert, last_expert), N_EXPERTS - 1)
    return n_used.reshape(1), tile_expert, row_token, row_dst, n_assign + N_EXPERTS * rows


def _rope_tables(seq):
    half = HEAD_DIM // 2
    inv_freq = ROPE_THETA ** (-jnp.arange(half, dtype=F32) / half)
    ang = jnp.arange(seq).astype(F32)[:, None] * inv_freq[None, :]
    cos, sin = jnp.cos(ang), jnp.sin(ang)
    return jnp.concatenate([cos, cos], axis=1), jnp.concatenate([-sin, sin], axis=1)


def _pad_lanes(w):
    return jnp.pad(w, ((0, 0), (0, LANES - w.shape[1])))


def kernel(x, a_w_in, a_conv_w, a_log_decay, a_dt_bias, a_norm_w, a_w_out, b_w_kv, b_w_q, b_w_o,
           ffn_w1, ffn_w3, ffn_w2, moe_router, moe_w1, moe_w3, moe_w2, ln_g, ln_b):
    batch, seq, _ = x.shape
    t = batch * seq
    width = N_HEADS * HEAD_DIM
    x0 = x.reshape(t, D_MODEL)
    row = lambda v: v.reshape(1, -1).astype(F32)

    w_in = a_w_in[0]
    q, k, v, gate, gb, moe_w1b, moe_w2b, w_outb, w_ob, w_kvb, w_qb = _in_proj(
        x0, w_in[:, :4 * width].astype(BF16), _pad_lanes(w_in[:, 4 * width:]).astype(BF16),
        a_conv_w[0], _pad_lanes(row(a_log_decay[0])), _pad_lanes(row(a_dt_bias[0])), batch, seq,
        riders=(moe_w1[0].reshape(N_EXPERTS * D_MODEL, FFN_HIDDEN),
                moe_w2[0].reshape(N_EXPERTS * FFN_HIDDEN, D_MODEL),
                a_w_out[0], b_w_o[0], b_w_kv, b_w_q[0]))
    og, moe_w3b, ffn_w1b, ffn_w3b = _delta_rule(
        q, k, v, gate, gb, row(a_norm_w[0]), batch, seq,
        riders=(moe_w3[0].reshape(N_EXPERTS * D_MODEL, FFN_HIDDEN), ffn_w1[0], ffn_w3[0]))
    x2 = _mixer_ffn(og, w_outb, x0, row(ln_g[0, 0]), row(ln_b[0, 0]),
                    ffn_w1b, ffn_w3b, ffn_w2[0].astype(BF16), row(ln_g[0, 1]), row(ln_b[0, 1]))

    cos_full, sin_signed = _rope_tables(seq)
    qm, km, vt, sel = _qkv_proj(x2, w_kvb, w_qb, cos_full, sin_signed, batch, seq)
    attn = _moba_attention(qm, km, vt, sel, batch, seq)
    x3_tiles, route = _proj_ln_router(attn, w_ob, x2, row(ln_g[1, 0]), row(ln_b[1, 0]),
                                      _pad_lanes(moe_router[0]))
    n_used, tile_expert, row_token, row_dst, n_out_rows = _routing_tables(route, t)
    ys = _moe_experts(n_used, tile_expert, row_token, row_dst, x3_tiles,
                      moe_w1b.reshape(N_EXPERTS, D_MODEL, FFN_HIDDEN),
                      moe_w3b.reshape(N_EXPERTS, D_MODEL, FFN_HIDDEN),
                      moe_w2b.reshape(N_EXPERTS, FFN_HIDDEN, D_MODEL), n_out_rows)
    x4 = _moe_combine(ys, x3_tiles, route, row(ln_g[1, 1]), row(ln_b[1, 1]))
    return x4.reshape(batch, seq, D_MODEL)
```

```python
import functools
import math

import jax
import jax.numpy as jnp
from jax import lax
from jax.experimental import pallas as pl
from jax.experimental.pallas import tpu as pltpu

D_MODEL = 1024
DEPTH = 2
DEEPNORM_ALPHA = (2.0 * DEPTH) ** 0.25
LN_EPS = 1e-5
N_HEADS = 8
HEAD_DIM = 128
DN_CONV = 4
DN_CHUNK = 64
DN_NORM_EPS = 1e-6
MOBA_BLOCK = 256
MOBA_TOPK = 3
ROPE_THETA = 10000.0
NEG_INF = -1e30
LOG2_E = math.log2(math.e)
FFN_HIDDEN = 3584
N_EXPERTS = 8
TOP_K = 2

LANES = 128
SUBLANES = 8
VMEM_LIMIT_BYTES = 56 * 1024 * 1024

ROWS_PROJ = 256
ROWS_DELTA = 256
ROWS_LN = 1024
ROWS_LN_SUB = 256
ROWS_FFN = 512
FFN_CHUNK = 1792
ROWS_COMBINE = 1024

F32 = jnp.float32
BF16 = jnp.bfloat16

_NT = (((1,), (1,)), ((), ()))


def _dot(a, b):
    return jnp.dot(a, b, preferred_element_type=F32)


def _dot_nt(a, b):
    return lax.dot_general(a, b, _NT, preferred_element_type=F32)


def _params(*sem):
    return pltpu.CompilerParams(dimension_semantics=sem, vmem_limit_bytes=VMEM_LIMIT_BYTES)


def _layer_norm(y, g, b):
    mu = jnp.mean(y, axis=-1, keepdims=True)
    d = y - mu
    var = jnp.mean(d * d, axis=-1, keepdims=True)
    return d * lax.rsqrt(var + LN_EPS) * g + b


def _sigmoid(x):
    return 1.0 / (1.0 + jnp.exp(-x))


def _split2(x):
    hi = x.astype(BF16)
    lo = (x - hi.astype(F32)).astype(BF16)
    return hi, lo


def _dot_split(a, b, dot):
    a_hi, a_lo = _split2(a)
    b_hi, b_lo = _split2(b)
    return dot(a_hi, b_hi) + dot(a_hi, b_lo) + dot(a_lo, b_hi)


def _rider_specs(arrays, n_steps, index_map):
    assert all(a.shape[0] % (n_steps * 2 * SUBLANES) == 0 for a in arrays)
    specs = [pl.BlockSpec((a.shape[0] // n_steps, a.shape[1]), index_map) for a in arrays]
    return specs, [jax.ShapeDtypeStruct(a.shape, BF16) for a in arrays]


def _cast_riders(src_refs, dst_refs):
    for src, dst in zip(src_refs, dst_refs):
        dst[...] = src[...].astype(BF16)


def _in_proj_kernel(n_riders, x_ref, w_ref, wab_ref, cw_ref, alog_ref, dtb_ref, *rest):
    riders_in, rest = rest[:n_riders], rest[n_riders:]
    q_ref, k_ref, v_ref, gate_ref, gb_ref = rest[:5]
    riders_out, conv_sc = rest[5:5 + n_riders], rest[5 + n_riders]
    rows = x_ref.shape[0]
    width = N_HEADS * HEAD_DIM
    hist = SUBLANES

    @pl.when(pl.program_id(1) == 0)
    def _():
        conv_sc[...] = jnp.zeros(conv_sc.shape, F32)

    xb = x_ref[...].astype(BF16)
    chunk = 2 * HEAD_DIM
    for c in range(3 * width // chunk):
        proj = _dot(xb, w_ref[:, c * chunk:(c + 1) * chunk])
        which = c * chunk // width
        for hh in range(chunk // HEAD_DIM):
            slab = c * (chunk // HEAD_DIM) + hh
            cols = slice(slab * HEAD_DIM, (slab + 1) * HEAD_DIM)
            cur = proj[:, hh * HEAD_DIM:(hh + 1) * HEAD_DIM]
            conv_sc[slab, pl.ds(2 * hist, rows, stride=2), :] = cur
            y = cur * cw_ref[DN_CONV - 1:DN_CONV, cols]
            for j in range(DN_CONV - 1):
                shifted = conv_sc[slab, pl.ds(2 * (hist - (DN_CONV - 1) + j), rows, stride=2), :]
                y = y + shifted * cw_ref[j:j + 1, cols]
            conv_sc[slab, 0:2 * hist, :] = conv_sc[slab, 2 * rows:2 * (rows + hist), :]
            t = y * _sigmoid(y)
            ocols = slice(slab * HEAD_DIM - which * width, (slab + 1) * HEAD_DIM - which * width)
            if which == 2:
                v_ref[:, ocols] = t.astype(BF16)
            else:
                inv_norm = lax.rsqrt(jnp.sum(t * t, axis=-1, keepdims=True) + DN_NORM_EPS)
                if which == 0:
                    q_ref[:, ocols] = (t * (inv_norm * (HEAD_DIM ** -0.5))).astype(BF16)
                else:
                    k_ref[:, ocols] = (t * inv_norm).astype(BF16)
        if c % 3 == 2:
            g0 = (c // 3) * chunk
            gate_ref[:, g0:g0 + chunk] = _dot(xb, w_ref[:, 3 * width + g0:3 * width + g0 + chunk]).astype(BF16)
    ab = _dot(xb, wab_ref[...])
    z = ab + dtb_ref[...]
    softplus = jnp.maximum(z, 0.0) + jnp.log(1.0 + jnp.exp(-jnp.abs(z)))
    g = -jnp.exp(alog_ref[...]) * softplus
    lane = lax.broadcasted_iota(jnp.int32, ab.shape, 1)
    gb_ref[...] = jnp.where(lane < N_HEADS, g, _sigmoid(ab))
    _cast_riders(riders_in, riders_out)


def _in_proj(x2d, w_main, w_ab, conv_w, a_log_row, dt_row, batch, seq, riders=()):
    t = x2d.shape[0]
    width = N_HEADS * HEAD_DIM
    rows = ROWS_PROJ
    nt = seq // rows
    row_map = lambda b, s: (b * nt + s, 0)
    const = lambda b, s: (0, 0)
    act = jax.ShapeDtypeStruct((t, width), BF16)
    rider_specs, rider_shapes = _rider_specs(riders, batch * nt, row_map)
    return pl.pallas_call(
        functools.partial(_in_proj_kernel, len(riders)),
        grid=(batch, nt),
        in_specs=[pl.BlockSpec((rows, D_MODEL), row_map),
                  pl.BlockSpec((D_MODEL, 4 * width), const),
                  pl.BlockSpec((D_MODEL, LANES), const),
                  pl.BlockSpec((DN_CONV, 3 * width), const),
                  pl.BlockSpec((1, LANES), const),
                  pl.BlockSpec((1, LANES), const)] + rider_specs,
        out_specs=[pl.BlockSpec((rows, width), row_map)] * 4 + [pl.BlockSpec((rows, LANES), row_map)]
        + rider_specs,
        out_shape=[act, act, act, act, jax.ShapeDtypeStruct((t, LANES), F32)] + rider_shapes,
        scratch_shapes=[pltpu.VMEM((3 * width // LANES, 2 * (rows + SUBLANES), LANES), F32)],
        compiler_params=_params("arbitrary", "arbitrary"),
        name="dn_in_proj",
    )(x2d, w_main, w_ab, conv_w, a_log_row, dt_row, *riders)


def _delta_kernel(n_riders, q_ref, k_ref, v_ref, gate_ref, gb_ref, nw_ref, *rest):
    riders_in, o_ref = rest[:n_riders], rest[n_riders]
    riders_out, state_sc = rest[n_riders + 1:2 * n_riders + 1], rest[2 * n_riders + 1]
    c = DN_CHUNK
    two = 2 * c
    n_sub = q_ref.shape[0] // c
    n_pairs = N_HEADS // 2

    @pl.when(pl.program_id(1) == 0)
    def _():
        state_sc[...] = jnp.zeros(state_sc.shape, F32)

    r = lax.broadcasted_iota(jnp.int32, (two, two), 0)
    cc = lax.broadcasted_iota(jnp.int32, (two, two), 1)
    same = (r < c) == (cc < c)
    tri = same & (r >= cc)
    stri = same & (r > cc)
    eye = (r == cc).astype(F32)
    top_rows = lax.broadcasted_iota(jnp.int32, (two, 1), 0) < c
    top_lanes = lax.broadcasted_iota(jnp.int32, (1, two), 1) < c
    row64 = lax.broadcasted_iota(jnp.int32, (c, LANES), 0)

    def stack(ref, rows, h):
        t = ref[rows, h * HEAD_DIM:(h + 2) * HEAD_DIM].astype(F32)
        return jnp.concatenate([t[:, :HEAD_DIM], t[:, HEAD_DIM:]], axis=0)

    items = []
    for s in range(n_sub):
        rows = slice(s * c, (s + 1) * c)
        gb = gb_ref[rows, :]
        gc = gb
        sh = 1
        while sh < c:
            gc = gc + jnp.where(row64 >= sh, pltpu.roll(gc, sh, axis=0), 0.0)
            sh *= 2
        gc_pair = jnp.concatenate([gc, pltpu.roll(gc, LANES - 1, axis=1)], axis=0)
        gb_pair = jnp.concatenate([gb, pltpu.roll(gb, LANES - 1, axis=1)], axis=0)
        gc_pair_t = gc_pair.T
        for p in range(n_pairs):
            h = 2 * p
            q2, k2, v2 = stack(q_ref, rows, h), stack(k_ref, rows, h), stack(v_ref, rows, h)
            gcol = gc_pair[:, h:h + 1]
            grow = gc_pair_t[h:h + 1, :]
            beta = gb_pair[:, N_HEADS + h:N_HEADS + h + 1]
            glast = jnp.where(top_rows, gc_pair[c - 1:c, h:h + 1], gc_pair[two - 1:two, h:h + 1])
            eg = jnp.exp(gcol)
            kb = k2 * beta
            items.append(dict(
                rows=rows, h=h, k2b=k2.astype(BF16), q2b=q2.astype(BF16), kbb=kb.astype(BF16),
                decay=jnp.exp(jnp.where(tri, gcol - grow, NEG_INF)),
                rhs=jnp.concatenate([v2 * beta, kb * eg], axis=1).astype(BF16),
                qd=(q2 * eg).astype(BF16),
                kd_t=(k2 * jnp.exp(glast - gcol)).T,
                eglast=jnp.exp(glast)))
    for it in items:
        it["pw"] = jnp.where(stri, _dot_nt(it["kbb"], it["k2b"]) * it["decay"], 0.0)
        it["inv"] = eye - it["pw"]
    for _ in range(int(math.log2(c)) - 1):
        for it in items:
            pwb = it["pw"].astype(BF16)
            it["pw"] = _dot(pwb, pwb)
        for it in items:
            it["inv"] = it["inv"] + _dot(it["inv"].astype(BF16), it["pw"].astype(BF16))
    for it in items:
        sol = _dot(it["inv"].astype(BF16), it["rhs"])
        it["u"] = sol[:, :HEAD_DIM]
        it["wb"] = sol[:, HEAD_DIM:].astype(BF16)
    for it in items:
        it["amat"] = jnp.where(tri, _dot_nt(it["q2b"], it["k2b"]) * it["decay"], 0.0).astype(BF16)

    for s in range(n_sub):
        group = items[s * n_pairs:(s + 1) * n_pairs]
        for it in group:
            h = it["h"]
            it["s0"] = state_sc[h]
            it["s1"] = state_sc[h + 1]
            it["r0"] = _dot(jnp.concatenate([it["wb"][:c], it["qd"][:c]], axis=0), it["s0"].astype(BF16))
            it["r1"] = _dot(jnp.concatenate([it["wb"][c:], it["qd"][c:]], axis=0), it["s1"].astype(BF16))
        for it in group:
            v_new = it["u"] - jnp.concatenate([it["r0"][:c], it["r1"][:c]], axis=0)
            it["vnb"] = v_new.astype(BF16)
            it["o"] = jnp.concatenate([it["r0"][c:], it["r1"][c:]], axis=0) + _dot(it["amat"], it["vnb"])
        for it in group:
            h = it["h"]
            kd_t = it["kd_t"]
            state_sc[h] = it["s0"] * it["eglast"][0:1, :] + _dot(
                jnp.where(top_lanes, kd_t, 0.0).astype(BF16), it["vnb"])
            state_sc[h + 1] = it["s1"] * it["eglast"][c:c + 1, :] + _dot(
                jnp.where(top_lanes, 0.0, kd_t).astype(BF16), it["vnb"])
        for it in group:
            h, rows, o = it["h"], it["rows"], it["o"]
            gate2 = stack(gate_ref, rows, h)
            o = o * lax.rsqrt(jnp.mean(o * o, axis=-1, keepdims=True) + DN_NORM_EPS) * nw_ref[...]
            o = o * (gate2 * _sigmoid(gate2))
            o_ref[rows, h * HEAD_DIM:(h + 1) * HEAD_DIM] = o[:c].astype(BF16)
            o_ref[rows, (h + 1) * HEAD_DIM:(h + 2) * HEAD_DIM] = o[c:].astype(BF16)
    _cast_riders(riders_in, riders_out)


def _delta_rule(q, k, v, gate, gb, norm_w_row, batch, seq, riders=()):
    t = q.shape[0]
    width = N_HEADS * HEAD_DIM
    rows = ROWS_DELTA
    ns = seq // rows
    row_map = lambda b, s: (b * ns + s, 0)
    const = lambda b, s: (0, 0)
    wide = pl.BlockSpec((rows, width), row_map)
    rider_specs, rider_shapes = _rider_specs(riders, batch * ns, row_map)
    return pl.pallas_call(
        functools.partial(_delta_kernel, len(riders)),
        grid=(batch, ns),
        in_specs=[wide, wide, wide, wide,
                  pl.BlockSpec((rows, LANES), row_map),
                  pl.BlockSpec((1, HEAD_DIM), const)] + rider_specs,
        out_specs=[wide] + rider_specs,
        out_shape=[jax.ShapeDtypeStruct((t, width), BF16)] + rider_shapes,
        scratch_shapes=[pltpu.VMEM((N_HEADS, HEAD_DIM, HEAD_DIM), F32)],
        compiler_params=_params("arbitrary", "arbitrary"),
        name="dn_delta_rule",
    )(q, k, v, gate, gb, norm_w_row, *riders)


def _rows_to_tiles(ref, base, y):
    for k in range(D_MODEL // LANES):
        ref[pl.ds(base + k, y.shape[0], stride=SUBLANES), :] = y[:, k * LANES:(k + 1) * LANES]


def _tiles_to_rows(ref, base, n):
    return jnp.concatenate([ref[pl.ds(base + k, n, stride=SUBLANES), :] for k in range(D_MODEL // LANES)],
                           axis=1)


def _proj_residual(a_ref, w_ref, res_ref):
    blocks = []
    for rb in range(a_ref.shape[0] // ROWS_LN_SUB):
        rows = slice(rb * ROWS_LN_SUB, (rb + 1) * ROWS_LN_SUB)
        blocks.append((rows, DEEPNORM_ALPHA * res_ref[rows, :] + _dot(a_ref[rows, :], w_ref[...])))
    return blocks


def _proj_ln_router_kernel(a_ref, w_ref, res_ref, g_ref, b_ref, wr_ref, ot_ref, route_ref):
    for rows, y in _proj_residual(a_ref, w_ref, res_ref):
        xn = _layer_norm(y, g_ref[...], b_ref[...])
        _rows_to_tiles(ot_ref, rows.start * SUBLANES, xn)
        logits = _dot_split(xn, wr_ref[...], _dot)
        lane = lax.broadcasted_iota(jnp.int32, logits.shape, 1)
        logits = jnp.where(lane < N_EXPERTS, logits, NEG_INF)
        l1 = jnp.max(logits, axis=-1, keepdims=True)
        i1 = jnp.min(jnp.where(logits == l1, lane, LANES), axis=-1, keepdims=True)
        rest = jnp.where(lane == i1, NEG_INF, logits)
        l2 = jnp.max(rest, axis=-1, keepdims=True)
        i2 = jnp.min(jnp.where(rest == l2, lane, LANES), axis=-1, keepdims=True)
        e2 = jnp.exp(l2 - l1)
        g1 = 1.0 / (1.0 + e2)
        g2 = e2 / (1.0 + e2)
        route_ref[rows, :] = jnp.where(lane == 0, g1,
                             jnp.where(lane == 1, g2,
                             jnp.where(lane == 2, i1.astype(F32),
                             jnp.where(lane == 3, i2.astype(F32), 0.0))))


def _proj_ln_router(a, w, res, g_row, b_row, w_router):
    t, kdim = a.shape
    rows = ROWS_LN
    row_map = lambda i: (i, 0)
    const = lambda i: (0, 0)
    in_specs = [pl.BlockSpec((rows, kdim), row_map),
                pl.BlockSpec((kdim, D_MODEL), const),
                pl.BlockSpec((rows, D_MODEL), row_map),
                pl.BlockSpec((1, D_MODEL), const),
                pl.BlockSpec((1, D_MODEL), const)]
    return pl.pallas_call(
        _proj_ln_router_kernel, grid=(t // rows,),
        in_specs=in_specs + [pl.BlockSpec((D_MODEL, LANES), const)],
        out_specs=[pl.BlockSpec((rows * SUBLANES, LANES), row_map),
                   pl.BlockSpec((rows, LANES), row_map)],
        out_shape=[jax.ShapeDtypeStruct((t * SUBLANES, LANES), F32),
                   jax.ShapeDtypeStruct((t, LANES), F32)],
        compiler_params=_params("arbitrary"), name="proj_ln_router",
    )(a, w, res, g_row, b_row, w_router)


def _swiglu_hidden(x, w1_ref, w3_ref, hid_sc):
    xb = x.astype(BF16)
    sub = 2 * LANES
    for c in range(hid_sc.shape[1] // sub):
        cols = slice(c * sub, (c + 1) * sub)
        h1 = _dot(xb, w1_ref[:, cols])
        h3 = _dot(xb, w3_ref[:, cols])
        hid_sc[:, cols] = (h1 * _sigmoid(h1) * h3).astype(BF16)


def _swiglu_part(x, w1_ref, w3_ref, w2_ref, hid_sc):
    _swiglu_hidden(x, w1_ref, w3_ref, hid_sc)
    return _dot(hid_sc[...], w2_ref[...])


def _mixer_ffn_kernel(a_ref, wo_ref, res_ref, g0_ref, b0_ref, w1_ref, w3_ref, w2_ref, g1_ref, b1_ref,
                      o_ref, x_sc, acc_sc, hid_sc):
    f = pl.program_id(1)
    last_f = pl.num_programs(1) - 1
    n_sub = a_ref.shape[0] // ROWS_LN_SUB

    @pl.when(f == 0)
    def _():
        for rows, y in _proj_residual(a_ref, wo_ref, res_ref):
            x_sc[rows, :] = _layer_norm(y, g0_ref[...], b0_ref[...])

    _swiglu_hidden(x_sc[...], w1_ref, w3_ref, hid_sc)

    @pl.when(f == 0)
    def _():
        acc_sc[...] = _dot(hid_sc[...], w2_ref[...])

    @pl.when((f > 0) & (f < last_f))
    def _():
        acc_sc[...] += _dot(hid_sc[...], w2_ref[...])

    @pl.when(f == last_f)
    def _():
        for rb in range(n_sub):
            rows = slice(rb * ROWS_LN_SUB, (rb + 1) * ROWS_LN_SUB)
            y = DEEPNORM_ALPHA * x_sc[rows, :] + (acc_sc[rows, :] + _dot(hid_sc[rows, :], w2_ref[...]))
            o_ref[rows, :] = _layer_norm(y, g1_ref[...], b1_ref[...])


def _mixer_ffn(a, wo, res, g0_row, b0_row, w1, w3, w2, g1_row, b1_row):
    t = res.shape[0]
    rows, fc = ROWS_FFN, FFN_CHUNK
    nf = FFN_HIDDEN // fc
    assert nf >= 2
    row_map = lambda i, f: (i, 0)
    const = lambda i, f: (0, 0)
    vec = pl.BlockSpec((1, D_MODEL), const)
    return pl.pallas_call(
        _mixer_ffn_kernel,
        grid=(t // rows, nf),
        in_specs=[pl.BlockSpec((rows, a.shape[1]), row_map),
                  pl.BlockSpec((a.shape[1], D_MODEL), const),
                  pl.BlockSpec((rows, D_MODEL), row_map), vec, vec,
                  pl.BlockSpec((D_MODEL, fc), lambda i, f: (0, f)),
                  pl.BlockSpec((D_MODEL, fc), lambda i, f: (0, f)),
                  pl.BlockSpec((fc, D_MODEL), lambda i, f: (f, 0)), vec, vec],
        out_specs=pl.BlockSpec((rows, D_MODEL), row_map),
        out_shape=jax.ShapeDtypeStruct((t, D_MODEL), F32),
        scratch_shapes=[pltpu.VMEM((rows, D_MODEL), F32), pltpu.VMEM((rows, D_MODEL), F32),
                        pltpu.VMEM((rows, fc), BF16)],
        compiler_params=_params("arbitrary", "arbitrary"),
        name="mixer_ffn_dense",
    )(a, wo, res, g0_row, b0_row, w1, w3, w2, g1_row, b1_row)


def _moe_experts_kernel(nu_ref, te_ref, idx_ref, idx_next_ref, dst_ref, x_hbm, w1_ref, w3_ref, w2_ref,
                        y_hbm, xbuf, ybuf, xb_sc, acc_sc, hid_sc, gsem, ssem):
    i = pl.program_id(0)
    f = pl.program_id(1)
    last_f = pl.num_programs(1) - 1
    rows = xb_sc.shape[0]
    tile_rows = rows * SUBLANES
    nu = nu_ref[0]
    slot = i % 2

    def gather_start(idx, s):
        for r in range(rows):
            src = x_hbm.at[pl.ds(pl.multiple_of(idx[0, r], SUBLANES), SUBLANES)]
            pltpu.make_async_copy(src, xbuf.at[pl.ds(s * tile_rows + r * SUBLANES, SUBLANES)], gsem.at[s]).start()

    def scatter_start(s):
        for r in range(rows):
            dst = y_hbm.at[pl.ds(pl.multiple_of(dst_ref[0, r], SUBLANES), SUBLANES)]
            pltpu.make_async_copy(ybuf.at[pl.ds(s * tile_rows + r * SUBLANES, SUBLANES)], dst, ssem.at[s]).start()

    def wait_all(buf, sem, s):
        view = buf.at[pl.ds(pl.multiple_of(s * tile_rows, tile_rows), tile_rows)]
        pltpu.make_async_copy(view, view, sem.at[s]).wait()

    def for_slot(cond, s, fn):
        for static_slot in range(2):
            pl.when(cond & (s == static_slot))(functools.partial(fn, static_slot))

    first = f == 0

    @pl.when(first & (i == 0))
    def _():
        n_real = TOP_K * x_hbm.shape[0]
        ybuf[tile_rows:2 * tile_rows, :] = jnp.zeros((tile_rows, LANES), F32)
        fills = [pltpu.make_async_copy(ybuf.at[pl.ds(tile_rows, tile_rows)],
                                       y_hbm.at[pl.ds(n_real + k * tile_rows, tile_rows)], ssem.at[1])
                 for k in range((y_hbm.shape[0] - n_real) // tile_rows)]
        for cp in fills:
            cp.start()
        for cp in fills:
            cp.wait()

    @pl.when(first & (i == 0) & (nu > 0))
    def _():
        gather_start(idx_ref, 0)

    @pl.when(first & (i < nu))
    def _():
        wait_all(xbuf, gsem, slot)

    for_slot(first & (i + 1 < nu), 1 - slot, functools.partial(gather_start, idx_next_ref))

    @pl.when(i < nu)
    def _():
        @pl.when(first)
        def _():
            base = pl.multiple_of(slot * tile_rows, tile_rows)
            for k in range(D_MODEL // LANES):
                xb_sc[:, k * LANES:(k + 1) * LANES] = xbuf[pl.ds(base + k, rows, stride=SUBLANES), :].astype(BF16)

        part = _swiglu_part(xb_sc[...], w1_ref, w3_ref, w2_ref, hid_sc)

        @pl.when(first)
        def _():
            acc_sc[...] = part

        @pl.when((f > 0) & (f < last_f))
        def _():
            acc_sc[...] += part

        @pl.when(f == last_f)
        def _():
            @pl.when(i >= 2)
            def _():
                wait_all(ybuf, ssem, slot)
            _rows_to_tiles(ybuf, pl.multiple_of(slot * tile_rows, tile_rows), acc_sc[...] + part)

        for_slot(f == last_f, slot, scatter_start)

    @pl.when((i == pl.num_programs(0) - 1) & (f == last_f))
    def _():
        @pl.when(nu >= 1)
        def _():
            wait_all(ybuf, ssem, (nu - 1) % 2)

        @pl.when(nu >= 2)
        def _():
            wait_all(ybuf, ssem, nu % 2)


def _moe_experts(n_used, tile_expert, row_token, row_dst, x_tiles, w1, w3, w2, n_out_rows):
    n_tiles = tile_expert.shape[0]
    rows, fc = ROWS_FFN, FFN_CHUNK
    nf = FFN_HIDDEN // fc
    idx3 = (row_token * SUBLANES).reshape(n_tiles, 1, rows)
    dst3 = (row_dst * SUBLANES).reshape(n_tiles, 1, rows)
    smem_tile = lambda fn: pl.BlockSpec((None, 1, rows), fn, memory_space=pltpu.SMEM)
    grid_spec = pltpu.PrefetchScalarGridSpec(
        num_scalar_prefetch=2,
        grid=(n_tiles, nf),
        in_specs=[smem_tile(lambda i, f, nu, te: (i, 0, 0)),
                  smem_tile(lambda i, f, nu, te: (jnp.minimum(i + 1, n_tiles - 1), 0, 0)),
                  smem_tile(lambda i, f, nu, te: (i, 0, 0)),
                  pl.BlockSpec(memory_space=pl.ANY),
                  pl.BlockSpec((None, D_MODEL, fc), lambda i, f, nu, te: (te[i], 0, f)),
                  pl.BlockSpec((None, D_MODEL, fc), lambda i, f, nu, te: (te[i], 0, f)),
                  pl.BlockSpec((None, fc, D_MODEL), lambda i, f, nu, te: (te[i], f, 0))],
        out_specs=pl.BlockSpec(memory_space=pl.ANY),
        scratch_shapes=[pltpu.VMEM((2 * rows * SUBLANES, LANES), F32), pltpu.VMEM((2 * rows * SUBLANES, LANES), F32),
                        pltpu.VMEM((rows, D_MODEL), BF16),
                        pltpu.VMEM((rows, D_MODEL), F32), pltpu.VMEM((rows, fc), BF16),
                        pltpu.SemaphoreType.DMA((2,)), pltpu.SemaphoreType.DMA((2,))],
    )
    return pl.pallas_call(
        _moe_experts_kernel,
        grid_spec=grid_spec,
        out_shape=jax.ShapeDtypeStruct((n_out_rows * SUBLANES, LANES), F32),
        compiler_params=_params("arbitrary", "arbitrary"),
        name="moe_experts",
    )(n_used, tile_expert, idx3, idx3, dst3, x_tiles, w1, w3, w2)


def _qkv_kernel(x_ref, wkv_ref, wq_ref, cos_ref, sin_ref, q_ref, k_ref, vt_ref, sel_ref, km_sc, qf_sc):
    i = pl.program_id(1)
    width = N_HEADS * HEAD_DIM
    half = HEAD_DIM // 2
    xb = x_ref[...].astype(BF16)
    cosf = cos_ref[...]
    sinf = sin_ref[...]

    @pl.when(i == 0)
    def _():
        km_sc[...] = jnp.zeros(km_sc.shape, F32)

    def rope(t):
        return t * cosf + pltpu.roll(t, half, axis=1) * sinf

    chunk = 2 * HEAD_DIM
    n_chunks = width // chunk
    nb = km_sc.shape[0]

    def q_chunk(c):
        qq = _dot(xb, wq_ref[:, c * chunk:(c + 1) * chunk])
        for hh in range(2):
            cols = slice((2 * c + hh) * HEAD_DIM, (2 * c + hh + 1) * HEAD_DIM)
            qr = rope(qq[:, hh * HEAD_DIM:(hh + 1) * HEAD_DIM])
            qf_sc[:, cols] = qr
            q_ref[:, cols] = (qr * (HEAD_DIM ** -0.5 * LOG2_E)).astype(BF16)

    def kv_chunk(c):
        kk = _dot(xb, wkv_ref[:, c * chunk:(c + 1) * chunk])
        vv = _dot(xb, wkv_ref[:, width + c * chunk:width + (c + 1) * chunk])
        for hh in range(2):
            h = 2 * c + hh
            cols = slice(h * HEAD_DIM, (h + 1) * HEAD_DIM)
            part = slice(hh * HEAD_DIM, (hh + 1) * HEAD_DIM)
            kr = rope(kk[:, part])
            k_ref[:, cols] = kr.astype(BF16)
            km_row = lax.broadcasted_iota(jnp.int32, (nb, HEAD_DIM), 0)
            km_sc[:, cols] = jnp.where(km_row == i, jnp.mean(kr, axis=0, keepdims=True), km_sc[:, cols])
            vt_ref[h] = vv[:, part].T.astype(BF16)

    def block_selection():
        km = km_sc[...]
        lane_head = lax.broadcasted_iota(jnp.int32, km.shape, 1) // HEAD_DIM
        km_rows = jnp.concatenate([jnp.where(lane_head == h, km, 0.0) for h in range(N_HEADS)], axis=0)
        gate_t = _dot_split(km_rows, qf_sc[...], _dot_nt)
        blk = lax.broadcasted_iota(jnp.int32, (nb, x_ref.shape[0]), 0)
        past = blk < i
        for h in range(N_HEADS):
            g = jnp.where(past, gate_t[h * nb:(h + 1) * nb, :], NEG_INF)
            rank = jnp.zeros(g.shape, jnp.int32)
            for n in range(nb):
                gn = g[n:n + 1, :]
                ahead = (gn > g) | ((gn == g) & (blk > n))
                rank = rank + ahead.astype(jnp.int32)
            sel_ref[h * nb:(h + 1) * nb, :] = jnp.where(past & (rank < MOBA_TOPK), 0.0, NEG_INF)

    for c in range(n_chunks):
        q_chunk(c)
    kv_chunk(0)
    block_selection()
    for c in range(1, n_chunks):
        kv_chunk(c)


def _qkv_proj(x, w_kv, w_q, cos_full, sin_signed, batch, seq):
    t = x.shape[0]
    width = N_HEADS * HEAD_DIM
    rows = MOBA_BLOCK
    nb = seq // rows
    row_map = lambda b, i: (b * nb + i, 0)
    act = jax.ShapeDtypeStruct((t, width), BF16)
    return pl.pallas_call(
        _qkv_kernel,
        grid=(batch, nb),
        in_specs=[pl.BlockSpec((rows, D_MODEL), row_map),
                  pl.BlockSpec((D_MODEL, 2 * width), lambda b, i: (0, 0)),
                  pl.BlockSpec((D_MODEL, width), lambda b, i: (0, 0)),
                  pl.BlockSpec((rows, HEAD_DIM), lambda b, i: (i, 0)),
                  pl.BlockSpec((rows, HEAD_DIM), lambda b, i: (i, 0))],
        out_specs=[pl.BlockSpec((rows, width), row_map),
                   pl.BlockSpec((rows, width), row_map),
                   pl.BlockSpec((None, N_HEADS, HEAD_DIM, rows), lambda b, i: (b, 0, 0, i)),
                   pl.BlockSpec((None, N_HEADS * nb, rows), lambda b, i: (b * nb + i, 0, 0))],
        out_shape=[act, act,
                   jax.ShapeDtypeStruct((batch, N_HEADS, HEAD_DIM, seq), BF16),
                   jax.ShapeDtypeStruct((batch * nb, N_HEADS * nb, rows), F32)],
        scratch_shapes=[pltpu.VMEM((nb, width), F32), pltpu.VMEM((rows, width), F32)],
        compiler_params=_params("arbitrary", "arbitrary"),
        name="moba_qkv_proj",
    )(x, w_kv, w_q, cos_full, sin_signed)


def _attn_kernel(q_ref, k_ref, vt_ref, sel_ref, o_ref, kaug_sc, vtaug_sc):
    blk = MOBA_BLOCK
    seq = q_ref.shape[0]
    nb = seq // blk
    extra = 2 * SUBLANES
    key = lax.broadcasted_iota(jnp.int32, (blk, blk), 0)
    qry = lax.broadcasted_iota(jnp.int32, (blk, blk), 1)
    causal = key <= qry

    key_blk = lax.broadcasted_iota(jnp.int32, (seq, LANES), 0) // blk
    kaug_sc[:, 0:HEAD_DIM] = k_ref[...]
    kaug_sc[:, HEAD_DIM:] = jnp.where(lax.broadcasted_iota(jnp.int32, (seq, LANES), 1) == key_blk,
                                      1.0, 0.0).astype(BF16)
    vtaug_sc[0:HEAD_DIM, :] = vt_ref[...]
    vtaug_sc[HEAD_DIM:, :] = jnp.where(lax.broadcasted_iota(jnp.int32, (extra, seq), 0) == 0,
                                       1.0, 0.0).astype(BF16)
    blk_row = lax.broadcasted_iota(jnp.int32, (nb, blk), 0)

    def scores(i):
        bias = jnp.where(blk_row == i, 0.0, sel_ref[i])
        bias_t = jnp.concatenate([bias, jnp.zeros((LANES - nb, blk), F32)], axis=0).T
        q_aug = jnp.concatenate([q_ref[i * blk:(i + 1) * blk, :], bias_t.astype(BF16)], axis=1)
        return _dot_nt(kaug_sc[0:(i + 1) * blk, :], q_aug)

    def finish(i, s):
        parts = [s[j * blk:(j + 1) * blk] for j in range(i)]
        parts.append(jnp.where(causal, s[i * blk:(i + 1) * blk], NEG_INF))
        m = jnp.max(parts[0], axis=0, keepdims=True)
        for part in parts[1:]:
            m = jnp.maximum(m, jnp.max(part, axis=0, keepdims=True))
        p = jnp.concatenate([jnp.exp2(part - m).astype(BF16) for part in parts], axis=0)
        acc = _dot(vtaug_sc[:, 0:(i + 1) * blk], p)
        o_ref[i * blk:(i + 1) * blk, :] = (acc[0:HEAD_DIM] / acc[HEAD_DIM:HEAD_DIM + 1]).T.astype(BF16)

    ahead = 3
    pending = [scores(i) for i in range(min(ahead, nb))]
    for i in range(nb):
        if i + ahead < nb:
            pending.append(scores(i + ahead))
        finish(i, pending.pop(0))


def _moba_attention(q, k, vt, sel, batch, seq):
    t = q.shape[0]
    width = N_HEADS * HEAD_DIM
    nb = seq // MOBA_BLOCK
    col = pl.BlockSpec((seq, HEAD_DIM), lambda b, h: (b, h))
    return pl.pallas_call(
        _attn_kernel,
        grid=(batch, N_HEADS),
        in_specs=[col, col,
                  pl.BlockSpec((None, None, HEAD_DIM, seq), lambda b, h: (b, h, 0, 0)),
                  pl.BlockSpec((nb, nb, MOBA_BLOCK), lambda b, h: (b, h, 0))],
        out_specs=col,
        out_shape=jax.ShapeDtypeStruct((t, width), BF16),
        scratch_shapes=[pltpu.VMEM((seq, HEAD_DIM + LANES), BF16),
                        pltpu.VMEM((HEAD_DIM + 2 * SUBLANES, seq), BF16)],
        compiler_params=_params("arbitrary", "arbitrary"),
        name="moba_attention",
    )(q, k, vt, sel)


def _combine_kernel(y0_ref, y1_ref, x_ref, route_ref, g_ref, b_ref, o_ref):
    route = route_ref[...]
    rows = o_ref.shape[0]
    y = (DEEPNORM_ALPHA * _tiles_to_rows(x_ref, 0, rows) + route[:, 0:1] * _tiles_to_rows(y0_ref, 0, rows)
         + route[:, 1:2] * _tiles_to_rows(y1_ref, 0, rows))
    o_ref[...] = _layer_norm(y, g_ref[...], b_ref[...])


def _moe_combine(ys, x_tiles, route, g_row, b_row):
    t = x_tiles.shape[0] // SUBLANES
    rows = ROWS_COMBINE
    nt = t // rows
    row_map = lambda i: (i, 0)
    const = lambda i: (0, 0)
    return pl.pallas_call(
        _combine_kernel,
        grid=(nt,),
        in_specs=[pl.BlockSpec((rows * SUBLANES, LANES), row_map),
                  pl.BlockSpec((rows * SUBLANES, LANES), lambda i: (nt + i, 0)),
                  pl.BlockSpec((rows * SUBLANES, LANES), row_map),
                  pl.BlockSpec((rows, LANES), row_map),
                  pl.BlockSpec((1, D_MODEL), const),
                  pl.BlockSpec((1, D_MODEL), const)],
        out_specs=pl.BlockSpec((rows, D_MODEL), row_map),
        out_shape=jax.ShapeDtypeStruct((t, D_MODEL), F32),
        compiler_params=_params("arbitrary"),
        name="moe_combine",
    )(ys, ys, x_tiles, route, g_row, b_row)


def _routing_tables(route, n_tokens):
    rows = ROWS_FFN
    n_assign = n_tokens * TOP_K
    n_tiles = n_assign // rows + N_EXPERTS
    experts = route[:, 2:4].astype(jnp.int32).reshape(-1)
    onehot = (experts[:, None] == jnp.arange(N_EXPERTS, dtype=jnp.int32)[None, :]).astype(jnp.int32)
    counts = jnp.sum(onehot, axis=0)
    tiles_per = (counts + rows - 1) // rows
    tile_end = jnp.cumsum(tiles_per)
    id_bits = max(n_assign, rows).bit_length()
    real_keys = (experts << (id_bits + 1)) | jnp.arange(n_assign, dtype=jnp.int32)
    pad_id = jnp.arange(rows, dtype=jnp.int32)[None, :]
    pad_needed = pad_id < (tiles_per * rows - counts)[:, None]
    pad_keys = (jnp.arange(N_EXPERTS, dtype=jnp.int32)[:, None] << (id_bits + 1)) | (1 << id_bits) | pad_id
    pad_keys = jnp.where(pad_needed, pad_keys, jnp.iinfo(jnp.int32).max)
    keys = jnp.sort(jnp.concatenate([real_keys, pad_keys.reshape(-1)]))
    real = ((keys >> id_bits) & 1) == 0
    assign = jnp.where(real, keys & ((1 << id_bits) - 1), -1)
    row_token = jnp.where(real, assign // TOP_K, 0)
    pad_rank = jnp.minimum(jnp.cumsum(1 - real.astype(jnp.int32)) - 1, N_EXPERTS * rows - 1)
    row_dst = jnp.where(real, (assign % TOP_K) * n_tokens + assign // TOP_K, n_assign + pad_rank)
    tile_id = jnp.arange(n_tiles, dtype=jnp.int32)
    n_used = tile_end[-1]
    tile_expert = jnp.sum((tile_id[:, None] >= tile_end[None, :]).astype(jnp.int32), axis=1)
    last_expert = jnp.sum((n_used - 1 >= tile_end).astype(jnp.int32))
    tile_expert = jnp.minimum(jnp.where(tile_id < n_used, tile_expert, last_expert), N_EXPERTS - 1)
    return n_used.reshape(1), tile_expert, row_token, row_dst, n_assign + N_EXPERTS * rows


def _rope_tables(seq):
    half = HEAD_DIM // 2
    inv_freq = ROPE_THETA ** (-jnp.arange(half, dtype=F32) / half)
    ang = jnp.arange(seq).astype(F32)[:, None] * inv_freq[None, :]
    cos, sin = jnp.cos(ang), jnp.sin(ang)
    return jnp.concatenate([cos, cos], axis=1), jnp.concatenate([-sin, sin], axis=1)


def _pad_lanes(w):
    return jnp.pad(w, ((0, 0), (0, LANES - w.shape[1])))


def kernel(x, a_w_in, a_conv_w, a_log_decay, a_dt_bias, a_norm_w, a_w_out, b_w_kv, b_w_q, b_w_o,
           ffn_w1, ffn_w3, ffn_w2, moe_router, moe_w1, moe_w3, moe_w2, ln_g, ln_b):
    batch, seq, _ = x.shape
    t = batch * seq
    width = N_HEADS * HEAD_DIM
    x0 = x.reshape(t, D_MODEL)
    row = lambda v: v.reshape(1, -1).astype(F32)

    w_in = a_w_in[0]
    q, k, v, gate, gb, moe_w1b, moe_w2b, w_outb, w_ob, w_kvb, w_qb = _in_proj(
        x0, w_in[:, :4 * width].astype(BF16), _pad_lanes(w_in[:, 4 * width:]).astype(BF16),
        a_conv_w[0], _pad_lanes(row(a_log_decay[0])), _pad_lanes(row(a_dt_bias[0])), batch, seq,
        riders=(moe_w1[0].reshape(N_EXPERTS * D_MODEL, FFN_HIDDEN),
                moe_w2[0].reshape(N_EXPERTS * FFN_HIDDEN, D_MODEL),
                a_w_out[0], b_w_o[0], b_w_kv, b_w_q[0]))
    og, moe_w3b, ffn_w1b, ffn_w3b = _delta_rule(
        q, k, v, gate, gb, row(a_norm_w[0]), batch, seq,
        riders=(moe_w3[0].reshape(N_EXPERTS * D_MODEL, FFN_HIDDEN), ffn_w1[0], ffn_w3[0]))
    x2 = _mixer_ffn(og, w_outb, x0, row(ln_g[0, 0]), row(ln_b[0, 0]),
                    ffn_w1b, ffn_w3b, ffn_w2[0].astype(BF16), row(ln_g[0, 1]), row(ln_b[0, 1]))

    cos_full, sin_signed = _rope_tables(seq)
    qm, km, vt, sel = _qkv_proj(x2, w_kvb, w_qb, cos_full, sin_signed, batch, seq)
    attn = _moba_attention(qm, km, vt, sel, batch, seq)
    x3_tiles, route = _proj_ln_router(attn, w_ob, x2, row(ln_g[1, 0]), row(ln_b[1, 0]),
                                      _pad_lanes(moe_router[0]))
    n_used, tile_expert, row_token, row_dst, n_out_rows = _routing_tables(route, t)
    ys = _moe_experts(n_used, tile_expert, row_token, row_dst, x3_tiles,
                      moe_w1b.reshape(N_EXPERTS, D_MODEL, FFN_HIDDEN),
                      moe_w3b.reshape(N_EXPERTS, D_MODEL, FFN_HIDDEN),
                      moe_w2b.reshape(N_EXPERTS, FFN_HIDDEN, D_MODEL), n_out_rows)
    x4 = _moe_combine(ys, x3_tiles, route, row(ln_g[1, 1]), row(ln_b[1, 1]))
    return x4.reshape(batch, seq, D_MODEL)
```

```python
import functools
import math

import jax
import jax.numpy as jnp
from jax import lax
from jax.experimental import pallas as pl
from jax.experimental.pallas import tpu as pltpu

D_MODEL = 1024
DEPTH = 2
DEEPNORM_ALPHA = (2.0 * DEPTH) ** 0.25
LN_EPS = 1e-5
N_HEADS = 8
HEAD_DIM = 128
DN_CONV = 4
DN_CHUNK = 64
DN_NORM_EPS = 1e-6
MOBA_BLOCK = 256
MOBA_TOPK = 3
ROPE_THETA = 10000.0
NEG_INF = -1e30
LOG2_E = math.log2(math.e)
FFN_HIDDEN = 3584
N_EXPERTS = 8
TOP_K = 2

LANES = 128
SUBLANES = 8
VMEM_LIMIT_BYTES = 56 * 1024 * 1024

ROWS_PROJ = 256
ROWS_DELTA = 256
ROWS_LN = 1024
ROWS_LN_SUB = 256
ROWS_FFN = 512
FFN_CHUNK = 1792
ROWS_COMBINE = 1024

F32 = jnp.float32
BF16 = jnp.bfloat16

_NT = (((1,), (1,)), ((), ()))


def _dot(a, b):
    return jnp.dot(a, b, preferred_element_type=F32)


def _dot_nt(a, b):
    return lax.dot_general(a, b, _NT, preferred_element_type=F32)


def _params(*sem):
    return pltpu.CompilerParams(dimension_semantics=sem, vmem_limit_bytes=VMEM_LIMIT_BYTES)


def _layer_norm(y, g, b):
    mu = jnp.mean(y, axis=-1, keepdims=True)
    d = y - mu
    var = jnp.mean(d * d, axis=-1, keepdims=True)
    return d * lax.rsqrt(var + LN_EPS) * g + b


def _sigmoid(x):
    return 1.0 / (1.0 + jnp.exp(-x))


def _split2(x):
    hi = x.astype(BF16)
    lo = (x - hi.astype(F32)).astype(BF16)
    return hi, lo


def _dot_split(a, b, dot):
    a_hi, a_lo = _split2(a)
    b_hi, b_lo = _split2(b)
    return dot(a_hi, b_hi) + dot(a_hi, b_lo) + dot(a_lo, b_hi)


def _rider_specs(arrays, n_steps, index_map):
    assert all(a.shape[0] % (n_steps * 2 * SUBLANES) == 0 for a in arrays)
    specs = [pl.BlockSpec((a.shape[0] // n_steps, a.shape[1]), index_map) for a in arrays]
    return specs, [jax.ShapeDtypeStruct(a.shape, BF16) for a in arrays]


def _cast_riders(src_refs, dst_refs):
    for src, dst in zip(src_refs, dst_refs):
        dst[...] = src[...].astype(BF16)


def _in_proj_kernel(n_riders, x_ref, w_ref, wab_ref, cw_ref, alog_ref, dtb_ref, *rest):
    riders_in, rest = rest[:n_riders], rest[n_riders:]
    q_ref, k_ref, v_ref, gate_ref, gb_ref = rest[:5]
    riders_out, conv_sc = rest[5:5 + n_riders], rest[5 + n_riders]
    rows = x_ref.shape[0]
    width = N_HEADS * HEAD_DIM
    hist = SUBLANES

    @pl.when(pl.program_id(1) == 0)
    def _():
        conv_sc[...] = jnp.zeros(conv_sc.shape, F32)

    xb = x_ref[...].astype(BF16)
    chunk = 2 * HEAD_DIM
    for c in range(3 * width // chunk):
        proj = _dot(xb, w_ref[:, c * chunk:(c + 1) * chunk])
        which = c * chunk // width
        for hh in range(chunk // HEAD_DIM):
            slab = c * (chunk // HEAD_DIM) + hh
            cols = slice(slab * HEAD_DIM, (slab + 1) * HEAD_DIM)
            cur = proj[:, hh * HEAD_DIM:(hh + 1) * HEAD_DIM]
            conv_sc[slab, pl.ds(2 * hist, rows, stride=2), :] = cur
            y = cur * cw_ref[DN_CONV - 1:DN_CONV, cols]
            for j in range(DN_CONV - 1):
                shifted = conv_sc[slab, pl.ds(2 * (hist - (DN_CONV - 1) + j), rows, stride=2), :]
                y = y + shifted * cw_ref[j:j + 1, cols]
            conv_sc[slab, 0:2 * hist, :] = conv_sc[slab, 2 * rows:2 * (rows + hist), :]
            t = y * _sigmoid(y)
            ocols = slice(slab * HEAD_DIM - which * width, (slab + 1) * HEAD_DIM - which * width)
            if which == 2:
                v_ref[:, ocols] = t.astype(BF16)
            else:
                inv_norm = lax.rsqrt(jnp.sum(t * t, axis=-1, keepdims=True) + DN_NORM_EPS)
                if which == 0:
                    q_ref[:, ocols] = (t * (inv_norm * (HEAD_DIM ** -0.5))).astype(BF16)
                else:
                    k_ref[:, ocols] = (t * inv_norm).astype(BF16)
        if c % 3 == 2:
            g0 = (c // 3) * chunk
            gate_ref[:, g0:g0 + chunk] = _dot(xb, w_ref[:, 3 * width + g0:3 * width + g0 + chunk]).astype(BF16)
    ab = _dot(xb, wab_ref[...])
    z = ab + dtb_ref[...]
    softplus = jnp.maximum(z, 0.0) + jnp.log(1.0 + jnp.exp(-jnp.abs(z)))
    g = -jnp.exp(alog_ref[...]) * softplus
    lane = lax.broadcasted_iota(jnp.int32, ab.shape, 1)
    gb_ref[...] = jnp.where(lane < N_HEADS, g, _sigmoid(ab))
    _cast_riders(riders_in, riders_out)


def _in_proj(x2d, w_main, w_ab, conv_w, a_log_row, dt_row, batch, seq, riders=()):
    t = x2d.shape[0]
    width = N_HEADS * HEAD_DIM
    rows = ROWS_PROJ
    nt = seq // rows
    row_map = lambda b, s: (b * nt + s, 0)
    const = lambda b, s: (0, 0)
    act = jax.ShapeDtypeStruct((t, width), BF16)
    rider_specs, rider_shapes = _rider_specs(riders, batch * nt, row_map)
    return pl.pallas_call(
        functools.partial(_in_proj_kernel, len(riders)),
        grid=(batch, nt),
        in_specs=[pl.BlockSpec((rows, D_MODEL), row_map),
                  pl.BlockSpec((D_MODEL, w_main.shape[1]), const),
                  pl.BlockSpec((D_MODEL, LANES), const),
                  pl.BlockSpec((DN_CONV, 3 * width), const),
                  pl.BlockSpec((1, LANES), const),
                  pl.BlockSpec((1, LANES), const)] + rider_specs,
        out_specs=[pl.BlockSpec((rows, width), row_map)] * 4 + [pl.BlockSpec((rows, LANES), row_map)]
        + rider_specs,
        out_shape=[act, act, act, act, jax.ShapeDtypeStruct((t, LANES), F32)] + rider_shapes,
        scratch_shapes=[pltpu.VMEM((3 * width // LANES, 2 * (rows + SUBLANES), LANES), F32)],
        compiler_params=_params("arbitrary", "arbitrary"),
        name="dn_in_proj",
    )(x2d, w_main, w_ab, conv_w, a_log_row, dt_row, *riders)


def _delta_kernel(n_riders, q_ref, k_ref, v_ref, gate_ref, gb_ref, nw_ref, *rest):
    riders_in, o_ref = rest[:n_riders], rest[n_riders]
    riders_out, state_sc = rest[n_riders + 1:2 * n_riders + 1], rest[2 * n_riders + 1]
    c = DN_CHUNK
    two = 2 * c
    n_sub = q_ref.shape[0] // c
    n_pairs = N_HEADS // 2

    @pl.when(pl.program_id(1) == 0)
    def _():
        state_sc[...] = jnp.zeros(state_sc.shape, F32)

    r = lax.broadcasted_iota(jnp.int32, (two, two), 0)
    cc = lax.broadcasted_iota(jnp.int32, (two, two), 1)
    same = (r < c) == (cc < c)
    tri = same & (r >= cc)
    stri = same & (r > cc)
    eye = (r == cc).astype(F32)
    top_rows = lax.broadcasted_iota(jnp.int32, (two, 1), 0) < c
    top_lanes = lax.broadcasted_iota(jnp.int32, (1, two), 1) < c
    row64 = lax.broadcasted_iota(jnp.int32, (c, LANES), 0)

    def stack(ref, rows, h):
        t = ref[rows, h * HEAD_DIM:(h + 2) * HEAD_DIM].astype(F32)
        return jnp.concatenate([t[:, :HEAD_DIM], t[:, HEAD_DIM:]], axis=0)

    items = []
    for s in range(n_sub):
        rows = slice(s * c, (s + 1) * c)
        gb = gb_ref[rows, :]
        gc = gb
        sh = 1
        while sh < c:
            gc = gc + jnp.where(row64 >= sh, pltpu.roll(gc, sh, axis=0), 0.0)
            sh *= 2
        gc_pair = jnp.concatenate([gc, pltpu.roll(gc, LANES - 1, axis=1)], axis=0)
        gb_pair = jnp.concatenate([gb, pltpu.roll(gb, LANES - 1, axis=1)], axis=0)
        gc_pair_t = gc_pair.T
        for p in range(n_pairs):
            h = 2 * p
            q2, k2, v2 = stack(q_ref, rows, h), stack(k_ref, rows, h), stack(v_ref, rows, h)
            gcol = gc_pair[:, h:h + 1]
            grow = gc_pair_t[h:h + 1, :]
            beta = gb_pair[:, N_HEADS + h:N_HEADS + h + 1]
            glast = jnp.where(top_rows, gc_pair[c - 1:c, h:h + 1], gc_pair[two - 1:two, h:h + 1])
            eg = jnp.exp(gcol)
            kb = k2 * beta
            items.append(dict(
                rows=rows, h=h, k2b=k2.astype(BF16), q2b=q2.astype(BF16), kbb=kb.astype(BF16),
                decay=jnp.exp(jnp.where(tri, gcol - grow, NEG_INF)),
                rhs=jnp.concatenate([v2 * beta, kb * eg], axis=1).astype(BF16),
                qd=(q2 * eg).astype(BF16),
                kd_t=(k2 * jnp.exp(glast - gcol)).T,
                eglast=jnp.exp(glast)))
    for it in items:
        it["pw"] = jnp.where(stri, _dot_nt(it["kbb"], it["k2b"]) * it["decay"], 0.0)
        it["inv"] = eye - it["pw"]
    for _ in range(int(math.log2(c)) - 1):
        for it in items:
            pwb = it["pw"].astype(BF16)
            it["pw"] = _dot(pwb, pwb)
        for it in items:
            it["inv"] = it["inv"] + _dot(it["inv"].astype(BF16), it["pw"].astype(BF16))
    for it in items:
        sol = _dot(it["inv"].astype(BF16), it["rhs"])
        it["u"] = sol[:, :HEAD_DIM]
        it["wb"] = sol[:, HEAD_DIM:].astype(BF16)
    for it in items:
        it["amat"] = jnp.where(tri, _dot_nt(it["q2b"], it["k2b"]) * it["decay"], 0.0).astype(BF16)

    for s in range(n_sub):
        group = items[s * n_pairs:(s + 1) * n_pairs]
        for it in group:
            h = it["h"]
            it["s0"] = state_sc[h]
            it["s1"] = state_sc[h + 1]
            it["r0"] = _dot(jnp.concatenate([it["wb"][:c], it["qd"][:c]], axis=0), it["s0"].astype(BF16))
            it["r1"] = _dot(jnp.concatenate([it["wb"][c:], it["qd"][c:]], axis=0), it["s1"].astype(BF16))
        for it in group:
            v_new = it["u"] - jnp.concatenate([it["r0"][:c], it["r1"][:c]], axis=0)
            it["vnb"] = v_new.astype(BF16)
            it["o"] = jnp.concatenate([it["r0"][c:], it["r1"][c:]], axis=0) + _dot(it["amat"], it["vnb"])
        for it in group:
            h = it["h"]
            kd_t = it["kd_t"]
            state_sc[h] = it["s0"] * it["eglast"][0:1, :] + _dot(
                jnp.where(top_lanes, kd_t, 0.0).astype(BF16), it["vnb"])
            state_sc[h + 1] = it["s1"] * it["eglast"][c:c + 1, :] + _dot(
                jnp.where(top_lanes, 0.0, kd_t).astype(BF16), it["vnb"])
        for it in group:
            h, rows, o = it["h"], it["rows"], it["o"]
            gate2 = stack(gate_ref, rows, h)
            o = o * lax.rsqrt(jnp.mean(o * o, axis=-1, keepdims=True) + DN_NORM_EPS) * nw_ref[...]
            o = o * (gate2 * _sigmoid(gate2))
            o_ref[rows, h * HEAD_DIM:(h + 1) * HEAD_DIM] = o[:c].astype(BF16)
            o_ref[rows, (h + 1) * HEAD_DIM:(h + 2) * HEAD_DIM] = o[c:].astype(BF16)
    _cast_riders(riders_in, riders_out)


def _delta_rule(q, k, v, gate, gb, norm_w_row, batch, seq, riders=()):
    t = q.shape[0]
    width = N_HEADS * HEAD_DIM
    rows = ROWS_DELTA
    ns = seq // rows
    row_map = lambda b, s: (b * ns + s, 0)
    const = lambda b, s: (0, 0)
    wide = pl.BlockSpec((rows, width), row_map)
    rider_specs, rider_shapes = _rider_specs(riders, batch * ns, row_map)
    return pl.pallas_call(
        functools.partial(_delta_kernel, len(riders)),
        grid=(batch, ns),
        in_specs=[wide, wide, wide, wide,
                  pl.BlockSpec((rows, LANES), row_map),
                  pl.BlockSpec((1, HEAD_DIM), const)] + rider_specs,
        out_specs=[wide] + rider_specs,
        out_shape=[jax.ShapeDtypeStruct((t, width), BF16)] + rider_shapes,
        scratch_shapes=[pltpu.VMEM((N_HEADS, HEAD_DIM, HEAD_DIM), F32)],
        compiler_params=_params("arbitrary", "arbitrary"),
        name="dn_delta_rule",
    )(q, k, v, gate, gb, norm_w_row, *riders)


def _rows_to_tiles(ref, base, y):
    for k in range(D_MODEL // LANES):
        ref[pl.ds(base + k, y.shape[0], stride=SUBLANES), :] = y[:, k * LANES:(k + 1) * LANES]


def _tiles_to_rows(ref, base, n):
    return jnp.concatenate([ref[pl.ds(base + k, n, stride=SUBLANES), :] for k in range(D_MODEL // LANES)],
                           axis=1)


def _proj_residual(a_ref, w_ref, res_ref):
    blocks = []
    for rb in range(a_ref.shape[0] // ROWS_LN_SUB):
        rows = slice(rb * ROWS_LN_SUB, (rb + 1) * ROWS_LN_SUB)
        blocks.append((rows, DEEPNORM_ALPHA * res_ref[rows, :] + _dot(a_ref[rows, :], w_ref[...])))
    return blocks


def _proj_ln_router_kernel(a_ref, w_ref, res_ref, g_ref, b_ref, wr_ref, ot_ref, route_ref):
    for rows, y in _proj_residual(a_ref, w_ref, res_ref):
        xn = _layer_norm(y, g_ref[...], b_ref[...])
        _rows_to_tiles(ot_ref, rows.start * SUBLANES, xn)
        logits = _dot_split(xn, wr_ref[...], _dot)
        lane = lax.broadcasted_iota(jnp.int32, logits.shape, 1)
        logits = jnp.where(lane < N_EXPERTS, logits, NEG_INF)
        l1 = jnp.max(logits, axis=-1, keepdims=True)
        i1 = jnp.min(jnp.where(logits == l1, lane, LANES), axis=-1, keepdims=True)
        rest = jnp.where(lane == i1, NEG_INF, logits)
        l2 = jnp.max(rest, axis=-1, keepdims=True)
        i2 = jnp.min(jnp.where(rest == l2, lane, LANES), axis=-1, keepdims=True)
        e2 = jnp.exp(l2 - l1)
        g1 = 1.0 / (1.0 + e2)
        g2 = e2 / (1.0 + e2)
        route_ref[rows, :] = jnp.where(lane == 0, g1,
                             jnp.where(lane == 1, g2,
                             jnp.where(lane == 2, i1.astype(F32),
                             jnp.where(lane == 3, i2.astype(F32), 0.0))))


def _proj_ln_router(a, w, res, g_row, b_row, w_router):
    t, kdim = a.shape
    rows = ROWS_LN
    row_map = lambda i: (i, 0)
    const = lambda i: (0, 0)
    in_specs = [pl.BlockSpec((rows, kdim), row_map),
                pl.BlockSpec((kdim, D_MODEL), const),
                pl.BlockSpec((rows, D_MODEL), row_map),
                pl.BlockSpec((1, D_MODEL), const),
                pl.BlockSpec((1, D_MODEL), const)]
    return pl.pallas_call(
        _proj_ln_router_kernel, grid=(t // rows,),
        in_specs=in_specs + [pl.BlockSpec((D_MODEL, LANES), const)],
        out_specs=[pl.BlockSpec((rows * SUBLANES, LANES), row_map),
                   pl.BlockSpec((rows, LANES), row_map)],
        out_shape=[jax.ShapeDtypeStruct((t * SUBLANES, LANES), F32),
                   jax.ShapeDtypeStruct((t, LANES), F32)],
        compiler_params=_params("arbitrary"), name="proj_ln_router",
    )(a, w, res, g_row, b_row, w_router)


def _swiglu_hidden(x, w1_ref, w3_ref, hid_sc):
    xb = x.astype(BF16)
    sub = 2 * LANES
    for c in range(hid_sc.shape[1] // sub):
        cols = slice(c * sub, (c + 1) * sub)
        h1 = _dot(xb, w1_ref[:, cols])
        h3 = _dot(xb, w3_ref[:, cols])
        hid_sc[:, cols] = (h1 * _sigmoid(h1) * h3).astype(BF16)


def _swiglu_part(x, w1_ref, w3_ref, w2_ref, hid_sc):
    _swiglu_hidden(x, w1_ref, w3_ref, hid_sc)
    return _dot(hid_sc[...], w2_ref[...])


def _mixer_ffn_kernel(a_ref, wo_ref, res_ref, g0_ref, b0_ref, w1_ref, w3_ref, w2_ref, g1_ref, b1_ref,
                      o_ref, x_sc, acc_sc, hid_sc):
    f = pl.program_id(1)
    last_f = pl.num_programs(1) - 1
    n_sub = a_ref.shape[0] // ROWS_LN_SUB

    @pl.when(f == 0)
    def _():
        for rows, y in _proj_residual(a_ref, wo_ref, res_ref):
            x_sc[rows, :] = _layer_norm(y, g0_ref[...], b0_ref[...])

    _swiglu_hidden(x_sc[...], w1_ref, w3_ref, hid_sc)

    @pl.when(f == 0)
    def _():
        acc_sc[...] = _dot(hid_sc[...], w2_ref[...])

    @pl.when((f > 0) & (f < last_f))
    def _():
        acc_sc[...] += _dot(hid_sc[...], w2_ref[...])

    @pl.when(f == last_f)
    def _():
        for rb in range(n_sub):
            rows = slice(rb * ROWS_LN_SUB, (rb + 1) * ROWS_LN_SUB)
            y = DEEPNORM_ALPHA * x_sc[rows, :] + (acc_sc[rows, :] + _dot(hid_sc[rows, :], w2_ref[...]))
            o_ref[rows, :] = _layer_norm(y, g1_ref[...], b1_ref[...])


def _mixer_ffn(a, wo, res, g0_row, b0_row, w1, w3, w2, g1_row, b1_row):
    t = res.shape[0]
    rows, fc = ROWS_FFN, FFN_CHUNK
    nf = FFN_HIDDEN // fc
    assert nf >= 2
    row_map = lambda i, f: (i, 0)
    const = lambda i, f: (0, 0)
    vec = pl.BlockSpec((1, D_MODEL), const)
    return pl.pallas_call(
        _mixer_ffn_kernel,
        grid=(t // rows, nf),
        in_specs=[pl.BlockSpec((rows, a.shape[1]), row_map),
                  pl.BlockSpec((a.shape[1], D_MODEL), const),
                  pl.BlockSpec((rows, D_MODEL), row_map), vec, vec,
                  pl.BlockSpec((D_MODEL, fc), lambda i, f: (0, f)),
                  pl.BlockSpec((D_MODEL, fc), lambda i, f: (0, f)),
                  pl.BlockSpec((fc, D_MODEL), lambda i, f: (f, 0)), vec, vec],
        out_specs=pl.BlockSpec((rows, D_MODEL), row_map),
        out_shape=jax.ShapeDtypeStruct((t, D_MODEL), F32),
        scratch_shapes=[pltpu.VMEM((rows, D_MODEL), F32), pltpu.VMEM((rows, D_MODEL), F32),
                        pltpu.VMEM((rows, fc), BF16)],
        compiler_params=_params("arbitrary", "arbitrary"),
        name="mixer_ffn_dense",
    )(a, wo, res, g0_row, b0_row, w1, w3, w2, g1_row, b1_row)


def _moe_experts_kernel(nu_ref, te_ref, idx_ref, idx_next_ref, dst_ref, x_hbm, w1_ref, w3_ref, w2_ref,
                        y_hbm, xbuf, ybuf, xb_sc, acc_sc, hid_sc, gsem, ssem):
    i = pl.program_id(0)
    f = pl.program_id(1)
    last_f = pl.num_programs(1) - 1
    rows = xb_sc.shape[0]
    tile_rows = rows * SUBLANES
    nu = nu_ref[0]
    slot = i % 2

    def gather_start(idx, s):
        for r in range(rows):
            src = x_hbm.at[pl.ds(pl.multiple_of(idx[0, r], SUBLANES), SUBLANES)]
            pltpu.make_async_copy(src, xbuf.at[pl.ds(s * tile_rows + r * SUBLANES, SUBLANES)], gsem.at[s]).start()

    def scatter_start(s):
        for r in range(rows):
            dst = y_hbm.at[pl.ds(pl.multiple_of(dst_ref[0, r], SUBLANES), SUBLANES)]
            pltpu.make_async_copy(ybuf.at[pl.ds(s * tile_rows + r * SUBLANES, SUBLANES)], dst, ssem.at[s]).start()

    def wait_all(buf, sem, s):
        view = buf.at[pl.ds(pl.multiple_of(s * tile_rows, tile_rows), tile_rows)]
        pltpu.make_async_copy(view, view, sem.at[s]).wait()

    def for_slot(cond, s, fn):
        for static_slot in range(2):
            pl.when(cond & (s == static_slot))(functools.partial(fn, static_slot))

    first = f == 0

    @pl.when(first & (i == 0))
    def _():
        n_real = TOP_K * x_hbm.shape[0]
        ybuf[tile_rows:2 * tile_rows, :] = jnp.zeros((tile_rows, LANES), F32)
        fills = [pltpu.make_async_copy(ybuf.at[pl.ds(tile_rows, tile_rows)],
                                       y_hbm.at[pl.ds(n_real + k * tile_rows, tile_rows)], ssem.at[1])
                 for k in range((y_hbm.shape[0] - n_real) // tile_rows)]
        for cp in fills:
            cp.start()
        for cp in fills:
            cp.wait()

    @pl.when(first & (i == 0) & (nu > 0))
    def _():
        gather_start(idx_ref, 0)

    @pl.when(first & (i < nu))
    def _():
        wait_all(xbuf, gsem, slot)

    for_slot(first & (i + 1 < nu), 1 - slot, functools.partial(gather_start, idx_next_ref))

    @pl.when(i < nu)
    def _():
        @pl.when(first)
        def _():
            base = pl.multiple_of(slot * tile_rows, tile_rows)
            for k in range(D_MODEL // LANES):
                xb_sc[:, k * LANES:(k + 1) * LANES] = xbuf[pl.ds(base + k, rows, stride=SUBLANES), :].astype(BF16)

        part = _swiglu_part(xb_sc[...], w1_ref, w3_ref, w2_ref, hid_sc)

        @pl.when(first)
        def _():
            acc_sc[...] = part

        @pl.when((f > 0) & (f < last_f))
        def _():
            acc_sc[...] += part

        @pl.when(f == last_f)
        def _():
            @pl.when(i >= 2)
            def _():
                wait_all(ybuf, ssem, slot)
            _rows_to_tiles(ybuf, pl.multiple_of(slot * tile_rows, tile_rows), acc_sc[...] + part)

        for_slot(f == last_f, slot, scatter_start)

    @pl.when((i == pl.num_programs(0) - 1) & (f == last_f))
    def _():
        @pl.when(nu >= 1)
        def _():
            wait_all(ybuf, ssem, (nu - 1) % 2)

        @pl.when(nu >= 2)
        def _():
            wait_all(ybuf, ssem, nu % 2)


def _moe_experts(n_used, tile_expert, row_token, row_dst, x_tiles, w1, w3, w2, n_out_rows):
    n_tiles = tile_expert.shape[0]
    rows, fc = ROWS_FFN, FFN_CHUNK
    nf = FFN_HIDDEN // fc
    idx3 = (row_token * SUBLANES).reshape(n_tiles, 1, rows)
    dst3 = (row_dst * SUBLANES).reshape(n_tiles, 1, rows)
    smem_tile = lambda fn: pl.BlockSpec((None, 1, rows), fn, memory_space=pltpu.SMEM)
    grid_spec = pltpu.PrefetchScalarGridSpec(
        num_scalar_prefetch=2,
        grid=(n_tiles, nf),
        in_specs=[smem_tile(lambda i, f, nu, te: (i, 0, 0)),
                  smem_tile(lambda i, f, nu, te: (jnp.minimum(i + 1, n_tiles - 1), 0, 0)),
                  smem_tile(lambda i, f, nu, te: (i, 0, 0)),
                  pl.BlockSpec(memory_space=pl.ANY),
                  pl.BlockSpec((None, D_MODEL, fc), lambda i, f, nu, te: (te[i], 0, f)),
                  pl.BlockSpec((None, D_MODEL, fc), lambda i, f, nu, te: (te[i], 0, f)),
                  pl.BlockSpec((None, fc, D_MODEL), lambda i, f, nu, te: (te[i], f, 0))],
        out_specs=pl.BlockSpec(memory_space=pl.ANY),
        scratch_shapes=[pltpu.VMEM((2 * rows * SUBLANES, LANES), F32), pltpu.VMEM((2 * rows * SUBLANES, LANES), F32),
                        pltpu.VMEM((rows, D_MODEL), BF16),
                        pltpu.VMEM((rows, D_MODEL), F32), pltpu.VMEM((rows, fc), BF16),
                        pltpu.SemaphoreType.DMA((2,)), pltpu.SemaphoreType.DMA((2,))],
    )
    return pl.pallas_call(
        _moe_experts_kernel,
        grid_spec=grid_spec,
        out_shape=jax.ShapeDtypeStruct((n_out_rows * SUBLANES, LANES), F32),
        compiler_params=_params("arbitrary", "arbitrary"),
        name="moe_experts",
    )(n_used, tile_expert, idx3, idx3, dst3, x_tiles, w1, w3, w2)


def _qkv_kernel(x_ref, wkv_ref, wq_ref, cos_ref, sin_ref, q_ref, k_ref, vt_ref, sel_ref, km_sc, qf_sc):
    i = pl.program_id(1)
    width = N_HEADS * HEAD_DIM
    half = HEAD_DIM // 2
    xb = x_ref[...].astype(BF16)
    cosf = cos_ref[...]
    sinf = sin_ref[...]

    @pl.when(i == 0)
    def _():
        km_sc[...] = jnp.zeros(km_sc.shape, F32)

    def rope(t):
        return t * cosf + pltpu.roll(t, half, axis=1) * sinf

    chunk = 2 * HEAD_DIM
    n_chunks = width // chunk
    nb = km_sc.shape[0]

    def q_chunk(c):
        qq = _dot(xb, wq_ref[:, c * chunk:(c + 1) * chunk])
        for hh in range(2):
            cols = slice((2 * c + hh) * HEAD_DIM, (2 * c + hh + 1) * HEAD_DIM)
            qr = rope(qq[:, hh * HEAD_DIM:(hh + 1) * HEAD_DIM])
            qf_sc[:, cols] = qr
            q_ref[:, cols] = (qr * (HEAD_DIM ** -0.5 * LOG2_E)).astype(BF16)

    def kv_chunk(c):
        kk = _dot(xb, wkv_ref[:, c * chunk:(c + 1) * chunk])
        vv = _dot(xb, wkv_ref[:, width + c * chunk:width + (c + 1) * chunk])
        for hh in range(2):
            h = 2 * c + hh
            cols = slice(h * HEAD_DIM, (h + 1) * HEAD_DIM)
            part = slice(hh * HEAD_DIM, (hh + 1) * HEAD_DIM)
            kr = rope(kk[:, part])
            k_ref[:, cols] = kr.astype(BF16)
            km_row = lax.broadcasted_iota(jnp.int32, (nb, HEAD_DIM), 0)
            km_sc[:, cols] = jnp.where(km_row == i, jnp.mean(kr, axis=0, keepdims=True), km_sc[:, cols])
            vt_ref[h] = vv[:, part].T.astype(BF16)

    def block_selection():
        km = km_sc[...]
        lane_head = lax.broadcasted_iota(jnp.int32, km.shape, 1) // HEAD_DIM
        km_rows = jnp.concatenate([jnp.where(lane_head == h, km, 0.0) for h in range(N_HEADS)], axis=0)
        gate_t = _dot_split(km_rows, qf_sc[...], _dot_nt)
        blk = lax.broadcasted_iota(jnp.int32, (nb, x_ref.shape[0]), 0)
        past = blk < i
        for h in range(N_HEADS):
            g = jnp.where(past, gate_t[h * nb:(h + 1) * nb, :], NEG_INF)
            rank = jnp.zeros(g.shape, jnp.int32)
            for n in range(nb):
                gn = g[n:n + 1, :]
                ahead = (gn > g) | ((gn == g) & (blk > n))
                rank = rank + ahead.astype(jnp.int32)
            sel_ref[h * nb:(h + 1) * nb, :] = jnp.where(past & (rank < MOBA_TOPK), 0.0, NEG_INF)

    for c in range(n_chunks):
        q_chunk(c)
    kv_chunk(0)
    block_selection()
    for c in range(1, n_chunks):
        kv_chunk(c)


def _qkv_proj(x, w_kv, w_q, cos_full, sin_signed, batch, seq):
    t = x.shape[0]
    width = N_HEADS * HEAD_DIM
    rows = MOBA_BLOCK
    nb = seq // rows
    row_map = lambda b, i: (b * nb + i, 0)
    act = jax.ShapeDtypeStruct((t, width), BF16)
    return pl.pallas_call(
        _qkv_kernel,
        grid=(batch, nb),
        in_specs=[pl.BlockSpec((rows, D_MODEL), row_map),
                  pl.BlockSpec((D_MODEL, 2 * width), lambda b, i: (0, 0)),
                  pl.BlockSpec((D_MODEL, width), lambda b, i: (0, 0)),
                  pl.BlockSpec((rows, HEAD_DIM), lambda b, i: (i, 0)),
                  pl.BlockSpec((rows, HEAD_DIM), lambda b, i: (i, 0))],
        out_specs=[pl.BlockSpec((rows, width), row_map),
                   pl.BlockSpec((rows, width), row_map),
                   pl.BlockSpec((None, N_HEADS, HEAD_DIM, rows), lambda b, i: (b, 0, 0, i)),
                   pl.BlockSpec((None, N_HEADS * nb, rows), lambda b, i: (b * nb + i, 0, 0))],
        out_shape=[act, act,
                   jax.ShapeDtypeStruct((batch, N_HEADS, HEAD_DIM, seq), BF16),
                   jax.ShapeDtypeStruct((batch * nb, N_HEADS * nb, rows), F32)],
        scratch_shapes=[pltpu.VMEM((nb, width), F32), pltpu.VMEM((rows, width), F32)],
        compiler_params=_params("arbitrary", "arbitrary"),
        name="moba_qkv_proj",
    )(x, w_kv, w_q, cos_full, sin_signed)


def _attn_kernel(q_ref, k_ref, vt_ref, sel_ref, o_ref, kaug_sc, vtaug_sc):
    blk = MOBA_BLOCK
    seq = q_ref.shape[0]
    nb = seq // blk
    extra = 2 * SUBLANES
    key = lax.broadcasted_iota(jnp.int32, (blk, blk), 0)
    qry = lax.broadcasted_iota(jnp.int32, (blk, blk), 1)
    causal = key <= qry

    key_blk = lax.broadcasted_iota(jnp.int32, (seq, LANES), 0) // blk
    kaug_sc[:, 0:HEAD_DIM] = k_ref[...]
    kaug_sc[:, HEAD_DIM:] = jnp.where(lax.broadcasted_iota(jnp.int32, (seq, LANES), 1) == key_blk,
                                      1.0, 0.0).astype(BF16)
    vtaug_sc[0:HEAD_DIM, :] = vt_ref[...]
    vtaug_sc[HEAD_DIM:, :] = jnp.where(lax.broadcasted_iota(jnp.int32, (extra, seq), 0) == 0,
                                       1.0, 0.0).astype(BF16)
    blk_row = lax.broadcasted_iota(jnp.int32, (nb, blk), 0)

    def scores(i):
        bias = jnp.where(blk_row == i, 0.0, sel_ref[i])
        bias_t = jnp.concatenate([bias, jnp.zeros((LANES - nb, blk), F32)], axis=0).T
        q_aug = jnp.concatenate([q_ref[i * blk:(i + 1) * blk, :], bias_t.astype(BF16)], axis=1)
        return _dot_nt(kaug_sc[0:(i + 1) * blk, :], q_aug)

    def finish(i, s):
        parts = [s[j * blk:(j + 1) * blk] for j in range(i)]
        parts.append(jnp.where(causal, s[i * blk:(i + 1) * blk], NEG_INF))
        m = jnp.max(parts[0], axis=0, keepdims=True)
        for part in parts[1:]:
            m = jnp.maximum(m, jnp.max(part, axis=0, keepdims=True))
        p = jnp.concatenate([jnp.exp2(part - m).astype(BF16) for part in parts], axis=0)
        acc = _dot(vtaug_sc[:, 0:(i + 1) * blk], p)
        o_ref[i * blk:(i + 1) * blk, :] = (acc[0:HEAD_DIM] / acc[HEAD_DIM:HEAD_DIM + 1]).T.astype(BF16)

    ahead = 3
    pending = [scores(i) for i in range(min(ahead, nb))]
    for i in range(nb):
        if i + ahead < nb:
            pending.append(scores(i + ahead))
        finish(i, pending.pop(0))


def _moba_attention(q, k, vt, sel, batch, seq):
    t = q.shape[0]
    width = N_HEADS * HEAD_DIM
    nb = seq // MOBA_BLOCK
    col = pl.BlockSpec((seq, HEAD_DIM), lambda b, h: (b, h))
    return pl.pallas_call(
        _attn_kernel,
        grid=(batch, N_HEADS),
        in_specs=[col, col,
                  pl.BlockSpec((None, None, HEAD_DIM, seq), lambda b, h: (b, h, 0, 0)),
                  pl.BlockSpec((nb, nb, MOBA_BLOCK), lambda b, h: (b, h, 0))],
        out_specs=col,
        out_shape=jax.ShapeDtypeStruct((t, width), BF16),
        scratch_shapes=[pltpu.VMEM((seq, HEAD_DIM + LANES), BF16),
                        pltpu.VMEM((HEAD_DIM + 2 * SUBLANES, seq), BF16)],
        compiler_params=_params("arbitrary", "arbitrary"),
        name="moba_attention",
    )(q, k, vt, sel)


def _combine_kernel(y0_ref, y1_ref, x_ref, route_ref, g_ref, b_ref, o_ref):
    route = route_ref[...]
    rows = o_ref.shape[0]
    y = (DEEPNORM_ALPHA * _tiles_to_rows(x_ref, 0, rows) + route[:, 0:1] * _tiles_to_rows(y0_ref, 0, rows)
         + route[:, 1:2] * _tiles_to_rows(y1_ref, 0, rows))
    o_ref[...] = _layer_norm(y, g_ref[...], b_ref[...])


def _moe_combine(ys, x_tiles, route, g_row, b_row):
    t = x_tiles.shape[0] // SUBLANES
    rows = ROWS_COMBINE
    nt = t // rows
    row_map = lambda i: (i, 0)
    const = lambda i: (0, 0)
    return pl.pallas_call(
        _combine_kernel,
        grid=(nt,),
        in_specs=[pl.BlockSpec((rows * SUBLANES, LANES), row_map),
                  pl.BlockSpec((rows * SUBLANES, LANES), lambda i: (nt + i, 0)),
                  pl.BlockSpec((rows * SUBLANES, LANES), row_map),
                  pl.BlockSpec((rows, LANES), row_map),
                  pl.BlockSpec((1, D_MODEL), const),
                  pl.BlockSpec((1, D_MODEL), const)],
        out_specs=pl.BlockSpec((rows, D_MODEL), row_map),
        out_shape=jax.ShapeDtypeStruct((t, D_MODEL), F32),
        compiler_params=_params("arbitrary"),
        name="moe_combine",
    )(ys, ys, x_tiles, route, g_row, b_row)


def _routing_tables(route, n_tokens):
    rows = ROWS_FFN
    n_assign = n_tokens * TOP_K
    n_tiles = n_assign // rows + N_EXPERTS
    experts = route[:, 2:4].astype(jnp.int32).reshape(-1)
    onehot = (experts[:, None] == jnp.arange(N_EXPERTS, dtype=jnp.int32)[None, :]).astype(jnp.int32)
    counts = jnp.sum(onehot, axis=0)
    tiles_per = (counts + rows - 1) // rows
    tile_end = jnp.cumsum(tiles_per)
    id_bits = max(n_assign, rows).bit_length()
    real_keys = (experts << (id_bits + 1)) | jnp.arange(n_assign, dtype=jnp.int32)
    pad_id = jnp.arange(rows, dtype=jnp.int32)[None, :]
    pad_needed = pad_id < (tiles_per * rows - counts)[:, None]
    pad_keys = (jnp.arange(N_EXPERTS, dtype=jnp.int32)[:, None] << (id_bits + 1)) | (1 << id_bits) | pad_id
    pad_keys = jnp.where(pad_needed, pad_keys, jnp.iinfo(jnp.int32).max)
    keys = jnp.sort(jnp.concatenate([real_keys, pad_keys.reshape(-1)]))
    real = ((keys >> id_bits) & 1) == 0
    assign = jnp.where(real, keys & ((1 << id_bits) - 1), -1)
    row_token = jnp.where(real, assign // TOP_K, 0)
    pad_rank = jnp.minimum(jnp.cumsum(1 - real.astype(jnp.int32)) - 1, N_EXPERTS * rows - 1)
    row_dst = jnp.where(real, (assign % TOP_K) * n_tokens + assign // TOP_K, n_assign + pad_rank)
    tile_id = jnp.arange(n_tiles, dtype=jnp.int32)
    n_used = tile_end[-1]
    tile_expert = jnp.sum((tile_id[:, None] >= tile_end[None, :]).astype(jnp.int32), axis=1)
    last_expert = jnp.sum((n_used - 1 >= tile_end).astype(jnp.int32))
    tile_expert = jnp.minimum(jnp.where(tile_id < n_used, tile_expert, last_expert), N_EXPERTS - 1)
    return n_used.reshape(1), tile_expert, row_token, row_dst, n_assign + N_EXPERTS * rows


def _rope_tables(seq):
    half = HEAD_DIM // 2
    inv_freq = ROPE_THETA ** (-jnp.arange(half, dtype=F32) / half)
    ang = jnp.arange(seq).astype(F32)[:, None] * inv_freq[None, :]
    cos, sin = jnp.cos(ang), jnp.sin(ang)
    return jnp.concatenate([cos, cos], axis=1), jnp.concatenate([-sin, sin], axis=1)


def _pad_lanes(w):
    return jnp.pad(w, ((0, 0), (0, LANES - w.shape[1])))


def kernel(x, a_w_in, a_conv_w, a_log_decay, a_dt_bias, a_norm_w, a_w_out, b_w_kv, b_w_q, b_w_o,
           ffn_w1, ffn_w3, ffn_w2, moe_router, moe_w1, moe_w3, moe_w2, ln_g, ln_b):
    batch, seq, _ = x.shape
    t = batch * seq
    width = N_HEADS * HEAD_DIM
    x0 = x.reshape(t, D_MODEL)
    row = lambda v: v.reshape(1, -1).astype(F32)

    w_in = a_w_in[0].astype(BF16)
    q, k, v, gate, gb, moe_w1b, moe_w2b, w_outb, w_ob, w_kvb, w_qb = _in_proj(
        x0, w_in, _pad_lanes(w_in[:, 4 * width:]),
        a_conv_w[0], _pad_lanes(row(a_log_decay[0])), _pad_lanes(row(a_dt_bias[0])), batch, seq,
        riders=(moe_w1[0].reshape(N_EXPERTS * D_MODEL, FFN_HIDDEN),
                moe_w2[0].reshape(N_EXPERTS * FFN_HIDDEN, D_MODEL),
                a_w_out[0], b_w_o[0], b_w_kv, b_w_q[0]))
    og, moe_w3b, ffn_w1b, ffn_w3b = _delta_rule(
        q, k, v, gate, gb, row(a_norm_w[0]), batch, seq,
        riders=(moe_w3[0].reshape(N_EXPERTS * D_MODEL, FFN_HIDDEN), ffn_w1[0], ffn_w3[0]))
    x2 = _mixer_ffn(og, w_outb, x0, row(ln_g[0, 0]), row(ln_b[0, 0]),
                    ffn_w1b, ffn_w3b, ffn_w2[0].astype(BF16), row(ln_g[0, 1]), row(ln_b[0, 1]))

    cos_full, sin_signed = _rope_tables(seq)
    qm, km, vt, sel = _qkv_proj(x2, w_kvb, w_qb, cos_full, sin_signed, batch, seq)
    attn = _moba_attention(qm, km, vt, sel, batch, seq)
    x3_tiles, route = _proj_ln_router(attn, w_ob, x2, row(ln_g[1, 0]), row(ln_b[1, 0]),
                                      _pad_lanes(moe_router[0]))
    n_used, tile_expert, row_token, row_dst, n_out_rows = _routing_tables(route, t)
    ys = _moe_experts(n_used, tile_expert, row_token, row_dst, x3_tiles,
                      moe_w1b.reshape(N_EXPERTS, D_MODEL, FFN_HIDDEN),
                      moe_w3b.reshape(N_EXPERTS, D_MODEL, FFN_HIDDEN),
                      moe_w2b.reshape(N_EXPERTS, FFN_HIDDEN, D_MODEL), n_out_rows)
    x4 = _moe_combine(ys, x3_tiles, route, row(ln_g[1, 1]), row(ln_b[1, 1]))
    return x4.reshape(batch, seq, D_MODEL)
```

```python
import functools
import math

import jax
import jax.numpy as jnp
from jax import lax
from jax.experimental import pallas as pl
from jax.experimental.pallas import tpu as pltpu

D_MODEL = 1024
DEPTH = 2
DEEPNORM_ALPHA = (2.0 * DEPTH) ** 0.25
LN_EPS = 1e-5
N_HEADS = 8
HEAD_DIM = 128
DN_CONV = 4
DN_CHUNK = 64
DN_NORM_EPS = 1e-6
MOBA_BLOCK = 256
MOBA_TOPK = 3
ROPE_THETA = 10000.0
NEG_INF = -1e30
LOG2_E = math.log2(math.e)
FFN_HIDDEN = 3584
N_EXPERTS = 8
TOP_K = 2

LANES = 128
SUBLANES = 8
VMEM_LIMIT_BYTES = 56 * 1024 * 1024

ROWS_PROJ = 256
ROWS_DELTA = 256
ROWS_LN = 1024
ROWS_LN_SUB = 256
ROWS_FFN = 512
FFN_CHUNK = 1792
ROWS_COMBINE = 1024

F32 = jnp.float32
BF16 = jnp.bfloat16

_NT = (((1,), (1,)), ((), ()))


def _dot(a, b):
    return jnp.dot(a, b, preferred_element_type=F32)


def _dot_nt(a, b):
    return lax.dot_general(a, b, _NT, preferred_element_type=F32)


def _params(*sem):
    return pltpu.CompilerParams(dimension_semantics=sem, vmem_limit_bytes=VMEM_LIMIT_BYTES)


def _layer_norm(y, g, b):
    mu = jnp.mean(y, axis=-1, keepdims=True)
    d = y - mu
    var = jnp.mean(d * d, axis=-1, keepdims=True)
    return d * lax.rsqrt(var + LN_EPS) * g + b


def _sigmoid(x):
    return 1.0 / (1.0 + jnp.exp(-x))


def _split2(x):
    hi = x.astype(BF16)
    lo = (x - hi.astype(F32)).astype(BF16)
    return hi, lo


def _dot_split(a, b, dot):
    a_hi, a_lo = _split2(a)
    b_hi, b_lo = _split2(b)
    return dot(a_hi, b_hi) + dot(a_hi, b_lo) + dot(a_lo, b_hi)


def _rider_specs(arrays, n_steps, index_map):
    assert all(a.shape[0] % (n_steps * 2 * SUBLANES) == 0 for a in arrays)
    specs = [pl.BlockSpec((a.shape[0] // n_steps, a.shape[1]), index_map) for a in arrays]
    return specs, [jax.ShapeDtypeStruct(a.shape, BF16) for a in arrays]


def _cast_riders(src_refs, dst_refs):
    for src, dst in zip(src_refs, dst_refs):
        dst[...] = src[...].astype(BF16)


def _in_proj_kernel(n_riders, x_ref, w_ref, wab_ref, cw_ref, alog_ref, dtb_ref, *rest):
    riders_in, rest = rest[:n_riders], rest[n_riders:]
    q_ref, k_ref, v_ref, gate_ref, gb_ref = rest[:5]
    riders_out, conv_sc = rest[5:5 + n_riders], rest[5 + n_riders]
    rows = x_ref.shape[0]
    width = N_HEADS * HEAD_DIM
    hist = SUBLANES

    @pl.when(pl.program_id(1) == 0)
    def _():
        conv_sc[...] = jnp.zeros(conv_sc.shape, F32)

    xb = x_ref[...].astype(BF16)
    chunk = 2 * HEAD_DIM
    for c in range(3 * width // chunk):
        proj = _dot(xb, w_ref[:, c * chunk:(c + 1) * chunk])
        which = c * chunk // width
        for hh in range(chunk // HEAD_DIM):
            slab = c * (chunk // HEAD_DIM) + hh
            cols = slice(slab * HEAD_DIM, (slab + 1) * HEAD_DIM)
            cur = proj[:, hh * HEAD_DIM:(hh + 1) * HEAD_DIM]
            conv_sc[slab, pl.ds(2 * hist, rows, stride=2), :] = cur
            y = cur * cw_ref[DN_CONV - 1:DN_CONV, cols]
            for j in range(DN_CONV - 1):
                shifted = conv_sc[slab, pl.ds(2 * (hist - (DN_CONV - 1) + j), rows, stride=2), :]
                y = y + shifted * cw_ref[j:j + 1, cols]
            conv_sc[slab, 0:2 * hist, :] = conv_sc[slab, 2 * rows:2 * (rows + hist), :]
            t = y * _sigmoid(y)
            ocols = slice(slab * HEAD_DIM - which * width, (slab + 1) * HEAD_DIM - which * width)
            if which == 2:
                v_ref[:, ocols] = t.astype(BF16)
            else:
                inv_norm = lax.rsqrt(jnp.sum(t * t, axis=-1, keepdims=True) + DN_NORM_EPS)
                if which == 0:
                    q_ref[:, ocols] = (t * (inv_norm * (HEAD_DIM ** -0.5))).astype(BF16)
                else:
                    k_ref[:, ocols] = (t * inv_norm).astype(BF16)
        if c % 3 == 2:
            g0 = (c // 3) * chunk
            gate_ref[:, g0:g0 + chunk] = _dot(xb, w_ref[:, 3 * width + g0:3 * width + g0 + chunk]).astype(BF16)
    ab = _dot(xb, wab_ref[...])
    z = ab + dtb_ref[...]
    softplus = jnp.maximum(z, 0.0) + jnp.log(1.0 + jnp.exp(-jnp.abs(z)))
    g = -jnp.exp(alog_ref[...]) * softplus
    lane = lax.broadcasted_iota(jnp.int32, ab.shape, 1)
    gb_ref[...] = jnp.where(lane < N_HEADS, g, _sigmoid(ab))
    _cast_riders(riders_in, riders_out)


def _in_proj(x2d, w_main, w_ab, conv_w, a_log_row, dt_row, batch, seq, riders=()):
    t = x2d.shape[0]
    width = N_HEADS * HEAD_DIM
    rows = ROWS_PROJ
    nt = seq // rows
    row_map = lambda b, s: (b * nt + s, 0)
    const = lambda b, s: (0, 0)
    act = jax.ShapeDtypeStruct((t, width), BF16)
    rider_specs, rider_shapes = _rider_specs(riders, batch * nt, row_map)
    return pl.pallas_call(
        functools.partial(_in_proj_kernel, len(riders)),
        grid=(batch, nt),
        in_specs=[pl.BlockSpec((rows, D_MODEL), row_map),
                  pl.BlockSpec((D_MODEL, w_main.shape[1]), const),
                  pl.BlockSpec((D_MODEL, LANES), const),
                  pl.BlockSpec((DN_CONV, 3 * width), const),
                  pl.BlockSpec((1, LANES), const),
                  pl.BlockSpec((1, LANES), const)] + rider_specs,
        out_specs=[pl.BlockSpec((rows, width), row_map)] * 4 + [pl.BlockSpec((rows, LANES), row_map)]
        + rider_specs,
        out_shape=[act, act, act, act, jax.ShapeDtypeStruct((t, LANES), F32)] + rider_shapes,
        scratch_shapes=[pltpu.VMEM((3 * width // LANES, 2 * (rows + SUBLANES), LANES), F32)],
        compiler_params=_params("arbitrary", "arbitrary"),
        name="dn_in_proj",
    )(x2d, w_main, w_ab, conv_w, a_log_row, dt_row, *riders)


def _delta_kernel(n_riders, q_ref, k_ref, v_ref, gate_ref, gb_ref, nw_ref, *rest):
    riders_in, o_ref = rest[:n_riders], rest[n_riders]
    riders_out, state_sc = rest[n_riders + 1:2 * n_riders + 1], rest[2 * n_riders + 1]
    c = DN_CHUNK
    two = 2 * c
    n_sub = q_ref.shape[0] // c
    n_pairs = N_HEADS // 2

    @pl.when(pl.program_id(1) == 0)
    def _():
        state_sc[...] = jnp.zeros(state_sc.shape, F32)

    r = lax.broadcasted_iota(jnp.int32, (two, two), 0)
    cc = lax.broadcasted_iota(jnp.int32, (two, two), 1)
    same = (r < c) == (cc < c)
    tri = same & (r >= cc)
    stri = same & (r > cc)
    eye = (r == cc).astype(F32)
    top_rows = lax.broadcasted_iota(jnp.int32, (two, 1), 0) < c
    top_lanes = lax.broadcasted_iota(jnp.int32, (1, two), 1) < c
    row64 = lax.broadcasted_iota(jnp.int32, (c, LANES), 0)

    def stack(ref, rows, h):
        t = ref[rows, h * HEAD_DIM:(h + 2) * HEAD_DIM].astype(F32)
        return jnp.concatenate([t[:, :HEAD_DIM], t[:, HEAD_DIM:]], axis=0)

    items = []
    for s in range(n_sub):
        rows = slice(s * c, (s + 1) * c)
        gb = gb_ref[rows, :]
        gc = gb
        sh = 1
        while sh < c:
            gc = gc + jnp.where(row64 >= sh, pltpu.roll(gc, sh, axis=0), 0.0)
            sh *= 2
        gc_pair = jnp.concatenate([gc, pltpu.roll(gc, LANES - 1, axis=1)], axis=0)
        gb_pair = jnp.concatenate([gb, pltpu.roll(gb, LANES - 1, axis=1)], axis=0)
        gc_pair_t = gc_pair.T
        for p in range(n_pairs):
            h = 2 * p
            q2, k2, v2 = stack(q_ref, rows, h), stack(k_ref, rows, h), stack(v_ref, rows, h)
            gcol = gc_pair[:, h:h + 1]
            grow = gc_pair_t[h:h + 1, :]
            beta = gb_pair[:, N_HEADS + h:N_HEADS + h + 1]
            glast = jnp.where(top_rows, gc_pair[c - 1:c, h:h + 1], gc_pair[two - 1:two, h:h + 1])
            eg = jnp.exp(gcol)
            kb = k2 * beta
            items.append(dict(
                rows=rows, h=h, k2b=k2.astype(BF16), q2b=q2.astype(BF16), kbb=kb.astype(BF16),
                decay=jnp.exp(jnp.where(tri, gcol - grow, NEG_INF)),
                rhs=jnp.concatenate([v2 * beta, kb * eg], axis=1).astype(BF16),
                qd=(q2 * eg).astype(BF16),
                kd_t=(k2 * jnp.exp(glast - gcol)).T,
                eglast=jnp.exp(glast)))
    for it in items:
        it["pw"] = jnp.where(stri, _dot_nt(it["kbb"], it["k2b"]) * it["decay"], 0.0)
        it["inv"] = eye - it["pw"]
    for _ in range(int(math.log2(c)) - 1):
        for it in items:
            pwb = it["pw"].astype(BF16)
            it["pw"] = _dot(pwb, pwb)
        for it in items:
            it["inv"] = it["inv"] + _dot(it["inv"].astype(BF16), it["pw"].astype(BF16))
    for it in items:
        sol = _dot(it["inv"].astype(BF16), it["rhs"])
        it["u"] = sol[:, :HEAD_DIM]
        it["wb"] = sol[:, HEAD_DIM:].astype(BF16)
    for it in items:
        it["amat"] = jnp.where(tri, _dot_nt(it["q2b"], it["k2b"]) * it["decay"], 0.0).astype(BF16)

    for s in range(n_sub):
        group = items[s * n_pairs:(s + 1) * n_pairs]
        for it in group:
            h = it["h"]
            it["s0"] = state_sc[h]
            it["s1"] = state_sc[h + 1]
            it["r0"] = _dot(jnp.concatenate([it["wb"][:c], it["qd"][:c]], axis=0), it["s0"].astype(BF16))
            it["r1"] = _dot(jnp.concatenate([it["wb"][c:], it["qd"][c:]], axis=0), it["s1"].astype(BF16))
        for it in group:
            v_new = it["u"] - jnp.concatenate([it["r0"][:c], it["r1"][:c]], axis=0)
            it["vnb"] = v_new.astype(BF16)
            it["o"] = jnp.concatenate([it["r0"][c:], it["r1"][c:]], axis=0) + _dot(it["amat"], it["vnb"])
        for it in group:
            h = it["h"]
            kd_t = it["kd_t"]
            state_sc[h] = it["s0"] * it["eglast"][0:1, :] + _dot(
                jnp.where(top_lanes, kd_t, 0.0).astype(BF16), it["vnb"])
            state_sc[h + 1] = it["s1"] * it["eglast"][c:c + 1, :] + _dot(
                jnp.where(top_lanes, 0.0, kd_t).astype(BF16), it["vnb"])
        for it in group:
            h, rows, o = it["h"], it["rows"], it["o"]
            gate2 = stack(gate_ref, rows, h)
            o = o * lax.rsqrt(jnp.mean(o * o, axis=-1, keepdims=True) + DN_NORM_EPS) * nw_ref[...]
            o = o * (gate2 * _sigmoid(gate2))
            o_ref[rows, h * HEAD_DIM:(h + 1) * HEAD_DIM] = o[:c].astype(BF16)
            o_ref[rows, (h + 1) * HEAD_DIM:(h + 2) * HEAD_DIM] = o[c:].astype(BF16)
    _cast_riders(riders_in, riders_out)


def _delta_rule(q, k, v, gate, gb, norm_w_row, batch, seq, riders=()):
    t = q.shape[0]
    width = N_HEADS * HEAD_DIM
    rows = ROWS_DELTA
    ns = seq // rows
    row_map = lambda b, s: (b * ns + s, 0)
    const = lambda b, s: (0, 0)
    wide = pl.BlockSpec((rows, width), row_map)
    rider_specs, rider_shapes = _rider_specs(riders, batch * ns, row_map)
    return pl.pallas_call(
        functools.partial(_delta_kernel, len(riders)),
        grid=(batch, ns),
        in_specs=[wide, wide, wide, wide,
                  pl.BlockSpec((rows, LANES), row_map),
                  pl.BlockSpec((1, HEAD_DIM), const)] + rider_specs,
        out_specs=[wide] + rider_specs,
        out_shape=[jax.ShapeDtypeStruct((t, width), BF16)] + rider_shapes,
        scratch_shapes=[pltpu.VMEM((N_HEADS, HEAD_DIM, HEAD_DIM), F32)],
        compiler_params=_params("arbitrary", "arbitrary"),
        name="dn_delta_rule",
    )(q, k, v, gate, gb, norm_w_row, *riders)


def _rows_to_tiles(ref, base, y):
    for k in range(D_MODEL // LANES):
        ref[pl.ds(base + k, y.shape[0], stride=SUBLANES), :] = y[:, k * LANES:(k + 1) * LANES]


def _tiles_to_rows(ref, base, n):
    return jnp.concatenate([ref[pl.ds(base + k, n, stride=SUBLANES), :] for k in range(D_MODEL // LANES)],
                           axis=1)


def _proj_residual(a_ref, w_ref, res_ref):
    blocks = []
    for rb in range(a_ref.shape[0] // ROWS_LN_SUB):
        rows = slice(rb * ROWS_LN_SUB, (rb + 1) * ROWS_LN_SUB)
        blocks.append((rows, DEEPNORM_ALPHA * res_ref[rows, :] + _dot(a_ref[rows, :], w_ref[...])))
    return blocks


def _proj_ln_router_kernel(a_ref, w_ref, res_ref, g_ref, b_ref, wr_ref, ot_ref, route_ref):
    for rows, y in _proj_residual(a_ref, w_ref, res_ref):
        xn = _layer_norm(y, g_ref[...], b_ref[...])
        _rows_to_tiles(ot_ref, rows.start * SUBLANES, xn)
        logits = _dot_split(xn, wr_ref[...], _dot)
        lane = lax.broadcasted_iota(jnp.int32, logits.shape, 1)
        logits = jnp.where(lane < N_EXPERTS, logits, NEG_INF)
        l1 = jnp.max(logits, axis=-1, keepdims=True)
        i1 = jnp.min(jnp.where(logits == l1, lane, LANES), axis=-1, keepdims=True)
        rest = jnp.where(lane == i1, NEG_INF, logits)
        l2 = jnp.max(rest, axis=-1, keepdims=True)
        i2 = jnp.min(jnp.where(rest == l2, lane, LANES), axis=-1, keepdims=True)
        e2 = jnp.exp(l2 - l1)
        g1 = 1.0 / (1.0 + e2)
        g2 = e2 / (1.0 + e2)
        route_ref[rows, :] = jnp.where(lane == 0, g1,
                             jnp.where(lane == 1, g2,
                             jnp.where(lane == 2, i1.astype(F32),
                             jnp.where(lane == 3, i2.astype(F32), 0.0))))


def _proj_ln_router(a, w, res, g_row, b_row, w_router):
    t, kdim = a.shape
    rows = ROWS_LN
    row_map = lambda i: (i, 0)
    const = lambda i: (0, 0)
    in_specs = [pl.BlockSpec((rows, kdim), row_map),
                pl.BlockSpec((kdim, D_MODEL), const),
                pl.BlockSpec((rows, D_MODEL), row_map),
                pl.BlockSpec((1, D_MODEL), const),
                pl.BlockSpec((1, D_MODEL), const)]
    return pl.pallas_call(
        _proj_ln_router_kernel, grid=(t // rows,),
        in_specs=in_specs + [pl.BlockSpec((D_MODEL, LANES), const)],
        out_specs=[pl.BlockSpec((rows * SUBLANES, LANES), row_map),
                   pl.BlockSpec((rows, LANES), row_map)],
        out_shape=[jax.ShapeDtypeStruct((t * SUBLANES, LANES), F32),
                   jax.ShapeDtypeStruct((t, LANES), F32)],
        compiler_params=_params("arbitrary"), name="proj_ln_router",
    )(a, w, res, g_row, b_row, w_router)


def _swiglu_hidden(x, w1_ref, w3_ref, hid_sc):
    xb = x.astype(BF16)
    sub = 2 * LANES
    for c in range(hid_sc.shape[1] // sub):
        cols = slice(c * sub, (c + 1) * sub)
        h1 = _dot(xb, w1_ref[:, cols])
        h3 = _dot(xb, w3_ref[:, cols])
        hid_sc[:, cols] = (h1 * _sigmoid(h1) * h3).astype(BF16)


def _swiglu_part(x, w1_ref, w3_ref, w2_ref, hid_sc):
    _swiglu_hidden(x, w1_ref, w3_ref, hid_sc)
    return _dot(hid_sc[...], w2_ref[...])


def _mixer_ffn_kernel(a_ref, wo_ref, res_ref, g0_ref, b0_ref, w1_ref, w3_ref, w2_ref, g1_ref, b1_ref,
                      o_ref, x_sc, acc_sc, hid_sc):
    f = pl.program_id(1)
    last_f = pl.num_programs(1) - 1
    n_sub = a_ref.shape[0] // ROWS_LN_SUB

    @pl.when(f == 0)
    def _():
        for rows, y in _proj_residual(a_ref, wo_ref, res_ref):
            x_sc[rows, :] = _layer_norm(y, g0_ref[...], b0_ref[...])

    _swiglu_hidden(x_sc[...], w1_ref, w3_ref, hid_sc)

    @pl.when(f == 0)
    def _():
        acc_sc[...] = _dot(hid_sc[...], w2_ref[...])

    @pl.when((f > 0) & (f < last_f))
    def _():
        acc_sc[...] += _dot(hid_sc[...], w2_ref[...])

    @pl.when(f == last_f)
    def _():
        for rb in range(n_sub):
            rows = slice(rb * ROWS_LN_SUB, (rb + 1) * ROWS_LN_SUB)
            y = DEEPNORM_ALPHA * x_sc[rows, :] + (acc_sc[rows, :] + _dot(hid_sc[rows, :], w2_ref[...]))
            o_ref[rows, :] = _layer_norm(y, g1_ref[...], b1_ref[...])


def _mixer_ffn(a, wo, res, g0_row, b0_row, w1, w3, w2, g1_row, b1_row):
    t = res.shape[0]
    rows, fc = ROWS_FFN, FFN_CHUNK
    nf = FFN_HIDDEN // fc
    assert nf >= 2
    row_map = lambda i, f: (i, 0)
    const = lambda i, f: (0, 0)
    vec = pl.BlockSpec((1, D_MODEL), const)
    return pl.pallas_call(
        _mixer_ffn_kernel,
        grid=(t // rows, nf),
        in_specs=[pl.BlockSpec((rows, a.shape[1]), row_map),
                  pl.BlockSpec((a.shape[1], D_MODEL), const),
                  pl.BlockSpec((rows, D_MODEL), row_map), vec, vec,
                  pl.BlockSpec((D_MODEL, fc), lambda i, f: (0, f)),
                  pl.BlockSpec((D_MODEL, fc), lambda i, f: (0, f)),
                  pl.BlockSpec((fc, D_MODEL), lambda i, f: (f, 0)), vec, vec],
        out_specs=pl.BlockSpec((rows, D_MODEL), row_map),
        out_shape=jax.ShapeDtypeStruct((t, D_MODEL), F32),
        scratch_shapes=[pltpu.VMEM((rows, D_MODEL), F32), pltpu.VMEM((rows, D_MODEL), F32),
                        pltpu.VMEM((rows, fc), BF16)],
        compiler_params=_params("arbitrary", "arbitrary"),
        name="mixer_ffn_dense",
    )(a, wo, res, g0_row, b0_row, w1, w3, w2, g1_row, b1_row)


def _moe_experts_kernel(nu_ref, te_ref, idx_ref, idx_next_ref, dst_ref, x_hbm, w1_ref, w3_ref, w2_ref,
                        y_hbm, xbuf, ybuf, xb_sc, acc_sc, hid_sc, gsem, ssem):
    i = pl.program_id(0)
    f = pl.program_id(1)
    last_f = pl.num_programs(1) - 1
    rows = xb_sc.shape[0]
    tile_rows = rows * SUBLANES
    nu = nu_ref[0]
    slot = i % 2

    def gather_start(idx, s):
        for r in range(rows):
            src = x_hbm.at[pl.ds(pl.multiple_of(idx[0, r], SUBLANES), SUBLANES)]
            pltpu.make_async_copy(src, xbuf.at[pl.ds(s * tile_rows + r * SUBLANES, SUBLANES)], gsem.at[s]).start()

    def scatter_start(s):
        for r in range(rows):
            dst = y_hbm.at[pl.ds(pl.multiple_of(dst_ref[0, r], SUBLANES), SUBLANES)]
            pltpu.make_async_copy(ybuf.at[pl.ds(s * tile_rows + r * SUBLANES, SUBLANES)], dst,
                                  ssem.at[s]).start(priority=r % 2)

    def wait_all(buf, sem, s):
        view = buf.at[pl.ds(pl.multiple_of(s * tile_rows, tile_rows), tile_rows)]
        pltpu.make_async_copy(view, view, sem.at[s]).wait()

    def for_slot(cond, s, fn):
        for static_slot in range(2):
            pl.when(cond & (s == static_slot))(functools.partial(fn, static_slot))

    first = f == 0

    @pl.when(first & (i == 0))
    def _():
        n_real = TOP_K * x_hbm.shape[0]
        ybuf[tile_rows:2 * tile_rows, :] = jnp.zeros((tile_rows, LANES), F32)
        fills = [pltpu.make_async_copy(ybuf.at[pl.ds(tile_rows, tile_rows)],
                                       y_hbm.at[pl.ds(n_real + k * tile_rows, tile_rows)], ssem.at[1])
                 for k in range((y_hbm.shape[0] - n_real) // tile_rows)]
        for cp in fills:
            cp.start()
        for cp in fills:
            cp.wait()

    @pl.when(first & (i == 0) & (nu > 0))
    def _():
        gather_start(idx_ref, 0)

    @pl.when(first & (i < nu))
    def _():
        wait_all(xbuf, gsem, slot)

    for_slot(first & (i + 1 < nu), 1 - slot, functools.partial(gather_start, idx_next_ref))

    @pl.when(i < nu)
    def _():
        @pl.when(first)
        def _():
            base = pl.multiple_of(slot * tile_rows, tile_rows)
            for k in range(D_MODEL // LANES):
                xb_sc[:, k * LANES:(k + 1) * LANES] = xbuf[pl.ds(base + k, rows, stride=SUBLANES), :].astype(BF16)

        part = _swiglu_part(xb_sc[...], w1_ref, w3_ref, w2_ref, hid_sc)

        @pl.when(first)
        def _():
            acc_sc[...] = part

        @pl.when((f > 0) & (f < last_f))
        def _():
            acc_sc[...] += part

        @pl.when(f == last_f)
        def _():
            @pl.when(i >= 2)
            def _():
                wait_all(ybuf, ssem, slot)
            _rows_to_tiles(ybuf, pl.multiple_of(slot * tile_rows, tile_rows), acc_sc[...] + part)

        for_slot(f == last_f, slot, scatter_start)

    @pl.when((i == pl.num_programs(0) - 1) & (f == last_f))
    def _():
        @pl.when(nu >= 1)
        def _():
            wait_all(ybuf, ssem, (nu - 1) % 2)

        @pl.when(nu >= 2)
        def _():
            wait_all(ybuf, ssem, nu % 2)


def _moe_experts(n_used, tile_expert, row_token, row_dst, x_tiles, w1, w3, w2, n_out_rows):
    n_tiles = tile_expert.shape[0]
    rows, fc = ROWS_FFN, FFN_CHUNK
    nf = FFN_HIDDEN // fc
    idx3 = (row_token * SUBLANES).reshape(n_tiles, 1, rows)
    dst3 = (row_dst * SUBLANES).reshape(n_tiles, 1, rows)
    smem_tile = lambda fn: pl.BlockSpec((None, 1, rows), fn, memory_space=pltpu.SMEM)
    grid_spec = pltpu.PrefetchScalarGridSpec(
        num_scalar_prefetch=2,
        grid=(n_tiles, nf),
        in_specs=[smem_tile(lambda i, f, nu, te: (i, 0, 0)),
                  smem_tile(lambda i, f, nu, te: (jnp.minimum(i + 1, n_tiles - 1), 0, 0)),
                  smem_tile(lambda i, f, nu, te: (i, 0, 0)),
                  pl.BlockSpec(memory_space=pl.ANY),
                  pl.BlockSpec((None, D_MODEL, fc), lambda i, f, nu, te: (te[i], 0, f)),
                  pl.BlockSpec((None, D_MODEL, fc), lambda i, f, nu, te: (te[i], 0, f)),
                  pl.BlockSpec((None, fc, D_MODEL), lambda i, f, nu, te: (te[i], f, 0))],
        out_specs=pl.BlockSpec(memory_space=pl.ANY),
        scratch_shapes=[pltpu.VMEM((2 * rows * SUBLANES, LANES), F32), pltpu.VMEM((2 * rows * SUBLANES, LANES), F32),
                        pltpu.VMEM((rows, D_MODEL), BF16),
                        pltpu.VMEM((rows, D_MODEL), F32), pltpu.VMEM((rows, fc), BF16),
                        pltpu.SemaphoreType.DMA((2,)), pltpu.SemaphoreType.DMA((2,))],
    )
    return pl.pallas_call(
        _moe_experts_kernel,
        grid_spec=grid_spec,
        out_shape=jax.ShapeDtypeStruct((n_out_rows * SUBLANES, LANES), F32),
        compiler_params=_params("arbitrary", "arbitrary"),
        name="moe_experts",
    )(n_used, tile_expert, idx3, idx3, dst3, x_tiles, w1, w3, w2)


def _qkv_kernel(x_ref, wkv_ref, wq_ref, cos_ref, sin_ref, q_ref, k_ref, vt_ref, sel_ref, km_sc, qf_sc):
    i = pl.program_id(1)
    width = N_HEADS * HEAD_DIM
    half = HEAD_DIM // 2
    xb = x_ref[...].astype(BF16)
    cosf = cos_ref[...]
    sinf = sin_ref[...]

    @pl.when(i == 0)
    def _():
        km_sc[...] = jnp.zeros(km_sc.shape, F32)

    def rope(t):
        return t * cosf + pltpu.roll(t, half, axis=1) * sinf

    chunk = 2 * HEAD_DIM
    n_chunks = width // chunk
    nb = km_sc.shape[0]

    def q_chunk(c):
        qq = _dot(xb, wq_ref[:, c * chunk:(c + 1) * chunk])
        for hh in range(2):
            cols = slice((2 * c + hh) * HEAD_DIM, (2 * c + hh + 1) * HEAD_DIM)
            qr = rope(qq[:, hh * HEAD_DIM:(hh + 1) * HEAD_DIM])
            qf_sc[:, cols] = qr
            q_ref[:, cols] = (qr * (HEAD_DIM ** -0.5 * LOG2_E)).astype(BF16)

    def kv_chunk(c):
        kk = _dot(xb, wkv_ref[:, c * chunk:(c + 1) * chunk])
        vv = _dot(xb, wkv_ref[:, width + c * chunk:width + (c + 1) * chunk])
        for hh in range(2):
            h = 2 * c + hh
            cols = slice(h * HEAD_DIM, (h + 1) * HEAD_DIM)
            part = slice(hh * HEAD_DIM, (hh + 1) * HEAD_DIM)
            kr = rope(kk[:, part])
            k_ref[:, cols] = kr.astype(BF16)
            km_row = lax.broadcasted_iota(jnp.int32, (nb, HEAD_DIM), 0)
            km_sc[:, cols] = jnp.where(km_row == i, jnp.mean(kr, axis=0, keepdims=True), km_sc[:, cols])
            vt_ref[h] = vv[:, part].T.astype(BF16)

    def block_selection():
        km = km_sc[...]
        lane_head = lax.broadcasted_iota(jnp.int32, km.shape, 1) // HEAD_DIM
        km_rows = jnp.concatenate([jnp.where(lane_head == h, km, 0.0) for h in range(N_HEADS)], axis=0)
        gate_t = _dot_split(km_rows, qf_sc[...], _dot_nt)
        blk = lax.broadcasted_iota(jnp.int32, (nb, x_ref.shape[0]), 0)
        past = blk < i
        for h in range(N_HEADS):
            g = jnp.where(past, gate_t[h * nb:(h + 1) * nb, :], NEG_INF)
            rank = jnp.zeros(g.shape, jnp.int32)
            for n in range(nb):
                gn = g[n:n + 1, :]
                ahead = (gn > g) | ((gn == g) & (blk > n))
                rank = rank + ahead.astype(jnp.int32)
            sel_ref[h * nb:(h + 1) * nb, :] = jnp.where(past & (rank < MOBA_TOPK), 0.0, NEG_INF)

    for c in range(n_chunks):
        q_chunk(c)
    kv_chunk(0)
    block_selection()
    for c in range(1, n_chunks):
        kv_chunk(c)


def _qkv_proj(x, w_kv, w_q, cos_full, sin_signed, batch, seq):
    t = x.shape[0]
    width = N_HEADS * HEAD_DIM
    rows = MOBA_BLOCK
    nb = seq // rows
    row_map = lambda b, i: (b * nb + i, 0)
    act = jax.ShapeDtypeStruct((t, width), BF16)
    return pl.pallas_call(
        _qkv_kernel,
        grid=(batch, nb),
        in_specs=[pl.BlockSpec((rows, D_MODEL), row_map),
                  pl.BlockSpec((D_MODEL, 2 * width), lambda b, i: (0, 0)),
                  pl.BlockSpec((D_MODEL, width), lambda b, i: (0, 0)),
                  pl.BlockSpec((rows, HEAD_DIM), lambda b, i: (i, 0)),
                  pl.BlockSpec((rows, HEAD_DIM), lambda b, i: (i, 0))],
        out_specs=[pl.BlockSpec((rows, width), row_map),
                   pl.BlockSpec((rows, width), row_map),
                   pl.BlockSpec((None, N_HEADS, HEAD_DIM, rows), lambda b, i: (b, 0, 0, i)),
                   pl.BlockSpec((None, N_HEADS * nb, rows), lambda b, i: (b * nb + i, 0, 0))],
        out_shape=[act, act,
                   jax.ShapeDtypeStruct((batch, N_HEADS, HEAD_DIM, seq), BF16),
                   jax.ShapeDtypeStruct((batch * nb, N_HEADS * nb, rows), F32)],
        scratch_shapes=[pltpu.VMEM((nb, width), F32), pltpu.VMEM((rows, width), F32)],
        compiler_params=_params("arbitrary", "arbitrary"),
        name="moba_qkv_proj",
    )(x, w_kv, w_q, cos_full, sin_signed)


def _attn_kernel(q_ref, k_ref, vt_ref, sel_ref, o_ref, kaug_sc, vtaug_sc):
    blk = MOBA_BLOCK
    seq = q_ref.shape[0]
    nb = seq // blk
    extra = 2 * SUBLANES
    key = lax.broadcasted_iota(jnp.int32, (blk, blk), 0)
    qry = lax.broadcasted_iota(jnp.int32, (blk, blk), 1)
    causal = key <= qry

    key_blk = lax.broadcasted_iota(jnp.int32, (seq, LANES), 0) // blk
    kaug_sc[:, 0:HEAD_DIM] = k_ref[...]
    kaug_sc[:, HEAD_DIM:] = jnp.where(lax.broadcasted_iota(jnp.int32, (seq, LANES), 1) == key_blk,
                                      1.0, 0.0).astype(BF16)
    vtaug_sc[0:HEAD_DIM, :] = vt_ref[...]
    vtaug_sc[HEAD_DIM:, :] = jnp.where(lax.broadcasted_iota(jnp.int32, (extra, seq), 0) == 0,
                                       1.0, 0.0).astype(BF16)
    blk_row = lax.broadcasted_iota(jnp.int32, (nb, blk), 0)

    def scores(i):
        bias = jnp.where(blk_row == i, 0.0, sel_ref[i])
        bias_t = jnp.concatenate([bias, jnp.zeros((LANES - nb, blk), F32)], axis=0).T
        q_aug = jnp.concatenate([q_ref[i * blk:(i + 1) * blk, :], bias_t.astype(BF16)], axis=1)
        return _dot_nt(kaug_sc[0:(i + 1) * blk, :], q_aug)

    def finish(i, s):
        parts = [s[j * blk:(j + 1) * blk] for j in range(i)]
        parts.append(jnp.where(causal, s[i * blk:(i + 1) * blk], NEG_INF))
        m = jnp.max(parts[0], axis=0, keepdims=True)
        for part in parts[1:]:
            m = jnp.maximum(m, jnp.max(part, axis=0, keepdims=True))
        p = jnp.concatenate([jnp.exp2(part - m).astype(BF16) for part in parts], axis=0)
        acc = _dot(vtaug_sc[:, 0:(i + 1) * blk], p)
        o_ref[i * blk:(i + 1) * blk, :] = (acc[0:HEAD_DIM] / acc[HEAD_DIM:HEAD_DIM + 1]).T.astype(BF16)

    ahead = 3
    pending = [scores(i) for i in range(min(ahead, nb))]
    for i in range(nb):
        if i + ahead < nb:
            pending.append(scores(i + ahead))
        finish(i, pending.pop(0))


def _moba_attention(q, k, vt, sel, batch, seq):
    t = q.shape[0]
    width = N_HEADS * HEAD_DIM
    nb = seq // MOBA_BLOCK
    col = pl.BlockSpec((seq, HEAD_DIM), lambda b, h: (b, h))
    return pl.pallas_call(
        _attn_kernel,
        grid=(batch, N_HEADS),
        in_specs=[col, col,
                  pl.BlockSpec((None, None, HEAD_DIM, seq), lambda b, h: (b, h, 0, 0)),
                  pl.BlockSpec((nb, nb, MOBA_BLOCK), lambda b, h: (b, h, 0))],
        out_specs=col,
        out_shape=jax.ShapeDtypeStruct((t, width), BF16),
        scratch_shapes=[pltpu.VMEM((seq, HEAD_DIM + LANES), BF16),
                        pltpu.VMEM((HEAD_DIM + 2 * SUBLANES, seq), BF16)],
        compiler_params=_params("arbitrary", "arbitrary"),
        name="moba_attention",
    )(q, k, vt, sel)


def _combine_kernel(y0_ref, y1_ref, x_ref, route_ref, g_ref, b_ref, o_ref):
    route = route_ref[...]
    rows = o_ref.shape[0]
    y = (DEEPNORM_ALPHA * _tiles_to_rows(x_ref, 0, rows) + route[:, 0:1] * _tiles_to_rows(y0_ref, 0, rows)
         + route[:, 1:2] * _tiles_to_rows(y1_ref, 0, rows))
    o_ref[...] = _layer_norm(y, g_ref[...], b_ref[...])


def _moe_combine(ys, x_tiles, route, g_row, b_row):
    t = x_tiles.shape[0] // SUBLANES
    rows = ROWS_COMBINE
    nt = t // rows
    row_map = lambda i: (i, 0)
    const = lambda i: (0, 0)
    return pl.pallas_call(
        _combine_kernel,
        grid=(nt,),
        in_specs=[pl.BlockSpec((rows * SUBLANES, LANES), row_map),
                  pl.BlockSpec((rows * SUBLANES, LANES), lambda i: (nt + i, 0)),
                  pl.BlockSpec((rows * SUBLANES, LANES), row_map),
                  pl.BlockSpec((rows, LANES), row_map),
                  pl.BlockSpec((1, D_MODEL), const),
                  pl.BlockSpec((1, D_MODEL), const)],
        out_specs=pl.BlockSpec((rows, D_MODEL), row_map),
        out_shape=jax.ShapeDtypeStruct((t, D_MODEL), F32),
        compiler_params=_params("arbitrary"),
        name="moe_combine",
    )(ys, ys, x_tiles, route, g_row, b_row)


def _routing_tables(route, n_tokens):
    rows = ROWS_FFN
    n_assign = n_tokens * TOP_K
    n_tiles = n_assign // rows + N_EXPERTS
    experts = route[:, 2:4].astype(jnp.int32).reshape(-1)
    onehot = (experts[:, None] == jnp.arange(N_EXPERTS, dtype=jnp.int32)[None, :]).astype(jnp.int32)
    counts = jnp.sum(onehot, axis=0)
    tiles_per = (counts + rows - 1) // rows
    tile_end = jnp.cumsum(tiles_per)
    id_bits = max(n_assign, rows).bit_length()
    real_keys = (experts << (id_bits + 1)) | jnp.arange(n_assign, dtype=jnp.int32)
    pad_id = jnp.arange(rows, dtype=jnp.int32)[None, :]
    pad_needed = pad_id < (tiles_per * rows - counts)[:, None]
    pad_keys = (jnp.arange(N_EXPERTS, dtype=jnp.int32)[:, None] << (id_bits + 1)) | (1 << id_bits) | pad_id
    pad_keys = jnp.where(pad_needed, pad_keys, jnp.iinfo(jnp.int32).max)
    keys = jnp.sort(jnp.concatenate([real_keys, pad_keys.reshape(-1)]))
    real = ((keys >> id_bits) & 1) == 0
    assign = jnp.where(real, keys & ((1 << id_bits) - 1), -1)
    row_token = jnp.where(real, assign // TOP_K, 0)
    pad_rank = jnp.minimum(jnp.cumsum(1 - real.astype(jnp.int32)) - 1, N_EXPERTS * rows - 1)
    row_dst = jnp.where(real, (assign % TOP_K) * n_tokens + assign // TOP_K, n_assign + pad_rank)
    tile_id = jnp.arange(n_tiles, dtype=jnp.int32)
    n_used = tile_end[-1]
    tile_expert = jnp.sum((tile_id[:, None] >= tile_end[None, :]).astype(jnp.int32), axis=1)
    last_expert = jnp.sum((n_used - 1 >= tile_end).astype(jnp.int32))
    tile_expert = jnp.minimum(jnp.where(tile_id < n_used, tile_expert, last_expert), N_EXPERTS - 1)
    return n_used.reshape(1), tile_expert, row_token, row_dst, n_assign + N_EXPERTS * rows


def _rope_tables(seq):
    half = HEAD_DIM // 2
    inv_freq = ROPE_THETA ** (-jnp.arange(half, dtype=F32) / half)
    ang = jnp.arange(seq).astype(F32)[:, None] * inv_freq[None, :]
    cos, sin = jnp.cos(ang), jnp.sin(ang)
    return jnp.concatenate([cos, cos], axis=1), jnp.concatenate([-sin, sin], axis=1)


def _pad_lanes(w):
    return jnp.pad(w, ((0, 0), (0, LANES - w.shape[1])))


def kernel(x, a_w_in, a_conv_w, a_log_decay, a_dt_bias, a_norm_w, a_w_out, b_w_kv, b_w_q, b_w_o,
           ffn_w1, ffn_w3, ffn_w2, moe_router, moe_w1, moe_w3, moe_w2, ln_g, ln_b):
    batch, seq, _ = x.shape
    t = batch * seq
    width = N_HEADS * HEAD_DIM
    x0 = x.reshape(t, D_MODEL)
    row = lambda v: v.reshape(1, -1).astype(F32)

    w_in = a_w_in[0].astype(BF16)
    q, k, v, gate, gb, moe_w1b, moe_w2b, w_outb, w_ob, w_kvb, w_qb = _in_proj(
        x0, w_in, _pad_lanes(w_in[:, 4 * width:]),
        a_conv_w[0], _pad_lanes(row(a_log_decay[0])), _pad_lanes(row(a_dt_bias[0])), batch, seq,
        riders=(moe_w1[0].reshape(N_EXPERTS * D_MODEL, FFN_HIDDEN),
                moe_w2[0].reshape(N_EXPERTS * FFN_HIDDEN, D_MODEL),
                a_w_out[0], b_w_o[0], b_w_kv, b_w_q[0]))
    og, moe_w3b, ffn_w1b, ffn_w3b = _delta_rule(
        q, k, v, gate, gb, row(a_norm_w[0]), batch, seq,
        riders=(moe_w3[0].reshape(N_EXPERTS * D_MODEL, FFN_HIDDEN), ffn_w1[0], ffn_w3[0]))
    x2 = _mixer_ffn(og, w_outb, x0, row(ln_g[0, 0]), row(ln_b[0, 0]),
                    ffn_w1b, ffn_w3b, ffn_w2[0].astype(BF16), row(ln_g[0, 1]), row(ln_b[0, 1]))

    cos_full, sin_signed = _rope_tables(seq)
    qm, km, vt, sel = _qkv_proj(x2, w_kvb, w_qb, cos_full, sin_signed, batch, seq)
    attn = _moba_attention(qm, km, vt, sel, batch, seq)
    x3_tiles, route = _proj_ln_router(attn, w_ob, x2, row(ln_g[1, 0]), row(ln_b[1, 0]),
                                      _pad_lanes(moe_router[0]))
    n_used, tile_expert, row_token, row_dst, n_out_rows = _routing_tables(route, t)
    ys = _moe_experts(n_used, tile_expert, row_token, row_dst, x3_tiles,
                      moe_w1b.reshape(N_EXPERTS, D_MODEL, FFN_HIDDEN),
                      moe_w3b.reshape(N_EXPERTS, D_MODEL, FFN_HIDDEN),
                      moe_w2b.reshape(N_EXPERTS, FFN_HIDDEN, D_MODEL), n_out_rows)
    x4 = _moe_combine(ys, x3_tiles, route, row(ln_g[1, 1]), row(ln_b[1, 1]))
    return x4.reshape(batch, seq, D_MODEL)
```
